```python
import math
import jax, jax.numpy as jnp
from jax import lax
import numpy as np

D_MODEL = 1024
BATCH = 8
SEQ = 2048
DEPTH = 1

N_META = 16
BLOCK = 128
WINDOW = 128
PAD_FRONT = BLOCK - N_META

ATTN_HEADS = 8
ATTN_KV_HEADS = 2
HEAD_DIM = 64
ROPE_THETA = 10000.0
ATTN_Q_W = ATTN_HEADS * HEAD_DIM
ATTN_KV_W = ATTN_KV_HEADS * HEAD_DIM

GLA_HEADS = 4
GLA_KEY_DIM = 64
GLA_VAL_DIM = 128
GLA_K_W = GLA_HEADS * GLA_KEY_DIM
GLA_V_W = GLA_HEADS * GLA_VAL_DIM
GLA_GATE_RANK = 16
GLA_GATE_NORM = 16.0
GLA_CHUNK = 64

N_GROUPS = 4
EXPERTS_PER_GROUP = 8
EXPERT_TOP_K = 2
EXPERT_FF = 256

NORM_EPS = 1e-6
MASK_VALUE = -1e30

IN_WIDTHS = (ATTN_Q_W, ATTN_KV_W, ATTN_KV_W,
             GLA_K_W, GLA_K_W, GLA_V_W, GLA_V_W,
             GLA_GATE_RANK,
             D_MODEL, D_MODEL)
IN_TOTAL = sum(IN_WIDTHS)

kernel_name = "hybrid_swa_sink_gla_hmoe_meta"


def rms_norm(x, w):
    xf = x.astype(jnp.float32)
    y = xf * lax.rsqrt(jnp.mean(xf * xf, axis=-1, keepdims=True) + NORM_EPS)
    return (y * w.astype(jnp.float32)).astype(x.dtype)


def rope(x, pos):
    half = HEAD_DIM // 2
    inv_freq = ROPE_THETA ** (-(jnp.arange(half, dtype=jnp.float32) * 2.0) / HEAD_DIM)
    ang = pos.astype(jnp.float32)[:, None] * inv_freq[None, :]
    cos = jnp.cos(ang)[None, :, None, :]
    sin = jnp.sin(ang)[None, :, None, :]
    xf = x.astype(jnp.float32)
    x1, x2 = xf[..., :half], xf[..., half:]
    return jnp.concatenate([x1 * cos - x2 * sin, x2 * cos + x1 * sin], axis=-1).astype(x.dtype)


def swa_with_sinks(q, k, v, sinks):
    B, T = q.shape[0], q.shape[1]
    nb = T // BLOCK
    G = ATTN_HEADS // ATTN_KV_HEADS
    qb = q.reshape(B, nb, BLOCK, ATTN_KV_HEADS, G, HEAD_DIM)
    kb = k.reshape(B, nb, BLOCK, ATTN_KV_HEADS, HEAD_DIM)
    vb = v.reshape(B, nb, BLOCK, ATTN_KV_HEADS, HEAD_DIM)
    pad5 = ((0, 0), (1, 0), (0, 0), (0, 0), (0, 0))
    k_band = jnp.concatenate([jnp.pad(kb, pad5)[:, :-1], kb], axis=2)
    v_band = jnp.concatenate([jnp.pad(vb, pad5)[:, :-1], vb], axis=2)
    k_meta = k[:, PAD_FRONT:BLOCK]
    v_meta = v[:, PAD_FRONT:BLOCK]
    scale = 1.0 / math.sqrt(HEAD_DIM)
    s_band = jnp.einsum('bnqhgd,bnkhd->bnhgqk', qb, k_band).astype(jnp.float32) * scale
    s_meta = jnp.einsum('bnqhgd,bmhd->bnhgqm', qb, k_meta).astype(jnp.float32) * scale
    q_idx = jnp.arange(nb)[:, None] * BLOCK + jnp.arange(BLOCK)[None, :]
    k_idx = (jnp.arange(nb)[:, None] - 1) * BLOCK + jnp.arange(2 * BLOCK)[None, :]
    dq = q_idx[:, :, None] - k_idx[:, None, :]
    band_ok = (dq >= 0) & (dq < WINDOW) & (k_idx[:, None, :] >= BLOCK)
    meta_idx = PAD_FRONT + jnp.arange(N_META)
    meta_ok = meta_idx[None, None, :] <= q_idx[:, :, None]
    s_band = jnp.where(band_ok[None, :, None, None], s_band, MASK_VALUE)
    s_meta = jnp.where(meta_ok[None, :, None, None], s_meta, MASK_VALUE)
    sink = jnp.broadcast_to(sinks.astype(jnp.float32).reshape(1, 1, ATTN_KV_HEADS, G, 1, 1),
                            s_meta.shape[:-1] + (1,))
    p = jax.nn.softmax(jnp.concatenate([s_meta, s_band, sink], axis=-1), axis=-1)
    p_meta = p[..., :N_META].astype(v.dtype)
    p_band = p[..., N_META:N_META + 2 * BLOCK].astype(v.dtype)
    o = (jnp.einsum('bnhgqk,bnkhd->bnqhgd', p_band, v_band)
         + jnp.einsum('bnhgqm,bmhd->bnqhgd', p_meta, v_meta))
    return o.reshape(B, T, ATTN_Q_W)


def gla_chunked(q, k, v, gk):
    B, T, H, DK = q.shape
    DV = v.shape[-1]
    n = T // GLA_CHUNK
    def chunks(a):
        return a.astype(jnp.float32).reshape(B, n, GLA_CHUNK, H, a.shape[-1]).transpose(0, 3, 1, 2, 4)
    qc = chunks(q) * (DK ** -0.5)
    kc, vc, gc = chunks(k), chunks(v), chunks(gk)
    b = jnp.cumsum(gc, axis=-2)
    b_mid = b[..., GLA_CHUNK // 2:GLA_CHUNK // 2 + 1, :]
    b_last = b[..., -1:, :]
    att = jnp.einsum('bhnid,bhnjd->bhnij', qc * jnp.exp(b - b_mid), kc * jnp.exp(b_mid - b))
    causal = jnp.tril(jnp.ones((GLA_CHUNK, GLA_CHUNK), dtype=bool))
    att = jnp.where(causal, att, 0.0)
    o_intra = jnp.einsum('bhnij,bhnjv->bhniv', att, vc)
    chunk_kv = jnp.einsum('bhnjd,bhnjv->nbhdv', kc * jnp.exp(b_last - b), vc)
    chunk_decay = jnp.exp(b_last[..., 0, :]).transpose(2, 0, 1, 3)
    def step(state, inp):
        kv, dec = inp
        return dec[..., None] * state + kv, state
    _, states = lax.scan(step, jnp.zeros((B, H, DK, DV), jnp.float32), (chunk_kv, chunk_decay))
    o_inter = jnp.einsum('bhnid,nbhdv->bhniv', qc * jnp.exp(b), states)
    o = (o_intra + o_inter).transpose(0, 2, 3, 1, 4).reshape(B, T, H, DV)
    return o.astype(v.dtype)


def hier_moe(xn, router_group, router_expert, w_gate, w_up, w_down):
    xf = xn.astype(jnp.float32)
    g_logits = xf @ router_group.astype(jnp.float32)
    g_prob = jax.nn.softmax(g_logits, axis=-1)
    g_idx = jnp.argmax(g_logits, axis=-1)
    g_w = jnp.take_along_axis(g_prob, g_idx[:, None], axis=-1)
    e_logits_all = jnp.einsum('nd,gde->nge', xf, router_expert.astype(jnp.float32))
    e_logits = jnp.take_along_axis(e_logits_all, g_idx[:, None, None], axis=1)[:, 0]
    top_v, top_i = lax.top_k(e_logits, EXPERT_TOP_K)
    top_p = jax.nn.softmax(top_v, axis=-1) * g_w
    e_w = jnp.sum(jax.nn.one_hot(top_i, EXPERTS_PER_GROUP, dtype=jnp.float32) * top_p[..., None], axis=1)
    combine = (jax.nn.one_hot(g_idx, N_GROUPS, dtype=jnp.float32)[:, :, None] * e_w[:, None, :]).astype(xn.dtype)
    y = jnp.zeros_like(xn)
    for g in range(N_GROUPS):
        hg = jax.nn.silu(jnp.einsum('nd,edf->nef', xn, w_gate[g])) * jnp.einsum('nd,edf->nef', xn, w_up[g])
        y = y + jnp.einsum('nef,efd->nd', hg * combine[:, g, :, None], w_down[g])
    return y


def hybrid_layer(h, valid, pos, norm1_w, w_in, q_norm_w, k_norm_w, attn_sinks, gla_gate_up,
                 gla_gate_bias, gla_norm_w, w_attn_branch, w_gla_branch, w_out, norm2_w,
                 router_group, router_expert, expert_w_gate, expert_w_up, expert_w_down):
    B, T, D = h.shape
    xn = rms_norm(h, norm1_w)
    proj = xn @ w_in
    parts = []
    off = 0
    for w in IN_WIDTHS:
        parts.append(proj[..., off:off + w])
        off += w
    qa, ka, va, qg, kg, vg, rg, gdown, gate_a, gate_g = parts
    qa = rope(rms_norm(qa.reshape(B, T, ATTN_HEADS, HEAD_DIM), q_norm_w), pos)
    ka = rope(rms_norm(ka.reshape(B, T, ATTN_KV_HEADS, HEAD_DIM), k_norm_w), pos)
    va = va.reshape(B, T, ATTN_KV_HEADS, HEAD_DIM)
    y_attn = swa_with_sinks(qa, ka, va, attn_sinks) @ w_attn_branch
    gk = jax.nn.log_sigmoid((gdown @ gla_gate_up + gla_gate_bias).astype(jnp.float32)) / GLA_GATE_NORM
    gk = gk.reshape(B, T, GLA_HEADS, GLA_KEY_DIM)
    kg = kg.reshape(B, T, GLA_HEADS, GLA_KEY_DIM) * valid[None, :, None, None]
    og = gla_chunked(qg.reshape(B, T, GLA_HEADS, GLA_KEY_DIM), kg,
                     vg.reshape(B, T, GLA_HEADS, GLA_VAL_DIM), gk)
    og = rms_norm(og, gla_norm_w) * jax.nn.silu(rg.reshape(B, T, GLA_HEADS, GLA_VAL_DIM))
    y_gla = og.reshape(B, T, GLA_V_W) @ w_gla_branch
    merged = jax.nn.sigmoid(gate_a) * y_attn + jax.nn.sigmoid(gate_g) * y_gla
    h = h + (merged @ w_out) * valid[None, :, None]
    xn2 = rms_norm(h, norm2_w).reshape(B * T, D)
    y_ffn = hier_moe(xn2, router_group, router_expert, expert_w_gate, expert_w_up, expert_w_down)
    return h + y_ffn.reshape(B, T, D) * valid[None, :, None]


def setup_inputs(seed: int = 0) -> dict:
    key = jax.random.key(seed)
    ks = jax.random.split(key, 20)
    f32 = jnp.float32
    def nrm(k, shape, scale):
        return jax.random.normal(k, shape, f32) * scale
    L = DEPTH
    E = EXPERTS_PER_GROUP
    return {
        "x": nrm(ks[0], (BATCH, SEQ, D_MODEL), 1.0),
        "meta_tokens": nrm(ks[1], (N_META, D_MODEL), 1.0),
        "norm1_w": 1.0 + nrm(ks[2], (L, D_MODEL), 0.02),
        "w_in": nrm(ks[3], (L, D_MODEL, IN_TOTAL), D_MODEL ** -0.5),
        "q_norm_w": 1.0 + nrm(ks[4], (L, HEAD_DIM), 0.02),
        "k_norm_w": 1.0 + nrm(ks[5], (L, HEAD_DIM), 0.02),
        "attn_sinks": nrm(ks[6], (L, ATTN_HEADS), 0.5),
        "gla_gate_up": nrm(ks[7], (L, GLA_GATE_RANK, GLA_K_W), GLA_GATE_RANK ** -0.5),
        "gla_gate_bias": nrm(ks[8], (L, GLA_K_W), 0.02),
        "gla_norm_w": 1.0 + nrm(ks[9], (L, GLA_VAL_DIM), 0.02),
        "w_attn_branch": nrm(ks[10], (L, ATTN_Q_W, D_MODEL), ATTN_Q_W ** -0.5),
        "w_gla_branch": nrm(ks[11], (L, GLA_V_W, D_MODEL), GLA_V_W ** -0.5),
        "w_out": nrm(ks[12], (L, D_MODEL, D_MODEL), D_MODEL ** -0.5),
        "norm2_w": 1.0 + nrm(ks[13], (L, D_MODEL), 0.02),
        "router_group": nrm(ks[14], (L, D_MODEL, N_GROUPS), D_MODEL ** -0.5),
        "router_expert": nrm(ks[15], (L, N_GROUPS, D_MODEL, E), D_MODEL ** -0.5),
        "expert_w_gate": nrm(ks[16], (L, N_GROUPS, E, D_MODEL, EXPERT_FF), D_MODEL ** -0.5),
        "expert_w_up": nrm(ks[17], (L, N_GROUPS, E, D_MODEL, EXPERT_FF), D_MODEL ** -0.5),
        "expert_w_down": nrm(ks[18], (L, N_GROUPS, E, EXPERT_FF, D_MODEL), EXPERT_FF ** -0.5),
    }


def reference(x, meta_tokens, norm1_w, w_in, q_norm_w, k_norm_w, attn_sinks, gla_gate_up,
              gla_gate_bias, gla_norm_w, w_attn_branch, w_gla_branch, w_out, norm2_w,
              router_group, router_expert, expert_w_gate, expert_w_up, expert_w_down):
    B, S, D = x.shape
    meta = jnp.broadcast_to(meta_tokens.astype(x.dtype)[None], (B, N_META, D))
    h = jnp.concatenate([jnp.zeros((B, PAD_FRONT, D), x.dtype), meta, x], axis=1)
    T = h.shape[1]
    idx = jnp.arange(T)
    valid = (idx >= PAD_FRONT).astype(x.dtype)
    pos = idx - PAD_FRONT
    for l in range(DEPTH):
        h = hybrid_layer(h, valid, pos, norm1_w[l], w_in[l], q_norm_w[l], k_norm_w[l],
                         attn_sinks[l], gla_gate_up[l], gla_gate_bias[l], gla_norm_w[l],
                         w_attn_branch[l], w_gla_branch[l], w_out[l], norm2_w[l],
                         router_group[l], router_expert[l], expert_w_gate[l],
                         expert_w_up[l], expert_w_down[l])
    return h[:, BLOCK:]
```

```python
import functools
import math

import jax
import jax.numpy as jnp
from jax import lax
from jax.experimental import pallas as pl
from jax.experimental.pallas import tpu as pltpu

F32 = jnp.float32
BF16 = jnp.bfloat16

D_MODEL = 1024
N_META = 16
BLOCK = 128
ATTN_HEADS = 8
ATTN_KV_HEADS = 2
HEAD_DIM = 64
ROPE_THETA = 10000.0
ATTN_Q_W = ATTN_HEADS * HEAD_DIM
ATTN_KV_W = ATTN_KV_HEADS * HEAD_DIM
GLA_HEADS = 4
GLA_KEY_DIM = 64
GLA_VAL_DIM = 128
GLA_K_W = GLA_HEADS * GLA_KEY_DIM
GLA_V_W = GLA_HEADS * GLA_VAL_DIM
GLA_GATE_RANK = 16
GLA_GATE_NORM = 16.0
GLA_CHUNK = 64
N_GROUPS = 4
EXPERTS_PER_GROUP = 8
N_EXPERTS = N_GROUPS * EXPERTS_PER_GROUP
EXPERT_FF = 256
NORM_EPS = 1e-6
MASK_VALUE = -1e30

LANES = 128
ATTN_KV_DUP_W = 2 * ATTN_KV_W
VMEM_LIMIT = 56 * 1024 * 1024

_OFF_Q = 0
_OFF_K = _OFF_Q + ATTN_Q_W
_OFF_V = _OFF_K + ATTN_KV_W
_OFF_GQ = _OFF_V + ATTN_KV_W
_OFF_GD = _OFF_GQ + 2 * GLA_K_W + 2 * GLA_V_W
_OFF_GATE = _OFF_GD + GLA_GATE_RANK
GLA_W = 2 * GLA_K_W + 2 * GLA_V_W


def _dot(a, b):
    return jnp.dot(a, b, preferred_element_type=F32)


def _dot_nt(a, b):
    return lax.dot_general(a, b, (((1,), (1,)), ((), ())), preferred_element_type=F32)


def _split_hi_lo(x):
    hi = x.astype(BF16)
    lo = (x - hi.astype(F32)).astype(BF16)
    return hi, lo


def _group_mean_sq(x, group):
    w = x.shape[-1]
    shift = int(math.log2(group))
    r = lax.broadcasted_iota(jnp.int32, (w, w), 0) >> shift
    c = lax.broadcasted_iota(jnp.int32, (w, w), 1) >> shift
    ones = jnp.where(r == c, 1.0, 0.0).astype(BF16)
    hi, lo = _split_hi_lo(x * x)
    return (_dot(hi, ones) + _dot(lo, ones)) * (1.0 / group)


def _rope(x, cos, sin_signed):
    w = x.shape[-1]
    reps = w // LANES
    if reps > 1:
        cos = jnp.concatenate([cos] * reps, axis=1)
        sin_signed = jnp.concatenate([sin_signed] * reps, axis=1)
    lane = lax.broadcasted_iota(jnp.int32, x.shape, 1)
    first_half = (lane & (HEAD_DIM // 2)) == 0
    swapped = jnp.where(first_half, pltpu.roll(x, w - HEAD_DIM // 2, 1), pltpu.roll(x, HEAD_DIM // 2, 1))
    return x * cos + swapped * sin_signed


def _sigmoid(x):
    return 1.0 / (1.0 + jnp.exp(-x))


def _log_sigmoid(x):
    return jnp.minimum(x, 0.0) - jnp.log(1.0 + jnp.exp(-jnp.abs(x)))


def _proj_body(x_ref, n1_ref, wq_ref, wk_ref, wv_ref, wgla_ref, wgd_ref, wgate_ref,
               qn_ref, kn_ref, cos_ref, sin_ref, gup_ref, gb_ref,
               q_out, k_out, v_out, qg_out, kg_out, vg_out, rg_out, gk_out, sga_out, sgg_out):
    x = x_ref[...]
    ms = jnp.mean(x * x, axis=-1, keepdims=True)
    xn = (x * lax.rsqrt(ms + NORM_EPS) * n1_ref[...]).astype(BF16)
    cos = cos_ref[...]
    sin = sin_ref[...]

    q = _dot(xn, wq_ref[...])
    q = q * lax.rsqrt(_group_mean_sq(q, HEAD_DIM) + NORM_EPS) * qn_ref[...]
    q_out[...] = (_rope(q, cos, sin) * (HEAD_DIM ** -0.5)).astype(BF16)

    k = _dot(xn, wk_ref[...])
    k = k * lax.rsqrt(_group_mean_sq(k, HEAD_DIM) + NORM_EPS) * kn_ref[...]
    k_out[...] = _rope(k, cos, sin).astype(BF16)

    v_out[...] = _dot(xn, wv_ref[...]).astype(BF16)

    g = _dot(xn, wgla_ref[...])
    qg_out[...] = (g[:, :GLA_K_W] * (GLA_KEY_DIM ** -0.5)).astype(BF16)
    kg_out[...] = g[:, GLA_K_W:2 * GLA_K_W].astype(BF16)
    vg_out[...] = g[:, 2 * GLA_K_W:2 * GLA_K_W + GLA_V_W].astype(BF16)
    r = g[:, 2 * GLA_K_W + GLA_V_W:]
    rg_out[...] = (r * _sigmoid(r)).astype(BF16)

    gd = _dot(xn, wgd_ref[...]).astype(BF16)
    z = _dot(gd, gup_ref[...]) + gb_ref[...]
    gk_out[...] = _log_sigmoid(z) * (1.0 / GLA_GATE_NORM)

    gates = _dot(xn, wgate_ref[...])
    sga_out[...] = _sigmoid(gates[:, :D_MODEL]).astype(BF16)
    sgg_out[...] = _sigmoid(gates[:, D_MODEL:]).astype(BF16)


def _const_spec(shape):
    nd = len(shape)
    return pl.BlockSpec(shape, lambda *_: (0,) * nd)


def _proj_call(x2d, tm, tiles_per_seq, w, cos, sin):
    n = x2d.shape[0]
    grid = (n // tm,)
    row = lambda width: pl.BlockSpec((tm, width), lambda i: (i, 0))
    tab = pl.BlockSpec((tm, LANES), lambda i: (i % tiles_per_seq, 0))
    in_specs = [
        row(D_MODEL), _const_spec((1, D_MODEL)),
        _const_spec((D_MODEL, ATTN_Q_W)), _const_spec((D_MODEL, ATTN_KV_DUP_W)),
        _const_spec((D_MODEL, ATTN_KV_DUP_W)), _const_spec((D_MODEL, GLA_W)),
        _const_spec((D_MODEL, LANES)), _const_spec((D_MODEL, 2 * D_MODEL)),
        _const_spec((1, ATTN_Q_W)), _const_spec((1, ATTN_KV_DUP_W)),
        tab, tab, _const_spec((LANES, GLA_K_W)), _const_spec((1, GLA_K_W)),
    ]
    out_widths = [ATTN_Q_W, ATTN_KV_DUP_W, ATTN_KV_DUP_W, GLA_K_W, GLA_K_W, GLA_V_W, GLA_V_W,
                  GLA_K_W, D_MODEL, D_MODEL]
    out_dtypes = [BF16] * 7 + [F32] + [BF16] * 2
    return pl.pallas_call(
        _proj_body,
        grid=grid,
        in_specs=in_specs,
        out_specs=[row(wd) for wd in out_widths],
        out_shape=[jax.ShapeDtypeStruct((n, wd), dt) for wd, dt in zip(out_widths, out_dtypes)],
        compiler_params=pltpu.CompilerParams(dimension_semantics=("arbitrary",),
                                             vmem_limit_bytes=VMEM_LIMIT),
        name="proj",
    )(x2d, w["norm1"], w["wq"], w["wk"], w["wv"], w["wgla"], w["wgd"], w["wgate"],
      w["qn"], w["kn"], cos, sin, w["gup"], w["gb"])


def _attn_body(sink_ref, q_ref, kp_ref, kc_ref, km_ref, vp_ref, vc_ref, vm_ref, o_ref):
    i = pl.program_id(1)
    lane = lax.broadcasted_iota(jnp.int32, (BLOCK, LANES), 1)
    low = lane < HEAD_DIM
    key = lax.broadcasted_iota(jnp.int32, (BLOCK, 2 * LANES), 1) & (LANES - 1)
    rowi = lax.broadcasted_iota(jnp.int32, (BLOCK, 2 * LANES), 0)
    mask_meta = key < N_META
    mask_prev = jnp.logical_and(key > rowi, i > 0)
    mask_cur = key <= rowi
    masks = (mask_meta, mask_prev, mask_cur)
    zero = jnp.zeros((), BF16)

    for kvh in range(ATTN_KV_HEADS):
        sl = slice(kvh * LANES, (kvh + 1) * LANES)
        kblks = []
        vblks = []
        for k_ref, v_ref in ((km_ref, vm_ref), (kp_ref, vp_ref), (kc_ref, vc_ref)):
            kp = k_ref[:, sl]
            vp = v_ref[:, sl]
            kblks.append(jnp.concatenate([jnp.where(low, kp, zero), jnp.where(low, zero, kp)], axis=0))
            vblks.append(jnp.concatenate([jnp.where(low, vp, zero), jnp.where(low, zero, vp)], axis=0))
        for pair in range(2):
            j = kvh * 2 + pair
            q2 = q_ref[:, j * LANES:(j + 1) * LANES]
            s_parts = []
            for part in range(3):
                s = _dot_nt(q2, kblks[part])
                s_parts.append(jnp.where(masks[part], s, MASK_VALUE))
            p_heads = []
            inv_l = []
            for hh in range(2):
                sink = sink_ref[2 * j + hh]
                sh = [sp[:, hh * LANES:(hh + 1) * LANES] for sp in s_parts]
                m = jnp.maximum(jnp.maximum(jnp.max(sh[0], axis=1, keepdims=True),
                                            jnp.max(sh[1], axis=1, keepdims=True)),
                                jnp.max(sh[2], axis=1, keepdims=True))
                m = jnp.maximum(m, sink)
                ph = [jnp.exp(s - m) for s in sh]
                l = (jnp.sum(ph[0], axis=1, keepdims=True) + jnp.sum(ph[1], axis=1, keepdims=True)
                     + jnp.sum(ph[2], axis=1, keepdims=True) + jnp.exp(sink - m))
                p_heads.append(ph)
                inv_l.append(1.0 / l)
            o2 = jnp.zeros((BLOCK, LANES), F32)
            for part in range(3):
                p2 = jnp.concatenate([p_heads[0][part], p_heads[1][part]], axis=1).astype(BF16)
                o2 = o2 + _dot(p2, vblks[part])
            o2 = o2 * jnp.where(low, inv_l[0], inv_l[1])
            o_ref[:, j * LANES:(j + 1) * LANES] = o2.astype(BF16)


def _attn_call(q, kd, vd, km, vm, sinks, batch, seq):
    nb = seq // BLOCK
    qspec = pl.BlockSpec((None, BLOCK, ATTN_Q_W), lambda b, i, s: (b, i, 0))
    prev = pl.BlockSpec((None, BLOCK, ATTN_KV_DUP_W), lambda b, i, s: (b, jnp.maximum(i - 1, 0), 0))
    cur = pl.BlockSpec((None, BLOCK, ATTN_KV_DUP_W), lambda b, i, s: (b, i, 0))
    meta = pl.BlockSpec((BLOCK, ATTN_KV_DUP_W), lambda b, i, s: (0, 0))
    grid_spec = pltpu.PrefetchScalarGridSpec(
        num_scalar_prefetch=1,
        grid=(batch, nb),
        in_specs=[qspec, prev, cur, meta, prev, cur, meta],
        out_specs=pl.BlockSpec((None, BLOCK, ATTN_Q_W), lambda b, i, s: (b, i, 0)),
    )
    return pl.pallas_call(
        _attn_body,
        grid_spec=grid_spec,
        out_shape=jax.ShapeDtypeStruct((batch, seq, ATTN_Q_W), BF16),
        compiler_params=pltpu.CompilerParams(dimension_semantics=("arbitrary", "arbitrary"),
                                             vmem_limit_bytes=VMEM_LIMIT),
        name="attn",
    )(sinks, q, kd, kd, km, vd, vd, vm)


GLA_STEP = 2 * GLA_CHUNK
GLA_PAIR_K = 2 * GLA_KEY_DIM
GLA_PAIR_V = 2 * GLA_VAL_DIM


def _gla_consts():
    r = lax.broadcasted_iota(jnp.int32, (GLA_STEP, 2 * GLA_STEP), 0)
    c = lax.broadcasted_iota(jnp.int32, (GLA_STEP, 2 * GLA_STEP), 1) & (GLA_STEP - 1)
    causal = jnp.logical_and((r >> 6) == (c >> 6), c <= r)
    tri = jnp.where(causal[:, :GLA_STEP], 1.0, 0.0).astype(BF16)
    return causal, tri


def _gla_step(q2, k2, v2, gk2, s_prev, causal, tri):
    row = lax.broadcasted_iota(jnp.int32, (GLA_STEP, GLA_PAIR_K), 0)
    lane = lax.broadcasted_iota(jnp.int32, (GLA_STEP, GLA_PAIR_K), 1)
    first = row < GLA_CHUNK
    low = lane < GLA_KEY_DIM
    g_hi = gk2.astype(BF16)
    g_r = gk2 - g_hi.astype(F32)
    g_mid = g_r.astype(BF16)
    g_lo = (g_r - g_mid.astype(F32)).astype(BF16)
    b = _dot(tri, g_hi) + _dot(tri, g_mid) + _dot(tri, g_lo)
    half = GLA_CHUNK // 2
    b_mid = jnp.where(first, b[half:half + 1, :], b[GLA_CHUNK + half:GLA_CHUNK + half + 1, :])
    qi = (q2 * jnp.exp(b - b_mid)).astype(BF16)
    ki = (k2 * jnp.exp(b_mid - b)).astype(BF16)
    qx = q2 * jnp.exp(b)
    zero = jnp.zeros((), BF16)
    kblk = jnp.concatenate([jnp.where(low, ki, zero), jnp.where(low, zero, ki)], axis=0)
    att = _dot_nt(qi, kblk)
    att = jnp.where(causal, att, 0.0).astype(BF16)
    zv = jnp.zeros((GLA_STEP, GLA_VAL_DIM), BF16)
    vblk = jnp.concatenate([jnp.concatenate([v2[:, :GLA_VAL_DIM], zv], axis=1),
                            jnp.concatenate([zv, v2[:, GLA_VAL_DIM:]], axis=1)], axis=0)
    o = _dot(att, vblk)
    b_t = b.T
    k_t = k2.T
    lane_t = lax.broadcasted_iota(jnp.int32, (GLA_PAIR_K, GLA_STEP), 1)
    first_t = lane_t < GLA_CHUNK
    bl_a = b_t[:, GLA_CHUNK - 1:GLA_CHUNK]
    bl_b = b_t[:, GLA_STEP - 1:GLA_STEP]
    kx_t = k_t * jnp.exp(jnp.where(first_t, bl_a, bl_b) - b_t)
    zf = jnp.zeros((), F32)
    kx_a = jnp.where(first_t, kx_t, zf).astype(BF16)
    kx_b = jnp.where(first_t, zf, kx_t).astype(BF16)
    srow = lax.broadcasted_iota(jnp.int32, (GLA_PAIR_K, GLA_PAIR_V), 0)
    slane = lax.broadcasted_iota(jnp.int32, (GLA_PAIR_K, GLA_PAIR_V), 1)
    diag = (srow < GLA_KEY_DIM) == (slane < GLA_VAL_DIM)
    s_a = jnp.exp(bl_a) * s_prev + jnp.where(diag, _dot(kx_a, v2), 0.0)
    s_b = jnp.exp(bl_b) * s_a + jnp.where(diag, _dot(kx_b, v2), 0.0)
    o = o + _dot(jnp.where(first, qx, zf).astype(BF16), s_prev.astype(BF16))
    o = o + _dot(jnp.where(first, zf, qx).astype(BF16), s_a.astype(BF16))
    return o, s_b


def _gla_init_body(k_ref, v_ref, gk_ref, s_out):
    causal, tri = _gla_consts()
    for p in range(GLA_HEADS // 2):
        k2 = k_ref[:, p * GLA_PAIR_K:(p + 1) * GLA_PAIR_K].astype(F32)
        v2 = v_ref[:, p * GLA_PAIR_V:(p + 1) * GLA_PAIR_V]
        gk2 = gk_ref[:, p * GLA_PAIR_K:(p + 1) * GLA_PAIR_K]
        s0 = jnp.zeros((GLA_PAIR_K, GLA_PAIR_V), F32)
        _, s = _gla_step(jnp.zeros_like(k2), k2, v2, gk2, s0, causal, tri)
        s_out[p] = s


def _gla_init_call(kg_m, vg_m, gk_m):
    return pl.pallas_call(
        _gla_init_body,
        out_shape=jax.ShapeDtypeStruct((GLA_HEADS // 2, GLA_PAIR_K, GLA_PAIR_V), F32),
        name="gla_init",
    )(kg_m, vg_m, gk_m)


def _gla_body(s0_ref, q_ref, k_ref, v_ref, gk_ref, rg_ref, nw_ref, o_ref, s_ref):
    t = pl.program_id(1)

    @pl.when(t == 0)
    def _():
        s_ref[...] = s0_ref[...]

    causal, tri = _gla_consts()
    nw = nw_ref[...]
    for p in range(GLA_HEADS // 2):
        ks = slice(p * GLA_PAIR_K, (p + 1) * GLA_PAIR_K)
        vs = slice(p * GLA_PAIR_V, (p + 1) * GLA_PAIR_V)
        o, s_new = _gla_step(q_ref[:, ks].astype(F32), k_ref[:, ks].astype(F32), v_ref[:, vs],
                             gk_ref[:, ks], s_ref[p], causal, tri)
        s_ref[p] = s_new
        outs = []
        for hh in range(2):
            oh = o[:, hh * GLA_VAL_DIM:(hh + 1) * GLA_VAL_DIM]
            ms = jnp.mean(oh * oh, axis=-1, keepdims=True)
            outs.append(oh * lax.rsqrt(ms + NORM_EPS) * nw)
        on = jnp.concatenate(outs, axis=1)
        o_ref[:, vs] = (on * rg_ref[:, vs].astype(F32)).astype(BF16)


def _gla_call(s0, qg, kg, vg, gk, rg, nw, batch, seq):
    steps = seq // GLA_STEP
    spec = lambda width: pl.BlockSpec((None, GLA_STEP, width), lambda b, t: (b, t, 0))
    return pl.pallas_call(
        _gla_body,
        grid=(batch, steps),
        in_specs=[_const_spec((GLA_HEADS // 2, GLA_PAIR_K, GLA_PAIR_V)),
                  spec(GLA_K_W), spec(GLA_K_W), spec(GLA_V_W), spec(GLA_K_W), spec(GLA_V_W),
                  _const_spec((1, GLA_VAL_DIM))],
        out_specs=spec(GLA_V_W),
        out_shape=jax.ShapeDtypeStruct((batch, seq, GLA_V_W), BF16),
        scratch_shapes=[pltpu.VMEM((GLA_HEADS // 2, GLA_PAIR_K, GLA_PAIR_V), F32)],
        compiler_params=pltpu.CompilerParams(dimension_semantics=("arbitrary", "arbitrary"),
                                             vmem_limit_bytes=VMEM_LIMIT),
        name="gla",
    )(s0, qg, kg, vg, gk, rg, nw)


ROUTER_W = LANES


def _merge_body(x_ref, ao_ref, go_ref, sga_ref, sgg_ref, wab_ref, wgb_ref, wo_ref, n2_ref,
                wrh_ref, wrl_ref, h_out, xn_out, cw_out):
    ya = _dot(ao_ref[...], wab_ref[...])
    yg = _dot(go_ref[...], wgb_ref[...])
    merged = sga_ref[...].astype(F32) * ya + sgg_ref[...].astype(F32) * yg
    h = x_ref[...] + _dot(merged.astype(BF16), wo_ref[...])
    h_out[...] = h
    ms = jnp.mean(h * h, axis=-1, keepdims=True)
    xn = h * lax.rsqrt(ms + NORM_EPS) * n2_ref[...]
    xn_out[...] = xn.astype(BF16)

    xh, xl = _split_hi_lo(xn)
    wrh = wrh_ref[...]
    lg = _dot(xh, wrh) + _dot(xl, wrh) + _dot(xh, wrl_ref[...])
    lane = lax.broadcasted_iota(jnp.int32, lg.shape, 1)
    neg = -jnp.inf
    big = jnp.int32(1 << 20)
    gmask = jnp.logical_and(lane >= N_EXPERTS, lane < N_EXPERTS + N_GROUPS)
    gl = jnp.where(gmask, lg, neg)
    gmax = jnp.max(gl, axis=1, keepdims=True)
    g_lane = jnp.min(jnp.where(gl == gmax, lane, big), axis=1, keepdims=True)
    g_w = 1.0 / jnp.sum(jnp.exp(gl - gmax), axis=1, keepdims=True)
    g_idx = g_lane - N_EXPERTS
    e_lo = g_idx * EXPERTS_PER_GROUP
    emask = jnp.logical_and(lane >= e_lo, lane < e_lo + EXPERTS_PER_GROUP)
    el = jnp.where(emask, lg, neg)
    m1 = jnp.max(el, axis=1, keepdims=True)
    i1 = jnp.min(jnp.where(el == m1, lane, big), axis=1, keepdims=True)
    el2 = jnp.where(lane == i1, neg, el)
    m2 = jnp.max(el2, axis=1, keepdims=True)
    i2 = jnp.min(jnp.where(el2 == m2, lane, big), axis=1, keepdims=True)
    e2 = jnp.exp(m2 - m1)
    p1 = g_w / (1.0 + e2)
    p2 = g_w * e2 / (1.0 + e2)
    cw_out[...] = jnp.where(lane == i1, p1, 0.0) + jnp.where(lane == i2, p2, 0.0)


def _merge_call(x2d, ao, go, sga, sgg, w, tm):
    n = x2d.shape[0]
    row = lambda width: pl.BlockSpec((tm, width), lambda i: (i, 0))
    return pl.pallas_call(
        _merge_body,
        grid=(n // tm,),
        in_specs=[row(D_MODEL), row(ATTN_Q_W), row(GLA_V_W), row(D_MODEL), row(D_MODEL),
                  _const_spec((ATTN_Q_W, D_MODEL)), _const_spec((GLA_V_W, D_MODEL)),
                  _const_spec((D_MODEL, D_MODEL)), _const_spec((1, D_MODEL)),
                  _const_spec((D_MODEL, ROUTER_W)), _const_spec((D_MODEL, ROUTER_W))],
        out_specs=[row(D_MODEL), row(D_MODEL), row(ROUTER_W)],
        out_shape=[jax.ShapeDtypeStruct((n, D_MODEL), F32),
                   jax.ShapeDtypeStruct((n, D_MODEL), BF16),
                   jax.ShapeDtypeStruct((n, ROUTER_W), F32)],
        compiler_params=pltpu.CompilerParams(dimension_semantics=("arbitrary",),
                                             vmem_limit_bytes=VMEM_LIMIT),
        name="merge",
    )(x2d, ao, go, sga, sgg, w["wab"], w["wgb"], w["wo"], w["norm2"], w["wrh"], w["wrl"])


def _moe_body(xn_ref, cw_ref, h_ref, wg_ref, wu_ref, wd_ref, o_ref):
    e = pl.program_id(1)

    @pl.when(e == 0)
    def _():
        o_ref[...] = h_ref[...]

    xn = xn_ref[...]
    cw = cw_ref[...]
    lane = lax.broadcasted_iota(jnp.int32, cw.shape, 1)
    c = jnp.sum(jnp.where(lane == e, cw, 0.0), axis=1, keepdims=True)
    g = _dot(xn, wg_ref[...])
    u = _dot(xn, wu_ref[...])
    hg = (g * _sigmoid(g) * u * c).astype(BF16)
    o_ref[...] += _dot(hg, wd_ref[...])


def _moe_call(xn, cw, h, wg, wu, wd, tm):
    n = xn.shape[0]
    row = lambda width: pl.BlockSpec((tm, width), lambda i, e: (i, 0))
    return pl.pallas_call(
        _moe_body,
        grid=(n // tm, N_EXPERTS),
        in_specs=[row(D_MODEL), row(ROUTER_W), row(D_MODEL),
                  pl.BlockSpec((None, D_MODEL, EXPERT_FF), lambda i, e: (e, 0, 0)),
                  pl.BlockSpec((None, D_MODEL, EXPERT_FF), lambda i, e: (e, 0, 0)),
                  pl.BlockSpec((None, EXPERT_FF, D_MODEL), lambda i, e: (e, 0, 0))],
        out_specs=row(D_MODEL),
        out_shape=jax.ShapeDtypeStruct((n, D_MODEL), F32),
        compiler_params=pltpu.CompilerParams(dimension_semantics=("arbitrary", "arbitrary"),
                                             vmem_limit_bytes=VMEM_LIMIT),
        name="moe",
    )(xn, cw, h, wg, wu, wd)


def _rope_tables(pos):
    half = HEAD_DIM // 2
    inv_freq = ROPE_THETA ** (-(jnp.arange(half, dtype=F32) * 2.0) / HEAD_DIM)
    ang = pos.astype(F32)[:, None] * inv_freq[None, :]
    cos = jnp.cos(ang)
    sin = jnp.sin(ang)
    cos64 = jnp.concatenate([cos, cos], axis=1)
    sin64 = jnp.concatenate([-sin, sin], axis=1)
    return jnp.tile(cos64, (1, LANES // HEAD_DIM)), jnp.tile(sin64, (1, LANES // HEAD_DIM))


def _dup_heads(wcols):
    d = wcols.shape[0]
    w3 = wcols.reshape(d, ATTN_KV_HEADS, 1, HEAD_DIM)
    return jnp.broadcast_to(w3, (d, ATTN_KV_HEADS, 2, HEAD_DIM)).reshape(d, ATTN_KV_DUP_W)


def _prep_weights(norm1_w, w_in, q_norm_w, k_norm_w, gla_gate_up, gla_gate_bias, w_attn_branch,
                  w_gla_branch, w_out, norm2_w, router_group, router_expert):
    wb = w_in.astype(BF16)
    wgd = jnp.zeros((D_MODEL, LANES), BF16).at[:, :GLA_GATE_RANK].set(wb[:, _OFF_GD:_OFF_GD + GLA_GATE_RANK])
    gup = jnp.zeros((LANES, GLA_K_W), BF16).at[:GLA_GATE_RANK].set(gla_gate_up.astype(BF16))
    wr = jnp.zeros((D_MODEL, ROUTER_W), F32)
    wr = wr.at[:, :N_EXPERTS].set(router_expert.transpose(1, 0, 2).reshape(D_MODEL, N_EXPERTS))
    wr = wr.at[:, N_EXPERTS:N_EXPERTS + N_GROUPS].set(router_group)
    wrh = wr.astype(BF16)
    wrl = (wr - wrh.astype(F32)).astype(BF16)
    return {
        "norm1": norm1_w.reshape(1, D_MODEL),
        "wq": wb[:, _OFF_Q:_OFF_Q + ATTN_Q_W],
        "wk": _dup_heads(wb[:, _OFF_K:_OFF_K + ATTN_KV_W]),
        "wv": _dup_heads(wb[:, _OFF_V:_OFF_V + ATTN_KV_W]),
        "wgla": wb[:, _OFF_GQ:_OFF_GQ + GLA_W],
        "wgd": wgd,
        "wgate": wb[:, _OFF_GATE:_OFF_GATE + 2 * D_MODEL],
        "qn": jnp.tile(q_norm_w, ATTN_HEADS).reshape(1, ATTN_Q_W),
        "kn": jnp.tile(k_norm_w, ATTN_KV_DUP_W // HEAD_DIM).reshape(1, ATTN_KV_DUP_W),
        "gup": gup,
        "gb": gla_gate_bias.reshape(1, GLA_K_W),
        "wab": w_attn_branch.astype(BF16),
        "wgb": w_gla_branch.astype(BF16),
        "wo": w_out.astype(BF16),
        "norm2": norm2_w.reshape(1, D_MODEL),
        "wrh": wrh,
        "wrl": wrl,
    }


def _pick_tile(n, want):
    t = want
    while n % t:
        t //= 2
    return t


def kernel(x, meta_tokens, norm1_w, w_in, q_norm_w, k_norm_w, attn_sinks, gla_gate_up, gla_gate_bias,
           gla_norm_w, w_attn_branch, w_gla_branch, w_out, norm2_w, router_group, router_expert,
           expert_w_gate, expert_w_up, expert_w_down):
    batch, seq, d = x.shape
    assert d == D_MODEL and seq % GLA_STEP == 0 and norm1_w.shape[0] == 1
    n = batch * seq
    w = _prep_weights(norm1_w[0], w_in[0], q_norm_w[0], k_norm_w[0], gla_gate_up[0], gla_gate_bias[0],
                      w_attn_branch[0], w_gla_branch[0], w_out[0], norm2_w[0], router_group[0],
                      router_expert[0])
    x2d = x.reshape(n, D_MODEL)

    cos_m, sin_m = _rope_tables(jnp.arange(N_META))
    cos_r, sin_r = _rope_tables(jnp.arange(seq) + N_META)
    meta = _proj_call(meta_tokens.astype(F32), N_META, 1, w, cos_m, sin_m)
    tm = _pick_tile(seq, 256)
    (q, kd, vd, qg, kg, vg, rg, gk, sga, sgg) = _proj_call(x2d, tm, seq // tm, w, cos_r, sin_r)

    pad_after = ((0, BLOCK - N_META), (0, 0))
    km = jnp.pad(meta[1], pad_after)
    vm = jnp.pad(meta[2], pad_after)
    ao = _attn_call(q.reshape(batch, seq, ATTN_Q_W), kd.reshape(batch, seq, ATTN_KV_DUP_W),
                    vd.reshape(batch, seq, ATTN_KV_DUP_W), km, vm, attn_sinks[0].astype(F32), batch, seq)

    pad_before = ((GLA_STEP - N_META, 0), (0, 0))
    s0 = _gla_init_call(jnp.pad(meta[4], pad_before), jnp.pad(meta[5], pad_before),
                        jnp.pad(meta[7], pad_before))
    go = _gla_call(s0, qg.reshape(batch, seq, GLA_K_W), kg.reshape(batch, seq, GLA_K_W),
                   vg.reshape(batch, seq, GLA_V_W), gk.reshape(batch, seq, GLA_K_W),
                   rg.reshape(batch, seq, GLA_V_W), gla_norm_w[0].reshape(1, GLA_VAL_DIM), batch, seq)

    h, xn2, cw = _merge_call(x2d, ao.reshape(n, ATTN_Q_W), go.reshape(n, GLA_V_W), sga, sgg, w,
                             _pick_tile(n, 512))

    wg = expert_w_gate[0].reshape(N_EXPERTS, D_MODEL, EXPERT_FF).astype(BF16)
    wu = expert_w_up[0].reshape(N_EXPERTS, D_MODEL, EXPERT_FF).astype(BF16)
    wd = expert_w_down[0].reshape(N_EXPERTS, EXPERT_FF, D_MODEL).astype(BF16)
    out = _moe_call(xn2, cw, h, wg, wu, wd, _pick_tile(n, 1024))
    return out.reshape(batch, seq, D_MODEL)
```

```python
import functools
import math

import jax
import jax.numpy as jnp
from jax import lax
from jax.experimental import pallas as pl
from jax.experimental.pallas import tpu as pltpu

F32 = jnp.float32
BF16 = jnp.bfloat16

D_MODEL = 1024
N_META = 16
BLOCK = 128
ATTN_HEADS = 8
ATTN_KV_HEADS = 2
HEAD_DIM = 64
ROPE_THETA = 10000.0
ATTN_Q_W = ATTN_HEADS * HEAD_DIM
ATTN_KV_W = ATTN_KV_HEADS * HEAD_DIM
GLA_HEADS = 4
GLA_KEY_DIM = 64
GLA_VAL_DIM = 128
GLA_K_W = GLA_HEADS * GLA_KEY_DIM
GLA_V_W = GLA_HEADS * GLA_VAL_DIM
GLA_GATE_RANK = 16
GLA_GATE_NORM = 16.0
GLA_CHUNK = 64
N_GROUPS = 4
EXPERTS_PER_GROUP = 8
N_EXPERTS = N_GROUPS * EXPERTS_PER_GROUP
EXPERT_FF = 256
NORM_EPS = 1e-6
MASK_VALUE = -1e30

LANES = 128
ATTN_KV_DUP_W = 2 * ATTN_KV_W
VMEM_LIMIT = 56 * 1024 * 1024

_OFF_Q = 0
_OFF_K = _OFF_Q + ATTN_Q_W
_OFF_V = _OFF_K + ATTN_KV_W
_OFF_GQ = _OFF_V + ATTN_KV_W
_OFF_GD = _OFF_GQ + 2 * GLA_K_W + 2 * GLA_V_W
_OFF_GATE = _OFF_GD + GLA_GATE_RANK
GLA_W = 2 * GLA_K_W + 2 * GLA_V_W


def _dot(a, b):
    return jnp.dot(a, b, preferred_element_type=F32)


def _dot_nt(a, b):
    return lax.dot_general(a, b, (((1,), (1,)), ((), ())), preferred_element_type=F32)


def _split_hi_lo(x):
    hi = x.astype(BF16)
    lo = (x - hi.astype(F32)).astype(BF16)
    return hi, lo


def _group_mean_sq(x, group):
    w = x.shape[-1]
    shift = int(math.log2(group))
    r = lax.broadcasted_iota(jnp.int32, (w, w), 0) >> shift
    c = lax.broadcasted_iota(jnp.int32, (w, w), 1) >> shift
    ones = jnp.where(r == c, 1.0, 0.0).astype(BF16)
    hi, lo = _split_hi_lo(x * x)
    return (_dot(hi, ones) + _dot(lo, ones)) * (1.0 / group)


def _rope(x, cos, sin_signed):
    w = x.shape[-1]
    reps = w // LANES
    if reps > 1:
        cos = jnp.concatenate([cos] * reps, axis=1)
        sin_signed = jnp.concatenate([sin_signed] * reps, axis=1)
    lane = lax.broadcasted_iota(jnp.int32, x.shape, 1)
    first_half = (lane & (HEAD_DIM // 2)) == 0
    swapped = jnp.where(first_half, pltpu.roll(x, w - HEAD_DIM // 2, 1), pltpu.roll(x, HEAD_DIM // 2, 1))
    return x * cos + swapped * sin_signed


def _pack_bf16_pair(lo, hi):
    lo_bits = lax.bitcast_convert_type(lo.astype(BF16).astype(F32), jnp.int32)
    hi_bits = lax.bitcast_convert_type(hi.astype(BF16).astype(F32), jnp.int32)
    return (hi_bits & jnp.int32(-65536)) | lax.shift_right_logical(lo_bits, jnp.int32(16))


def _unpack_bf16_pair(words):
    lo = lax.bitcast_convert_type(words << 16, F32)
    hi = lax.bitcast_convert_type(words & jnp.int32(-65536), F32)
    return lo, hi


def _sigmoid(x):
    return 1.0 / (1.0 + jnp.exp(-x))


def _log_sigmoid(x):
    return jnp.minimum(x, 0.0) - jnp.log(1.0 + jnp.exp(-jnp.abs(x)))


def _proj_body(x_ref, n1_ref, wq_ref, wk_ref, wv_ref, wgla_ref, wgd_ref, wgate_ref,
               qn_ref, kn_ref, cos_ref, sin_ref, gup_ref, gb_ref,
               q_out, k_out, v_out, qg_out, kg_out, vg_out, rg_out, gk_out, sga_out, sgg_out):
    x = x_ref[...]
    ms = jnp.mean(x * x, axis=-1, keepdims=True)
    xn = (x * lax.rsqrt(ms + NORM_EPS) * n1_ref[...]).astype(BF16)
    cos = cos_ref[...]
    sin = sin_ref[...]

    q = _dot(xn, wq_ref[...])
    q = q * lax.rsqrt(_group_mean_sq(q, HEAD_DIM) + NORM_EPS) * qn_ref[...]
    q_out[...] = (_rope(q, cos, sin) * (HEAD_DIM ** -0.5)).astype(BF16)

    k = _dot(xn, wk_ref[...])
    k = k * lax.rsqrt(_group_mean_sq(k, HEAD_DIM) + NORM_EPS) * kn_ref[...]
    k_out[...] = _rope(k, cos, sin).astype(BF16)

    v_out[...] = _dot(xn, wv_ref[...]).astype(BF16)

    g = _dot(xn, wgla_ref[...])
    qg_out[...] = (g[:, :GLA_K_W] * (GLA_KEY_DIM ** -0.5)).astype(BF16)
    kg_out[...] = g[:, GLA_K_W:2 * GLA_K_W].astype(BF16)
    vg_out[...] = g[:, 2 * GLA_K_W:2 * GLA_K_W + GLA_V_W].astype(BF16)
    r = g[:, 2 * GLA_K_W + GLA_V_W:]
    rg_out[...] = (r * _sigmoid(r)).astype(BF16)

    gd = _dot(xn, wgd_ref[...]).astype(BF16)
    z = _dot(gd, gup_ref[...]) + gb_ref[...]
    gk_out[...] = _log_sigmoid(z) * (1.0 / GLA_GATE_NORM)

    gates = _dot(xn, wgate_ref[...])
    sga_out[...] = _sigmoid(gates[:, :D_MODEL]).astype(BF16)
    sgg_out[...] = _sigmoid(gates[:, D_MODEL:]).astype(BF16)


def _const_spec(shape):
    nd = len(shape)
    return pl.BlockSpec(shape, lambda *_: (0,) * nd)


def _proj_call(x2d, tm, tiles_per_seq, w, cos, sin):
    n = x2d.shape[0]
    grid = (n // tm,)
    row = lambda width: pl.BlockSpec((tm, width), lambda i: (i, 0))
    tab = pl.BlockSpec((tm, LANES), lambda i: (i % tiles_per_seq, 0))
    in_specs = [
        row(D_MODEL), _const_spec((1, D_MODEL)),
        _const_spec((D_MODEL, ATTN_Q_W)), _const_spec((D_MODEL, ATTN_KV_DUP_W)),
        _const_spec((D_MODEL, ATTN_KV_DUP_W)), _const_spec((D_MODEL, GLA_W)),
        _const_spec((D_MODEL, LANES)), _const_spec((D_MODEL, 2 * D_MODEL)),
        _const_spec((1, ATTN_Q_W)), _const_spec((1, ATTN_KV_DUP_W)),
        tab, tab, _const_spec((LANES, GLA_K_W)), _const_spec((1, GLA_K_W)),
    ]
    out_widths = [ATTN_Q_W, ATTN_KV_DUP_W, ATTN_KV_DUP_W, GLA_K_W, GLA_K_W, GLA_V_W, GLA_V_W,
                  GLA_K_W, D_MODEL, D_MODEL]
    out_dtypes = [BF16] * 7 + [F32] + [BF16] * 2
    return pl.pallas_call(
        _proj_body,
        grid=grid,
        in_specs=in_specs,
        out_specs=[row(wd) for wd in out_widths],
        out_shape=[jax.ShapeDtypeStruct((n, wd), dt) for wd, dt in zip(out_widths, out_dtypes)],
        compiler_params=pltpu.CompilerParams(dimension_semantics=("arbitrary",),
                                             vmem_limit_bytes=VMEM_LIMIT),
        name="proj",
    )(x2d, w["norm1"], w["wq"], w["wk"], w["wv"], w["wgla"], w["wgd"], w["wgate"],
      w["qn"], w["kn"], cos, sin, w["gup"], w["gb"])


def _attn_body(sink_ref, q_ref, kp_ref, kc_ref, km_ref, vp_ref, vc_ref, vm_ref, o_ref):
    i = pl.program_id(1)
    lane = lax.broadcasted_iota(jnp.int32, (BLOCK, LANES), 1)
    low = lane < HEAD_DIM
    key = lax.broadcasted_iota(jnp.int32, (BLOCK, 2 * LANES), 1) & (LANES - 1)
    rowi = lax.broadcasted_iota(jnp.int32, (BLOCK, 2 * LANES), 0)
    mask_meta = key < N_META
    mask_prev = jnp.logical_and(key > rowi, i > 0)
    mask_cur = key <= rowi
    masks = (mask_meta, mask_prev, mask_cur)
    zero = jnp.zeros((), BF16)

    for kvh in range(ATTN_KV_HEADS):
        sl = slice(kvh * LANES, (kvh + 1) * LANES)
        kblks = []
        vblks = []
        for k_ref, v_ref in ((km_ref, vm_ref), (kp_ref, vp_ref), (kc_ref, vc_ref)):
            kp = k_ref[:, sl]
            vp = v_ref[:, sl]
            kblks.append(jnp.concatenate([jnp.where(low, kp, zero), jnp.where(low, zero, kp)], axis=0))
            vblks.append(jnp.concatenate([jnp.where(low, vp, zero), jnp.where(low, zero, vp)], axis=0))
        for pair in range(2):
            j = kvh * 2 + pair
            q2 = q_ref[:, j * LANES:(j + 1) * LANES]
            s_parts = []
            for part in range(3):
                s = _dot_nt(q2, kblks[part])
                s_parts.append(jnp.where(masks[part], s, MASK_VALUE))
            p_heads = []
            inv_l = []
            for hh in range(2):
                sink = sink_ref[2 * j + hh]
                sh = [sp[:, hh * LANES:(hh + 1) * LANES] for sp in s_parts]
                m = jnp.maximum(jnp.maximum(jnp.max(sh[0], axis=1, keepdims=True),
                                            jnp.max(sh[1], axis=1, keepdims=True)),
                                jnp.max(sh[2], axis=1, keepdims=True))
                m = jnp.maximum(m, sink)
                ph = [jnp.exp(s - m) for s in sh]
                l = (jnp.sum(ph[0], axis=1, keepdims=True) + jnp.sum(ph[1], axis=1, keepdims=True)
                     + jnp.sum(ph[2], axis=1, keepdims=True) + jnp.exp(sink - m))
                p_heads.append(ph)
                inv_l.append(1.0 / l)
            o2 = jnp.zeros((BLOCK, LANES), F32)
            for part in range(3):
                p2 = jnp.concatenate([p_heads[0][part], p_heads[1][part]], axis=1).astype(BF16)
                o2 = o2 + _dot(p2, vblks[part])
            o2 = o2 * jnp.where(low, inv_l[0], inv_l[1])
            o_ref[:, j * LANES:(j + 1) * LANES] = o2.astype(BF16)


def _attn_call(q, kd, vd, km, vm, sinks, batch, seq):
    nb = seq // BLOCK
    qspec = pl.BlockSpec((None, BLOCK, ATTN_Q_W), lambda b, i, s: (b, i, 0))
    prev = pl.BlockSpec((None, BLOCK, ATTN_KV_DUP_W), lambda b, i, s: (b, jnp.maximum(i - 1, 0), 0))
    cur = pl.BlockSpec((None, BLOCK, ATTN_KV_DUP_W), lambda b, i, s: (b, i, 0))
    meta = pl.BlockSpec((BLOCK, ATTN_KV_DUP_W), lambda b, i, s: (0, 0))
    grid_spec = pltpu.PrefetchScalarGridSpec(
        num_scalar_prefetch=1,
        grid=(batch, nb),
        in_specs=[qspec, prev, cur, meta, prev, cur, meta],
        out_specs=pl.BlockSpec((None, BLOCK, ATTN_Q_W), lambda b, i, s: (b, i, 0)),
    )
    return pl.pallas_call(
        _attn_body,
        grid_spec=grid_spec,
        out_shape=jax.ShapeDtypeStruct((batch, seq, ATTN_Q_W), BF16),
        compiler_params=pltpu.CompilerParams(dimension_semantics=("arbitrary", "arbitrary"),
                                             vmem_limit_bytes=VMEM_LIMIT),
        name="attn",
    )(sinks, q, kd, kd, km, vd, vd, vm)


GLA_STEP = 2 * GLA_CHUNK
GLA_PAIR_K = 2 * GLA_KEY_DIM
GLA_PAIR_V = 2 * GLA_VAL_DIM


def _gla_consts():
    r = lax.broadcasted_iota(jnp.int32, (GLA_STEP, 2 * GLA_STEP), 0)
    c = lax.broadcasted_iota(jnp.int32, (GLA_STEP, 2 * GLA_STEP), 1) & (GLA_STEP - 1)
    causal = jnp.logical_and((r >> 6) == (c >> 6), c <= r)
    tri = jnp.where(causal[:, :GLA_STEP], 1.0, 0.0).astype(BF16)
    return causal, tri


def _gla_step(q2, k2, v2, gk2, s_prev, causal, tri):
    row = lax.broadcasted_iota(jnp.int32, (GLA_STEP, GLA_PAIR_K), 0)
    lane = lax.broadcasted_iota(jnp.int32, (GLA_STEP, GLA_PAIR_K), 1)
    first = row < GLA_CHUNK
    low = lane < GLA_KEY_DIM
    g_hi = gk2.astype(BF16)
    g_r = gk2 - g_hi.astype(F32)
    g_mid = g_r.astype(BF16)
    g_lo = (g_r - g_mid.astype(F32)).astype(BF16)
    b = _dot(tri, g_hi) + _dot(tri, g_mid) + _dot(tri, g_lo)
    half = GLA_CHUNK // 2
    b_mid = jnp.where(first, b[half:half + 1, :], b[GLA_CHUNK + half:GLA_CHUNK + half + 1, :])
    qi = (q2 * jnp.exp(b - b_mid)).astype(BF16)
    ki = (k2 * jnp.exp(b_mid - b)).astype(BF16)
    qx = q2 * jnp.exp(b)
    zero = jnp.zeros((), BF16)
    kblk = jnp.concatenate([jnp.where(low, ki, zero), jnp.where(low, zero, ki)], axis=0)
    att = _dot_nt(qi, kblk)
    att = jnp.where(causal, att, 0.0).astype(BF16)
    zv = jnp.zeros((GLA_STEP, GLA_VAL_DIM), BF16)
    vblk = jnp.concatenate([jnp.concatenate([v2[:, :GLA_VAL_DIM], zv], axis=1),
                            jnp.concatenate([zv, v2[:, GLA_VAL_DIM:]], axis=1)], axis=0)
    o = _dot(att, vblk)
    b_t = b.T
    k_t = k2.T
    lane_t = lax.broadcasted_iota(jnp.int32, (GLA_PAIR_K, GLA_STEP), 1)
    first_t = lane_t < GLA_CHUNK
    bl_a = b_t[:, GLA_CHUNK - 1:GLA_CHUNK]
    bl_b = b_t[:, GLA_STEP - 1:GLA_STEP]
    kx_t = k_t * jnp.exp(jnp.where(first_t, bl_a, bl_b) - b_t)
    zf = jnp.zeros((), F32)
    kx_a = jnp.where(first_t, kx_t, zf).astype(BF16)
    kx_b = jnp.where(first_t, zf, kx_t).astype(BF16)
    srow = lax.broadcasted_iota(jnp.int32, (GLA_PAIR_K, GLA_PAIR_V), 0)
    slane = lax.broadcasted_iota(jnp.int32, (GLA_PAIR_K, GLA_PAIR_V), 1)
    diag = (srow < GLA_KEY_DIM) == (slane < GLA_VAL_DIM)
    s_a = jnp.exp(bl_a) * s_prev + jnp.where(diag, _dot(kx_a, v2), 0.0)
    s_b = jnp.exp(bl_b) * s_a + jnp.where(diag, _dot(kx_b, v2), 0.0)
    o = o + _dot(jnp.where(first, qx, zf).astype(BF16), s_prev.astype(BF16))
    o = o + _dot(jnp.where(first, zf, qx).astype(BF16), s_a.astype(BF16))
    return o, s_b


def _gla_init_body(k_ref, v_ref, gk_ref, s_out):
    causal, tri = _gla_consts()
    for p in range(GLA_HEADS // 2):
        k2 = k_ref[:, p * GLA_PAIR_K:(p + 1) * GLA_PAIR_K].astype(F32)
        v2 = v_ref[:, p * GLA_PAIR_V:(p + 1) * GLA_PAIR_V]
        gk2 = gk_ref[:, p * GLA_PAIR_K:(p + 1) * GLA_PAIR_K]
        s0 = jnp.zeros((GLA_PAIR_K, GLA_PAIR_V), F32)
        _, s = _gla_step(jnp.zeros_like(k2), k2, v2, gk2, s0, causal, tri)
        s_out[p] = s


def _gla_init_call(kg_m, vg_m, gk_m):
    return pl.pallas_call(
        _gla_init_body,
        out_shape=jax.ShapeDtypeStruct((GLA_HEADS // 2, GLA_PAIR_K, GLA_PAIR_V), F32),
        name="gla_init",
    )(kg_m, vg_m, gk_m)


def _gla_body(s0_ref, q_ref, k_ref, v_ref, gk_ref, rg_ref, nw_ref, o_ref, s_ref):
    t = pl.program_id(1)

    @pl.when(t == 0)
    def _():
        s_ref[...] = s0_ref[...]

    causal, tri = _gla_consts()
    nw = nw_ref[...]
    for p in range(GLA_HEADS // 2):
        ks = slice(p * GLA_PAIR_K, (p + 1) * GLA_PAIR_K)
        vs = slice(p * GLA_PAIR_V, (p + 1) * GLA_PAIR_V)
        o, s_new = _gla_step(q_ref[:, ks].astype(F32), k_ref[:, ks].astype(F32), v_ref[:, vs],
                             gk_ref[:, ks], s_ref[p], causal, tri)
        s_ref[p] = s_new
        outs = []
        for hh in range(2):
            oh = o[:, hh * GLA_VAL_DIM:(hh + 1) * GLA_VAL_DIM]
            ms = jnp.mean(oh * oh, axis=-1, keepdims=True)
            outs.append(oh * lax.rsqrt(ms + NORM_EPS) * nw)
        on = jnp.concatenate(outs, axis=1)
        o_ref[:, vs] = (on * rg_ref[:, vs].astype(F32)).astype(BF16)


def _gla_call(s0, qg, kg, vg, gk, rg, nw, batch, seq):
    steps = seq // GLA_STEP
    spec = lambda width: pl.BlockSpec((None, GLA_STEP, width), lambda b, t: (b, t, 0))
    return pl.pallas_call(
        _gla_body,
        grid=(batch, steps),
        in_specs=[_const_spec((GLA_HEADS // 2, GLA_PAIR_K, GLA_PAIR_V)),
                  spec(GLA_K_W), spec(GLA_K_W), spec(GLA_V_W), spec(GLA_K_W), spec(GLA_V_W),
                  _const_spec((1, GLA_VAL_DIM))],
        out_specs=spec(GLA_V_W),
        out_shape=jax.ShapeDtypeStruct((batch, seq, GLA_V_W), BF16),
        scratch_shapes=[pltpu.VMEM((GLA_HEADS // 2, GLA_PAIR_K, GLA_PAIR_V), F32)],
        compiler_params=pltpu.CompilerParams(dimension_semantics=("arbitrary", "arbitrary"),
                                             vmem_limit_bytes=VMEM_LIMIT),
        name="gla",
    )(s0, qg, kg, vg, gk, rg, nw)


ROUTER_W = LANES


RT_E0, RT_E1, RT_W0, RT_W1 = 0, 1, 2, 3
PACKED_W = D_MODEL // 2


def _merge_body(x_ref, ao_ref, go_ref, sga_ref, sgg_ref, wab_ref, wgb_ref, wo_ref, n2_ref,
                wrh_ref, wrl_ref, h_out, xn_out, rt_out, cnt_out):
    ya = _dot(ao_ref[...], wab_ref[...])
    yg = _dot(go_ref[...], wgb_ref[...])
    merged = sga_ref[...].astype(F32) * ya + sgg_ref[...].astype(F32) * yg
    h = x_ref[...] + _dot(merged.astype(BF16), wo_ref[...])
    h_out[...] = h
    ms = jnp.mean(h * h, axis=-1, keepdims=True)
    xn = h * lax.rsqrt(ms + NORM_EPS) * n2_ref[...]
    xn_out[...] = _pack_bf16_pair(xn[:, :PACKED_W], xn[:, PACKED_W:])

    xh, xl = _split_hi_lo(xn)
    wrh = wrh_ref[...]
    lg = _dot(xh, wrh) + _dot(xl, wrh) + _dot(xh, wrl_ref[...])
    lane = lax.broadcasted_iota(jnp.int32, lg.shape, 1)
    neg = -jnp.inf
    big = jnp.int32(1 << 20)
    gmask = jnp.logical_and(lane >= N_EXPERTS, lane < N_EXPERTS + N_GROUPS)
    gl = jnp.where(gmask, lg, neg)
    gmax = jnp.max(gl, axis=1, keepdims=True)
    g_lane = jnp.min(jnp.where(gl == gmax, lane, big), axis=1, keepdims=True)
    g_w = 1.0 / jnp.sum(jnp.exp(gl - gmax), axis=1, keepdims=True)
    g_idx = g_lane - N_EXPERTS
    e_lo = g_idx * EXPERTS_PER_GROUP
    emask = jnp.logical_and(lane >= e_lo, lane < e_lo + EXPERTS_PER_GROUP)
    el = jnp.where(emask, lg, neg)
    m1 = jnp.max(el, axis=1, keepdims=True)
    i1 = jnp.min(jnp.where(el == m1, lane, big), axis=1, keepdims=True)
    el2 = jnp.where(lane == i1, neg, el)
    m2 = jnp.max(el2, axis=1, keepdims=True)
    i2 = jnp.min(jnp.where(el2 == m2, lane, big), axis=1, keepdims=True)
    e2 = jnp.exp(m2 - m1)
    p1 = g_w / (1.0 + e2)
    p2 = g_w * e2 / (1.0 + e2)
    rt_out[...] = jnp.where(lane == RT_E0, i1.astype(F32),
                            jnp.where(lane == RT_E1, i2.astype(F32),
                                      jnp.where(lane == RT_W0, p1, jnp.where(lane == RT_W1, p2, 0.0))))
    picked = jnp.where(lane == i1, 1.0, 0.0) + jnp.where(lane == i2, 1.0, 0.0)

    @pl.when(pl.program_id(0) == 0)
    def _():
        cnt_out[...] = jnp.zeros_like(cnt_out)

    cnt_out[0:1, :] += jnp.sum(picked, axis=0, keepdims=True)


def _merge_call(x2d, ao, go, sga, sgg, w, tm):
    n = x2d.shape[0]
    row = lambda width: pl.BlockSpec((tm, width), lambda i: (i, 0))
    return pl.pallas_call(
        _merge_body,
        grid=(n // tm,),
        in_specs=[row(D_MODEL), row(ATTN_Q_W), row(GLA_V_W), row(D_MODEL), row(D_MODEL),
                  _const_spec((ATTN_Q_W, D_MODEL)), _const_spec((GLA_V_W, D_MODEL)),
                  _const_spec((D_MODEL, D_MODEL)), _const_spec((1, D_MODEL)),
                  _const_spec((D_MODEL, ROUTER_W)), _const_spec((D_MODEL, ROUTER_W))],
        out_specs=[row(D_MODEL), row(PACKED_W), row(ROUTER_W), _const_spec((8, LANES))],
        out_shape=[jax.ShapeDtypeStruct((n, D_MODEL), F32),
                   jax.ShapeDtypeStruct((n, PACKED_W), jnp.int32),
                   jax.ShapeDtypeStruct((n, ROUTER_W), F32),
                   jax.ShapeDtypeStruct((8, LANES), F32)],
        compiler_params=pltpu.CompilerParams(dimension_semantics=("arbitrary",),
                                             vmem_limit_bytes=VMEM_LIMIT),
        name="merge",
    )(x2d, ao, go, sga, sgg, w["wab"], w["wgb"], w["wo"], w["norm2"], w["wrh"], w["wrl"])


EXPERT_TILE = 256
ROUTE_TILE = 512
VISIT_CAP = 256
VIS_EXPERT, VIS_TILE, VIS_LO, VIS_HI, VIS_COUNT = 0, 1, 2, 3, 4


def _num_visits(n_tokens):
    return (2 * n_tokens) // EXPERT_TILE + N_EXPERTS - 1


def _lane_cumsum_inclusive(x):
    lane = lax.broadcasted_iota(jnp.int32, x.shape, 1)
    s = 1
    while s < N_EXPERTS:
        x = x + jnp.where(lane >= s, pltpu.roll(x, s, 1), 0.0)
        s *= 2
    return x


def _route_body(rt_ref, cnt_ref, pos_ref, vis_ref, carry_ref):
    i = pl.program_id(0)
    tm = rt_ref.shape[0]
    lane1 = lax.broadcasted_iota(jnp.int32, (1, LANES), 1)
    cnt = jnp.where(lane1 < N_EXPERTS, cnt_ref[0:1, :], 0.0)
    end = _lane_cumsum_inclusive(cnt)
    base = end - cnt

    @pl.when(i == 0)
    def _():
        carry_ref[...] = jnp.zeros_like(carry_ref)
        inv = 1.0 / EXPERT_TILE
        first = jnp.floor(base * inv)
        n_vis = jnp.where(cnt > 0.0, jnp.floor((end - 1.0) * inv) - first + 1.0, 0.0)
        vend = _lane_cumsum_inclusive(n_vis)
        vstart = vend - n_vis
        lane = lax.broadcasted_iota(jnp.int32, (VISIT_CAP, LANES), 1)
        v = lax.broadcasted_iota(jnp.int32, (VISIT_CAP, LANES), 0).astype(F32)
        done = jnp.logical_and(lane < N_EXPERTS, vend <= v)
        ev = jnp.minimum(jnp.sum(jnp.where(done, 1.0, 0.0), axis=1, keepdims=True), N_EXPERTS - 1.0)
        mine = lane.astype(F32) == ev
        pick = lambda a: jnp.sum(jnp.where(mine, a, 0.0), axis=1, keepdims=True)
        tile = pick(first + v - vstart)
        lo = jnp.maximum(pick(base) - tile * EXPERT_TILE, 0.0)
        hi = jnp.minimum(pick(end) - tile * EXPERT_TILE, float(EXPERT_TILE))
        total = vend[:, N_EXPERTS - 1:N_EXPERTS]
        rec = jnp.where(lane == VIS_EXPERT, ev,
                        jnp.where(lane == VIS_TILE, tile,
                                  jnp.where(lane == VIS_LO, lo,
                                            jnp.where(lane == VIS_HI, hi,
                                                      jnp.where(lane == VIS_COUNT, total, 0.0)))))
        vis_ref[...] = rec.T[0:8, :].astype(jnp.int32)

    rt = rt_ref[...]
    lane = lax.broadcasted_iota(jnp.int32, (tm, LANES), 1)
    lanef = lane.astype(F32)
    oh0 = lanef == rt[:, RT_E0:RT_E0 + 1]
    oh1 = lanef == rt[:, RT_E1:RT_E1 + 1]
    picked = jnp.where(oh0, 1.0, 0.0) + jnp.where(oh1, 1.0, 0.0)
    r = lax.broadcasted_iota(jnp.int32, (tm, tm), 0)
    c = lax.broadcasted_iota(jnp.int32, (tm, tm), 1)
    before = jnp.where(c < r, 1.0, 0.0).astype(BF16)
    slot = _dot(before, picked.astype(BF16)) + (carry_ref[...] + base)
    p0 = jnp.sum(jnp.where(oh0, slot, 0.0), axis=1, keepdims=True)
    p1 = jnp.sum(jnp.where(oh1, slot, 0.0), axis=1, keepdims=True)
    rec = jnp.where(lane == 0, p0, jnp.where(lane == 1, p1, 0.0))
    pos_ref[...] = rec.T[0:8, :].astype(jnp.int32)
    carry_ref[...] += jnp.sum(picked, axis=0, keepdims=True)


def _route_call(rt, cnt):
    n = rt.shape[0]
    tm = _pick_tile(n, ROUTE_TILE)
    return pl.pallas_call(
        _route_body,
        grid=(n // tm,),
        in_specs=[pl.BlockSpec((tm, ROUTER_W), lambda i: (i, 0)), _const_spec((8, LANES))],
        out_specs=[pl.BlockSpec((8, tm), lambda i: (0, i)), _const_spec((8, VISIT_CAP))],
        out_shape=[jax.ShapeDtypeStruct((8, n), jnp.int32),
                   jax.ShapeDtypeStruct((8, VISIT_CAP), jnp.int32)],
        scratch_shapes=[pltpu.VMEM((1, LANES), F32)],
        compiler_params=pltpu.CompilerParams(dimension_semantics=("arbitrary",)),
        name="route",
    )(rt, cnt)


SCATTER_TILE = 1024


def _scatter_body(pos0_ref, pos1_ref, x_hbm, xs_hbm, sem):
    i = pl.program_id(0)
    tm = pos0_ref.shape[0]
    base = i * tm

    def issue(r, carry):
        src = x_hbm.at[pl.ds(base + r, 1)]
        pltpu.make_async_copy(src, xs_hbm.at[pl.ds(pos0_ref[r], 1)], sem).start()
        pltpu.make_async_copy(src, xs_hbm.at[pl.ds(pos1_ref[r], 1)], sem).start()
        return carry

    lax.fori_loop(0, tm, issue, 0, unroll=8)

    def drain_one_step():
        rows = x_hbm.at[pl.ds(0, tm)]
        pltpu.make_async_copy(rows, xs_hbm.at[pl.ds(0, tm)], sem).wait()
        pltpu.make_async_copy(rows, xs_hbm.at[pl.ds(0, tm)], sem).wait()

    @pl.when(i > 0)
    def _():
        drain_one_step()

    @pl.when(i == pl.num_programs(0) - 1)
    def _():
        drain_one_step()


def _scatter_call(pos0, pos1, xnu, n_slots):
    n = xnu.shape[0]
    tm = _pick_tile(n, SCATTER_TILE)
    smem = lambda: pl.BlockSpec((tm,), lambda i: (i,), memory_space=pltpu.SMEM)
    return pl.pallas_call(
        _scatter_body,
        grid=(n // tm,),
        in_specs=[smem(), smem(), pl.BlockSpec(memory_space=pl.ANY)],
        out_specs=pl.BlockSpec(memory_space=pl.ANY),
        out_shape=jax.ShapeDtypeStruct((n_slots, PACKED_W), jnp.int32),
        scratch_shapes=[pltpu.SemaphoreType.DMA],
        compiler_params=pltpu.CompilerParams(dimension_semantics=("arbitrary",)),
        name="scatter",
    )(pos0, pos1, xnu)


def _expert_body(vis_ref, x_ref, wg_ref, wu_ref, wd_ref, o_ref):
    j = pl.program_id(0)

    @pl.when(j < vis_ref[VIS_COUNT, 0])
    def _():
        lo_row = vis_ref[VIS_LO, j]
        hi_row = vis_ref[VIS_HI, j]
        lo, hi = _unpack_bf16_pair(x_ref[...])
        x = jnp.concatenate([lo.astype(BF16), hi.astype(BF16)], axis=1)
        g = _dot(x, wg_ref[...])
        u = _dot(x, wu_ref[...])
        hg = (g * _sigmoid(g) * u).astype(BF16)
        o = _dot(hg, wd_ref[...])
        new = _pack_bf16_pair(o[:, :PACKED_W], o[:, PACKED_W:])
        row = lax.broadcasted_iota(jnp.int32, new.shape, 0)

        @pl.when(lo_row == 0)
        def _():
            o_ref[...] = jnp.where(row < hi_row, new, 0)

        @pl.when(lo_row > 0)
        def _():
            o_ref[...] = jnp.where(jnp.logical_and(row >= lo_row, row < hi_row), new, o_ref[...])


def _expert_call(vis, xs, wg, wu, wd, n_visits):
    def visit(j, vis):
        return jnp.minimum(j, vis[VIS_COUNT, 0] - 1)

    def xmap(j, vis):
        return (vis[VIS_TILE, visit(j, vis)], 0)

    def wmap(j, vis):
        return (vis[VIS_EXPERT, visit(j, vis)], 0, 0)

    grid_spec = pltpu.PrefetchScalarGridSpec(
        num_scalar_prefetch=1,
        grid=(n_visits,),
        in_specs=[pl.BlockSpec((EXPERT_TILE, PACKED_W), xmap),
                  pl.BlockSpec((None, D_MODEL, EXPERT_FF), wmap),
                  pl.BlockSpec((None, D_MODEL, EXPERT_FF), wmap),
                  pl.BlockSpec((None, EXPERT_FF, D_MODEL), wmap)],
        out_specs=pl.BlockSpec((EXPERT_TILE, PACKED_W), xmap),
    )
    return pl.pallas_call(
        _expert_body,
        grid_spec=grid_spec,
        out_shape=jax.ShapeDtypeStruct(xs.shape, jnp.int32),
        compiler_params=pltpu.CompilerParams(dimension_semantics=("arbitrary",),
                                             vmem_limit_bytes=VMEM_LIMIT),
        name="experts",
    )(vis, xs, wg, wu, wd)


COMBINE_TILE = 256


def _combine_body(p0_ref, p1_ref, p0n_ref, p1n_ref, h_ref, rt_ref, y_hbm, o_ref, g_ref, sem):
    i = pl.program_id(0)
    tm = h_ref.shape[0]
    cur = i % 2

    def issue(q0_ref, q1_ref, buf):
        def one(r, carry):
            pltpu.make_async_copy(y_hbm.at[pl.ds(q0_ref[r], 1)], g_ref.at[buf, 0, pl.ds(r, 1)],
                                  sem.at[buf]).start()
            pltpu.make_async_copy(y_hbm.at[pl.ds(q1_ref[r], 1)], g_ref.at[buf, 1, pl.ds(r, 1)],
                                  sem.at[buf]).start()
            return carry
        lax.fori_loop(0, tm, one, 0, unroll=8)

    @pl.when(i == 0)
    def _():
        issue(p0_ref, p1_ref, 0)

    @pl.when(i + 1 < pl.num_programs(0))
    def _():
        issue(p0n_ref, p1n_ref, 1 - cur)

    for k in range(2):
        pltpu.make_async_copy(y_hbm.at[pl.ds(0, tm)], g_ref.at[cur, k], sem.at[cur]).wait()
    rt = rt_ref[...]
    acc_lo = []
    acc_hi = []
    for k, wl in ((0, RT_W0), (1, RT_W1)):
        lo, hi = _unpack_bf16_pair(g_ref[cur, k])
        wgt = rt[:, wl:wl + 1]
        acc_lo.append(wgt * lo)
        acc_hi.append(wgt * hi)
    y = jnp.concatenate([acc_lo[0] + acc_lo[1], acc_hi[0] + acc_hi[1]], axis=1)
    o_ref[...] = h_ref[...] + y


def _combine_call(pos0, pos1, h, rt, y):
    n = h.shape[0]
    tm = _pick_tile(n, COMBINE_TILE)
    steps = n // tm
    smem_cur = lambda: pl.BlockSpec((tm,), lambda i: (i,), memory_space=pltpu.SMEM)
    smem_next = lambda: pl.BlockSpec((tm,), lambda i: (jnp.minimum(i + 1, steps - 1),),
                                     memory_space=pltpu.SMEM)
    row = lambda width: pl.BlockSpec((tm, width), lambda i: (i, 0))
    return pl.pallas_call(
        _combine_body,
        grid=(steps,),
        in_specs=[smem_cur(), smem_cur(), smem_next(), smem_next(), row(D_MODEL), row(ROUTER_W),
                  pl.BlockSpec(memory_space=pl.ANY)],
        out_specs=row(D_MODEL),
        out_shape=jax.ShapeDtypeStruct((n, D_MODEL), F32),
        scratch_shapes=[pltpu.VMEM((2, 2, tm, PACKED_W), jnp.int32), pltpu.SemaphoreType.DMA((2,))],
        compiler_params=pltpu.CompilerParams(dimension_semantics=("arbitrary",),
                                             vmem_limit_bytes=VMEM_LIMIT),
        name="combine",
    )(pos0, pos1, pos0, pos1, h, rt, y)


def _rope_tables(pos):
    half = HEAD_DIM // 2
    inv_freq = ROPE_THETA ** (-(jnp.arange(half, dtype=F32) * 2.0) / HEAD_DIM)
    ang = pos.astype(F32)[:, None] * inv_freq[None, :]
    cos = jnp.cos(ang)
    sin = jnp.sin(ang)
    cos64 = jnp.concatenate([cos, cos], axis=1)
    sin64 = jnp.concatenate([-sin, sin], axis=1)
    return jnp.tile(cos64, (1, LANES // HEAD_DIM)), jnp.tile(sin64, (1, LANES // HEAD_DIM))


def _dup_heads(wcols):
    d = wcols.shape[0]
    w3 = wcols.reshape(d, ATTN_KV_HEADS, 1, HEAD_DIM)
    return jnp.broadcast_to(w3, (d, ATTN_KV_HEADS, 2, HEAD_DIM)).reshape(d, ATTN_KV_DUP_W)


def _prep_weights(norm1_w, w_in, q_norm_w, k_norm_w, gla_gate_up, gla_gate_bias, w_attn_branch,
                  w_gla_branch, w_out, norm2_w, router_group, router_expert):
    wb = w_in.astype(BF16)
    wgd = jnp.zeros((D_MODEL, LANES), BF16).at[:, :GLA_GATE_RANK].set(wb[:, _OFF_GD:_OFF_GD + GLA_GATE_RANK])
    gup = jnp.zeros((LANES, GLA_K_W), BF16).at[:GLA_GATE_RANK].set(gla_gate_up.astype(BF16))
    wr = jnp.zeros((D_MODEL, ROUTER_W), F32)
    wr = wr.at[:, :N_EXPERTS].set(router_expert.transpose(1, 0, 2).reshape(D_MODEL, N_EXPERTS))
    wr = wr.at[:, N_EXPERTS:N_EXPERTS + N_GROUPS].set(router_group)
    wrh = wr.astype(BF16)
    wrl = (wr - wrh.astype(F32)).astype(BF16)
    return {
        "norm1": norm1_w.reshape(1, D_MODEL),
        "wq": wb[:, _OFF_Q:_OFF_Q + ATTN_Q_W],
        "wk": _dup_heads(wb[:, _OFF_K:_OFF_K + ATTN_KV_W]),
        "wv": _dup_heads(wb[:, _OFF_V:_OFF_V + ATTN_KV_W]),
        "wgla": wb[:, _OFF_GQ:_OFF_GQ + GLA_W],
        "wgd": wgd,
        "wgate": wb[:, _OFF_GATE:_OFF_GATE + 2 * D_MODEL],
        "qn": jnp.tile(q_norm_w, ATTN_HEADS).reshape(1, ATTN_Q_W),
        "kn": jnp.tile(k_norm_w, ATTN_KV_DUP_W // HEAD_DIM).reshape(1, ATTN_KV_DUP_W),
        "gup": gup,
        "gb": gla_gate_bias.reshape(1, GLA_K_W),
        "wab": w_attn_branch.astype(BF16),
        "wgb": w_gla_branch.astype(BF16),
        "wo": w_out.astype(BF16),
        "norm2": norm2_w.reshape(1, D_MODEL),
        "wrh": wrh,
        "wrl": wrl,
    }


def _pick_tile(n, want):
    t = want
    while n % t:
        t //= 2
    return t


def kernel(x, meta_tokens, norm1_w, w_in, q_norm_w, k_norm_w, attn_sinks, gla_gate_up, gla_gate_bias,
           gla_norm_w, w_attn_branch, w_gla_branch, w_out, norm2_w, router_group, router_expert,
           expert_w_gate, expert_w_up, expert_w_down):
    batch, seq, d = x.shape
    assert d == D_MODEL and seq % GLA_STEP == 0 and norm1_w.shape[0] == 1
    n = batch * seq
    w = _prep_weights(norm1_w[0], w_in[0], q_norm_w[0], k_norm_w[0], gla_gate_up[0], gla_gate_bias[0],
                      w_attn_branch[0], w_gla_branch[0], w_out[0], norm2_w[0], router_group[0],
                      router_expert[0])
    x2d = x.reshape(n, D_MODEL)

    cos_m, sin_m = _rope_tables(jnp.arange(N_META))
    cos_r, sin_r = _rope_tables(jnp.arange(seq) + N_META)
    meta = _proj_call(meta_tokens.astype(F32), N_META, 1, w, cos_m, sin_m)
    tm = _pick_tile(seq, 256)
    (q, kd, vd, qg, kg, vg, rg, gk, sga, sgg) = _proj_call(x2d, tm, seq // tm, w, cos_r, sin_r)

    pad_after = ((0, BLOCK - N_META), (0, 0))
    km = jnp.pad(meta[1], pad_after)
    vm = jnp.pad(meta[2], pad_after)
    ao = _attn_call(q.reshape(batch, seq, ATTN_Q_W), kd.reshape(batch, seq, ATTN_KV_DUP_W),
                    vd.reshape(batch, seq, ATTN_KV_DUP_W), km, vm, attn_sinks[0].astype(F32), batch, seq)

    pad_before = ((GLA_STEP - N_META, 0), (0, 0))
    s0 = _gla_init_call(jnp.pad(meta[4], pad_before), jnp.pad(meta[5], pad_before),
                        jnp.pad(meta[7], pad_before))
    go = _gla_call(s0, qg.reshape(batch, seq, GLA_K_W), kg.reshape(batch, seq, GLA_K_W),
                   vg.reshape(batch, seq, GLA_V_W), gk.reshape(batch, seq, GLA_K_W),
                   rg.reshape(batch, seq, GLA_V_W), gla_norm_w[0].reshape(1, GLA_VAL_DIM), batch, seq)

    h, xnu, rt, cnt = _merge_call(x2d, ao.reshape(n, ATTN_Q_W), go.reshape(n, GLA_V_W), sga, sgg, w,
                                  _pick_tile(n, 512))

    assert (2 * n) % EXPERT_TILE == 0 and _num_visits(n) <= VISIT_CAP
    pos, vis = _route_call(rt, cnt)
    pos0, pos1 = pos[0], pos[1]
    xs = _scatter_call(pos0, pos1, xnu, 2 * n)
    wg = expert_w_gate[0].reshape(N_EXPERTS, D_MODEL, EXPERT_FF).astype(BF16)
    wu = expert_w_up[0].reshape(N_EXPERTS, D_MODEL, EXPERT_FF).astype(BF16)
    wd = expert_w_down[0].reshape(N_EXPERTS, EXPERT_FF, D_MODEL).astype(BF16)
    y = _expert_call(vis, xs, wg, wu, wd, _num_visits(n))
    out = _combine_call(pos0, pos1, h, rt, y)
    return out.reshape(batch, seq, D_MODEL)
```

```python
import functools
import math

import jax
import jax.numpy as jnp
from jax import lax
from jax.experimental import pallas as pl
from jax.experimental.pallas import tpu as pltpu

F32 = jnp.float32
BF16 = jnp.bfloat16

D_MODEL = 1024
N_META = 16
BLOCK = 128
ATTN_HEADS = 8
ATTN_KV_HEADS = 2
HEAD_DIM = 64
ROPE_THETA = 10000.0
ATTN_Q_W = ATTN_HEADS * HEAD_DIM
ATTN_KV_W = ATTN_KV_HEADS * HEAD_DIM
GLA_HEADS = 4
GLA_KEY_DIM = 64
GLA_VAL_DIM = 128
GLA_K_W = GLA_HEADS * GLA_KEY_DIM
GLA_V_W = GLA_HEADS * GLA_VAL_DIM
GLA_GATE_RANK = 16
GLA_GATE_NORM = 16.0
GLA_CHUNK = 64
N_GROUPS = 4
EXPERTS_PER_GROUP = 8
N_EXPERTS = N_GROUPS * EXPERTS_PER_GROUP
EXPERT_FF = 256
NORM_EPS = 1e-6
MASK_VALUE = -1e30

LANES = 128
ATTN_KV_DUP_W = 2 * ATTN_KV_W
VMEM_LIMIT = 56 * 1024 * 1024

_OFF_Q = 0
_OFF_K = _OFF_Q + ATTN_Q_W
_OFF_V = _OFF_K + ATTN_KV_W
_OFF_GQ = _OFF_V + ATTN_KV_W
_OFF_GD = _OFF_GQ + 2 * GLA_K_W + 2 * GLA_V_W
_OFF_GATE = _OFF_GD + GLA_GATE_RANK
GLA_W = 2 * GLA_K_W + 2 * GLA_V_W


def _dot(a, b):
    return jnp.dot(a, b, preferred_element_type=F32)


def _dot_nt(a, b):
    return lax.dot_general(a, b, (((1,), (1,)), ((), ())), preferred_element_type=F32)


def _split_hi_lo(x):
    hi = x.astype(BF16)
    lo = (x - hi.astype(F32)).astype(BF16)
    return hi, lo


def _group_mean_sq(x, group):
    w = x.shape[-1]
    shift = int(math.log2(group))
    r = lax.broadcasted_iota(jnp.int32, (w, w), 0) >> shift
    c = lax.broadcasted_iota(jnp.int32, (w, w), 1) >> shift
    ones = jnp.where(r == c, 1.0, 0.0).astype(BF16)
    hi, lo = _split_hi_lo(x * x)
    return (_dot(hi, ones) + _dot(lo, ones)) * (1.0 / group)


def _rope(x, cos, sin_signed):
    w = x.shape[-1]
    reps = w // LANES
    if reps > 1:
        cos = jnp.concatenate([cos] * reps, axis=1)
        sin_signed = jnp.concatenate([sin_signed] * reps, axis=1)
    lane = lax.broadcasted_iota(jnp.int32, x.shape, 1)
    first_half = (lane & (HEAD_DIM // 2)) == 0
    swapped = jnp.where(first_half, pltpu.roll(x, w - HEAD_DIM // 2, 1), pltpu.roll(x, HEAD_DIM // 2, 1))
    return x * cos + swapped * sin_signed


def _rms_norm(x, w):
    ms = jnp.mean(x * x, axis=-1, keepdims=True)
    return x * lax.rsqrt(ms + NORM_EPS) * w


def _sigmoid(x):
    return 1.0 / (1.0 + jnp.exp(-x))


def _log_sigmoid(x):
    return jnp.minimum(x, 0.0) - jnp.log(1.0 + jnp.exp(-jnp.abs(x)))


def _proj_body(x_ref, n1_ref, wq_ref, wk_ref, wv_ref, wgla_ref, wgd_ref, wgate_ref,
               qn_ref, kn_ref, cos_ref, sin_ref, gup_ref, gb_ref,
               q_out, k_out, v_out, qg_out, kg_out, vg_out, rg_out, gk_out, sga_out, sgg_out):
    xn = _rms_norm(x_ref[...], n1_ref[...]).astype(BF16)
    cos = cos_ref[...]
    sin = sin_ref[...]

    q = _dot(xn, wq_ref[...])
    q = q * lax.rsqrt(_group_mean_sq(q, HEAD_DIM) + NORM_EPS) * qn_ref[...]
    q_out[...] = (_rope(q, cos, sin) * (HEAD_DIM ** -0.5)).astype(BF16)

    k = _dot(xn, wk_ref[...])
    k = k * lax.rsqrt(_group_mean_sq(k, HEAD_DIM) + NORM_EPS) * kn_ref[...]
    k_out[...] = _rope(k, cos, sin).astype(BF16)

    v_out[...] = _dot(xn, wv_ref[...]).astype(BF16)

    g = _dot(xn, wgla_ref[...])
    qg_out[...] = (g[:, :GLA_K_W] * (GLA_KEY_DIM ** -0.5)).astype(BF16)
    kg_out[...] = g[:, GLA_K_W:2 * GLA_K_W].astype(BF16)
    vg_out[...] = g[:, 2 * GLA_K_W:2 * GLA_K_W + GLA_V_W].astype(BF16)
    r = g[:, 2 * GLA_K_W + GLA_V_W:]
    rg_out[...] = (r * _sigmoid(r)).astype(BF16)

    gd = _dot(xn, wgd_ref[...]).astype(BF16)
    z = _dot(gd, gup_ref[...]) + gb_ref[...]
    gk_out[...] = _log_sigmoid(z) * (1.0 / GLA_GATE_NORM)

    gates = _dot(xn, wgate_ref[...])
    sga_out[...] = _sigmoid(gates[:, :D_MODEL]).astype(BF16)
    sgg_out[...] = _sigmoid(gates[:, D_MODEL:]).astype(BF16)


def _const_spec(shape):
    nd = len(shape)
    return pl.BlockSpec(shape, lambda *_: (0,) * nd)


def _proj_call(x2d, tm, tiles_per_seq, w, cos, sin):
    n = x2d.shape[0]
    grid = (n // tm,)
    row = lambda width: pl.BlockSpec((tm, width), lambda i: (i, 0))
    tab = pl.BlockSpec((tm, LANES), lambda i: (i % tiles_per_seq, 0))
    in_specs = [
        row(D_MODEL), _const_spec((1, D_MODEL)),
        _const_spec((D_MODEL, ATTN_Q_W)), _const_spec((D_MODEL, ATTN_KV_DUP_W)),
        _const_spec((D_MODEL, ATTN_KV_DUP_W)), _const_spec((D_MODEL, GLA_W)),
        _const_spec((D_MODEL, LANES)), _const_spec((D_MODEL, 2 * D_MODEL)),
        _const_spec((1, ATTN_Q_W)), _const_spec((1, ATTN_KV_DUP_W)),
        tab, tab, _const_spec((LANES, GLA_K_W)), _const_spec((1, GLA_K_W)),
    ]
    out_widths = [ATTN_Q_W, ATTN_KV_DUP_W, ATTN_KV_DUP_W, GLA_K_W, GLA_K_W, GLA_V_W, GLA_V_W,
                  GLA_K_W, D_MODEL, D_MODEL]
    out_dtypes = [BF16] * 7 + [F32] + [BF16] * 2
    return pl.pallas_call(
        _proj_body,
        grid=grid,
        in_specs=in_specs,
        out_specs=[row(wd) for wd in out_widths],
        out_shape=[jax.ShapeDtypeStruct((n, wd), dt) for wd, dt in zip(out_widths, out_dtypes)],
        compiler_params=pltpu.CompilerParams(dimension_semantics=("arbitrary",),
                                             vmem_limit_bytes=VMEM_LIMIT),
        name="proj",
    )(x2d, w["norm1"], w["wq"], w["wk"], w["wv"], w["wgla"], w["wgd"], w["wgate"],
      w["qn"], w["kn"], cos, sin, w["gup"], w["gb"])


def _attn_body(sink_ref, q_ref, kp_ref, kc_ref, km_ref, vp_ref, vc_ref, vm_ref, o_ref):
    i = pl.program_id(1)
    lane = lax.broadcasted_iota(jnp.int32, (BLOCK, LANES), 1)
    low = lane < HEAD_DIM
    key = lax.broadcasted_iota(jnp.int32, (BLOCK, 2 * LANES), 1) & (LANES - 1)
    rowi = lax.broadcasted_iota(jnp.int32, (BLOCK, 2 * LANES), 0)
    mask_meta = key < N_META
    mask_prev = jnp.logical_and(key > rowi, i > 0)
    mask_cur = key <= rowi
    masks = (mask_meta, mask_prev, mask_cur)
    zero = jnp.zeros((), BF16)

    for kvh in range(ATTN_KV_HEADS):
        sl = slice(kvh * LANES, (kvh + 1) * LANES)
        kblks = []
        vblks = []
        for k_ref, v_ref in ((km_ref, vm_ref), (kp_ref, vp_ref), (kc_ref, vc_ref)):
            kp = k_ref[:, sl]
            vp = v_ref[:, sl]
            kblks.append(jnp.concatenate([jnp.where(low, kp, zero), jnp.where(low, zero, kp)], axis=0))
            vblks.append(jnp.concatenate([jnp.where(low, vp, zero), jnp.where(low, zero, vp)], axis=0))
        for pair in range(2):
            j = kvh * 2 + pair
            q2 = q_ref[:, j * LANES:(j + 1) * LANES]
            s_parts = []
            for part in range(3):
                s = _dot_nt(q2, kblks[part])
                s_parts.append(jnp.where(masks[part], s, MASK_VALUE))
            p_heads = []
            inv_l = []
            for hh in range(2):
                sink = sink_ref[2 * j + hh]
                sh = [sp[:, hh * LANES:(hh + 1) * LANES] for sp in s_parts]
                m = jnp.maximum(jnp.maximum(jnp.max(sh[0], axis=1, keepdims=True),
                                            jnp.max(sh[1], axis=1, keepdims=True)),
                                jnp.max(sh[2], axis=1, keepdims=True))
                m = jnp.maximum(m, sink)
                ph = [jnp.exp(s - m) for s in sh]
                l = (jnp.sum(ph[0], axis=1, keepdims=True) + jnp.sum(ph[1], axis=1, keepdims=True)
                     + jnp.sum(ph[2], axis=1, keepdims=True) + jnp.exp(sink - m))
                p_heads.append(ph)
                inv_l.append(1.0 / l)
            o2 = jnp.zeros((BLOCK, LANES), F32)
            for part in range(3):
                p2 = jnp.concatenate([p_heads[0][part], p_heads[1][part]], axis=1).astype(BF16)
                o2 = o2 + _dot(p2, vblks[part])
            o2 = o2 * jnp.where(low, inv_l[0], inv_l[1])
            o_ref[:, j * LANES:(j + 1) * LANES] = o2.astype(BF16)


def _attn_call(q, kd, vd, km, vm, sinks, batch, seq):
    nb = seq // BLOCK
    qspec = pl.BlockSpec((None, BLOCK, ATTN_Q_W), lambda b, i, s: (b, i, 0))
    prev = pl.BlockSpec((None, BLOCK, ATTN_KV_DUP_W), lambda b, i, s: (b, jnp.maximum(i - 1, 0), 0))
    cur = pl.BlockSpec((None, BLOCK, ATTN_KV_DUP_W), lambda b, i, s: (b, i, 0))
    meta = pl.BlockSpec((BLOCK, ATTN_KV_DUP_W), lambda b, i, s: (0, 0))
    grid_spec = pltpu.PrefetchScalarGridSpec(
        num_scalar_prefetch=1,
        grid=(batch, nb),
        in_specs=[qspec, prev, cur, meta, prev, cur, meta],
        out_specs=pl.BlockSpec((None, BLOCK, ATTN_Q_W), lambda b, i, s: (b, i, 0)),
    )
    return pl.pallas_call(
        _attn_body,
        grid_spec=grid_spec,
        out_shape=jax.ShapeDtypeStruct((batch, seq, ATTN_Q_W), BF16),
        compiler_params=pltpu.CompilerParams(dimension_semantics=("arbitrary", "arbitrary"),
                                             vmem_limit_bytes=VMEM_LIMIT),
        name="attn",
    )(sinks, q, kd, kd, km, vd, vd, vm)


GLA_STEP = 2 * GLA_CHUNK
GLA_PAIR_K = 2 * GLA_KEY_DIM
GLA_PAIR_V = 2 * GLA_VAL_DIM


def _gla_consts():
    r = lax.broadcasted_iota(jnp.int32, (GLA_STEP, 2 * GLA_STEP), 0)
    c = lax.broadcasted_iota(jnp.int32, (GLA_STEP, 2 * GLA_STEP), 1) & (GLA_STEP - 1)
    causal = jnp.logical_and((r >> 6) == (c >> 6), c <= r)
    tri = jnp.where(causal[:, :GLA_STEP], 1.0, 0.0).astype(BF16)
    return causal, tri


def _gla_step(q2, k2, v2, gk2, s_prev, causal, tri):
    row = lax.broadcasted_iota(jnp.int32, (GLA_STEP, GLA_PAIR_K), 0)
    lane = lax.broadcasted_iota(jnp.int32, (GLA_STEP, GLA_PAIR_K), 1)
    first = row < GLA_CHUNK
    low = lane < GLA_KEY_DIM
    g_hi = gk2.astype(BF16)
    g_r = gk2 - g_hi.astype(F32)
    g_mid = g_r.astype(BF16)
    g_lo = (g_r - g_mid.astype(F32)).astype(BF16)
    b = _dot(tri, g_hi) + _dot(tri, g_mid) + _dot(tri, g_lo)
    half = GLA_CHUNK // 2
    b_mid = jnp.where(first, b[half:half + 1, :], b[GLA_CHUNK + half:GLA_CHUNK + half + 1, :])
    qi = (q2 * jnp.exp(b - b_mid)).astype(BF16)
    ki = (k2 * jnp.exp(b_mid - b)).astype(BF16)
    qx = q2 * jnp.exp(b)
    zero = jnp.zeros((), BF16)
    kblk = jnp.concatenate([jnp.where(low, ki, zero), jnp.where(low, zero, ki)], axis=0)
    att = _dot_nt(qi, kblk)
    att = jnp.where(causal, att, 0.0).astype(BF16)
    zv = jnp.zeros((GLA_STEP, GLA_VAL_DIM), BF16)
    vblk = jnp.concatenate([jnp.concatenate([v2[:, :GLA_VAL_DIM], zv], axis=1),
                            jnp.concatenate([zv, v2[:, GLA_VAL_DIM:]], axis=1)], axis=0)
    o = _dot(att, vblk)
    b_t = b.T
    k_t = k2.T
    lane_t = lax.broadcasted_iota(jnp.int32, (GLA_PAIR_K, GLA_STEP), 1)
    first_t = lane_t < GLA_CHUNK
    bl_a = b_t[:, GLA_CHUNK - 1:GLA_CHUNK]
    bl_b = b_t[:, GLA_STEP - 1:GLA_STEP]
    kx_t = k_t * jnp.exp(jnp.where(first_t, bl_a, bl_b) - b_t)
    zf = jnp.zeros((), F32)
    kx_a = jnp.where(first_t, kx_t, zf).astype(BF16)
    kx_b = jnp.where(first_t, zf, kx_t).astype(BF16)
    srow = lax.broadcasted_iota(jnp.int32, (GLA_PAIR_K, GLA_PAIR_V), 0)
    slane = lax.broadcasted_iota(jnp.int32, (GLA_PAIR_K, GLA_PAIR_V), 1)
    diag = (srow < GLA_KEY_DIM) == (slane < GLA_VAL_DIM)
    s_a = jnp.exp(bl_a) * s_prev + jnp.where(diag, _dot(kx_a, v2), 0.0)
    s_b = jnp.exp(bl_b) * s_a + jnp.where(diag, _dot(kx_b, v2), 0.0)
    o = o + _dot(jnp.where(first, qx, zf).astype(BF16), s_prev.astype(BF16))
    o = o + _dot(jnp.where(first, zf, qx).astype(BF16), s_a.astype(BF16))
    return o, s_b


def _gla_init_body(k_ref, v_ref, gk_ref, s_out):
    causal, tri = _gla_consts()
    for p in range(GLA_HEADS // 2):
        k2 = k_ref[:, p * GLA_PAIR_K:(p + 1) * GLA_PAIR_K].astype(F32)
        v2 = v_ref[:, p * GLA_PAIR_V:(p + 1) * GLA_PAIR_V]
        gk2 = gk_ref[:, p * GLA_PAIR_K:(p + 1) * GLA_PAIR_K]
        s0 = jnp.zeros((GLA_PAIR_K, GLA_PAIR_V), F32)
        _, s = _gla_step(jnp.zeros_like(k2), k2, v2, gk2, s0, causal, tri)
        s_out[p] = s


def _gla_init_call(kg_m, vg_m, gk_m):
    return pl.pallas_call(
        _gla_init_body,
        out_shape=jax.ShapeDtypeStruct((GLA_HEADS // 2, GLA_PAIR_K, GLA_PAIR_V), F32),
        name="gla_init",
    )(kg_m, vg_m, gk_m)


def _gla_body(s0_ref, q_ref, k_ref, v_ref, gk_ref, rg_ref, nw_ref, o_ref, s_ref):
    t = pl.program_id(1)

    @pl.when(t == 0)
    def _():
        s_ref[...] = s0_ref[...]

    causal, tri = _gla_consts()
    nw = nw_ref[...]
    for p in range(GLA_HEADS // 2):
        ks = slice(p * GLA_PAIR_K, (p + 1) * GLA_PAIR_K)
        vs = slice(p * GLA_PAIR_V, (p + 1) * GLA_PAIR_V)
        o, s_new = _gla_step(q_ref[:, ks].astype(F32), k_ref[:, ks].astype(F32), v_ref[:, vs],
                             gk_ref[:, ks], s_ref[p], causal, tri)
        s_ref[p] = s_new
        outs = []
        for hh in range(2):
            oh = o[:, hh * GLA_VAL_DIM:(hh + 1) * GLA_VAL_DIM]
            ms = jnp.mean(oh * oh, axis=-1, keepdims=True)
            outs.append(oh * lax.rsqrt(ms + NORM_EPS) * nw)
        on = jnp.concatenate(outs, axis=1)
        o_ref[:, vs] = (on * rg_ref[:, vs].astype(F32)).astype(BF16)


def _gla_call(s0, qg, kg, vg, gk, rg, nw, batch, seq):
    steps = seq // GLA_STEP
    spec = lambda width: pl.BlockSpec((None, GLA_STEP, width), lambda b, t: (b, t, 0))
    return pl.pallas_call(
        _gla_body,
        grid=(batch, steps),
        in_specs=[_const_spec((GLA_HEADS // 2, GLA_PAIR_K, GLA_PAIR_V)),
                  spec(GLA_K_W), spec(GLA_K_W), spec(GLA_V_W), spec(GLA_K_W), spec(GLA_V_W),
                  _const_spec((1, GLA_VAL_DIM))],
        out_specs=spec(GLA_V_W),
        out_shape=jax.ShapeDtypeStruct((batch, seq, GLA_V_W), BF16),
        scratch_shapes=[pltpu.VMEM((GLA_HEADS // 2, GLA_PAIR_K, GLA_PAIR_V), F32)],
        compiler_params=pltpu.CompilerParams(dimension_semantics=("arbitrary", "arbitrary"),
                                             vmem_limit_bytes=VMEM_LIMIT),
        name="gla",
    )(s0, qg, kg, vg, gk, rg, nw)


ROUTER_W = LANES


RT_E0, RT_E1, RT_W0, RT_W1 = 0, 1, 2, 3


def _merge_body(x_ref, ao_ref, go_ref, sga_ref, sgg_ref, wab_ref, wgb_ref, wo_ref, n2_ref,
                wrh_ref, wrl_ref, h_out, rt_out, cnt_out):
    ya = _dot(ao_ref[...], wab_ref[...])
    yg = _dot(go_ref[...], wgb_ref[...])
    merged = sga_ref[...].astype(F32) * ya + sgg_ref[...].astype(F32) * yg
    h = x_ref[...] + _dot(merged.astype(BF16), wo_ref[...])
    h_out[...] = h
    xn = _rms_norm(h, n2_ref[...])

    xh, xl = _split_hi_lo(xn)
    wrh = wrh_ref[...]
    lg = _dot(xh, wrh) + _dot(xl, wrh) + _dot(xh, wrl_ref[...])
    lane = lax.broadcasted_iota(jnp.int32, lg.shape, 1)
    neg = -jnp.inf
    big = jnp.int32(1 << 20)
    gmask = jnp.logical_and(lane >= N_EXPERTS, lane < N_EXPERTS + N_GROUPS)
    gl = jnp.where(gmask, lg, neg)
    gmax = jnp.max(gl, axis=1, keepdims=True)
    g_lane = jnp.min(jnp.where(gl == gmax, lane, big), axis=1, keepdims=True)
    g_w = 1.0 / jnp.sum(jnp.exp(gl - gmax), axis=1, keepdims=True)
    g_idx = g_lane - N_EXPERTS
    e_lo = g_idx * EXPERTS_PER_GROUP
    emask = jnp.logical_and(lane >= e_lo, lane < e_lo + EXPERTS_PER_GROUP)
    el = jnp.where(emask, lg, neg)
    m1 = jnp.max(el, axis=1, keepdims=True)
    i1 = jnp.min(jnp.where(el == m1, lane, big), axis=1, keepdims=True)
    el2 = jnp.where(lane == i1, neg, el)
    m2 = jnp.max(el2, axis=1, keepdims=True)
    i2 = jnp.min(jnp.where(el2 == m2, lane, big), axis=1, keepdims=True)
    e2 = jnp.exp(m2 - m1)
    p1 = g_w / (1.0 + e2)
    p2 = g_w * e2 / (1.0 + e2)
    rt_out[...] = jnp.where(lane == RT_E0, i1.astype(F32),
                            jnp.where(lane == RT_E1, i2.astype(F32),
                                      jnp.where(lane == RT_W0, p1, jnp.where(lane == RT_W1, p2, 0.0))))
    picked = jnp.where(lane == i1, 1.0, 0.0) + jnp.where(lane == i2, 1.0, 0.0)

    @pl.when(pl.program_id(0) == 0)
    def _():
        cnt_out[...] = jnp.zeros_like(cnt_out)

    cnt_out[0:1, :] += jnp.sum(picked, axis=0, keepdims=True)


def _merge_call(x2d, ao, go, sga, sgg, w, tm):
    n = x2d.shape[0]
    row = lambda width: pl.BlockSpec((tm, width), lambda i: (i, 0))
    return pl.pallas_call(
        _merge_body,
        grid=(n // tm,),
        in_specs=[row(D_MODEL), row(ATTN_Q_W), row(GLA_V_W), row(D_MODEL), row(D_MODEL),
                  _const_spec((ATTN_Q_W, D_MODEL)), _const_spec((GLA_V_W, D_MODEL)),
                  _const_spec((D_MODEL, D_MODEL)), _const_spec((1, D_MODEL)),
                  _const_spec((D_MODEL, ROUTER_W)), _const_spec((D_MODEL, ROUTER_W))],
        out_specs=[row(D_MODEL), row(ROUTER_W), _const_spec((8, LANES))],
        out_shape=[jax.ShapeDtypeStruct((n, D_MODEL), F32),
                   jax.ShapeDtypeStruct((n, ROUTER_W), F32),
                   jax.ShapeDtypeStruct((8, LANES), F32)],
        compiler_params=pltpu.CompilerParams(dimension_semantics=("arbitrary",),
                                             vmem_limit_bytes=VMEM_LIMIT),
        name="merge",
    )(x2d, ao, go, sga, sgg, w["wab"], w["wgb"], w["wo"], w["norm2"], w["wrh"], w["wrl"])


EXPERT_TILE = 512
ROUTE_TILE = 512
VISIT_CAP = 256
VIS_EXPERT, VIS_TILE, VIS_LO, VIS_HI, VIS_COUNT = 0, 1, 2, 3, 4


def _num_visits(n_tokens):
    return (2 * n_tokens) // EXPERT_TILE + N_EXPERTS - 1


def _lane_cumsum_inclusive(x):
    lane = lax.broadcasted_iota(jnp.int32, x.shape, 1)
    s = 1
    while s < N_EXPERTS:
        x = x + jnp.where(lane >= s, pltpu.roll(x, s, 1), 0.0)
        s *= 2
    return x


def _route_body(rt_ref, cnt_ref, pos_ref, vis_ref, carry_ref):
    i = pl.program_id(0)
    tm = rt_ref.shape[0]
    lane1 = lax.broadcasted_iota(jnp.int32, (1, LANES), 1)
    cnt = jnp.where(lane1 < N_EXPERTS, cnt_ref[0:1, :], 0.0)
    end = _lane_cumsum_inclusive(cnt)
    base = end - cnt

    @pl.when(i == 0)
    def _():
        carry_ref[...] = jnp.zeros_like(carry_ref)
        inv = 1.0 / EXPERT_TILE
        first = jnp.floor(base * inv)
        n_vis = jnp.where(cnt > 0.0, jnp.floor((end - 1.0) * inv) - first + 1.0, 0.0)
        vend = _lane_cumsum_inclusive(n_vis)
        vstart = vend - n_vis
        lane = lax.broadcasted_iota(jnp.int32, (VISIT_CAP, LANES), 1)
        v = lax.broadcasted_iota(jnp.int32, (VISIT_CAP, LANES), 0).astype(F32)
        done = jnp.logical_and(lane < N_EXPERTS, vend <= v)
        ev = jnp.minimum(jnp.sum(jnp.where(done, 1.0, 0.0), axis=1, keepdims=True), N_EXPERTS - 1.0)
        mine = lane.astype(F32) == ev
        pick = lambda a: jnp.sum(jnp.where(mine, a, 0.0), axis=1, keepdims=True)
        tile = pick(first + v - vstart)
        lo = jnp.maximum(pick(base) - tile * EXPERT_TILE, 0.0)
        hi = jnp.minimum(pick(end) - tile * EXPERT_TILE, float(EXPERT_TILE))
        total = vend[:, N_EXPERTS - 1:N_EXPERTS]
        rec = jnp.where(lane == VIS_EXPERT, ev,
                        jnp.where(lane == VIS_TILE, tile,
                                  jnp.where(lane == VIS_LO, lo,
                                            jnp.where(lane == VIS_HI, hi,
                                                      jnp.where(lane == VIS_COUNT, total, 0.0)))))
        vis_ref[...] = rec.T[0:8, :].astype(jnp.int32)

    rt = rt_ref[...]
    lane = lax.broadcasted_iota(jnp.int32, (tm, LANES), 1)
    lanef = lane.astype(F32)
    oh0 = lanef == rt[:, RT_E0:RT_E0 + 1]
    oh1 = lanef == rt[:, RT_E1:RT_E1 + 1]
    picked = jnp.where(oh0, 1.0, 0.0) + jnp.where(oh1, 1.0, 0.0)
    r = lax.broadcasted_iota(jnp.int32, (tm, tm), 0)
    c = lax.broadcasted_iota(jnp.int32, (tm, tm), 1)
    before = jnp.where(c < r, 1.0, 0.0).astype(BF16)
    slot = _dot(before, picked.astype(BF16)) + (carry_ref[...] + base)
    p0 = jnp.sum(jnp.where(oh0, slot, 0.0), axis=1, keepdims=True)
    p1 = jnp.sum(jnp.where(oh1, slot, 0.0), axis=1, keepdims=True)
    rec = jnp.where(lane == 0, p0, jnp.where(lane == 1, p1, 0.0))
    pos_ref[...] = rec.T[0:8, :].astype(jnp.int32)
    carry_ref[...] += jnp.sum(picked, axis=0, keepdims=True)


def _route_call(rt, cnt):
    n = rt.shape[0]
    tm = _pick_tile(n, ROUTE_TILE)
    return pl.pallas_call(
        _route_body,
        grid=(n // tm,),
        in_specs=[pl.BlockSpec((tm, ROUTER_W), lambda i: (i, 0)), _const_spec((8, LANES))],
        out_specs=[pl.BlockSpec((8, tm), lambda i: (0, i)), _const_spec((8, VISIT_CAP))],
        out_shape=[jax.ShapeDtypeStruct((8, n), jnp.int32),
                   jax.ShapeDtypeStruct((8, VISIT_CAP), jnp.int32)],
        scratch_shapes=[pltpu.VMEM((1, LANES), F32)],
        compiler_params=pltpu.CompilerParams(dimension_semantics=("arbitrary",)),
        name="route",
    )(rt, cnt)


SCATTER_TILE = 512


def _scatter_body(pos0_ref, pos1_ref, h_ref, hs_hbm, sem):
    tm = h_ref.shape[0]

    def issue(r, carry):
        src = h_ref.at[pl.ds(r, 1)]
        pltpu.make_async_copy(src, hs_hbm.at[pl.ds(pos0_ref[r], 1)], sem).start()
        pltpu.make_async_copy(src, hs_hbm.at[pl.ds(pos1_ref[r], 1)], sem).start()
        return carry

    lax.fori_loop(0, tm, issue, 0, unroll=8)
    for _ in range(2):
        pltpu.make_async_copy(h_ref, hs_hbm.at[pl.ds(0, tm)], sem).wait()


def _scatter_call(pos0, pos1, h, n_slots):
    n = h.shape[0]
    tm = _pick_tile(n, SCATTER_TILE)
    smem = lambda: pl.BlockSpec((tm,), lambda i: (i,), memory_space=pltpu.SMEM)
    return pl.pallas_call(
        _scatter_body,
        grid=(n // tm,),
        in_specs=[smem(), smem(), pl.BlockSpec((tm, D_MODEL), lambda i: (i, 0))],
        out_specs=pl.BlockSpec(memory_space=pl.ANY),
        out_shape=jax.ShapeDtypeStruct((n_slots, D_MODEL), F32),
        scratch_shapes=[pltpu.SemaphoreType.DMA],
        compiler_params=pltpu.CompilerParams(dimension_semantics=("arbitrary",)),
        name="scatter",
    )(pos0, pos1, h)


def _expert_body(vis_ref, h_ref, n2_ref, wg_ref, wu_ref, wd_ref, o_ref):
    j = pl.program_id(0)

    @pl.when(j < vis_ref[VIS_COUNT, 0])
    def _():
        lo_row = vis_ref[VIS_LO, j]
        hi_row = vis_ref[VIS_HI, j]
        x = _rms_norm(h_ref[...], n2_ref[...]).astype(BF16)
        g = _dot(x, wg_ref[...])
        u = _dot(x, wu_ref[...])
        hg = (g * _sigmoid(g) * u).astype(BF16)
        new = _dot(hg, wd_ref[...])
        row = lax.broadcasted_iota(jnp.int32, new.shape, 0)

        @pl.when(lo_row == 0)
        def _():
            o_ref[...] = jnp.where(row < hi_row, new, 0.0)

        @pl.when(lo_row > 0)
        def _():
            o_ref[...] = jnp.where(jnp.logical_and(row >= lo_row, row < hi_row), new, o_ref[...])


def _expert_call(vis, hs, n2, wg, wu, wd, n_visits):
    def visit(j, vis):
        return jnp.minimum(j, vis[VIS_COUNT, 0] - 1)

    def xmap(j, vis):
        return (vis[VIS_TILE, visit(j, vis)], 0)

    def wmap(j, vis):
        return (vis[VIS_EXPERT, visit(j, vis)], 0, 0)

    grid_spec = pltpu.PrefetchScalarGridSpec(
        num_scalar_prefetch=1,
        grid=(n_visits,),
        in_specs=[pl.BlockSpec((EXPERT_TILE, D_MODEL), xmap),
                  pl.BlockSpec((1, D_MODEL), lambda j, vis: (0, 0)),
                  pl.BlockSpec((None, D_MODEL, EXPERT_FF), wmap),
                  pl.BlockSpec((None, D_MODEL, EXPERT_FF), wmap),
                  pl.BlockSpec((None, EXPERT_FF, D_MODEL), wmap)],
        out_specs=pl.BlockSpec((EXPERT_TILE, D_MODEL), xmap),
    )
    return pl.pallas_call(
        _expert_body,
        grid_spec=grid_spec,
        out_shape=jax.ShapeDtypeStruct(hs.shape, F32),
        compiler_params=pltpu.CompilerParams(dimension_semantics=("arbitrary",),
                                             vmem_limit_bytes=VMEM_LIMIT),
        name="experts",
    )(vis, hs, n2, wg, wu, wd)


COMBINE_TILE = 256


def _combine_body(p0_ref, p1_ref, p0n_ref, p1n_ref, h_ref, rt_ref, y_hbm, o_ref, g_ref, sem):
    i = pl.program_id(0)
    tm = h_ref.shape[0]
    cur = i % 2

    def issue(q0_ref, q1_ref, buf):
        def one(r, carry):
            pltpu.make_async_copy(y_hbm.at[pl.ds(q0_ref[r], 1)], g_ref.at[buf, 0, pl.ds(r, 1)],
                                  sem.at[buf]).start()
            pltpu.make_async_copy(y_hbm.at[pl.ds(q1_ref[r], 1)], g_ref.at[buf, 1, pl.ds(r, 1)],
                                  sem.at[buf]).start()
            return carry
        lax.fori_loop(0, tm, one, 0, unroll=8)

    @pl.when(i == 0)
    def _():
        issue(p0_ref, p1_ref, 0)

    @pl.when(i + 1 < pl.num_programs(0))
    def _():
        issue(p0n_ref, p1n_ref, 1 - cur)

    for k in range(2):
        pltpu.make_async_copy(y_hbm.at[pl.ds(0, tm)], g_ref.at[cur, k], sem.at[cur]).wait()
    rt = rt_ref[...]
    o_ref[...] = (h_ref[...] + rt[:, RT_W0:RT_W0 + 1] * g_ref[cur, 0]
                  + rt[:, RT_W1:RT_W1 + 1] * g_ref[cur, 1])


def _combine_call(pos0, pos1, h, rt, y):
    n = h.shape[0]
    tm = _pick_tile(n, COMBINE_TILE)
    steps = n // tm
    smem_cur = lambda: pl.BlockSpec((tm,), lambda i: (i,), memory_space=pltpu.SMEM)
    smem_next = lambda: pl.BlockSpec((tm,), lambda i: (jnp.minimum(i + 1, steps - 1),),
                                     memory_space=pltpu.SMEM)
    row = lambda width: pl.BlockSpec((tm, width), lambda i: (i, 0))
    return pl.pallas_call(
        _combine_body,
        grid=(steps,),
        in_specs=[smem_cur(), smem_cur(), smem_next(), smem_next(), row(D_MODEL), row(ROUTER_W),
                  pl.BlockSpec(memory_space=pl.ANY)],
        out_specs=row(D_MODEL),
        out_shape=jax.ShapeDtypeStruct((n, D_MODEL), F32),
        scratch_shapes=[pltpu.VMEM((2, 2, tm, D_MODEL), F32), pltpu.SemaphoreType.DMA((2,))],
        compiler_params=pltpu.CompilerParams(dimension_semantics=("arbitrary",),
                                             vmem_limit_bytes=VMEM_LIMIT),
        name="combine",
    )(pos0, pos1, pos0, pos1, h, rt, y)


def _rope_tables(pos):
    half = HEAD_DIM // 2
    inv_freq = ROPE_THETA ** (-(jnp.arange(half, dtype=F32) * 2.0) / HEAD_DIM)
    ang = pos.astype(F32)[:, None] * inv_freq[None, :]
    cos = jnp.cos(ang)
    sin = jnp.sin(ang)
    cos64 = jnp.concatenate([cos, cos], axis=1)
    sin64 = jnp.concatenate([-sin, sin], axis=1)
    return jnp.tile(cos64, (1, LANES // HEAD_DIM)), jnp.tile(sin64, (1, LANES // HEAD_DIM))


def _dup_heads(wcols):
    d = wcols.shape[0]
    w3 = wcols.reshape(d, ATTN_KV_HEADS, 1, HEAD_DIM)
    return jnp.broadcast_to(w3, (d, ATTN_KV_HEADS, 2, HEAD_DIM)).reshape(d, ATTN_KV_DUP_W)


def _prep_weights(norm1_w, w_in, q_norm_w, k_norm_w, gla_gate_up, gla_gate_bias, w_attn_branch,
                  w_gla_branch, w_out, norm2_w, router_group, router_expert):
    wb = w_in.astype(BF16)
    wgd = jnp.zeros((D_MODEL, LANES), BF16).at[:, :GLA_GATE_RANK].set(wb[:, _OFF_GD:_OFF_GD + GLA_GATE_RANK])
    gup = jnp.zeros((LANES, GLA_K_W), BF16).at[:GLA_GATE_RANK].set(gla_gate_up.astype(BF16))
    wr = jnp.zeros((D_MODEL, ROUTER_W), F32)
    wr = wr.at[:, :N_EXPERTS].set(router_expert.transpose(1, 0, 2).reshape(D_MODEL, N_EXPERTS))
    wr = wr.at[:, N_EXPERTS:N_EXPERTS + N_GROUPS].set(router_group)
    wrh = wr.astype(BF16)
    wrl = (wr - wrh.astype(F32)).astype(BF16)
    return {
        "norm1": norm1_w.reshape(1, D_MODEL),
        "wq": wb[:, _OFF_Q:_OFF_Q + ATTN_Q_W],
        "wk": _dup_heads(wb[:, _OFF_K:_OFF_K + ATTN_KV_W]),
        "wv": _dup_heads(wb[:, _OFF_V:_OFF_V + ATTN_KV_W]),
        "wgla": wb[:, _OFF_GQ:_OFF_GQ + GLA_W],
        "wgd": wgd,
        "wgate": wb[:, _OFF_GATE:_OFF_GATE + 2 * D_MODEL],
        "qn": jnp.tile(q_norm_w, ATTN_HEADS).reshape(1, ATTN_Q_W),
        "kn": jnp.tile(k_norm_w, ATTN_KV_DUP_W // HEAD_DIM).reshape(1, ATTN_KV_DUP_W),
        "gup": gup,
        "gb": gla_gate_bias.reshape(1, GLA_K_W),
        "wab": w_attn_branch.astype(BF16),
        "wgb": w_gla_branch.astype(BF16),
        "wo": w_out.astype(BF16),
        "norm2": norm2_w.reshape(1, D_MODEL),
        "wrh": wrh,
        "wrl": wrl,
    }


def _pick_tile(n, want):
    t = want
    while n % t:
        t //= 2
    return t


def kernel(x, meta_tokens, norm1_w, w_in, q_norm_w, k_norm_w, attn_sinks, gla_gate_up, gla_gate_bias,
           gla_norm_w, w_attn_branch, w_gla_branch, w_out, norm2_w, router_group, router_expert,
           expert_w_gate, expert_w_up, expert_w_down):
    batch, seq, d = x.shape
    assert d == D_MODEL and seq % GLA_STEP == 0 and norm1_w.shape[0] == 1
    n = batch * seq
    w = _prep_weights(norm1_w[0], w_in[0], q_norm_w[0], k_norm_w[0], gla_gate_up[0], gla_gate_bias[0],
                      w_attn_branch[0], w_gla_branch[0], w_out[0], norm2_w[0], router_group[0],
                      router_expert[0])
    x2d = x.reshape(n, D_MODEL)

    cos_m, sin_m = _rope_tables(jnp.arange(N_META))
    cos_r, sin_r = _rope_tables(jnp.arange(seq) + N_META)
    meta = _proj_call(meta_tokens.astype(F32), N_META, 1, w, cos_m, sin_m)
    tm = _pick_tile(seq, 512)
    (q, kd, vd, qg, kg, vg, rg, gk, sga, sgg) = _proj_call(x2d, tm, seq // tm, w, cos_r, sin_r)

    pad_after = ((0, BLOCK - N_META), (0, 0))
    km = jnp.pad(meta[1], pad_after)
    vm = jnp.pad(meta[2], pad_after)
    ao = _attn_call(q.reshape(batch, seq, ATTN_Q_W), kd.reshape(batch, seq, ATTN_KV_DUP_W),
                    vd.reshape(batch, seq, ATTN_KV_DUP_W), km, vm, attn_sinks[0].astype(F32), batch, seq)

    pad_before = ((GLA_STEP - N_META, 0), (0, 0))
    s0 = _gla_init_call(jnp.pad(meta[4], pad_before), jnp.pad(meta[5], pad_before),
                        jnp.pad(meta[7], pad_before))
    go = _gla_call(s0, qg.reshape(batch, seq, GLA_K_W), kg.reshape(batch, seq, GLA_K_W),
                   vg.reshape(batch, seq, GLA_V_W), gk.reshape(batch, seq, GLA_K_W),
                   rg.reshape(batch, seq, GLA_V_W), gla_norm_w[0].reshape(1, GLA_VAL_DIM), batch, seq)

    h, rt, cnt = _merge_call(x2d, ao.reshape(n, ATTN_Q_W), go.reshape(n, GLA_V_W), sga, sgg, w,
                             _pick_tile(n, 512))

    assert (2 * n) % EXPERT_TILE == 0 and _num_visits(n) <= VISIT_CAP
    pos, vis = _route_call(rt, cnt)
    pos0, pos1 = pos[0], pos[1]
    hs = _scatter_call(pos0, pos1, h, 2 * n)
    wg = expert_w_gate[0].reshape(N_EXPERTS, D_MODEL, EXPERT_FF).astype(BF16)
    wu = expert_w_up[0].reshape(N_EXPERTS, D_MODEL, EXPERT_FF).astype(BF16)
    wd = expert_w_down[0].reshape(N_EXPERTS, EXPERT_FF, D_MODEL).astype(BF16)
    y = _expert_call(vis, hs, w["norm2"], wg, wu, wd, _num_visits(n))
    out = _combine_call(pos0, pos1, h, rt, y)
    return out.reshape(batch, seq, D_MODEL)
```

```python
import functools
import math

import jax
import jax.numpy as jnp
from jax import lax
from jax.experimental import pallas as pl
from jax.experimental.pallas import tpu as pltpu

F32 = jnp.float32
BF16 = jnp.bfloat16

D_MODEL = 1024
N_META = 16
BLOCK = 128
ATTN_HEADS = 8
ATTN_KV_HEADS = 2
HEAD_DIM = 64
ROPE_THETA = 10000.0
ATTN_Q_W = ATTN_HEADS * HEAD_DIM
ATTN_KV_W = ATTN_KV_HEADS * HEAD_DIM
GLA_HEADS = 4
GLA_KEY_DIM = 64
GLA_VAL_DIM = 128
GLA_K_W = GLA_HEADS * GLA_KEY_DIM
GLA_V_W = GLA_HEADS * GLA_VAL_DIM
GLA_GATE_RANK = 16
GLA_GATE_NORM = 16.0
GLA_CHUNK = 64
N_GROUPS = 4
EXPERTS_PER_GROUP = 8
N_EXPERTS = N_GROUPS * EXPERTS_PER_GROUP
EXPERT_FF = 256
NORM_EPS = 1e-6
MASK_VALUE = -1e30

LANES = 128
ATTN_KV_DUP_W = 2 * ATTN_KV_W
VMEM_LIMIT = 56 * 1024 * 1024

_OFF_Q = 0
_OFF_K = _OFF_Q + ATTN_Q_W
_OFF_V = _OFF_K + ATTN_KV_W
_OFF_GQ = _OFF_V + ATTN_KV_W
_OFF_GD = _OFF_GQ + 2 * GLA_K_W + 2 * GLA_V_W
_OFF_GATE = _OFF_GD + GLA_GATE_RANK
GLA_W = 2 * GLA_K_W + 2 * GLA_V_W


def _dot(a, b):
    return jnp.dot(a, b, preferred_element_type=F32)


def _dot_nt(a, b):
    return lax.dot_general(a, b, (((1,), (1,)), ((), ())), preferred_element_type=F32)


def _split_hi_lo(x):
    hi = x.astype(BF16)
    lo = (x - hi.astype(F32)).astype(BF16)
    return hi, lo


def _group_mean_sq(x, group):
    w = x.shape[-1]
    shift = int(math.log2(group))
    r = lax.broadcasted_iota(jnp.int32, (w, w), 0) >> shift
    c = lax.broadcasted_iota(jnp.int32, (w, w), 1) >> shift
    ones = jnp.where(r == c, 1.0, 0.0).astype(BF16)
    hi, lo = _split_hi_lo(x * x)
    return (_dot(hi, ones) + _dot(lo, ones)) * (1.0 / group)


def _rope(x, cos, sin_signed):
    w = x.shape[-1]
    reps = w // LANES
    if reps > 1:
        cos = jnp.concatenate([cos] * reps, axis=1)
        sin_signed = jnp.concatenate([sin_signed] * reps, axis=1)
    lane = lax.broadcasted_iota(jnp.int32, x.shape, 1)
    first_half = (lane & (HEAD_DIM // 2)) == 0
    swapped = jnp.where(first_half, pltpu.roll(x, w - HEAD_DIM // 2, 1), pltpu.roll(x, HEAD_DIM // 2, 1))
    return x * cos + swapped * sin_signed


def _rms_norm(x, w):
    ms = jnp.mean(x * x, axis=-1, keepdims=True)
    return x * lax.rsqrt(ms + NORM_EPS) * w


def _sigmoid(x):
    return 0.5 * jnp.tanh(0.5 * x) + 0.5


def _log_sigmoid(x):
    return jnp.minimum(x, 0.0) - jnp.log(1.0 + jnp.exp(-jnp.abs(x)))


def _proj_body(x_ref, n1_ref, wq_ref, wk_ref, wv_ref, wgla_ref, wgd_ref, wgate_ref,
               qn_ref, kn_ref, cos_ref, sin_ref, gup_ref, gb_ref,
               q_out, k_out, v_out, qg_out, kg_out, vg_out, rg_out, gk_out, sga_out, sgg_out):
    xn = _rms_norm(x_ref[...], n1_ref[...]).astype(BF16)
    cos = cos_ref[...]
    sin = sin_ref[...]

    q = _dot(xn, wq_ref[...])
    q = q * lax.rsqrt(_group_mean_sq(q, HEAD_DIM) + NORM_EPS) * qn_ref[...]
    q_out[...] = (_rope(q, cos, sin) * (HEAD_DIM ** -0.5)).astype(BF16)

    k = _dot(xn, wk_ref[...])
    k = k * lax.rsqrt(_group_mean_sq(k, HEAD_DIM) + NORM_EPS) * kn_ref[...]
    k_out[...] = _rope(k, cos, sin).astype(BF16)

    v_out[...] = _dot(xn, wv_ref[...]).astype(BF16)

    g = _dot(xn, wgla_ref[...])
    qg_out[...] = (g[:, :GLA_K_W] * (GLA_KEY_DIM ** -0.5)).astype(BF16)
    kg_out[...] = g[:, GLA_K_W:2 * GLA_K_W].astype(BF16)
    vg_out[...] = g[:, 2 * GLA_K_W:2 * GLA_K_W + GLA_V_W].astype(BF16)
    r = g[:, 2 * GLA_K_W + GLA_V_W:]
    rg_out[...] = (r * _sigmoid(r)).astype(BF16)

    gd = _dot(xn, wgd_ref[...]).astype(BF16)
    z = _dot(gd, gup_ref[...]) + gb_ref[...]
    gk_out[...] = _log_sigmoid(z) * (1.0 / GLA_GATE_NORM)

    gates = _dot(xn, wgate_ref[...])
    sga_out[...] = _sigmoid(gates[:, :D_MODEL]).astype(BF16)
    sgg_out[...] = _sigmoid(gates[:, D_MODEL:]).astype(BF16)


def _const_spec(shape):
    nd = len(shape)
    return pl.BlockSpec(shape, lambda *_: (0,) * nd)


def _proj_call(x2d, tm, tiles_per_seq, w, cos, sin):
    n = x2d.shape[0]
    grid = (n // tm,)
    row = lambda width: pl.BlockSpec((tm, width), lambda i: (i, 0))
    tab = pl.BlockSpec((tm, LANES), lambda i: (i % tiles_per_seq, 0))
    in_specs = [
        row(D_MODEL), _const_spec((1, D_MODEL)),
        _const_spec((D_MODEL, ATTN_Q_W)), _const_spec((D_MODEL, ATTN_KV_DUP_W)),
        _const_spec((D_MODEL, ATTN_KV_DUP_W)), _const_spec((D_MODEL, GLA_W)),
        _const_spec((D_MODEL, LANES)), _const_spec((D_MODEL, 2 * D_MODEL)),
        _const_spec((1, ATTN_Q_W)), _const_spec((1, ATTN_KV_DUP_W)),
        tab, tab, _const_spec((LANES, GLA_K_W)), _const_spec((1, GLA_K_W)),
    ]
    out_widths = [ATTN_Q_W, ATTN_KV_DUP_W, ATTN_KV_DUP_W, GLA_K_W, GLA_K_W, GLA_V_W, GLA_V_W,
                  GLA_K_W, D_MODEL, D_MODEL]
    out_dtypes = [BF16] * 7 + [F32] + [BF16] * 2
    return pl.pallas_call(
        _proj_body,
        grid=grid,
        in_specs=in_specs,
        out_specs=[row(wd) for wd in out_widths],
        out_shape=[jax.ShapeDtypeStruct((n, wd), dt) for wd, dt in zip(out_widths, out_dtypes)],
        compiler_params=pltpu.CompilerParams(dimension_semantics=("arbitrary",),
                                             vmem_limit_bytes=VMEM_LIMIT),
        name="proj",
    )(x2d, w["norm1"], w["wq"], w["wk"], w["wv"], w["wgla"], w["wgd"], w["wgate"],
      w["qn"], w["kn"], cos, sin, w["gup"], w["gb"])


def _attn_body(sink_ref, q_ref, kp_ref, kc_ref, km_ref, vp_ref, vc_ref, vm_ref, o_ref):
    i = pl.program_id(1)
    lane = lax.broadcasted_iota(jnp.int32, (BLOCK, LANES), 1)
    low = lane < HEAD_DIM
    key = lax.broadcasted_iota(jnp.int32, (BLOCK, 2 * LANES), 1) & (LANES - 1)
    rowi = lax.broadcasted_iota(jnp.int32, (BLOCK, 2 * LANES), 0)
    mask_meta = key < N_META
    mask_prev = jnp.logical_and(key > rowi, i > 0)
    mask_cur = key <= rowi
    masks = (mask_meta, mask_prev, mask_cur)
    zero = jnp.zeros((), BF16)

    for kvh in range(ATTN_KV_HEADS):
        sl = slice(kvh * LANES, (kvh + 1) * LANES)
        kblks = []
        vblks = []
        for k_ref, v_ref in ((km_ref, vm_ref), (kp_ref, vp_ref), (kc_ref, vc_ref)):
            kp = k_ref[:, sl]
            vp = v_ref[:, sl]
            kblks.append(jnp.concatenate([jnp.where(low, kp, zero), jnp.where(low, zero, kp)], axis=0))
            vblks.append(jnp.concatenate([jnp.where(low, vp, zero), jnp.where(low, zero, vp)], axis=0))
        for pair in range(2):
            j = kvh * 2 + pair
            q2 = q_ref[:, j * LANES:(j + 1) * LANES]
            s_parts = []
            for part in range(3):
                s = _dot_nt(q2, kblks[part])
                s_parts.append(jnp.where(masks[part], s, MASK_VALUE))
            p_heads = []
            inv_l = []
            for hh in range(2):
                sink = sink_ref[2 * j + hh]
                sh = [sp[:, hh * LANES:(hh + 1) * LANES] for sp in s_parts]
                m = jnp.maximum(jnp.maximum(jnp.max(sh[0], axis=1, keepdims=True),
                                            jnp.max(sh[1], axis=1, keepdims=True)),
                                jnp.max(sh[2], axis=1, keepdims=True))
                m = jnp.maximum(m, sink)
                ph = [jnp.exp(s - m) for s in sh]
                l = (jnp.sum(ph[0], axis=1, keepdims=True) + jnp.sum(ph[1], axis=1, keepdims=True)
                     + jnp.sum(ph[2], axis=1, keepdims=True) + jnp.exp(sink - m))
                p_heads.append(ph)
                inv_l.append(1.0 / l)
            o2 = jnp.zeros((BLOCK, LANES), F32)
            for part in range(3):
                p2 = jnp.concatenate([p_heads[0][part], p_heads[1][part]], axis=1).astype(BF16)
                o2 = o2 + _dot(p2, vblks[part])
            o2 = o2 * jnp.where(low, inv_l[0], inv_l[1])
            o_ref[:, j * LANES:(j + 1) * LANES] = o2.astype(BF16)


def _attn_call(q, kd, vd, km, vm, sinks, batch, seq):
    nb = seq // BLOCK
    qspec = pl.BlockSpec((None, BLOCK, ATTN_Q_W), lambda b, i, s: (b, i, 0))
    prev = pl.BlockSpec((None, BLOCK, ATTN_KV_DUP_W), lambda b, i, s: (b, jnp.maximum(i - 1, 0), 0))
    cur = pl.BlockSpec((None, BLOCK, ATTN_KV_DUP_W), lambda b, i, s: (b, i, 0))
    meta = pl.BlockSpec((BLOCK, ATTN_KV_DUP_W), lambda b, i, s: (0, 0))
    grid_spec = pltpu.PrefetchScalarGridSpec(
        num_scalar_prefetch=1,
        grid=(batch, nb),
        in_specs=[qspec, prev, cur, meta, prev, cur, meta],
        out_specs=pl.BlockSpec((None, BLOCK, ATTN_Q_W), lambda b, i, s: (b, i, 0)),
    )
    return pl.pallas_call(
        _attn_body,
        grid_spec=grid_spec,
        out_shape=jax.ShapeDtypeStruct((batch, seq, ATTN_Q_W), BF16),
        compiler_params=pltpu.CompilerParams(dimension_semantics=("arbitrary", "arbitrary"),
                                             vmem_limit_bytes=VMEM_LIMIT),
        name="attn",
    )(sinks, q, kd, kd, km, vd, vd, vm)


GLA_STEP = 2 * GLA_CHUNK
GLA_PAIR_K = 2 * GLA_KEY_DIM
GLA_PAIR_V = 2 * GLA_VAL_DIM


def _gla_consts():
    r = lax.broadcasted_iota(jnp.int32, (GLA_STEP, 2 * GLA_STEP), 0)
    c = lax.broadcasted_iota(jnp.int32, (GLA_STEP, 2 * GLA_STEP), 1) & (GLA_STEP - 1)
    causal = jnp.logical_and((r >> 6) == (c >> 6), c <= r)
    tri = jnp.where(causal[:, :GLA_STEP], 1.0, 0.0).astype(BF16)
    return causal, tri


def _gla_step(q2, k2, v2, gk2, s_prev, causal, tri):
    row = lax.broadcasted_iota(jnp.int32, (GLA_STEP, GLA_PAIR_K), 0)
    lane = lax.broadcasted_iota(jnp.int32, (GLA_STEP, GLA_PAIR_K), 1)
    first = row < GLA_CHUNK
    low = lane < GLA_KEY_DIM
    g_hi = gk2.astype(BF16)
    g_r = gk2 - g_hi.astype(F32)
    g_mid = g_r.astype(BF16)
    g_lo = (g_r - g_mid.astype(F32)).astype(BF16)
    b = _dot(tri, g_hi) + _dot(tri, g_mid) + _dot(tri, g_lo)
    half = GLA_CHUNK // 2
    b_mid = jnp.where(first, b[half:half + 1, :], b[GLA_CHUNK + half:GLA_CHUNK + half + 1, :])
    qi = (q2 * jnp.exp(b - b_mid)).astype(BF16)
    ki = (k2 * jnp.exp(b_mid - b)).astype(BF16)
    qx = q2 * jnp.exp(b)
    zero = jnp.zeros((), BF16)
    kblk = jnp.concatenate([jnp.where(low, ki, zero), jnp.where(low, zero, ki)], axis=0)
    att = _dot_nt(qi, kblk)
    att = jnp.where(causal, att, 0.0).astype(BF16)
    zv = jnp.zeros((GLA_STEP, GLA_VAL_DIM), BF16)
    vblk = jnp.concatenate([jnp.concatenate([v2[:, :GLA_VAL_DIM], zv], axis=1),
                            jnp.concatenate([zv, v2[:, GLA_VAL_DIM:]], axis=1)], axis=0)
    o = _dot(att, vblk)
    b_t = b.T
    k_t = k2.T
    lane_t = lax.broadcasted_iota(jnp.int32, (GLA_PAIR_K, GLA_STEP), 1)
    first_t = lane_t < GLA_CHUNK
    bl_a = b_t[:, GLA_CHUNK - 1:GLA_CHUNK]
    bl_b = b_t[:, GLA_STEP - 1:GLA_STEP]
    kx_t = k_t * jnp.exp(jnp.where(first_t, bl_a, bl_b) - b_t)
    zf = jnp.zeros((), F32)
    kx_a = jnp.where(first_t, kx_t, zf).astype(BF16)
    kx_b = jnp.where(first_t, zf, kx_t).astype(BF16)
    srow = lax.broadcasted_iota(jnp.int32, (GLA_PAIR_K, GLA_PAIR_V), 0)
    slane = lax.broadcasted_iota(jnp.int32, (GLA_PAIR_K, GLA_PAIR_V), 1)
    diag = (srow < GLA_KEY_DIM) == (slane < GLA_VAL_DIM)
    s_a = jnp.exp(bl_a) * s_prev + jnp.where(diag, _dot(kx_a, v2), 0.0)
    s_b = jnp.exp(bl_b) * s_a + jnp.where(diag, _dot(kx_b, v2), 0.0)
    o = o + _dot(jnp.where(first, qx, zf).astype(BF16), s_prev.astype(BF16))
    o = o + _dot(jnp.where(first, zf, qx).astype(BF16), s_a.astype(BF16))
    return o, s_b


def _gla_init_body(k_ref, v_ref, gk_ref, s_out):
    causal, tri = _gla_consts()
    for p in range(GLA_HEADS // 2):
        k2 = k_ref[:, p * GLA_PAIR_K:(p + 1) * GLA_PAIR_K].astype(F32)
        v2 = v_ref[:, p * GLA_PAIR_V:(p + 1) * GLA_PAIR_V]
        gk2 = gk_ref[:, p * GLA_PAIR_K:(p + 1) * GLA_PAIR_K]
        s0 = jnp.zeros((GLA_PAIR_K, GLA_PAIR_V), F32)
        _, s = _gla_step(jnp.zeros_like(k2), k2, v2, gk2, s0, causal, tri)
        s_out[p] = s


def _gla_init_call(kg_m, vg_m, gk_m):
    return pl.pallas_call(
        _gla_init_body,
        out_shape=jax.ShapeDtypeStruct((GLA_HEADS // 2, GLA_PAIR_K, GLA_PAIR_V), F32),
        name="gla_init",
    )(kg_m, vg_m, gk_m)


def _gla_body(s0_ref, q_ref, k_ref, v_ref, gk_ref, rg_ref, nw_ref, o_ref, s_ref):
    t = pl.program_id(1)

    @pl.when(t == 0)
    def _():
        s_ref[...] = s0_ref[...]

    causal, tri = _gla_consts()
    nw = nw_ref[...]
    for p in range(GLA_HEADS // 2):
        ks = slice(p * GLA_PAIR_K, (p + 1) * GLA_PAIR_K)
        vs = slice(p * GLA_PAIR_V, (p + 1) * GLA_PAIR_V)
        o, s_new = _gla_step(q_ref[:, ks].astype(F32), k_ref[:, ks].astype(F32), v_ref[:, vs],
                             gk_ref[:, ks], s_ref[p], causal, tri)
        s_ref[p] = s_new
        outs = []
        for hh in range(2):
            oh = o[:, hh * GLA_VAL_DIM:(hh + 1) * GLA_VAL_DIM]
            ms = jnp.mean(oh * oh, axis=-1, keepdims=True)
            outs.append(oh * lax.rsqrt(ms + NORM_EPS) * nw)
        on = jnp.concatenate(outs, axis=1)
        o_ref[:, vs] = (on * rg_ref[:, vs].astype(F32)).astype(BF16)


def _gla_call(s0, qg, kg, vg, gk, rg, nw, batch, seq):
    steps = seq // GLA_STEP
    spec = lambda width: pl.BlockSpec((None, GLA_STEP, width), lambda b, t: (b, t, 0))
    return pl.pallas_call(
        _gla_body,
        grid=(batch, steps),
        in_specs=[_const_spec((GLA_HEADS // 2, GLA_PAIR_K, GLA_PAIR_V)),
                  spec(GLA_K_W), spec(GLA_K_W), spec(GLA_V_W), spec(GLA_K_W), spec(GLA_V_W),
                  _const_spec((1, GLA_VAL_DIM))],
        out_specs=spec(GLA_V_W),
        out_shape=jax.ShapeDtypeStruct((batch, seq, GLA_V_W), BF16),
        scratch_shapes=[pltpu.VMEM((GLA_HEADS // 2, GLA_PAIR_K, GLA_PAIR_V), F32)],
        compiler_params=pltpu.CompilerParams(dimension_semantics=("arbitrary", "arbitrary"),
                                             vmem_limit_bytes=VMEM_LIMIT),
        name="gla",
    )(s0, qg, kg, vg, gk, rg, nw)


ROUTER_W = LANES


RT_E0, RT_E1, RT_W0, RT_W1 = 0, 1, 2, 3


def _merge_body(x_ref, ao_ref, go_ref, sga_ref, sgg_ref, wab_ref, wgb_ref, wo_ref, n2_ref,
                wrh_ref, wrl_ref, h_out, rt_out, rtt_out, cnt_out):
    ya = _dot(ao_ref[...], wab_ref[...])
    yg = _dot(go_ref[...], wgb_ref[...])
    merged = sga_ref[...].astype(F32) * ya + sgg_ref[...].astype(F32) * yg
    h = x_ref[...] + _dot(merged.astype(BF16), wo_ref[...])
    h_out[...] = h
    xn = _rms_norm(h, n2_ref[...])

    xh, xl = _split_hi_lo(xn)
    wrh = wrh_ref[...]
    lg = _dot(xh, wrh) + _dot(xl, wrh) + _dot(xh, wrl_ref[...])
    lg_t = lg.T
    tm = lg.shape[0]
    neg = -jnp.inf
    big = jnp.int32(1 << 20)
    gl = lg_t[N_EXPERTS:N_EXPERTS + N_GROUPS, :]
    grow = lax.broadcasted_iota(jnp.int32, (N_GROUPS, tm), 0)
    gmax = jnp.max(gl, axis=0, keepdims=True)
    g_idx = jnp.min(jnp.where(gl == gmax, grow, big), axis=0, keepdims=True)
    g_w = 1.0 / jnp.sum(jnp.exp(gl - gmax), axis=0, keepdims=True)
    erow = lax.broadcasted_iota(jnp.int32, (N_EXPERTS, tm), 0)
    el = jnp.where((erow >> 3) == g_idx, lg_t[0:N_EXPERTS, :], neg)
    m1 = jnp.max(el, axis=0, keepdims=True)
    i1 = jnp.min(jnp.where(el == m1, erow, big), axis=0, keepdims=True)
    el2 = jnp.where(erow == i1, neg, el)
    m2 = jnp.max(el2, axis=0, keepdims=True)
    i2 = jnp.min(jnp.where(el2 == m2, erow, big), axis=0, keepdims=True)
    e2 = jnp.exp(m2 - m1)
    p1 = g_w / (1.0 + e2)
    p2 = g_w * e2 / (1.0 + e2)
    def record(rows):
        r = lax.broadcasted_iota(jnp.int32, (rows, tm), 0)
        return jnp.where(r == RT_E0, i1.astype(F32),
                         jnp.where(r == RT_E1, i2.astype(F32),
                                   jnp.where(r == RT_W0, p1, jnp.where(r == RT_W1, p2, 0.0))))
    rtt_out[...] = record(8)
    rt_out[...] = record(LANES).T
    picked = jnp.where(jnp.logical_or(erow == i1, erow == i2), 1.0, 0.0)

    @pl.when(pl.program_id(0) == 0)
    def _():
        cnt_out[...] = jnp.zeros_like(cnt_out)

    cnt_out[...] += jnp.sum(picked, axis=1, keepdims=True)


def _merge_call(x2d, ao, go, sga, sgg, w, tm):
    n = x2d.shape[0]
    row = lambda width: pl.BlockSpec((tm, width), lambda i: (i, 0))
    return pl.pallas_call(
        _merge_body,
        grid=(n // tm,),
        in_specs=[row(D_MODEL), row(ATTN_Q_W), row(GLA_V_W), row(D_MODEL), row(D_MODEL),
                  _const_spec((ATTN_Q_W, D_MODEL)), _const_spec((GLA_V_W, D_MODEL)),
                  _const_spec((D_MODEL, D_MODEL)), _const_spec((1, D_MODEL)),
                  _const_spec((D_MODEL, ROUTER_W)), _const_spec((D_MODEL, ROUTER_W))],
        out_specs=[row(D_MODEL), row(ROUTER_W), pl.BlockSpec((8, tm), lambda i: (0, i)),
                   _const_spec((N_EXPERTS, LANES))],
        out_shape=[jax.ShapeDtypeStruct((n, D_MODEL), F32),
                   jax.ShapeDtypeStruct((n, ROUTER_W), F32),
                   jax.ShapeDtypeStruct((8, n), F32),
                   jax.ShapeDtypeStruct((N_EXPERTS, LANES), F32)],
        compiler_params=pltpu.CompilerParams(dimension_semantics=("arbitrary",),
                                             vmem_limit_bytes=VMEM_LIMIT),
        name="merge",
    )(x2d, ao, go, sga, sgg, w["wab"], w["wgb"], w["wo"], w["norm2"], w["wrh"], w["wrl"])


EXPERT_TILE = 512
ROUTE_TILE = 512
VISIT_CAP = 256
VIS_EXPERT, VIS_TILE, VIS_LO, VIS_HI, VIS_COUNT = 0, 1, 2, 3, 4


def _num_visits(n_tokens):
    return (2 * n_tokens) // EXPERT_TILE + N_EXPERTS - 1


def _expert_cumsum_inclusive(x):
    row = lax.broadcasted_iota(jnp.int32, x.shape, 0)
    s = 1
    while s < N_EXPERTS:
        x = x + jnp.where(row >= s, pltpu.roll(x, s, 0), 0.0)
        s *= 2
    return x


def _route_body(rtt_ref, cnt_ref, pos_ref, vis_ref, carry_ref):
    i = pl.program_id(0)
    tm = rtt_ref.shape[1]
    cnt = cnt_ref[...]
    end = _expert_cumsum_inclusive(cnt)
    base = end - cnt

    @pl.when(i == 0)
    def _():
        carry_ref[...] = jnp.zeros_like(carry_ref)
        inv = 1.0 / EXPERT_TILE
        first = jnp.floor(base * inv)
        n_vis = jnp.where(cnt > 0.0, jnp.floor((end - 1.0) * inv) - first + 1.0, 0.0)
        vend = _expert_cumsum_inclusive(n_vis)
        vstart = vend - n_vis
        col = lambda a: a[:, 0:1]
        erow = lax.broadcasted_iota(jnp.int32, (N_EXPERTS, VISIT_CAP), 0).astype(F32)
        v = lax.broadcasted_iota(jnp.int32, (N_EXPERTS, VISIT_CAP), 1).astype(F32)
        ev = jnp.minimum(jnp.sum(jnp.where(col(vend) <= v, 1.0, 0.0), axis=0, keepdims=True),
                         N_EXPERTS - 1.0)
        mine = erow == ev
        pick = lambda a: jnp.sum(jnp.where(mine, a, 0.0), axis=0, keepdims=True)
        tile = pick(col(first) + v - col(vstart))
        lo = jnp.maximum(pick(col(base)) - tile * EXPERT_TILE, 0.0)
        hi = jnp.minimum(pick(col(end)) - tile * EXPERT_TILE, float(EXPERT_TILE))
        total = vend[N_EXPERTS - 1:N_EXPERTS, 0:1]
        r8 = lax.broadcasted_iota(jnp.int32, (8, VISIT_CAP), 0)
        rec = jnp.where(r8 == VIS_EXPERT, ev,
                        jnp.where(r8 == VIS_TILE, tile,
                                  jnp.where(r8 == VIS_LO, lo,
                                            jnp.where(r8 == VIS_HI, hi,
                                                      jnp.where(r8 == VIS_COUNT, total, 0.0)))))
        vis_ref[...] = rec.astype(jnp.int32)

    rtt = rtt_ref[...]
    erow = lax.broadcasted_iota(jnp.int32, (N_EXPERTS, tm), 0).astype(F32)
    oh0 = erow == rtt[RT_E0:RT_E0 + 1, :]
    oh1 = erow == rtt[RT_E1:RT_E1 + 1, :]
    picked = jnp.where(jnp.logical_or(oh0, oh1), 1.0, 0.0)
    r = lax.broadcasted_iota(jnp.int32, (tm, tm), 0)
    c = lax.broadcasted_iota(jnp.int32, (tm, tm), 1)
    earlier = jnp.where(r < c, 1.0, 0.0).astype(BF16)
    slot = _dot(picked.astype(BF16), earlier) + (carry_ref[:, 0:1] + base[:, 0:1])
    p0 = jnp.sum(jnp.where(oh0, slot, 0.0), axis=0, keepdims=True)
    p1 = jnp.sum(jnp.where(oh1, slot, 0.0), axis=0, keepdims=True)
    r8 = lax.broadcasted_iota(jnp.int32, (8, tm), 0)
    pos_ref[...] = jnp.where(r8 == 0, p0, jnp.where(r8 == 1, p1, 0.0)).astype(jnp.int32)
    carry_ref[...] += jnp.sum(picked, axis=1, keepdims=True)


def _route_call(rtt, cnt):
    n = rtt.shape[1]
    tm = _pick_tile(n, ROUTE_TILE)
    return pl.pallas_call(
        _route_body,
        grid=(n // tm,),
        in_specs=[pl.BlockSpec((8, tm), lambda i: (0, i)), _const_spec((N_EXPERTS, LANES))],
        out_specs=[pl.BlockSpec((8, tm), lambda i: (0, i)), _const_spec((8, VISIT_CAP))],
        out_shape=[jax.ShapeDtypeStruct((8, n), jnp.int32),
                   jax.ShapeDtypeStruct((8, VISIT_CAP), jnp.int32)],
        scratch_shapes=[pltpu.VMEM((N_EXPERTS, LANES), F32)],
        compiler_params=pltpu.CompilerParams(dimension_semantics=("arbitrary",)),
        name="route",
    )(rtt, cnt)


SCATTER_TILE = 512


def _scatter_body(pos0_ref, pos1_ref, h_ref, hs_hbm, sem):
    tm = h_ref.shape[0]

    def issue(r, carry):
        src = h_ref.at[pl.ds(r, 1)]
        pltpu.make_async_copy(src, hs_hbm.at[pl.ds(pos0_ref[r], 1)], sem).start(priority=0)
        pltpu.make_async_copy(src, hs_hbm.at[pl.ds(pos1_ref[r], 1)], sem).start(priority=1)
        return carry

    lax.fori_loop(0, tm, issue, 0, unroll=8)
    for _ in range(2):
        pltpu.make_async_copy(h_ref, hs_hbm.at[pl.ds(0, tm)], sem).wait()


def _scatter_call(pos0, pos1, h, n_slots):
    n = h.shape[0]
    tm = _pick_tile(n, SCATTER_TILE)
    smem = lambda: pl.BlockSpec((tm,), lambda i: (i,), memory_space=pltpu.SMEM)
    return pl.pallas_call(
        _scatter_body,
        grid=(n // tm,),
        in_specs=[smem(), smem(), pl.BlockSpec((tm, D_MODEL), lambda i: (i, 0))],
        out_specs=pl.BlockSpec(memory_space=pl.ANY),
        out_shape=jax.ShapeDtypeStruct((n_slots, D_MODEL), F32),
        scratch_shapes=[pltpu.SemaphoreType.DMA],
        compiler_params=pltpu.CompilerParams(dimension_semantics=("arbitrary",)),
        name="scatter",
    )(pos0, pos1, h)


def _expert_body(vis_ref, h_ref, n2_ref, wg_ref, wu_ref, wd_ref, o_ref):
    j = pl.program_id(0)

    @pl.when(j < vis_ref[VIS_COUNT, 0])
    def _():
        lo_row = vis_ref[VIS_LO, j]
        hi_row = vis_ref[VIS_HI, j]
        x = _rms_norm(h_ref[...], n2_ref[...]).astype(BF16)
        g = _dot(x, wg_ref[...].astype(BF16))
        u = _dot(x, wu_ref[...].astype(BF16))
        hg = (g * _sigmoid(g) * u).astype(BF16)
        new = _dot(hg, wd_ref[...].astype(BF16))
        row = lax.broadcasted_iota(jnp.int32, new.shape, 0)

        @pl.when(lo_row == 0)
        def _():
            o_ref[...] = jnp.where(row < hi_row, new, 0.0)

        @pl.when(lo_row > 0)
        def _():
            o_ref[...] = jnp.where(jnp.logical_and(row >= lo_row, row < hi_row), new, o_ref[...])


def _expert_call(vis, hs, n2, wg, wu, wd, n_visits):
    def visit(j, vis):
        return jnp.minimum(j, vis[VIS_COUNT, 0] - 1)

    def xmap(j, vis):
        return (vis[VIS_TILE, visit(j, vis)], 0)

    def wmap(j, vis):
        return (vis[VIS_EXPERT, visit(j, vis)], 0, 0)

    grid_spec = pltpu.PrefetchScalarGridSpec(
        num_scalar_prefetch=1,
        grid=(n_visits,),
        in_specs=[pl.BlockSpec((EXPERT_TILE, D_MODEL), xmap),
                  pl.BlockSpec((1, D_MODEL), lambda j, vis: (0, 0)),
                  pl.BlockSpec((None, D_MODEL, EXPERT_FF), wmap),
                  pl.BlockSpec((None, D_MODEL, EXPERT_FF), wmap),
                  pl.BlockSpec((None, EXPERT_FF, D_MODEL), wmap)],
        out_specs=pl.BlockSpec((EXPERT_TILE, D_MODEL), xmap),
    )
    return pl.pallas_call(
        _expert_body,
        grid_spec=grid_spec,
        out_shape=jax.ShapeDtypeStruct(hs.shape, F32),
        compiler_params=pltpu.CompilerParams(dimension_semantics=("arbitrary",),
                                             vmem_limit_bytes=VMEM_LIMIT),
        name="experts",
    )(vis, hs, n2, wg, wu, wd)


COMBINE_TILE = 256


def _combine_body(p0_ref, p1_ref, p0n_ref, p1n_ref, h_ref, rt_ref, y_hbm, o_ref, g_ref, sem):
    i = pl.program_id(0)
    tm = h_ref.shape[0]
    cur = i % 2

    def issue(q0_ref, q1_ref, buf):
        def one(r, carry):
            pltpu.make_async_copy(y_hbm.at[pl.ds(q0_ref[r], 1)], g_ref.at[buf, 0, pl.ds(r, 1)],
                                  sem.at[buf]).start(priority=0)
            pltpu.make_async_copy(y_hbm.at[pl.ds(q1_ref[r], 1)], g_ref.at[buf, 1, pl.ds(r, 1)],
                                  sem.at[buf]).start(priority=1)
            return carry
        lax.fori_loop(0, tm, one, 0, unroll=8)

    @pl.when(i == 0)
    def _():
        issue(p0_ref, p1_ref, 0)

    @pl.when(i + 1 < pl.num_programs(0))
    def _():
        issue(p0n_ref, p1n_ref, 1 - cur)

    for k in range(2):
        pltpu.make_async_copy(y_hbm.at[pl.ds(0, tm)], g_ref.at[cur, k], sem.at[cur]).wait()
    rt = rt_ref[...]
    o_ref[...] = (h_ref[...] + rt[:, RT_W0:RT_W0 + 1] * g_ref[cur, 0]
                  + rt[:, RT_W1:RT_W1 + 1] * g_ref[cur, 1])


def _combine_call(pos0, pos1, h, rt, y):
    n = h.shape[0]
    tm = _pick_tile(n, COMBINE_TILE)
    steps = n // tm
    smem_cur = lambda: pl.BlockSpec((tm,), lambda i: (i,), memory_space=pltpu.SMEM)
    smem_next = lambda: pl.BlockSpec((tm,), lambda i: (jnp.minimum(i + 1, steps - 1),),
                                     memory_space=pltpu.SMEM)
    row = lambda width: pl.BlockSpec((tm, width), lambda i: (i, 0))
    return pl.pallas_call(
        _combine_body,
        grid=(steps,),
        in_specs=[smem_cur(), smem_cur(), smem_next(), smem_next(), row(D_MODEL), row(ROUTER_W),
                  pl.BlockSpec(memory_space=pl.ANY)],
        out_specs=row(D_MODEL),
        out_shape=jax.ShapeDtypeStruct((n, D_MODEL), F32),
        scratch_shapes=[pltpu.VMEM((2, 2, tm, D_MODEL), F32), pltpu.SemaphoreType.DMA((2,))],
        compiler_params=pltpu.CompilerParams(dimension_semantics=("arbitrary",),
                                             vmem_limit_bytes=VMEM_LIMIT),
        name="combine",
    )(pos0, pos1, pos0, pos1, h, rt, y)


def _rope_tables(pos):
    half = HEAD_DIM // 2
    inv_freq = ROPE_THETA ** (-(jnp.arange(half, dtype=F32) * 2.0) / HEAD_DIM)
    ang = pos.astype(F32)[:, None] * inv_freq[None, :]
    cos = jnp.cos(ang)
    sin = jnp.sin(ang)
    cos64 = jnp.concatenate([cos, cos], axis=1)
    sin64 = jnp.concatenate([-sin, sin], axis=1)
    return jnp.tile(cos64, (1, LANES // HEAD_DIM)), jnp.tile(sin64, (1, LANES // HEAD_DIM))


def _dup_heads(wcols):
    d = wcols.shape[0]
    w3 = wcols.reshape(d, ATTN_KV_HEADS, 1, HEAD_DIM)
    return jnp.broadcast_to(w3, (d, ATTN_KV_HEADS, 2, HEAD_DIM)).reshape(d, ATTN_KV_DUP_W)


def _prep_weights(norm1_w, w_in, q_norm_w, k_norm_w, gla_gate_up, gla_gate_bias, w_attn_branch,
                  w_gla_branch, w_out, norm2_w, router_group, router_expert):
    wb = w_in.astype(BF16)
    wgd = jnp.zeros((D_MODEL, LANES), BF16).at[:, :GLA_GATE_RANK].set(wb[:, _OFF_GD:_OFF_GD + GLA_GATE_RANK])
    gup = jnp.zeros((LANES, GLA_K_W), BF16).at[:GLA_GATE_RANK].set(gla_gate_up.astype(BF16))
    wr = jnp.zeros((D_MODEL, ROUTER_W), F32)
    wr = wr.at[:, :N_EXPERTS].set(router_expert.transpose(1, 0, 2).reshape(D_MODEL, N_EXPERTS))
    wr = wr.at[:, N_EXPERTS:N_EXPERTS + N_GROUPS].set(router_group)
    wrh = wr.astype(BF16)
    wrl = (wr - wrh.astype(F32)).astype(BF16)
    return {
        "norm1": norm1_w.reshape(1, D_MODEL),
        "wq": wb[:, _OFF_Q:_OFF_Q + ATTN_Q_W],
        "wk": _dup_heads(wb[:, _OFF_K:_OFF_K + ATTN_KV_W]),
        "wv": _dup_heads(wb[:, _OFF_V:_OFF_V + ATTN_KV_W]),
        "wgla": wb[:, _OFF_GQ:_OFF_GQ + GLA_W],
        "wgd": wgd,
        "wgate": wb[:, _OFF_GATE:_OFF_GATE + 2 * D_MODEL],
        "qn": jnp.tile(q_norm_w, ATTN_HEADS).reshape(1, ATTN_Q_W),
        "kn": jnp.tile(k_norm_w, ATTN_KV_DUP_W // HEAD_DIM).reshape(1, ATTN_KV_DUP_W),
        "gup": gup,
        "gb": gla_gate_bias.reshape(1, GLA_K_W),
        "wab": w_attn_branch.astype(BF16),
        "wgb": w_gla_branch.astype(BF16),
        "wo": w_out.astype(BF16),
        "norm2": norm2_w.reshape(1, D_MODEL),
        "wrh": wrh,
        "wrl": wrl,
    }


def _pick_tile(n, want):
    t = want
    while n % t:
        t //= 2
    return t


def kernel(x, meta_tokens, norm1_w, w_in, q_norm_w, k_norm_w, attn_sinks, gla_gate_up, gla_gate_bias,
           gla_norm_w, w_attn_branch, w_gla_branch, w_out, norm2_w, router_group, router_expert,
           expert_w_gate, expert_w_up, expert_w_down):
    batch, seq, d = x.shape
    assert d == D_MODEL and seq % GLA_STEP == 0 and norm1_w.shape[0] == 1
    n = batch * seq
    w = _prep_weights(norm1_w[0], w_in[0], q_norm_w[0], k_norm_w[0], gla_gate_up[0], gla_gate_bias[0],
                      w_attn_branch[0], w_gla_branch[0], w_out[0], norm2_w[0], router_group[0],
                      router_expert[0])
    x2d = x.reshape(n, D_MODEL)

    cos_m, sin_m = _rope_tables(jnp.arange(N_META))
    cos_r, sin_r = _rope_tables(jnp.arange(seq) + N_META)
    meta = _proj_call(meta_tokens.astype(F32), N_META, 1, w, cos_m, sin_m)
    tm = _pick_tile(seq, 512)
    (q, kd, vd, qg, kg, vg, rg, gk, sga, sgg) = _proj_call(x2d, tm, seq // tm, w, cos_r, sin_r)

    pad_after = ((0, BLOCK - N_META), (0, 0))
    km = jnp.pad(meta[1], pad_after)
    vm = jnp.pad(meta[2], pad_after)
    ao = _attn_call(q.reshape(batch, seq, ATTN_Q_W), kd.reshape(batch, seq, ATTN_KV_DUP_W),
                    vd.reshape(batch, seq, ATTN_KV_DUP_W), km, vm, attn_sinks[0].astype(F32), batch, seq)

    pad_before = ((GLA_STEP - N_META, 0), (0, 0))
    s0 = _gla_init_call(jnp.pad(meta[4], pad_before), jnp.pad(meta[5], pad_before),
                        jnp.pad(meta[7], pad_before))
    go = _gla_call(s0, qg.reshape(batch, seq, GLA_K_W), kg.reshape(batch, seq, GLA_K_W),
                   vg.reshape(batch, seq, GLA_V_W), gk.reshape(batch, seq, GLA_K_W),
                   rg.reshape(batch, seq, GLA_V_W), gla_norm_w[0].reshape(1, GLA_VAL_DIM), batch, seq)

    h, rt, rtt, cnt = _merge_call(x2d, ao.reshape(n, ATTN_Q_W), go.reshape(n, GLA_V_W), sga, sgg, w,
                                  _pick_tile(n, 512))

    assert (2 * n) % EXPERT_TILE == 0 and _num_visits(n) <= VISIT_CAP
    pos, vis = _route_call(rtt, cnt)
    pos0, pos1 = pos[0], pos[1]
    hs = _scatter_call(pos0, pos1, h, 2 * n)
    wg = expert_w_gate[0].reshape(N_EXPERTS, D_MODEL, EXPERT_FF)
    wu = expert_w_up[0].reshape(N_EXPERTS, D_MODEL, EXPERT_FF)
    wd = expert_w_down[0].reshape(N_EXPERTS, EXPERT_FF, D_MODEL)
    y = _expert_call(vis, hs, w["norm2"], wg, wu, wd, _num_visits(n))
    out = _combine_call(pos0, pos1, h, rt, y)
    return out.reshape(batch, seq, D_MODEL)
```

```python
import functools
import math

import jax
import jax.numpy as jnp
from jax import lax
from jax.experimental import pallas as pl
from jax.experimental.pallas import tpu as pltpu

F32 = jnp.float32
BF16 = jnp.bfloat16

D_MODEL = 1024
N_META = 16
BLOCK = 128
ATTN_HEADS = 8
ATTN_KV_HEADS = 2
HEAD_DIM = 64
ROPE_THETA = 10000.0
ATTN_Q_W = ATTN_HEADS * HEAD_DIM
ATTN_KV_W = ATTN_KV_HEADS * HEAD_DIM
GLA_HEADS = 4
GLA_KEY_DIM = 64
GLA_VAL_DIM = 128
GLA_K_W = GLA_HEADS * GLA_KEY_DIM
GLA_V_W = GLA_HEADS * GLA_VAL_DIM
GLA_GATE_RANK = 16
GLA_GATE_NORM = 16.0
GLA_CHUNK = 64
N_GROUPS = 4
EXPERTS_PER_GROUP = 8
N_EXPERTS = N_GROUPS * EXPERTS_PER_GROUP
EXPERT_FF = 256
NORM_EPS = 1e-6
MASK_VALUE = -1e30

LANES = 128
ATTN_KV_DUP_W = 2 * ATTN_KV_W
VMEM_LIMIT = 56 * 1024 * 1024

_OFF_Q = 0
_OFF_K = _OFF_Q + ATTN_Q_W
_OFF_V = _OFF_K + ATTN_KV_W
_OFF_GQ = _OFF_V + ATTN_KV_W
_OFF_GD = _OFF_GQ + 2 * GLA_K_W + 2 * GLA_V_W
_OFF_GATE = _OFF_GD + GLA_GATE_RANK
GLA_W = 2 * GLA_K_W + 2 * GLA_V_W


def _dot(a, b):
    return jnp.dot(a, b, preferred_element_type=F32)


def _dot_nt(a, b):
    return lax.dot_general(a, b, (((1,), (1,)), ((), ())), preferred_element_type=F32)


def _split_hi_lo(x):
    hi = x.astype(BF16)
    lo = (x - hi.astype(F32)).astype(BF16)
    return hi, lo


def _group_mean_sq(x, group):
    w = x.shape[-1]
    shift = int(math.log2(group))
    r = lax.broadcasted_iota(jnp.int32, (w, w), 0) >> shift
    c = lax.broadcasted_iota(jnp.int32, (w, w), 1) >> shift
    ones = jnp.where(r == c, 1.0, 0.0).astype(BF16)
    hi, lo = _split_hi_lo(x * x)
    return (_dot(hi, ones) + _dot(lo, ones)) * (1.0 / group)


def _rope(x, cos, sin_signed):
    w = x.shape[-1]
    reps = w // LANES
    if reps > 1:
        cos = jnp.concatenate([cos] * reps, axis=1)
        sin_signed = jnp.concatenate([sin_signed] * reps, axis=1)
    lane = lax.broadcasted_iota(jnp.int32, x.shape, 1)
    first_half = (lane & (HEAD_DIM // 2)) == 0
    swapped = jnp.where(first_half, pltpu.roll(x, w - HEAD_DIM // 2, 1), pltpu.roll(x, HEAD_DIM // 2, 1))
    return x * cos + swapped * sin_signed


def _rms_norm(x, w):
    ms = jnp.mean(x * x, axis=-1, keepdims=True)
    return x * lax.rsqrt(ms + NORM_EPS) * w


def _sigmoid(x):
    return 0.5 * jnp.tanh(0.5 * x) + 0.5


def _log_sigmoid(x):
    return jnp.minimum(x, 0.0) - jnp.log(1.0 + jnp.exp(-jnp.abs(x)))


def _proj_body(x_ref, n1_ref, wq_ref, wk_ref, wv_ref, wgla_ref, wgd_ref, wgate_ref,
               qn_ref, kn_ref, cos_ref, sin_ref, gup_ref, gb_ref,
               q_out, k_out, v_out, qg_out, kg_out, vg_out, rg_out, gk_out, sga_out, sgg_out):
    xn = _rms_norm(x_ref[...], n1_ref[...]).astype(BF16)
    cos = cos_ref[...]
    sin = sin_ref[...]

    q = _dot(xn, wq_ref[...])
    q = q * lax.rsqrt(_group_mean_sq(q, HEAD_DIM) + NORM_EPS) * qn_ref[...]
    q_out[...] = (_rope(q, cos, sin) * (HEAD_DIM ** -0.5)).astype(BF16)

    k = _dot(xn, wk_ref[...])
    k = k * lax.rsqrt(_group_mean_sq(k, HEAD_DIM) + NORM_EPS) * kn_ref[...]
    k_out[...] = _rope(k, cos, sin).astype(BF16)

    v_out[...] = _dot(xn, wv_ref[...]).astype(BF16)

    g = _dot(xn, wgla_ref[...])
    qg_out[...] = (g[:, :GLA_K_W] * (GLA_KEY_DIM ** -0.5)).astype(BF16)
    kg_out[...] = g[:, GLA_K_W:2 * GLA_K_W].astype(BF16)
    vg_out[...] = g[:, 2 * GLA_K_W:2 * GLA_K_W + GLA_V_W].astype(BF16)
    r = g[:, 2 * GLA_K_W + GLA_V_W:]
    rg_out[...] = (r * _sigmoid(r)).astype(BF16)

    gd = _dot(xn, wgd_ref[...]).astype(BF16)
    z = _dot(gd, gup_ref[...]) + gb_ref[...]
    gk_out[...] = _log_sigmoid(z) * (1.0 / GLA_GATE_NORM)

    gates = _dot(xn, wgate_ref[...])
    sga_out[...] = _sigmoid(gates[:, :D_MODEL]).astype(BF16)
    sgg_out[...] = _sigmoid(gates[:, D_MODEL:]).astype(BF16)


def _const_spec(shape):
    nd = len(shape)
    return pl.BlockSpec(shape, lambda *_: (0,) * nd)


def _proj_call(x2d, tm, tiles_per_seq, w, cos, sin):
    n = x2d.shape[0]
    grid = (n // tm,)
    row = lambda width: pl.BlockSpec((tm, width), lambda i: (i, 0))
    tab = pl.BlockSpec((tm, LANES), lambda i: (i % tiles_per_seq, 0))
    in_specs = [
        row(D_MODEL), _const_spec((1, D_MODEL)),
        _const_spec((D_MODEL, ATTN_Q_W)), _const_spec((D_MODEL, ATTN_KV_DUP_W)),
        _const_spec((D_MODEL, ATTN_KV_DUP_W)), _const_spec((D_MODEL, GLA_W)),
        _const_spec((D_MODEL, LANES)), _const_spec((D_MODEL, 2 * D_MODEL)),
        _const_spec((1, ATTN_Q_W)), _const_spec((1, ATTN_KV_DUP_W)),
        tab, tab, _const_spec((LANES, GLA_K_W)), _const_spec((1, GLA_K_W)),
    ]
    out_widths = [ATTN_Q_W, ATTN_KV_DUP_W, ATTN_KV_DUP_W, GLA_K_W, GLA_K_W, GLA_V_W, GLA_V_W,
                  GLA_K_W, D_MODEL, D_MODEL]
    out_dtypes = [BF16] * 7 + [F32] + [BF16] * 2
    return pl.pallas_call(
        _proj_body,
        grid=grid,
        in_specs=in_specs,
        out_specs=[row(wd) for wd in out_widths],
        out_shape=[jax.ShapeDtypeStruct((n, wd), dt) for wd, dt in zip(out_widths, out_dtypes)],
        compiler_params=pltpu.CompilerParams(dimension_semantics=("arbitrary",),
                                             vmem_limit_bytes=VMEM_LIMIT),
        name="proj",
    )(x2d, w["norm1"], w["wq"], w["wk"], w["wv"], w["wgla"], w["wgd"], w["wgate"],
      w["qn"], w["kn"], cos, sin, w["gup"], w["gb"])


ATTN_STEP_BLOCKS = 4


def _attn_body(sink_ref, q_ref, kp_ref, kc_ref, km_ref, vp_ref, vc_ref, vm_ref, o_ref):
    i = pl.program_id(1)
    nblk = ATTN_STEP_BLOCKS
    pair_rows = 2 * BLOCK
    lane = lax.broadcasted_iota(jnp.int32, (BLOCK, LANES), 1)
    low = lane < HEAD_DIM
    low2 = lax.broadcasted_iota(jnp.int32, (pair_rows, LANES), 1) < HEAD_DIM
    top2 = lax.broadcasted_iota(jnp.int32, (pair_rows, 1), 0) < BLOCK
    key = lax.broadcasted_iota(jnp.int32, (pair_rows, 2 * LANES), 1) & (LANES - 1)
    rowi = lax.broadcasted_iota(jnp.int32, (pair_rows, 2 * LANES), 0) & (BLOCK - 1)
    mask_meta = key < N_META
    mask_prev = key > rowi
    mask_first_prev = jnp.logical_and(mask_prev, i > 0)
    mask_cur = key <= rowi
    zero = jnp.zeros((), BF16)

    def block_diag(x):
        return jnp.concatenate([jnp.where(low, x, zero), jnp.where(low, zero, x)], axis=0)

    blk_row = lax.broadcasted_iota(jnp.int32, (2 * BLOCK, LANES), 0)
    blk_lane = lax.broadcasted_iota(jnp.int32, (2 * BLOCK, LANES), 1)
    ones_blk = jnp.where((blk_row < BLOCK) == (blk_lane < HEAD_DIM), 1.0, 0.0).astype(BF16)

    def masked(s, mask):
        return jnp.where(mask, s, MASK_VALUE)

    for kvh in range(ATTN_KV_HEADS):
        sl = slice(kvh * LANES, (kvh + 1) * LANES)
        k_seq = [block_diag(kp_ref[:, sl])]
        v_seq = [block_diag(vp_ref[:, sl])]
        for t in range(nblk):
            rows = slice(t * BLOCK, (t + 1) * BLOCK)
            k_seq.append(block_diag(kc_ref[rows, sl]))
            v_seq.append(block_diag(vc_ref[rows, sl]))
        q_all = jnp.concatenate(
            [q_ref[t * BLOCK:(t + 1) * BLOCK, (2 * kvh + pr) * LANES:(2 * kvh + pr + 1) * LANES]
             for t in range(nblk) for pr in range(2)], axis=0)
        s_meta = _dot_nt(q_all, block_diag(km_ref[:, sl]))
        s_seq = []
        for j in range(nblk + 1):
            lo_blk, hi_blk = max(j - 1, 0), min(j, nblk - 1)
            s_seq.append(_dot_nt(q_all[lo_blk * pair_rows:(hi_blk + 1) * pair_rows], k_seq[j]))

        def sink_col(hh):
            return jnp.where(top2, sink_ref[4 * kvh + hh], sink_ref[4 * kvh + 2 + hh])

        p_meta, p_prev, p_cur, l_all = [], [], [], []
        for t in range(nblk):
            sm = masked(s_meta[t * pair_rows:(t + 1) * pair_rows], mask_meta)
            prev_rows = slice(0, pair_rows) if t == 0 else slice(pair_rows, 2 * pair_rows)
            sp = masked(s_seq[t][prev_rows], mask_first_prev if t == 0 else mask_prev)
            sc = masked(s_seq[t + 1][0:pair_rows], mask_cur)
            s_max = jnp.maximum(jnp.maximum(sm, sp), sc)
            m2 = []
            sink_terms = []
            for hh in range(2):
                sink = sink_col(hh)
                m = jnp.maximum(jnp.max(s_max[:, hh * LANES:(hh + 1) * LANES], axis=1, keepdims=True), sink)
                m2.append(m)
                sink_terms.append(jnp.exp(sink - m))

            def probs(s):
                return jnp.concatenate([jnp.exp(s[:, hh * LANES:(hh + 1) * LANES] - m2[hh])
                                        for hh in range(2)], axis=1)

            pm, pp, pc = probs(sm), probs(sp), probs(sc)
            p_meta.append(pm.astype(BF16))
            p_prev.append(pp.astype(BF16))
            p_cur.append(pc.astype(BF16))
            l_all.append(_dot((pm + pp + pc).astype(BF16), ones_blk)
                         + jnp.where(low2, sink_terms[0], sink_terms[1]))

        o_meta = _dot(jnp.concatenate(p_meta, axis=0), block_diag(vm_ref[:, sl]))
        o_seq = []
        for j in range(nblk + 1):
            parts = ([p_cur[j - 1]] if j >= 1 else []) + ([p_prev[j]] if j < nblk else [])
            o_seq.append(_dot(jnp.concatenate(parts, axis=0) if len(parts) > 1 else parts[0], v_seq[j]))
        for t in range(nblk):
            prev_rows = slice(0, pair_rows) if t == 0 else slice(pair_rows, 2 * pair_rows)
            o = (o_meta[t * pair_rows:(t + 1) * pair_rows] + o_seq[t][prev_rows]
                 + o_seq[t + 1][0:pair_rows]) / l_all[t]
            for pr in range(2):
                col = (2 * kvh + pr) * LANES
                o_ref[t * BLOCK:(t + 1) * BLOCK, col:col + LANES] = o[pr * BLOCK:(pr + 1) * BLOCK].astype(BF16)


def _attn_call(q, kd, vd, km, vm, sinks, batch, seq):
    step = ATTN_STEP_BLOCKS * BLOCK
    assert seq % step == 0
    qspec = pl.BlockSpec((None, step, ATTN_Q_W), lambda b, i, s: (b, i, 0))
    prev = pl.BlockSpec((None, BLOCK, ATTN_KV_DUP_W),
                        lambda b, i, s: (b, jnp.maximum(i * ATTN_STEP_BLOCKS - 1, 0), 0))
    cur = pl.BlockSpec((None, step, ATTN_KV_DUP_W), lambda b, i, s: (b, i, 0))
    meta = pl.BlockSpec((BLOCK, ATTN_KV_DUP_W), lambda b, i, s: (0, 0))
    grid_spec = pltpu.PrefetchScalarGridSpec(
        num_scalar_prefetch=1,
        grid=(batch, seq // step),
        in_specs=[qspec, prev, cur, meta, prev, cur, meta],
        out_specs=pl.BlockSpec((None, step, ATTN_Q_W), lambda b, i, s: (b, i, 0)),
    )
    return pl.pallas_call(
        _attn_body,
        grid_spec=grid_spec,
        out_shape=jax.ShapeDtypeStruct((batch, seq, ATTN_Q_W), BF16),
        compiler_params=pltpu.CompilerParams(dimension_semantics=("arbitrary", "arbitrary"),
                                             vmem_limit_bytes=VMEM_LIMIT),
        name="attn",
    )(sinks, q, kd, kd, km, vd, vd, vm)


GLA_STEP = 2 * GLA_CHUNK
GLA_PAIR_K = 2 * GLA_KEY_DIM
GLA_PAIR_V = 2 * GLA_VAL_DIM


def _gla_consts():
    r = lax.broadcasted_iota(jnp.int32, (GLA_STEP, 2 * GLA_STEP), 0)
    c = lax.broadcasted_iota(jnp.int32, (GLA_STEP, 2 * GLA_STEP), 1) & (GLA_STEP - 1)
    causal = jnp.logical_and((r >> 6) == (c >> 6), c <= r)
    tri = jnp.where(causal[:, :GLA_STEP], 1.0, 0.0).astype(BF16)
    return causal, tri


def _gla_step(q2, k2, v2, gk2, s_prev, causal, tri):
    row = lax.broadcasted_iota(jnp.int32, (GLA_STEP, GLA_PAIR_K), 0)
    lane = lax.broadcasted_iota(jnp.int32, (GLA_STEP, GLA_PAIR_K), 1)
    first = row < GLA_CHUNK
    low = lane < GLA_KEY_DIM
    g_hi = gk2.astype(BF16)
    g_r = gk2 - g_hi.astype(F32)
    g_mid = g_r.astype(BF16)
    g_lo = (g_r - g_mid.astype(F32)).astype(BF16)
    b = _dot(tri, g_hi) + _dot(tri, g_mid) + _dot(tri, g_lo)
    half = GLA_CHUNK // 2
    b_mid = jnp.where(first, b[half:half + 1, :], b[GLA_CHUNK + half:GLA_CHUNK + half + 1, :])
    qi = (q2 * jnp.exp(b - b_mid)).astype(BF16)
    ki = (k2 * jnp.exp(b_mid - b)).astype(BF16)
    qx = q2 * jnp.exp(b)
    zero = jnp.zeros((), BF16)
    kblk = jnp.concatenate([jnp.where(low, ki, zero), jnp.where(low, zero, ki)], axis=0)
    att = _dot_nt(qi, kblk)
    att = jnp.where(causal, att, 0.0).astype(BF16)
    zv = jnp.zeros((GLA_STEP, GLA_VAL_DIM), BF16)
    vblk = jnp.concatenate([jnp.concatenate([v2[:, :GLA_VAL_DIM], zv], axis=1),
                            jnp.concatenate([zv, v2[:, GLA_VAL_DIM:]], axis=1)], axis=0)
    o = _dot(att, vblk)
    b_t = b.T
    k_t = k2.T
    lane_t = lax.broadcasted_iota(jnp.int32, (GLA_PAIR_K, GLA_STEP), 1)
    first_t = lane_t < GLA_CHUNK
    bl_a = b_t[:, GLA_CHUNK - 1:GLA_CHUNK]
    bl_b = b_t[:, GLA_STEP - 1:GLA_STEP]
    kx_t = k_t * jnp.exp(jnp.where(first_t, bl_a, bl_b) - b_t)
    zf = jnp.zeros((), F32)
    kx_a = jnp.where(first_t, kx_t, zf).astype(BF16)
    kx_b = jnp.where(first_t, zf, kx_t).astype(BF16)
    srow = lax.broadcasted_iota(jnp.int32, (GLA_PAIR_K, GLA_PAIR_V), 0)
    slane = lax.broadcasted_iota(jnp.int32, (GLA_PAIR_K, GLA_PAIR_V), 1)
    diag = (srow < GLA_KEY_DIM) == (slane < GLA_VAL_DIM)
    s_a = jnp.exp(bl_a) * s_prev + jnp.where(diag, _dot(kx_a, v2), 0.0)
    s_b = jnp.exp(bl_b) * s_a + jnp.where(diag, _dot(kx_b, v2), 0.0)
    o = o + _dot(jnp.where(first, qx, zf).astype(BF16), s_prev.astype(BF16))
    o = o + _dot(jnp.where(first, zf, qx).astype(BF16), s_a.astype(BF16))
    return o, s_b


def _gla_init_body(k_ref, v_ref, gk_ref, s_out):
    causal, tri = _gla_consts()
    for p in range(GLA_HEADS // 2):
        k2 = k_ref[:, p * GLA_PAIR_K:(p + 1) * GLA_PAIR_K].astype(F32)
        v2 = v_ref[:, p * GLA_PAIR_V:(p + 1) * GLA_PAIR_V]
        gk2 = gk_ref[:, p * GLA_PAIR_K:(p + 1) * GLA_PAIR_K]
        s0 = jnp.zeros((GLA_PAIR_K, GLA_PAIR_V), F32)
        _, s = _gla_step(jnp.zeros_like(k2), k2, v2, gk2, s0, causal, tri)
        s_out[p] = s


def _gla_init_call(kg_m, vg_m, gk_m):
    return pl.pallas_call(
        _gla_init_body,
        out_shape=jax.ShapeDtypeStruct((GLA_HEADS // 2, GLA_PAIR_K, GLA_PAIR_V), F32),
        name="gla_init",
    )(kg_m, vg_m, gk_m)


def _gla_body(s0_ref, q_ref, k_ref, v_ref, gk_ref, rg_ref, nw_ref, o_ref, s_ref):
    t = pl.program_id(1)

    @pl.when(t == 0)
    def _():
        s_ref[...] = s0_ref[...]

    causal, tri = _gla_consts()
    nw = nw_ref[...]
    for p in range(GLA_HEADS // 2):
        ks = slice(p * GLA_PAIR_K, (p + 1) * GLA_PAIR_K)
        vs = slice(p * GLA_PAIR_V, (p + 1) * GLA_PAIR_V)
        s = s_ref[p]
        for u in range(GLA_GRID_STEPS):
            rows = slice(u * GLA_STEP, (u + 1) * GLA_STEP)
            o, s = _gla_step(q_ref[rows, ks].astype(F32), k_ref[rows, ks].astype(F32), v_ref[rows, vs],
                             gk_ref[rows, ks], s, causal, tri)
            outs = []
            for hh in range(2):
                oh = o[:, hh * GLA_VAL_DIM:(hh + 1) * GLA_VAL_DIM]
                ms = jnp.mean(oh * oh, axis=-1, keepdims=True)
                outs.append(oh * lax.rsqrt(ms + NORM_EPS) * nw)
            on = jnp.concatenate(outs, axis=1)
            o_ref[rows, vs] = (on * rg_ref[rows, vs].astype(F32)).astype(BF16)
        s_ref[p] = s


GLA_GRID_STEPS = 4


def _gla_call(s0, qg, kg, vg, gk, rg, nw, batch, seq):
    rows = GLA_GRID_STEPS * GLA_STEP
    assert seq % rows == 0
    steps = seq // rows
    spec = lambda width: pl.BlockSpec((None, rows, width), lambda b, t: (b, t, 0))
    return pl.pallas_call(
        _gla_body,
        grid=(batch, steps),
        in_specs=[_const_spec((GLA_HEADS // 2, GLA_PAIR_K, GLA_PAIR_V)),
                  spec(GLA_K_W), spec(GLA_K_W), spec(GLA_V_W), spec(GLA_K_W), spec(GLA_V_W),
                  _const_spec((1, GLA_VAL_DIM))],
        out_specs=spec(GLA_V_W),
        out_shape=jax.ShapeDtypeStruct((batch, seq, GLA_V_W), BF16),
        scratch_shapes=[pltpu.VMEM((GLA_HEADS // 2, GLA_PAIR_K, GLA_PAIR_V), F32)],
        compiler_params=pltpu.CompilerParams(dimension_semantics=("arbitrary", "arbitrary"),
                                             vmem_limit_bytes=VMEM_LIMIT),
        name="gla",
    )(s0, qg, kg, vg, gk, rg, nw)


ROUTER_W = LANES


RT_E0, RT_E1, RT_W0, RT_W1 = 0, 1, 2, 3


def _merge_body(x_ref, ao_ref, go_ref, sga_ref, sgg_ref, wab_ref, wgb_ref, wo_ref, n2_ref,
                wrh_ref, wrl_ref, h_out, rt_out, rtt_out, cnt_out):
    ya = _dot(ao_ref[...], wab_ref[...])
    yg = _dot(go_ref[...], wgb_ref[...])
    merged = sga_ref[...].astype(F32) * ya + sgg_ref[...].astype(F32) * yg
    h = x_ref[...] + _dot(merged.astype(BF16), wo_ref[...])
    h_out[...] = h
    xn = _rms_norm(h, n2_ref[...])

    xh, xl = _split_hi_lo(xn)
    wrh = wrh_ref[...]
    lg = _dot(xh, wrh) + _dot(xl, wrh) + _dot(xh, wrl_ref[...])
    lg_t = lg.T
    tm = lg.shape[0]
    neg = -jnp.inf
    big = jnp.int32(1 << 20)
    gl = lg_t[N_EXPERTS:N_EXPERTS + N_GROUPS, :]
    grow = lax.broadcasted_iota(jnp.int32, (N_GROUPS, tm), 0)
    gmax = jnp.max(gl, axis=0, keepdims=True)
    g_idx = jnp.min(jnp.where(gl == gmax, grow, big), axis=0, keepdims=True)
    g_w = 1.0 / jnp.sum(jnp.exp(gl - gmax), axis=0, keepdims=True)
    erow = lax.broadcasted_iota(jnp.int32, (N_EXPERTS, tm), 0)
    el = jnp.where((erow >> 3) == g_idx, lg_t[0:N_EXPERTS, :], neg)
    m1 = jnp.max(el, axis=0, keepdims=True)
    i1 = jnp.min(jnp.where(el == m1, erow, big), axis=0, keepdims=True)
    el2 = jnp.where(erow == i1, neg, el)
    m2 = jnp.max(el2, axis=0, keepdims=True)
    i2 = jnp.min(jnp.where(el2 == m2, erow, big), axis=0, keepdims=True)
    e2 = jnp.exp(m2 - m1)
    p1 = g_w / (1.0 + e2)
    p2 = g_w * e2 / (1.0 + e2)
    def record(rows):
        r = lax.broadcasted_iota(jnp.int32, (rows, tm), 0)
        return jnp.where(r == RT_E0, i1.astype(F32),
                         jnp.where(r == RT_E1, i2.astype(F32),
                                   jnp.where(r == RT_W0, p1, jnp.where(r == RT_W1, p2, 0.0))))
    rtt_out[...] = record(8)
    rt_out[...] = record(LANES).T
    picked = jnp.where(jnp.logical_or(erow == i1, erow == i2), 1.0, 0.0)

    @pl.when(pl.program_id(0) == 0)
    def _():
        cnt_out[...] = jnp.zeros_like(cnt_out)

    cnt_out[...] += jnp.sum(picked, axis=1, keepdims=True)


def _merge_call(x2d, ao, go, sga, sgg, w, tm):
    n = x2d.shape[0]
    row = lambda width: pl.BlockSpec((tm, width), lambda i: (i, 0))
    return pl.pallas_call(
        _merge_body,
        grid=(n // tm,),
        in_specs=[row(D_MODEL), row(ATTN_Q_W), row(GLA_V_W), row(D_MODEL), row(D_MODEL),
                  _const_spec((ATTN_Q_W, D_MODEL)), _const_spec((GLA_V_W, D_MODEL)),
                  _const_spec((D_MODEL, D_MODEL)), _const_spec((1, D_MODEL)),
                  _const_spec((D_MODEL, ROUTER_W)), _const_spec((D_MODEL, ROUTER_W))],
        out_specs=[row(D_MODEL), row(ROUTER_W), pl.BlockSpec((8, tm), lambda i: (0, i)),
                   _const_spec((N_EXPERTS, LANES))],
        out_shape=[jax.ShapeDtypeStruct((n, D_MODEL), F32),
                   jax.ShapeDtypeStruct((n, ROUTER_W), F32),
                   jax.ShapeDtypeStruct((8, n), F32),
                   jax.ShapeDtypeStruct((N_EXPERTS, LANES), F32)],
        compiler_params=pltpu.CompilerParams(dimension_semantics=("arbitrary",),
                                             vmem_limit_bytes=VMEM_LIMIT),
        name="merge",
    )(x2d, ao, go, sga, sgg, w["wab"], w["wgb"], w["wo"], w["norm2"], w["wrh"], w["wrl"])


EXPERT_TILE = 512
ROUTE_TILE = 512
VISIT_CAP = 256
VIS_EXPERT, VIS_TILE, VIS_LO, VIS_HI, VIS_COUNT = 0, 1, 2, 3, 4


def _num_visits(n_tokens):
    return (2 * n_tokens) // EXPERT_TILE + N_EXPERTS - 1


def _expert_cumsum_inclusive(x):
    row = lax.broadcasted_iota(jnp.int32, x.shape, 0)
    s = 1
    while s < N_EXPERTS:
        x = x + jnp.where(row >= s, pltpu.roll(x, s, 0), 0.0)
        s *= 2
    return x


def _route_body(rtt_ref, cnt_ref, pos_ref, vis_ref, carry_ref):
    i = pl.program_id(0)
    tm = rtt_ref.shape[1]
    cnt = cnt_ref[...]
    end = _expert_cumsum_inclusive(cnt)
    base = end - cnt

    @pl.when(i == 0)
    def _():
        carry_ref[...] = jnp.zeros_like(carry_ref)
        inv = 1.0 / EXPERT_TILE
        first = jnp.floor(base * inv)
        n_vis = jnp.where(cnt > 0.0, jnp.floor((end - 1.0) * inv) - first + 1.0, 0.0)
        vend = _expert_cumsum_inclusive(n_vis)
        vstart = vend - n_vis
        col = lambda a: a[:, 0:1]
        erow = lax.broadcasted_iota(jnp.int32, (N_EXPERTS, VISIT_CAP), 0).astype(F32)
        v = lax.broadcasted_iota(jnp.int32, (N_EXPERTS, VISIT_CAP), 1).astype(F32)
        ev = jnp.minimum(jnp.sum(jnp.where(col(vend) <= v, 1.0, 0.0), axis=0, keepdims=True),
                         N_EXPERTS - 1.0)
        mine = erow == ev
        pick = lambda a: jnp.sum(jnp.where(mine, a, 0.0), axis=0, keepdims=True)
        tile = pick(col(first) + v - col(vstart))
        lo = jnp.maximum(pick(col(base)) - tile * EXPERT_TILE, 0.0)
        hi = jnp.minimum(pick(col(end)) - tile * EXPERT_TILE, float(EXPERT_TILE))
        total = vend[N_EXPERTS - 1:N_EXPERTS, 0:1]
        r8 = lax.broadcasted_iota(jnp.int32, (8, VISIT_CAP), 0)
        rec = jnp.where(r8 == VIS_EXPERT, ev,
                        jnp.where(r8 == VIS_TILE, tile,
                                  jnp.where(r8 == VIS_LO, lo,
                                            jnp.where(r8 == VIS_HI, hi,
                                                      jnp.where(r8 == VIS_COUNT, total, 0.0)))))
        vis_ref[...] = rec.astype(jnp.int32)

    rtt = rtt_ref[...]
    erow = lax.broadcasted_iota(jnp.int32, (N_EXPERTS, tm), 0).astype(F32)
    oh0 = erow == rtt[RT_E0:RT_E0 + 1, :]
    oh1 = erow == rtt[RT_E1:RT_E1 + 1, :]
    picked = jnp.where(jnp.logical_or(oh0, oh1), 1.0, 0.0)
    r = lax.broadcasted_iota(jnp.int32, (tm, tm), 0)
    c = lax.broadcasted_iota(jnp.int32, (tm, tm), 1)
    earlier = jnp.where(r < c, 1.0, 0.0).astype(BF16)
    slot = _dot(picked.astype(BF16), earlier) + (carry_ref[:, 0:1] + base[:, 0:1])
    p0 = jnp.sum(jnp.where(oh0, slot, 0.0), axis=0, keepdims=True)
    p1 = jnp.sum(jnp.where(oh1, slot, 0.0), axis=0, keepdims=True)
    r8 = lax.broadcasted_iota(jnp.int32, (8, tm), 0)
    pos_ref[...] = jnp.where(r8 == 0, p0, jnp.where(r8 == 1, p1, 0.0)).astype(jnp.int32)
    carry_ref[...] += jnp.sum(picked, axis=1, keepdims=True)


def _route_call(rtt, cnt):
    n = rtt.shape[1]
    tm = _pick_tile(n, ROUTE_TILE)
    return pl.pallas_call(
        _route_body,
        grid=(n // tm,),
        in_specs=[pl.BlockSpec((8, tm), lambda i: (0, i)), _const_spec((N_EXPERTS, LANES))],
        out_specs=[pl.BlockSpec((8, tm), lambda i: (0, i)), _const_spec((8, VISIT_CAP))],
        out_shape=[jax.ShapeDtypeStruct((8, n), jnp.int32),
                   jax.ShapeDtypeStruct((8, VISIT_CAP), jnp.int32)],
        scratch_shapes=[pltpu.VMEM((N_EXPERTS, LANES), F32)],
        compiler_params=pltpu.CompilerParams(dimension_semantics=("arbitrary",)),
        name="route",
    )(rtt, cnt)


SCATTER_TILE = 512


def _scatter_body(pos0_ref, pos1_ref, h_ref, hs_hbm, sem):
    tm = h_ref.shape[0]

    def issue(r, carry):
        src = h_ref.at[pl.ds(r, 1)]
        pltpu.make_async_copy(src, hs_hbm.at[pl.ds(pos0_ref[r], 1)], sem).start(priority=0)
        pltpu.make_async_copy(src, hs_hbm.at[pl.ds(pos1_ref[r], 1)], sem).start(priority=1)
        return carry

    lax.fori_loop(0, tm, issue, 0, unroll=8)
    for _ in range(2):
        pltpu.make_async_copy(h_ref, hs_hbm.at[pl.ds(0, tm)], sem).wait()


def _scatter_call(pos0, pos1, h, n_slots):
    n = h.shape[0]
    tm = _pick_tile(n, SCATTER_TILE)
    smem = lambda: pl.BlockSpec((tm,), lambda i: (i,), memory_space=pltpu.SMEM)
    return pl.pallas_call(
        _scatter_body,
        grid=(n // tm,),
        in_specs=[smem(), smem(), pl.BlockSpec((tm, D_MODEL), lambda i: (i, 0))],
        out_specs=pl.BlockSpec(memory_space=pl.ANY),
        out_shape=jax.ShapeDtypeStruct((n_slots, D_MODEL), F32),
        scratch_shapes=[pltpu.SemaphoreType.DMA],
        compiler_params=pltpu.CompilerParams(dimension_semantics=("arbitrary",)),
        name="scatter",
    )(pos0, pos1, h)


def _expert_body(vis_ref, h_ref, n2_ref, wg_ref, wu_ref, wd_ref, o_ref):
    j = pl.program_id(0)

    @pl.when(j < vis_ref[VIS_COUNT, 0])
    def _():
        lo_row = vis_ref[VIS_LO, j]
        hi_row = vis_ref[VIS_HI, j]
        x = _rms_norm(h_ref[...], n2_ref[...]).astype(BF16)
        g = _dot(x, wg_ref[...].astype(BF16))
        u = _dot(x, wu_ref[...].astype(BF16))
        hg = (g * _sigmoid(g) * u).astype(BF16)
        new = _dot(hg, wd_ref[...].astype(BF16))
        row = lax.broadcasted_iota(jnp.int32, new.shape, 0)

        @pl.when(lo_row == 0)
        def _():
            o_ref[...] = jnp.where(row < hi_row, new, 0.0)

        @pl.when(lo_row > 0)
        def _():
            o_ref[...] = jnp.where(jnp.logical_and(row >= lo_row, row < hi_row), new, o_ref[...])


def _expert_call(vis, hs, n2, wg, wu, wd, n_visits):
    def visit(j, vis):
        return jnp.minimum(j, vis[VIS_COUNT, 0] - 1)

    def xmap(j, vis):
        return (vis[VIS_TILE, visit(j, vis)], 0)

    def wmap(j, vis):
        return (vis[VIS_EXPERT, visit(j, vis)], 0, 0)

    grid_spec = pltpu.PrefetchScalarGridSpec(
        num_scalar_prefetch=1,
        grid=(n_visits,),
        in_specs=[pl.BlockSpec((EXPERT_TILE, D_MODEL), xmap),
                  pl.BlockSpec((1, D_MODEL), lambda j, vis: (0, 0)),
                  pl.BlockSpec((None, D_MODEL, EXPERT_FF), wmap),
                  pl.BlockSpec((None, D_MODEL, EXPERT_FF), wmap),
                  pl.BlockSpec((None, EXPERT_FF, D_MODEL), wmap)],
        out_specs=pl.BlockSpec((EXPERT_TILE, D_MODEL), xmap),
    )
    return pl.pallas_call(
        _expert_body,
        grid_spec=grid_spec,
        out_shape=jax.ShapeDtypeStruct(hs.shape, F32),
        compiler_params=pltpu.CompilerParams(dimension_semantics=("arbitrary",),
                                             vmem_limit_bytes=VMEM_LIMIT),
        name="experts",
    )(vis, hs, n2, wg, wu, wd)


COMBINE_TILE = 256


def _combine_body(p0_ref, p1_ref, p0n_ref, p1n_ref, h_ref, rt_ref, y_hbm, o_ref, g_ref, sem):
    i = pl.program_id(0)
    tm = h_ref.shape[0]
    cur = i % 2

    def issue(q0_ref, q1_ref, buf):
        def one(r, carry):
            pltpu.make_async_copy(y_hbm.at[pl.ds(q0_ref[r], 1)], g_ref.at[buf, 0, pl.ds(r, 1)],
                                  sem.at[buf]).start(priority=0)
            pltpu.make_async_copy(y_hbm.at[pl.ds(q1_ref[r], 1)], g_ref.at[buf, 1, pl.ds(r, 1)],
                                  sem.at[buf]).start(priority=1)
            return carry
        lax.fori_loop(0, tm, one, 0, unroll=8)

    @pl.when(i == 0)
    def _():
        issue(p0_ref, p1_ref, 0)

    @pl.when(i + 1 < pl.num_programs(0))
    def _():
        issue(p0n_ref, p1n_ref, 1 - cur)

    for k in range(2):
        pltpu.make_async_copy(y_hbm.at[pl.ds(0, tm)], g_ref.at[cur, k], sem.at[cur]).wait()
    rt = rt_ref[...]
    o_ref[...] = (h_ref[...] + rt[:, RT_W0:RT_W0 + 1] * g_ref[cur, 0]
                  + rt[:, RT_W1:RT_W1 + 1] * g_ref[cur, 1])


def _combine_call(pos0, pos1, h, rt, y):
    n = h.shape[0]
    tm = _pick_tile(n, COMBINE_TILE)
    steps = n // tm
    smem_cur = lambda: pl.BlockSpec((tm,), lambda i: (i,), memory_space=pltpu.SMEM)
    smem_next = lambda: pl.BlockSpec((tm,), lambda i: (jnp.minimum(i + 1, steps - 1),),
                                     memory_space=pltpu.SMEM)
    row = lambda width: pl.BlockSpec((tm, width), lambda i: (i, 0))
    return pl.pallas_call(
        _combine_body,
        grid=(steps,),
        in_specs=[smem_cur(), smem_cur(), smem_next(), smem_next(), row(D_MODEL), row(ROUTER_W),
                  pl.BlockSpec(memory_space=pl.ANY)],
        out_specs=row(D_MODEL),
        out_shape=jax.ShapeDtypeStruct((n, D_MODEL), F32),
        scratch_shapes=[pltpu.VMEM((2, 2, tm, D_MODEL), F32), pltpu.SemaphoreType.DMA((2,))],
        compiler_params=pltpu.CompilerParams(dimension_semantics=("arbitrary",),
                                             vmem_limit_bytes=VMEM_LIMIT),
        name="combine",
    )(pos0, pos1, pos0, pos1, h, rt, y)


def _rope_tables(pos):
    half = HEAD_DIM // 2
    inv_freq = ROPE_THETA ** (-(jnp.arange(half, dtype=F32) * 2.0) / HEAD_DIM)
    ang = pos.astype(F32)[:, None] * inv_freq[None, :]
    cos = jnp.cos(ang)
    sin = jnp.sin(ang)
    cos64 = jnp.concatenate([cos, cos], axis=1)
    sin64 = jnp.concatenate([-sin, sin], axis=1)
    return jnp.tile(cos64, (1, LANES // HEAD_DIM)), jnp.tile(sin64, (1, LANES // HEAD_DIM))


def _dup_heads(wcols):
    d = wcols.shape[0]
    w3 = wcols.reshape(d, ATTN_KV_HEADS, 1, HEAD_DIM)
    return jnp.broadcast_to(w3, (d, ATTN_KV_HEADS, 2, HEAD_DIM)).reshape(d, ATTN_KV_DUP_W)


def _prep_weights(norm1_w, w_in, q_norm_w, k_norm_w, gla_gate_up, gla_gate_bias, w_attn_branch,
                  w_gla_branch, w_out, norm2_w, router_group, router_expert):
    wb = w_in.astype(BF16)
    wgd = jnp.zeros((D_MODEL, LANES), BF16).at[:, :GLA_GATE_RANK].set(wb[:, _OFF_GD:_OFF_GD + GLA_GATE_RANK])
    gup = jnp.zeros((LANES, GLA_K_W), BF16).at[:GLA_GATE_RANK].set(gla_gate_up.astype(BF16))
    wr = jnp.zeros((D_MODEL, ROUTER_W), F32)
    wr = wr.at[:, :N_EXPERTS].set(router_expert.transpose(1, 0, 2).reshape(D_MODEL, N_EXPERTS))
    wr = wr.at[:, N_EXPERTS:N_EXPERTS + N_GROUPS].set(router_group)
    wrh = wr.astype(BF16)
    wrl = (wr - wrh.astype(F32)).astype(BF16)
    return {
        "norm1": norm1_w.reshape(1, D_MODEL),
        "wq": wb[:, _OFF_Q:_OFF_Q + ATTN_Q_W],
        "wk": _dup_heads(wb[:, _OFF_K:_OFF_K + ATTN_KV_W]),
        "wv": _dup_heads(wb[:, _OFF_V:_OFF_V + ATTN_KV_W]),
        "wgla": wb[:, _OFF_GQ:_OFF_GQ + GLA_W],
        "wgd": wgd,
        "wgate": wb[:, _OFF_GATE:_OFF_GATE + 2 * D_MODEL],
        "qn": jnp.tile(q_norm_w, ATTN_HEADS).reshape(1, ATTN_Q_W),
        "kn": jnp.tile(k_norm_w, ATTN_KV_DUP_W // HEAD_DIM).reshape(1, ATTN_KV_DUP_W),
        "gup": gup,
        "gb": gla_gate_bias.reshape(1, GLA_K_W),
        "wab": w_attn_branch.astype(BF16),
        "wgb": w_gla_branch.astype(BF16),
        "wo": w_out.astype(BF16),
        "norm2": norm2_w.reshape(1, D_MODEL),
        "wrh": wrh,
        "wrl": wrl,
    }


def _pick_tile(n, want):
    t = want
    while n % t:
        t //= 2
    return t


def kernel(x, meta_tokens, norm1_w, w_in, q_norm_w, k_norm_w, attn_sinks, gla_gate_up, gla_gate_bias,
           gla_norm_w, w_attn_branch, w_gla_branch, w_out, norm2_w, router_group, router_expert,
           expert_w_gate, expert_w_up, expert_w_down):
    batch, seq, d = x.shape
    assert d == D_MODEL and seq % GLA_STEP == 0 and norm1_w.shape[0] == 1
    n = batch * seq
    w = _prep_weights(norm1_w[0], w_in[0], q_norm_w[0], k_norm_w[0], gla_gate_up[0], gla_gate_bias[0],
                      w_attn_branch[0], w_gla_branch[0], w_out[0], norm2_w[0], router_group[0],
                      router_expert[0])
    x2d = x.reshape(n, D_MODEL)

    cos_m, sin_m = _rope_tables(jnp.arange(N_META))
    cos_r, sin_r = _rope_tables(jnp.arange(seq) + N_META)
    meta = _proj_call(meta_tokens.astype(F32), N_META, 1, w, cos_m, sin_m)
    tm = _pick_tile(seq, 512)
    (q, kd, vd, qg, kg, vg, rg, gk, sga, sgg) = _proj_call(x2d, tm, seq // tm, w, cos_r, sin_r)

    pad_after = ((0, BLOCK - N_META), (0, 0))
    km = jnp.pad(meta[1], pad_after)
    vm = jnp.pad(meta[2], pad_after)
    ao = _attn_call(q.reshape(batch, seq, ATTN_Q_W), kd.reshape(batch, seq, ATTN_KV_DUP_W),
                    vd.reshape(batch, seq, ATTN_KV_DUP_W), km, vm, attn_sinks[0].astype(F32), batch, seq)

    pad_before = ((GLA_STEP - N_META, 0), (0, 0))
    s0 = _gla_init_call(jnp.pad(meta[4], pad_before), jnp.pad(meta[5], pad_before),
                        jnp.pad(meta[7], pad_before))
    go = _gla_call(s0, qg.reshape(batch, seq, GLA_K_W), kg.reshape(batch, seq, GLA_K_W),
                   vg.reshape(batch, seq, GLA_V_W), gk.reshape(batch, seq, GLA_K_W),
                   rg.reshape(batch, seq, GLA_V_W), gla_norm_w[0].reshape(1, GLA_VAL_DIM), batch, seq)

    h, rt, rtt, cnt = _merge_call(x2d, ao.reshape(n, ATTN_Q_W), go.reshape(n, GLA_V_W), sga, sgg, w,
                                  _pick_tile(n, 512))

    assert (2 * n) % EXPERT_TILE == 0 and _num_visits(n) <= VISIT_CAP
    pos, vis = _route_call(rtt, cnt)
    pos0, pos1 = pos[0], pos[1]
    hs = _scatter_call(pos0, pos1, h, 2 * n)
    wg = expert_w_gate[0].reshape(N_EXPERTS, D_MODEL, EXPERT_FF)
    wu = expert_w_up[0].reshape(N_EXPERTS, D_MODEL, EXPERT_FF)
    wd = expert_w_down[0].reshape(N_EXPERTS, EXPERT_FF, D_MODEL)
    y = _expert_call(vis, hs, w["norm2"], wg, wu, wd, _num_visits(n))
    out = _combine_call(pos0, pos1, h, rt, y)
    return out.reshape(batch, seq, D_MODEL)
```

```python
import functools
import math

import jax
import jax.numpy as jnp
from jax import lax
from jax.experimental import pallas as pl
from jax.experimental.pallas import tpu as pltpu

F32 = jnp.float32
BF16 = jnp.bfloat16

D_MODEL = 1024
N_META = 16
BLOCK = 128
ATTN_HEADS = 8
ATTN_KV_HEADS = 2
HEAD_DIM = 64
ROPE_THETA = 10000.0
ATTN_Q_W = ATTN_HEADS * HEAD_DIM
ATTN_KV_W = ATTN_KV_HEADS * HEAD_DIM
GLA_HEADS = 4
GLA_KEY_DIM = 64
GLA_VAL_DIM = 128
GLA_K_W = GLA_HEADS * GLA_KEY_DIM
GLA_V_W = GLA_HEADS * GLA_VAL_DIM
GLA_GATE_RANK = 16
GLA_GATE_NORM = 16.0
GLA_CHUNK = 64
N_GROUPS = 4
EXPERTS_PER_GROUP = 8
N_EXPERTS = N_GROUPS * EXPERTS_PER_GROUP
EXPERT_FF = 256
NORM_EPS = 1e-6
MASK_VALUE = -1e30

LANES = 128
ATTN_KV_DUP_W = 2 * ATTN_KV_W
VMEM_LIMIT = 56 * 1024 * 1024

_OFF_Q = 0
_OFF_K = _OFF_Q + ATTN_Q_W
_OFF_V = _OFF_K + ATTN_KV_W
_OFF_GQ = _OFF_V + ATTN_KV_W
_OFF_GD = _OFF_GQ + 2 * GLA_K_W + 2 * GLA_V_W
_OFF_GATE = _OFF_GD + GLA_GATE_RANK
GLA_W = 2 * GLA_K_W + 2 * GLA_V_W


def _dot(a, b):
    return jnp.dot(a, b, preferred_element_type=F32)


def _dot_nt(a, b):
    return lax.dot_general(a, b, (((1,), (1,)), ((), ())), preferred_element_type=F32)


def _split_hi_lo(x):
    hi = x.astype(BF16)
    lo = (x - hi.astype(F32)).astype(BF16)
    return hi, lo


def _group_mean_sq(x, group):
    w = x.shape[-1]
    shift = int(math.log2(group))
    r = lax.broadcasted_iota(jnp.int32, (w, w), 0) >> shift
    c = lax.broadcasted_iota(jnp.int32, (w, w), 1) >> shift
    ones = jnp.where(r == c, 1.0, 0.0).astype(BF16)
    hi, lo = _split_hi_lo(x * x)
    return (_dot(hi, ones) + _dot(lo, ones)) * (1.0 / group)


def _rope(x, cos, sin_signed):
    w = x.shape[-1]
    reps = w // LANES
    if reps > 1:
        cos = jnp.concatenate([cos] * reps, axis=1)
        sin_signed = jnp.concatenate([sin_signed] * reps, axis=1)
    lane = lax.broadcasted_iota(jnp.int32, x.shape, 1)
    first_half = (lane & (HEAD_DIM // 2)) == 0
    swapped = jnp.where(first_half, pltpu.roll(x, w - HEAD_DIM // 2, 1), pltpu.roll(x, HEAD_DIM // 2, 1))
    return x * cos + swapped * sin_signed


def _rms_norm(x, w):
    ms = jnp.mean(x * x, axis=-1, keepdims=True)
    return x * lax.rsqrt(ms + NORM_EPS) * w


def _sigmoid(x):
    return 0.5 * jnp.tanh(0.5 * x) + 0.5


def _log_sigmoid(x):
    return jnp.minimum(x, 0.0) - jnp.log(1.0 + jnp.exp(-jnp.abs(x)))


def _proj_body(x_ref, n1_ref, wq_ref, wk_ref, wv_ref, wgla_ref, wgd_ref, wgate_ref,
               qn_ref, kn_ref, cos_ref, sin_ref, gup_ref, gb_ref,
               q_out, k_out, v_out, qg_out, kg_out, vg_out, rg_out, gk_out, sga_out, sgg_out):
    xn = _rms_norm(x_ref[...], n1_ref[...]).astype(BF16)
    cos = cos_ref[...]
    sin = sin_ref[...]

    q = _dot(xn, wq_ref[...])
    q = q * lax.rsqrt(_group_mean_sq(q, HEAD_DIM) + NORM_EPS) * qn_ref[...]
    q_out[...] = (_rope(q, cos, sin) * (HEAD_DIM ** -0.5)).astype(BF16)

    k = _dot(xn, wk_ref[...])
    k = k * lax.rsqrt(_group_mean_sq(k, HEAD_DIM) + NORM_EPS) * kn_ref[...]
    k_out[...] = _rope(k, cos, sin).astype(BF16)

    v_out[...] = _dot(xn, wv_ref[...]).astype(BF16)

    g = _dot(xn, wgla_ref[...])
    qg_out[...] = (g[:, :GLA_K_W] * (GLA_KEY_DIM ** -0.5)).astype(BF16)
    kg_out[...] = g[:, GLA_K_W:2 * GLA_K_W].astype(BF16)
    vg_out[...] = g[:, 2 * GLA_K_W:2 * GLA_K_W + GLA_V_W].astype(BF16)
    r = g[:, 2 * GLA_K_W + GLA_V_W:]
    rg_out[...] = (r * _sigmoid(r)).astype(BF16)

    gd = _dot(xn, wgd_ref[...]).astype(BF16)
    z = _dot(gd, gup_ref[...]) + gb_ref[...]
    gk_out[...] = _log_sigmoid(z) * (1.0 / GLA_GATE_NORM)

    gates = _dot(xn, wgate_ref[...])
    sga_out[...] = _sigmoid(gates[:, :D_MODEL]).astype(BF16)
    sgg_out[...] = _sigmoid(gates[:, D_MODEL:]).astype(BF16)


def _const_spec(shape):
    nd = len(shape)
    return pl.BlockSpec(shape, lambda *_: (0,) * nd)


def _proj_call(x2d, tm, tiles_per_seq, w, cos, sin):
    n = x2d.shape[0]
    grid = (n // tm,)
    row = lambda width: pl.BlockSpec((tm, width), lambda i: (i, 0))
    tab = pl.BlockSpec((tm, LANES), lambda i: (i % tiles_per_seq, 0))
    in_specs = [
        row(D_MODEL), _const_spec((1, D_MODEL)),
        _const_spec((D_MODEL, ATTN_Q_W)), _const_spec((D_MODEL, ATTN_KV_DUP_W)),
        _const_spec((D_MODEL, ATTN_KV_DUP_W)), _const_spec((D_MODEL, GLA_W)),
        _const_spec((D_MODEL, LANES)), _const_spec((D_MODEL, 2 * D_MODEL)),
        _const_spec((1, ATTN_Q_W)), _const_spec((1, ATTN_KV_DUP_W)),
        tab, tab, _const_spec((LANES, GLA_K_W)), _const_spec((1, GLA_K_W)),
    ]
    out_widths = [ATTN_Q_W, ATTN_KV_DUP_W, ATTN_KV_DUP_W, GLA_K_W, GLA_K_W, GLA_V_W, GLA_V_W,
                  GLA_K_W, D_MODEL, D_MODEL]
    out_dtypes = [BF16] * 7 + [F32] + [BF16] * 2
    return pl.pallas_call(
        _proj_body,
        grid=grid,
        in_specs=in_specs,
        out_specs=[row(wd) for wd in out_widths],
        out_shape=[jax.ShapeDtypeStruct((n, wd), dt) for wd, dt in zip(out_widths, out_dtypes)],
        compiler_params=pltpu.CompilerParams(dimension_semantics=("arbitrary",),
                                             vmem_limit_bytes=VMEM_LIMIT),
        name="proj",
    )(x2d, w["norm1"], w["wq"], w["wk"], w["wv"], w["wgla"], w["wgd"], w["wgate"],
      w["qn"], w["kn"], cos, sin, w["gup"], w["gb"])


ATTN_STEP_BLOCKS = 4


def _attn_body(sink_ref, q_ref, kp_ref, kc_ref, km_ref, vp_ref, vc_ref, vm_ref, o_ref):
    i = pl.program_id(1)
    nblk = ATTN_STEP_BLOCKS
    pair_rows = 2 * BLOCK
    lane = lax.broadcasted_iota(jnp.int32, (BLOCK, LANES), 1)
    low = lane < HEAD_DIM
    low2 = lax.broadcasted_iota(jnp.int32, (pair_rows, LANES), 1) < HEAD_DIM
    top2 = lax.broadcasted_iota(jnp.int32, (pair_rows, 1), 0) < BLOCK
    key = lax.broadcasted_iota(jnp.int32, (pair_rows, 2 * LANES), 1) & (LANES - 1)
    rowi = lax.broadcasted_iota(jnp.int32, (pair_rows, 2 * LANES), 0) & (BLOCK - 1)
    mask_meta = key < N_META
    mask_prev = key > rowi
    mask_first_prev = jnp.logical_and(mask_prev, i > 0)
    mask_cur = key <= rowi
    zero = jnp.zeros((), BF16)

    def block_diag(x):
        return jnp.concatenate([jnp.where(low, x, zero), jnp.where(low, zero, x)], axis=0)

    blk_row = lax.broadcasted_iota(jnp.int32, (2 * BLOCK, LANES), 0)
    blk_lane = lax.broadcasted_iota(jnp.int32, (2 * BLOCK, LANES), 1)
    ones_blk = jnp.where((blk_row < BLOCK) == (blk_lane < HEAD_DIM), 1.0, 0.0).astype(BF16)

    def masked(s, mask):
        return jnp.where(mask, s, MASK_VALUE)

    for kvh in range(ATTN_KV_HEADS):
        sl = slice(kvh * LANES, (kvh + 1) * LANES)
        k_seq = [block_diag(kp_ref[:, sl])]
        v_seq = [block_diag(vp_ref[:, sl])]
        for t in range(nblk):
            rows = slice(t * BLOCK, (t + 1) * BLOCK)
            k_seq.append(block_diag(kc_ref[rows, sl]))
            v_seq.append(block_diag(vc_ref[rows, sl]))
        q_all = jnp.concatenate(
            [q_ref[t * BLOCK:(t + 1) * BLOCK, (2 * kvh + pr) * LANES:(2 * kvh + pr + 1) * LANES]
             for t in range(nblk) for pr in range(2)], axis=0)
        s_meta = _dot_nt(q_all, block_diag(km_ref[:, sl]))
        s_seq = []
        for j in range(nblk + 1):
            lo_blk, hi_blk = max(j - 1, 0), min(j, nblk - 1)
            s_seq.append(_dot_nt(q_all[lo_blk * pair_rows:(hi_blk + 1) * pair_rows], k_seq[j]))

        def sink_col(hh):
            return jnp.where(top2, sink_ref[4 * kvh + hh], sink_ref[4 * kvh + 2 + hh])

        p_meta, p_prev, p_cur, l_all = [], [], [], []
        for t in range(nblk):
            sm = masked(s_meta[t * pair_rows:(t + 1) * pair_rows], mask_meta)
            prev_rows = slice(0, pair_rows) if t == 0 else slice(pair_rows, 2 * pair_rows)
            sp = masked(s_seq[t][prev_rows], mask_first_prev if t == 0 else mask_prev)
            sc = masked(s_seq[t + 1][0:pair_rows], mask_cur)
            s_max = jnp.maximum(jnp.maximum(sm, sp), sc)
            m2 = []
            sink_terms = []
            for hh in range(2):
                sink = sink_col(hh)
                m = jnp.maximum(jnp.max(s_max[:, hh * LANES:(hh + 1) * LANES], axis=1, keepdims=True), sink)
                m2.append(m)
                sink_terms.append(jnp.exp(sink - m))

            def probs(s):
                return jnp.concatenate([jnp.exp(s[:, hh * LANES:(hh + 1) * LANES] - m2[hh])
                                        for hh in range(2)], axis=1)

            pm, pp, pc = probs(sm), probs(sp), probs(sc)
            p_meta.append(pm.astype(BF16))
            p_prev.append(pp.astype(BF16))
            p_cur.append(pc.astype(BF16))
            l_all.append(_dot((pm + pp + pc).astype(BF16), ones_blk)
                         + jnp.where(low2, sink_terms[0], sink_terms[1]))

        o_meta = _dot(jnp.concatenate(p_meta, axis=0), block_diag(vm_ref[:, sl]))
        o_seq = []
        for j in range(nblk + 1):
            parts = ([p_cur[j - 1]] if j >= 1 else []) + ([p_prev[j]] if j < nblk else [])
            o_seq.append(_dot(jnp.concatenate(parts, axis=0) if len(parts) > 1 else parts[0], v_seq[j]))
        for t in range(nblk):
            prev_rows = slice(0, pair_rows) if t == 0 else slice(pair_rows, 2 * pair_rows)
            o = (o_meta[t * pair_rows:(t + 1) * pair_rows] + o_seq[t][prev_rows]
                 + o_seq[t + 1][0:pair_rows]) / l_all[t]
            for pr in range(2):
                col = (2 * kvh + pr) * LANES
                o_ref[t * BLOCK:(t + 1) * BLOCK, col:col + LANES] = o[pr * BLOCK:(pr + 1) * BLOCK].astype(BF16)


def _attn_call(q, kd, vd, km, vm, sinks, batch, seq):
    step = ATTN_STEP_BLOCKS * BLOCK
    assert seq % step == 0
    qspec = pl.BlockSpec((None, step, ATTN_Q_W), lambda b, i, s: (b, i, 0))
    prev = pl.BlockSpec((None, BLOCK, ATTN_KV_DUP_W),
                        lambda b, i, s: (b, jnp.maximum(i * ATTN_STEP_BLOCKS - 1, 0), 0))
    cur = pl.BlockSpec((None, step, ATTN_KV_DUP_W), lambda b, i, s: (b, i, 0))
    meta = pl.BlockSpec((BLOCK, ATTN_KV_DUP_W), lambda b, i, s: (0, 0))
    grid_spec = pltpu.PrefetchScalarGridSpec(
        num_scalar_prefetch=1,
        grid=(batch, seq // step),
        in_specs=[qspec, prev, cur, meta, prev, cur, meta],
        out_specs=pl.BlockSpec((None, step, ATTN_Q_W), lambda b, i, s: (b, i, 0)),
    )
    return pl.pallas_call(
        _attn_body,
        grid_spec=grid_spec,
        out_shape=jax.ShapeDtypeStruct((batch, seq, ATTN_Q_W), BF16),
        compiler_params=pltpu.CompilerParams(dimension_semantics=("arbitrary", "arbitrary"),
                                             vmem_limit_bytes=VMEM_LIMIT),
        name="attn",
    )(sinks, q, kd, kd, km, vd, vd, vm)


GLA_STEP = 2 * GLA_CHUNK
GLA_PAIR_K = 2 * GLA_KEY_DIM
GLA_PAIR_V = 2 * GLA_VAL_DIM


def _gla_consts():
    r = lax.broadcasted_iota(jnp.int32, (GLA_STEP, 2 * GLA_STEP), 0)
    c = lax.broadcasted_iota(jnp.int32, (GLA_STEP, 2 * GLA_STEP), 1) & (GLA_STEP - 1)
    causal = jnp.logical_and((r >> 6) == (c >> 6), c <= r)
    tri = jnp.where(causal[:, :GLA_STEP], 1.0, 0.0).astype(BF16)
    return causal, tri


def _gla_step(q2, k2, v2, gk2, s_prev, causal, tri):
    row = lax.broadcasted_iota(jnp.int32, (GLA_STEP, GLA_PAIR_K), 0)
    lane = lax.broadcasted_iota(jnp.int32, (GLA_STEP, GLA_PAIR_K), 1)
    first = row < GLA_CHUNK
    low = lane < GLA_KEY_DIM
    g_hi = gk2.astype(BF16)
    g_r = gk2 - g_hi.astype(F32)
    g_mid = g_r.astype(BF16)
    g_lo = (g_r - g_mid.astype(F32)).astype(BF16)
    b = _dot(tri, g_hi) + _dot(tri, g_mid) + _dot(tri, g_lo)
    half = GLA_CHUNK // 2
    b_mid = jnp.where(first, b[half:half + 1, :], b[GLA_CHUNK + half:GLA_CHUNK + half + 1, :])
    qi = (q2 * jnp.exp(b - b_mid)).astype(BF16)
    ki = (k2 * jnp.exp(b_mid - b)).astype(BF16)
    qx = q2 * jnp.exp(b)
    zero = jnp.zeros((), BF16)
    kblk = jnp.concatenate([jnp.where(low, ki, zero), jnp.where(low, zero, ki)], axis=0)
    att = _dot_nt(qi, kblk)
    att = jnp.where(causal, att, 0.0).astype(BF16)
    zv = jnp.zeros((GLA_STEP, GLA_VAL_DIM), BF16)
    vblk = jnp.concatenate([jnp.concatenate([v2[:, :GLA_VAL_DIM], zv], axis=1),
                            jnp.concatenate([zv, v2[:, GLA_VAL_DIM:]], axis=1)], axis=0)
    o = _dot(att, vblk)
    b_t = b.T
    k_t = k2.T
    lane_t = lax.broadcasted_iota(jnp.int32, (GLA_PAIR_K, GLA_STEP), 1)
    first_t = lane_t < GLA_CHUNK
    bl_a = b_t[:, GLA_CHUNK - 1:GLA_CHUNK]
    bl_b = b_t[:, GLA_STEP - 1:GLA_STEP]
    kx_t = k_t * jnp.exp(jnp.where(first_t, bl_a, bl_b) - b_t)
    zf = jnp.zeros((), F32)
    kx_a = jnp.where(first_t, kx_t, zf).astype(BF16)
    kx_b = jnp.where(first_t, zf, kx_t).astype(BF16)
    srow = lax.broadcasted_iota(jnp.int32, (GLA_PAIR_K, GLA_PAIR_V), 0)
    slane = lax.broadcasted_iota(jnp.int32, (GLA_PAIR_K, GLA_PAIR_V), 1)
    diag = (srow < GLA_KEY_DIM) == (slane < GLA_VAL_DIM)
    s_a = jnp.exp(bl_a) * s_prev + jnp.where(diag, _dot(kx_a, v2), 0.0)
    s_b = jnp.exp(bl_b) * s_a + jnp.where(diag, _dot(kx_b, v2), 0.0)
    o = o + _dot(jnp.where(first, qx, zf).astype(BF16), s_prev.astype(BF16))
    o = o + _dot(jnp.where(first, zf, qx).astype(BF16), s_a.astype(BF16))
    return o, s_b


def _gla_init_body(k_ref, v_ref, gk_ref, s_out):
    causal, tri = _gla_consts()
    for p in range(GLA_HEADS // 2):
        k2 = k_ref[:, p * GLA_PAIR_K:(p + 1) * GLA_PAIR_K].astype(F32)
        v2 = v_ref[:, p * GLA_PAIR_V:(p + 1) * GLA_PAIR_V]
        gk2 = gk_ref[:, p * GLA_PAIR_K:(p + 1) * GLA_PAIR_K]
        s0 = jnp.zeros((GLA_PAIR_K, GLA_PAIR_V), F32)
        _, s = _gla_step(jnp.zeros_like(k2), k2, v2, gk2, s0, causal, tri)
        s_out[p] = s


def _gla_init_call(kg_m, vg_m, gk_m):
    return pl.pallas_call(
        _gla_init_body,
        out_shape=jax.ShapeDtypeStruct((GLA_HEADS // 2, GLA_PAIR_K, GLA_PAIR_V), F32),
        name="gla_init",
    )(kg_m, vg_m, gk_m)


def _gla_body(s0_ref, q_ref, k_ref, v_ref, gk_ref, rg_ref, nw_ref, o_ref, s_ref):
    t = pl.program_id(1)

    @pl.when(t == 0)
    def _():
        s_ref[...] = s0_ref[...]

    causal, tri = _gla_consts()
    nw = nw_ref[...]
    for p in range(GLA_HEADS // 2):
        ks = slice(p * GLA_PAIR_K, (p + 1) * GLA_PAIR_K)
        vs = slice(p * GLA_PAIR_V, (p + 1) * GLA_PAIR_V)
        s = s_ref[p]
        for u in range(GLA_GRID_STEPS):
            rows = slice(u * GLA_STEP, (u + 1) * GLA_STEP)
            o, s = _gla_step(q_ref[rows, ks].astype(F32), k_ref[rows, ks].astype(F32), v_ref[rows, vs],
                             gk_ref[rows, ks], s, causal, tri)
            outs = []
            for hh in range(2):
                oh = o[:, hh * GLA_VAL_DIM:(hh + 1) * GLA_VAL_DIM]
                ms = jnp.mean(oh * oh, axis=-1, keepdims=True)
                outs.append(oh * lax.rsqrt(ms + NORM_EPS) * nw)
            on = jnp.concatenate(outs, axis=1)
            o_ref[rows, vs] = (on * rg_ref[rows, vs].astype(F32)).astype(BF16)
        s_ref[p] = s


GLA_GRID_STEPS = 4


def _gla_call(s0, qg, kg, vg, gk, rg, nw, batch, seq):
    rows = GLA_GRID_STEPS * GLA_STEP
    assert seq % rows == 0
    steps = seq // rows
    spec = lambda width: pl.BlockSpec((None, rows, width), lambda b, t: (b, t, 0))
    return pl.pallas_call(
        _gla_body,
        grid=(batch, steps),
        in_specs=[_const_spec((GLA_HEADS // 2, GLA_PAIR_K, GLA_PAIR_V)),
                  spec(GLA_K_W), spec(GLA_K_W), spec(GLA_V_W), spec(GLA_K_W), spec(GLA_V_W),
                  _const_spec((1, GLA_VAL_DIM))],
        out_specs=spec(GLA_V_W),
        out_shape=jax.ShapeDtypeStruct((batch, seq, GLA_V_W), BF16),
        scratch_shapes=[pltpu.VMEM((GLA_HEADS // 2, GLA_PAIR_K, GLA_PAIR_V), F32)],
        compiler_params=pltpu.CompilerParams(dimension_semantics=("arbitrary", "arbitrary"),
                                             vmem_limit_bytes=VMEM_LIMIT),
        name="gla",
    )(s0, qg, kg, vg, gk, rg, nw)


ROUTER_W = LANES


RT_E0, RT_E1, RT_W0, RT_W1 = 0, 1, 2, 3


def _merge_body(x_ref, ao_ref, go_ref, sga_ref, sgg_ref, wab_ref, wgb_ref, wo_ref, n2_ref,
                wrh_ref, wrl_ref, h_out, rt_out, rtt_out, cnt_out):
    ya = _dot(ao_ref[...], wab_ref[...])
    yg = _dot(go_ref[...], wgb_ref[...])
    merged = sga_ref[...].astype(F32) * ya + sgg_ref[...].astype(F32) * yg
    h = x_ref[...] + _dot(merged.astype(BF16), wo_ref[...])
    _store_slabs(h_out, h)
    xn = _rms_norm(h, n2_ref[...])

    xh, xl = _split_hi_lo(xn)
    wrh = wrh_ref[...]
    lg = _dot(xh, wrh) + _dot(xl, wrh) + _dot(xh, wrl_ref[...])
    lg_t = lg.T
    tm = lg.shape[0]
    neg = -jnp.inf
    big = jnp.int32(1 << 20)
    gl = lg_t[N_EXPERTS:N_EXPERTS + N_GROUPS, :]
    grow = lax.broadcasted_iota(jnp.int32, (N_GROUPS, tm), 0)
    gmax = jnp.max(gl, axis=0, keepdims=True)
    g_idx = jnp.min(jnp.where(gl == gmax, grow, big), axis=0, keepdims=True)
    g_w = 1.0 / jnp.sum(jnp.exp(gl - gmax), axis=0, keepdims=True)
    erow = lax.broadcasted_iota(jnp.int32, (N_EXPERTS, tm), 0)
    el = jnp.where((erow >> 3) == g_idx, lg_t[0:N_EXPERTS, :], neg)
    m1 = jnp.max(el, axis=0, keepdims=True)
    i1 = jnp.min(jnp.where(el == m1, erow, big), axis=0, keepdims=True)
    el2 = jnp.where(erow == i1, neg, el)
    m2 = jnp.max(el2, axis=0, keepdims=True)
    i2 = jnp.min(jnp.where(el2 == m2, erow, big), axis=0, keepdims=True)
    e2 = jnp.exp(m2 - m1)
    p1 = g_w / (1.0 + e2)
    p2 = g_w * e2 / (1.0 + e2)
    def record(rows):
        r = lax.broadcasted_iota(jnp.int32, (rows, tm), 0)
        return jnp.where(r == RT_E0, i1.astype(F32),
                         jnp.where(r == RT_E1, i2.astype(F32),
                                   jnp.where(r == RT_W0, p1, jnp.where(r == RT_W1, p2, 0.0))))
    rtt_out[...] = record(8)
    rt_out[...] = record(LANES).T
    picked = jnp.where(jnp.logical_or(erow == i1, erow == i2), 1.0, 0.0)

    @pl.when(pl.program_id(0) == 0)
    def _():
        cnt_out[...] = jnp.zeros_like(cnt_out)

    cnt_out[...] += jnp.sum(picked, axis=1, keepdims=True)


def _merge_call(x2d, ao, go, sga, sgg, w, tm):
    n = x2d.shape[0]
    row = lambda width: pl.BlockSpec((tm, width), lambda i: (i, 0))
    return pl.pallas_call(
        _merge_body,
        grid=(n // tm,),
        in_specs=[row(D_MODEL), row(ATTN_Q_W), row(GLA_V_W), row(D_MODEL), row(D_MODEL),
                  _const_spec((ATTN_Q_W, D_MODEL)), _const_spec((GLA_V_W, D_MODEL)),
                  _const_spec((D_MODEL, D_MODEL)), _const_spec((1, D_MODEL)),
                  _const_spec((D_MODEL, ROUTER_W)), _const_spec((D_MODEL, ROUTER_W))],
        out_specs=[pl.BlockSpec((ROW_TILES, tm, LANES), lambda i: (0, i, 0)), row(ROUTER_W),
                   pl.BlockSpec((8, tm), lambda i: (0, i)), _const_spec((N_EXPERTS, LANES))],
        out_shape=[jax.ShapeDtypeStruct((ROW_TILES, n, LANES), F32),
                   jax.ShapeDtypeStruct((n, ROUTER_W), F32),
                   jax.ShapeDtypeStruct((8, n), F32),
                   jax.ShapeDtypeStruct((N_EXPERTS, LANES), F32)],
        compiler_params=pltpu.CompilerParams(dimension_semantics=("arbitrary",),
                                             vmem_limit_bytes=VMEM_LIMIT),
        name="merge",
    )(x2d, ao, go, sga, sgg, w["wab"], w["wgb"], w["wo"], w["norm2"], w["wrh"], w["wrl"])


EXPERT_TILE = 512
ROUTE_TILE = 512
VISIT_CAP = 256
VIS_EXPERT, VIS_TILE, VIS_LO, VIS_HI, VIS_COUNT = 0, 1, 2, 3, 4


def _num_visits(n_tokens):
    return (2 * n_tokens) // EXPERT_TILE + N_EXPERTS - 1


def _expert_cumsum_inclusive(x):
    row = lax.broadcasted_iota(jnp.int32, x.shape, 0)
    s = 1
    while s < N_EXPERTS:
        x = x + jnp.where(row >= s, pltpu.roll(x, s, 0), 0.0)
        s *= 2
    return x


def _route_body(rtt_ref, cnt_ref, pos_ref, vis_ref, carry_ref):
    i = pl.program_id(0)
    tm = rtt_ref.shape[1]
    cnt = cnt_ref[...]
    end = _expert_cumsum_inclusive(cnt)
    base = end - cnt

    @pl.when(i == 0)
    def _():
        carry_ref[...] = jnp.zeros_like(carry_ref)
        inv = 1.0 / EXPERT_TILE
        first = jnp.floor(base * inv)
        n_vis = jnp.where(cnt > 0.0, jnp.floor((end - 1.0) * inv) - first + 1.0, 0.0)
        vend = _expert_cumsum_inclusive(n_vis)
        vstart = vend - n_vis
        col = lambda a: a[:, 0:1]
        erow = lax.broadcasted_iota(jnp.int32, (N_EXPERTS, VISIT_CAP), 0).astype(F32)
        v = lax.broadcasted_iota(jnp.int32, (N_EXPERTS, VISIT_CAP), 1).astype(F32)
        ev = jnp.minimum(jnp.sum(jnp.where(col(vend) <= v, 1.0, 0.0), axis=0, keepdims=True),
                         N_EXPERTS - 1.0)
        mine = erow == ev
        pick = lambda a: jnp.sum(jnp.where(mine, a, 0.0), axis=0, keepdims=True)
        tile = pick(col(first) + v - col(vstart))
        lo = jnp.maximum(pick(col(base)) - tile * EXPERT_TILE, 0.0)
        hi = jnp.minimum(pick(col(end)) - tile * EXPERT_TILE, float(EXPERT_TILE))
        total = vend[N_EXPERTS - 1:N_EXPERTS, 0:1]
        r8 = lax.broadcasted_iota(jnp.int32, (8, VISIT_CAP), 0)
        rec = jnp.where(r8 == VIS_EXPERT, ev,
                        jnp.where(r8 == VIS_TILE, tile,
                                  jnp.where(r8 == VIS_LO, lo,
                                            jnp.where(r8 == VIS_HI, hi,
                                                      jnp.where(r8 == VIS_COUNT, total, 0.0)))))
        vis_ref[...] = rec.astype(jnp.int32)

    rtt = rtt_ref[...]
    erow = lax.broadcasted_iota(jnp.int32, (N_EXPERTS, tm), 0).astype(F32)
    oh0 = erow == rtt[RT_E0:RT_E0 + 1, :]
    oh1 = erow == rtt[RT_E1:RT_E1 + 1, :]
    picked = jnp.where(jnp.logical_or(oh0, oh1), 1.0, 0.0)
    r = lax.broadcasted_iota(jnp.int32, (tm, tm), 0)
    c = lax.broadcasted_iota(jnp.int32, (tm, tm), 1)
    earlier = jnp.where(r < c, 1.0, 0.0).astype(BF16)
    slot = _dot(picked.astype(BF16), earlier) + (carry_ref[:, 0:1] + base[:, 0:1])
    p0 = jnp.sum(jnp.where(oh0, slot, 0.0), axis=0, keepdims=True)
    p1 = jnp.sum(jnp.where(oh1, slot, 0.0), axis=0, keepdims=True)
    r8 = lax.broadcasted_iota(jnp.int32, (8, tm), 0)
    pos_ref[...] = jnp.where(r8 == 0, p0, jnp.where(r8 == 1, p1, 0.0)).astype(jnp.int32)
    carry_ref[...] += jnp.sum(picked, axis=1, keepdims=True)


def _route_call(rtt, cnt):
    n = rtt.shape[1]
    tm = _pick_tile(n, ROUTE_TILE)
    return pl.pallas_call(
        _route_body,
        grid=(n // tm,),
        in_specs=[pl.BlockSpec((8, tm), lambda i: (0, i)), _const_spec((N_EXPERTS, LANES))],
        out_specs=[pl.BlockSpec((8, tm), lambda i: (0, i)), _const_spec((8, VISIT_CAP))],
        out_shape=[jax.ShapeDtypeStruct((8, n), jnp.int32),
                   jax.ShapeDtypeStruct((8, VISIT_CAP), jnp.int32)],
        scratch_shapes=[pltpu.VMEM((N_EXPERTS, LANES), F32)],
        compiler_params=pltpu.CompilerParams(dimension_semantics=("arbitrary",)),
        name="route",
    )(rtt, cnt)


SCATTER_TILE = 512
ROW_TILES = D_MODEL // LANES


def _store_slabs(ref, val):
    for c in range(ROW_TILES):
        ref[c] = val[:, c * LANES:(c + 1) * LANES]


def _load_slabs(ref):
    return jnp.concatenate([ref[c] for c in range(ROW_TILES)], axis=1)


def _scatter_body(pos0_ref, pos1_ref, h_ref, hs_hbm, sem):
    tm = h_ref.shape[1]
    for r in range(tm):
        src = h_ref.at[:, r, :]
        pltpu.make_async_copy(src, hs_hbm.at[:, pos0_ref[r], :], sem).start(priority=0)
        pltpu.make_async_copy(src, hs_hbm.at[:, pos1_ref[r], :], sem).start(priority=1)
    for _ in range(2 * tm):
        pltpu.make_async_copy(h_ref.at[:, 0, :], hs_hbm.at[:, 0, :], sem).wait()


def _scatter_call(pos0, pos1, h_slabs, n_slots):
    n = h_slabs.shape[1]
    tm = _pick_tile(n, SCATTER_TILE)
    smem = lambda: pl.BlockSpec((tm,), lambda i: (i,), memory_space=pltpu.SMEM)
    return pl.pallas_call(
        _scatter_body,
        grid=(n // tm,),
        in_specs=[smem(), smem(), pl.BlockSpec((ROW_TILES, tm, LANES), lambda i: (0, i, 0))],
        out_specs=pl.BlockSpec(memory_space=pl.ANY),
        out_shape=jax.ShapeDtypeStruct((ROW_TILES, n_slots, LANES), F32),
        scratch_shapes=[pltpu.SemaphoreType.DMA],
        compiler_params=pltpu.CompilerParams(dimension_semantics=("arbitrary",)),
        name="scatter",
    )(pos0, pos1, h_slabs)


def _expert_body(vis_ref, h_ref, n2_ref, wg_ref, wu_ref, wd_ref, o_ref):
    j = pl.program_id(0)

    @pl.when(j < vis_ref[VIS_COUNT, 0])
    def _():
        lo_row = vis_ref[VIS_LO, j]
        hi_row = vis_ref[VIS_HI, j]
        x = _rms_norm(_load_slabs(h_ref), n2_ref[...]).astype(BF16)
        g = _dot(x, wg_ref[...].astype(BF16))
        u = _dot(x, wu_ref[...].astype(BF16))
        hg = (g * _sigmoid(g) * u).astype(BF16)
        new = _dot(hg, wd_ref[...].astype(BF16))
        row = lax.broadcasted_iota(jnp.int32, new.shape, 0)

        @pl.when(lo_row == 0)
        def _():
            _store_slabs(o_ref, jnp.where(row < hi_row, new, 0.0))

        @pl.when(lo_row > 0)
        def _():
            keep = jnp.logical_and(row >= lo_row, row < hi_row)
            _store_slabs(o_ref, jnp.where(keep, new, _load_slabs(o_ref)))


def _expert_call(vis, hs, n2, wg, wu, wd, n_visits):
    def visit(j, vis):
        return jnp.minimum(j, vis[VIS_COUNT, 0] - 1)

    def xmap(j, vis):
        return (0, vis[VIS_TILE, visit(j, vis)], 0)

    def wmap(j, vis):
        return (vis[VIS_EXPERT, visit(j, vis)], 0, 0)

    grid_spec = pltpu.PrefetchScalarGridSpec(
        num_scalar_prefetch=1,
        grid=(n_visits,),
        in_specs=[pl.BlockSpec((ROW_TILES, EXPERT_TILE, LANES), xmap),
                  pl.BlockSpec((1, D_MODEL), lambda j, vis: (0, 0)),
                  pl.BlockSpec((None, D_MODEL, EXPERT_FF), wmap),
                  pl.BlockSpec((None, D_MODEL, EXPERT_FF), wmap),
                  pl.BlockSpec((None, EXPERT_FF, D_MODEL), wmap)],
        out_specs=pl.BlockSpec((ROW_TILES, EXPERT_TILE, LANES), xmap),
    )
    return pl.pallas_call(
        _expert_body,
        grid_spec=grid_spec,
        out_shape=jax.ShapeDtypeStruct(hs.shape, F32),
        compiler_params=pltpu.CompilerParams(dimension_semantics=("arbitrary",),
                                             vmem_limit_bytes=VMEM_LIMIT),
        name="experts",
    )(vis, hs, n2, wg, wu, wd)


COMBINE_TILE = 256


def _combine_body(p0_ref, p1_ref, p0n_ref, p1n_ref, h_ref, rt_ref, y_hbm, o_ref, g_ref, sem):
    i = pl.program_id(0)
    tm = h_ref.shape[1]
    cur = i % 2

    def issue(q0_ref, q1_ref, buf):
        for r in range(tm):
            pltpu.make_async_copy(y_hbm.at[:, q0_ref[r], :], g_ref.at[buf, 0, :, r, :],
                                  sem.at[buf]).start(priority=0)
            pltpu.make_async_copy(y_hbm.at[:, q1_ref[r], :], g_ref.at[buf, 1, :, r, :],
                                  sem.at[buf]).start(priority=1)

    @pl.when(i == 0)
    def _():
        issue(p0_ref, p1_ref, 0)

    @pl.when(i + 1 < pl.num_programs(0))
    def _():
        issue(p0n_ref, p1n_ref, 1 - cur)

    for _ in range(2 * tm):
        pltpu.make_async_copy(y_hbm.at[:, 0, :], g_ref.at[cur, 0, :, 0, :], sem.at[cur]).wait()
    rt = rt_ref[...]
    o_ref[...] = (_load_slabs(h_ref) + rt[:, RT_W0:RT_W0 + 1] * _load_slabs(g_ref.at[cur, 0])
                  + rt[:, RT_W1:RT_W1 + 1] * _load_slabs(g_ref.at[cur, 1]))


def _combine_call(pos0, pos1, h_slabs, rt, y):
    n = h_slabs.shape[1]
    tm = _pick_tile(n, COMBINE_TILE)
    steps = n // tm
    smem_cur = lambda: pl.BlockSpec((tm,), lambda i: (i,), memory_space=pltpu.SMEM)
    smem_next = lambda: pl.BlockSpec((tm,), lambda i: (jnp.minimum(i + 1, steps - 1),),
                                     memory_space=pltpu.SMEM)
    row = lambda width: pl.BlockSpec((tm, width), lambda i: (i, 0))
    return pl.pallas_call(
        _combine_body,
        grid=(steps,),
        in_specs=[smem_cur(), smem_cur(), smem_next(), smem_next(),
                  pl.BlockSpec((ROW_TILES, tm, LANES), lambda i: (0, i, 0)), row(ROUTER_W),
                  pl.BlockSpec(memory_space=pl.ANY)],
        out_specs=row(D_MODEL),
        out_shape=jax.ShapeDtypeStruct((n, D_MODEL), F32),
        scratch_shapes=[pltpu.VMEM((2, 2, ROW_TILES, tm, LANES), F32), pltpu.SemaphoreType.DMA((2,))],
        compiler_params=pltpu.CompilerParams(dimension_semantics=("arbitrary",),
                                             vmem_limit_bytes=VMEM_LIMIT),
        name="combine",
    )(pos0, pos1, pos0, pos1, h_slabs, rt, y)


def _rope_tables(pos):
    half = HEAD_DIM // 2
    inv_freq = ROPE_THETA ** (-(jnp.arange(half, dtype=F32) * 2.0) / HEAD_DIM)
    ang = pos.astype(F32)[:, None] * inv_freq[None, :]
    cos = jnp.cos(ang)
    sin = jnp.sin(ang)
    cos64 = jnp.concatenate([cos, cos], axis=1)
    sin64 = jnp.concatenate([-sin, sin], axis=1)
    return jnp.tile(cos64, (1, LANES // HEAD_DIM)), jnp.tile(sin64, (1, LANES // HEAD_DIM))


def _dup_heads(wcols):
    d = wcols.shape[0]
    w3 = wcols.reshape(d, ATTN_KV_HEADS, 1, HEAD_DIM)
    return jnp.broadcast_to(w3, (d, ATTN_KV_HEADS, 2, HEAD_DIM)).reshape(d, ATTN_KV_DUP_W)


def _prep_weights(norm1_w, w_in, q_norm_w, k_norm_w, gla_gate_up, gla_gate_bias, w_attn_branch,
                  w_gla_branch, w_out, norm2_w, router_group, router_expert):
    wb = w_in.astype(BF16)
    wgd = jnp.zeros((D_MODEL, LANES), BF16).at[:, :GLA_GATE_RANK].set(wb[:, _OFF_GD:_OFF_GD + GLA_GATE_RANK])
    gup = jnp.zeros((LANES, GLA_K_W), BF16).at[:GLA_GATE_RANK].set(gla_gate_up.astype(BF16))
    wr = jnp.zeros((D_MODEL, ROUTER_W), F32)
    wr = wr.at[:, :N_EXPERTS].set(router_expert.transpose(1, 0, 2).reshape(D_MODEL, N_EXPERTS))
    wr = wr.at[:, N_EXPERTS:N_EXPERTS + N_GROUPS].set(router_group)
    wrh = wr.astype(BF16)
    wrl = (wr - wrh.astype(F32)).astype(BF16)
    return {
        "norm1": norm1_w.reshape(1, D_MODEL),
        "wq": wb[:, _OFF_Q:_OFF_Q + ATTN_Q_W],
        "wk": _dup_heads(wb[:, _OFF_K:_OFF_K + ATTN_KV_W]),
        "wv": _dup_heads(wb[:, _OFF_V:_OFF_V + ATTN_KV_W]),
        "wgla": wb[:, _OFF_GQ:_OFF_GQ + GLA_W],
        "wgd": wgd,
        "wgate": wb[:, _OFF_GATE:_OFF_GATE + 2 * D_MODEL],
        "qn": jnp.tile(q_norm_w, ATTN_HEADS).reshape(1, ATTN_Q_W),
        "kn": jnp.tile(k_norm_w, ATTN_KV_DUP_W // HEAD_DIM).reshape(1, ATTN_KV_DUP_W),
        "gup": gup,
        "gb": gla_gate_bias.reshape(1, GLA_K_W),
        "wab": w_attn_branch.astype(BF16),
        "wgb": w_gla_branch.astype(BF16),
        "wo": w_out.astype(BF16),
        "norm2": norm2_w.reshape(1, D_MODEL),
        "wrh": wrh,
        "wrl": wrl,
    }


def _pick_tile(n, want):
    t = want
    while n % t:
        t //= 2
    return t


def kernel(x, meta_tokens, norm1_w, w_in, q_norm_w, k_norm_w, attn_sinks, gla_gate_up, gla_gate_bias,
           gla_norm_w, w_attn_branch, w_gla_branch, w_out, norm2_w, router_group, router_expert,
           expert_w_gate, expert_w_up, expert_w_down):
    batch, seq, d = x.shape
    assert d == D_MODEL and seq % GLA_STEP == 0 and norm1_w.shape[0] == 1
    n = batch * seq
    w = _prep_weights(norm1_w[0], w_in[0], q_norm_w[0], k_norm_w[0], gla_gate_up[0], gla_gate_bias[0],
                      w_attn_branch[0], w_gla_branch[0], w_out[0], norm2_w[0], router_group[0],
                      router_expert[0])
    x2d = x.reshape(n, D_MODEL)

    cos_m, sin_m = _rope_tables(jnp.arange(N_META))
    cos_r, sin_r = _rope_tables(jnp.arange(seq) + N_META)
    meta = _proj_call(meta_tokens.astype(F32), N_META, 1, w, cos_m, sin_m)
    tm = _pick_tile(seq, 512)
    (q, kd, vd, qg, kg, vg, rg, gk, sga, sgg) = _proj_call(x2d, tm, seq // tm, w, cos_r, sin_r)

    pad_after = ((0, BLOCK - N_META), (0, 0))
    km = jnp.pad(meta[1], pad_after)
    vm = jnp.pad(meta[2], pad_after)
    ao = _attn_call(q.reshape(batch, seq, ATTN_Q_W), kd.reshape(batch, seq, ATTN_KV_DUP_W),
                    vd.reshape(batch, seq, ATTN_KV_DUP_W), km, vm, attn_sinks[0].astype(F32), batch, seq)

    pad_before = ((GLA_STEP - N_META, 0), (0, 0))
    s0 = _gla_init_call(jnp.pad(meta[4], pad_before), jnp.pad(meta[5], pad_before),
                        jnp.pad(meta[7], pad_before))
    go = _gla_call(s0, qg.reshape(batch, seq, GLA_K_W), kg.reshape(batch, seq, GLA_K_W),
                   vg.reshape(batch, seq, GLA_V_W), gk.reshape(batch, seq, GLA_K_W),
                   rg.reshape(batch, seq, GLA_V_W), gla_norm_w[0].reshape(1, GLA_VAL_DIM), batch, seq)

    h, rt, rtt, cnt = _merge_call(x2d, ao.reshape(n, ATTN_Q_W), go.reshape(n, GLA_V_W), sga, sgg, w,
                                  _pick_tile(n, 512))

    assert (2 * n) % EXPERT_TILE == 0 and _num_visits(n) <= VISIT_CAP
    pos, vis = _route_call(rtt, cnt)
    pos0, pos1 = pos[0], pos[1]
    hs = _scatter_call(pos0, pos1, h, 2 * n)
    wg = expert_w_gate[0].reshape(N_EXPERTS, D_MODEL, EXPERT_FF)
    wu = expert_w_up[0].reshape(N_EXPERTS, D_MODEL, EXPERT_FF)
    wd = expert_w_down[0].reshape(N_EXPERTS, EXPERT_FF, D_MODEL)
    y = _expert_call(vis, hs, w["norm2"], wg, wu, wd, _num_visits(n))
    out = _combine_call(pos0, pos1, h, rt, y)
    return out.reshape(batch, seq, D_MODEL)
```

```python
import functools
import math

import jax
import jax.numpy as jnp
from jax import lax
from jax.experimental import pallas as pl
from jax.experimental.pallas import tpu as pltpu

F32 = jnp.float32
BF16 = jnp.bfloat16

D_MODEL = 1024
N_META = 16
BLOCK = 128
ATTN_HEADS = 8
ATTN_KV_HEADS = 2
HEAD_DIM = 64
ROPE_THETA = 10000.0
ATTN_Q_W = ATTN_HEADS * HEAD_DIM
ATTN_KV_W = ATTN_KV_HEADS * HEAD_DIM
GLA_HEADS = 4
GLA_KEY_DIM = 64
GLA_VAL_DIM = 128
GLA_K_W = GLA_HEADS * GLA_KEY_DIM
GLA_V_W = GLA_HEADS * GLA_VAL_DIM
GLA_GATE_RANK = 16
GLA_GATE_NORM = 16.0
GLA_CHUNK = 64
N_GROUPS = 4
EXPERTS_PER_GROUP = 8
N_EXPERTS = N_GROUPS * EXPERTS_PER_GROUP
EXPERT_FF = 256
NORM_EPS = 1e-6
MASK_VALUE = -1e30

LANES = 128
ATTN_KV_DUP_W = 2 * ATTN_KV_W
VMEM_LIMIT = 56 * 1024 * 1024

_OFF_Q = 0
_OFF_K = _OFF_Q + ATTN_Q_W
_OFF_V = _OFF_K + ATTN_KV_W
_OFF_GQ = _OFF_V + ATTN_KV_W
_OFF_GD = _OFF_GQ + 2 * GLA_K_W + 2 * GLA_V_W
_OFF_GATE = _OFF_GD + GLA_GATE_RANK
GLA_W = 2 * GLA_K_W + 2 * GLA_V_W
QKV_GD_OFF = _OFF_GQ
QKV_W = D_MODEL


def _dot(a, b):
    return jnp.dot(a, b, preferred_element_type=F32)


def _dot_nt(a, b):
    return lax.dot_general(a, b, (((1,), (1,)), ((), ())), preferred_element_type=F32)


def _split_hi_lo(x):
    hi = x.astype(BF16)
    lo = (x - hi.astype(F32)).astype(BF16)
    return hi, lo


MXU_TILE = 256


def _group_mean_sq(x, group):
    w = x.shape[-1]
    slab = min(w, MXU_TILE)
    shift = int(math.log2(group))
    r = lax.broadcasted_iota(jnp.int32, (slab, slab), 0) >> shift
    c = lax.broadcasted_iota(jnp.int32, (slab, slab), 1) >> shift
    ones = jnp.where(r == c, 1.0, 0.0).astype(BF16)
    hi, lo = _split_hi_lo(x * x)
    sums = [_dot(hi[:, s:s + slab], ones) + _dot(lo[:, s:s + slab], ones) for s in range(0, w, slab)]
    return (sums[0] if len(sums) == 1 else jnp.concatenate(sums, axis=1)) * (1.0 / group)


def _rope(x, cos, sin_signed):
    w = x.shape[-1]
    reps = w // LANES
    if reps > 1:
        cos = jnp.concatenate([cos] * reps, axis=1)
        sin_signed = jnp.concatenate([sin_signed] * reps, axis=1)
    lane = lax.broadcasted_iota(jnp.int32, x.shape, 1)
    first_half = (lane & (HEAD_DIM // 2)) == 0
    swapped = jnp.where(first_half, pltpu.roll(x, w - HEAD_DIM // 2, 1), pltpu.roll(x, HEAD_DIM // 2, 1))
    return x * cos + swapped * sin_signed


def _rms_norm(x, w):
    ms = jnp.mean(x * x, axis=-1, keepdims=True)
    return x * lax.rsqrt(ms + NORM_EPS) * w


def _sigmoid(x):
    return 0.5 * jnp.tanh(0.5 * x) + 0.5


def _log_sigmoid(x):
    return jnp.minimum(x, 0.0) - jnp.log(1.0 + jnp.exp(-jnp.abs(x)))


def _dup_heads_on_lanes(x):
    low = lax.broadcasted_iota(jnp.int32, x.shape, 1) < HEAD_DIM
    swapped = pltpu.roll(x, HEAD_DIM, 1)
    return jnp.concatenate([jnp.where(low, x, swapped), jnp.where(low, swapped, x)], axis=1)


def _proj_body(x_ref, n1_ref, wqkv_ref, wgla_ref, wgate_ref,
               qn_ref, kn_ref, cos_ref, sin_ref, gup_ref, gb_ref,
               q_out, k_out, v_out, qg_out, kg_out, vg_out, rg_out, gk_out, sga_out, sgg_out):
    xn = _rms_norm(x_ref[...], n1_ref[...]).astype(BF16)
    cos = cos_ref[...]
    sin = sin_ref[...]

    a = _dot(xn, wqkv_ref[...])
    q = a[:, :ATTN_Q_W]
    q = q * lax.rsqrt(_group_mean_sq(q, HEAD_DIM) + NORM_EPS) * qn_ref[...]
    q_out[...] = (_rope(q, cos, sin) * (HEAD_DIM ** -0.5)).astype(BF16)

    k = a[:, _OFF_K:_OFF_K + ATTN_KV_W]
    k = k * lax.rsqrt(_group_mean_sq(k, HEAD_DIM) + NORM_EPS) * kn_ref[...]
    k_out[...] = _dup_heads_on_lanes(_rope(k, cos, sin)).astype(BF16)

    v_out[...] = _dup_heads_on_lanes(a[:, _OFF_V:_OFF_V + ATTN_KV_W]).astype(BF16)

    g = _dot(xn, wgla_ref[...])
    qg_out[...] = (g[:, :GLA_K_W] * (GLA_KEY_DIM ** -0.5)).astype(BF16)
    kg_out[...] = g[:, GLA_K_W:2 * GLA_K_W].astype(BF16)
    vg_out[...] = g[:, 2 * GLA_K_W:2 * GLA_K_W + GLA_V_W].astype(BF16)
    r = g[:, 2 * GLA_K_W + GLA_V_W:]
    rg_out[...] = (r * _sigmoid(r)).astype(BF16)

    gd = a[:, QKV_GD_OFF:QKV_GD_OFF + LANES].astype(BF16)
    z = _dot(gd, gup_ref[...]) + gb_ref[...]
    gk_out[...] = _log_sigmoid(z) * (1.0 / GLA_GATE_NORM)

    gates = _dot(xn, wgate_ref[...])
    sga_out[...] = _sigmoid(gates[:, :D_MODEL]).astype(BF16)
    sgg_out[...] = _sigmoid(gates[:, D_MODEL:]).astype(BF16)


def _const_spec(shape):
    nd = len(shape)
    return pl.BlockSpec(shape, lambda *_: (0,) * nd)


def _proj_call(x2d, tm, tiles_per_seq, w, cos, sin):
    n = x2d.shape[0]
    grid = (n // tm,)
    row = lambda width: pl.BlockSpec((tm, width), lambda i: (i, 0))
    tab = pl.BlockSpec((tm, LANES), lambda i: (i % tiles_per_seq, 0))
    in_specs = [
        row(D_MODEL), _const_spec((1, D_MODEL)),
        _const_spec((D_MODEL, QKV_W)), _const_spec((D_MODEL, GLA_W)),
        _const_spec((D_MODEL, 2 * D_MODEL)),
        _const_spec((1, ATTN_Q_W)), _const_spec((1, ATTN_KV_W)),
        tab, tab, _const_spec((LANES, GLA_K_W)), _const_spec((1, GLA_K_W)),
    ]
    out_widths = [ATTN_Q_W, ATTN_KV_DUP_W, ATTN_KV_DUP_W, GLA_K_W, GLA_K_W, GLA_V_W, GLA_V_W,
                  GLA_K_W, D_MODEL, D_MODEL]
    out_dtypes = [BF16] * 7 + [F32] + [BF16] * 2
    return pl.pallas_call(
        _proj_body,
        grid=grid,
        in_specs=in_specs,
        out_specs=[row(wd) for wd in out_widths],
        out_shape=[jax.ShapeDtypeStruct((n, wd), dt) for wd, dt in zip(out_widths, out_dtypes)],
        compiler_params=pltpu.CompilerParams(dimension_semantics=("arbitrary",),
                                             vmem_limit_bytes=VMEM_LIMIT),
        name="proj",
    )(x2d, w["norm1"], w["wqkv"], w["wgla"], w["wgate"],
      w["qn"], w["kn"], cos, sin, w["gup"], w["gb"])


ATTN_STEP_BLOCKS = 4


def _attn_body(sink_ref, q_ref, kp_ref, kc_ref, km_ref, vp_ref, vc_ref, vm_ref, o_ref):
    i = pl.program_id(1)
    nblk = ATTN_STEP_BLOCKS
    pair_rows = 2 * BLOCK
    lane = lax.broadcasted_iota(jnp.int32, (BLOCK, LANES), 1)
    low = lane < HEAD_DIM
    low2 = lax.broadcasted_iota(jnp.int32, (pair_rows, LANES), 1) < HEAD_DIM
    top2 = lax.broadcasted_iota(jnp.int32, (pair_rows, 1), 0) < BLOCK
    key = lax.broadcasted_iota(jnp.int32, (pair_rows, 2 * LANES), 1) & (LANES - 1)
    rowi = lax.broadcasted_iota(jnp.int32, (pair_rows, 2 * LANES), 0) & (BLOCK - 1)
    mask_meta = key < N_META
    mask_prev = key > rowi
    mask_cur = key <= rowi
    zero = jnp.zeros((), BF16)

    def block_diag(x):
        return jnp.concatenate([jnp.where(low, x, zero), jnp.where(low, zero, x)], axis=0)

    blk_row = lax.broadcasted_iota(jnp.int32, (2 * BLOCK, LANES), 0)
    blk_lane = lax.broadcasted_iota(jnp.int32, (2 * BLOCK, LANES), 1)
    ones_blk = jnp.where((blk_row < BLOCK) == (blk_lane < HEAD_DIM), 1.0, 0.0).astype(BF16)

    def masked(s, mask):
        return jnp.where(mask, s, MASK_VALUE)

    for kvh in range(ATTN_KV_HEADS):
        sl = slice(kvh * LANES, (kvh + 1) * LANES)
        k_seq = [block_diag(kp_ref[:, sl])]
        v_seq = [block_diag(vp_ref[:, sl])]
        for t in range(nblk):
            rows = slice(t * BLOCK, (t + 1) * BLOCK)
            k_seq.append(block_diag(kc_ref[rows, sl]))
            v_seq.append(block_diag(vc_ref[rows, sl]))
        q_all = jnp.concatenate(
            [q_ref[t * BLOCK:(t + 1) * BLOCK, (2 * kvh + pr) * LANES:(2 * kvh + pr + 1) * LANES]
             for t in range(nblk) for pr in range(2)], axis=0)
        s_meta = _dot_nt(q_all, block_diag(km_ref[:, sl]))
        s_seq = []
        for j in range(nblk + 1):
            lo_blk, hi_blk = max(j - 1, 0), min(j, nblk - 1)
            s_seq.append(_dot_nt(q_all[lo_blk * pair_rows:(hi_blk + 1) * pair_rows], k_seq[j]))

        def sink_col(hh):
            return jnp.where(top2, sink_ref[4 * kvh + hh], sink_ref[4 * kvh + 2 + hh])

        p_meta, p_prev, p_cur, l_all = [], [], [], []
        for t in range(nblk):
            sm = masked(s_meta[t * pair_rows:(t + 1) * pair_rows], mask_meta)
            prev_rows = slice(0, pair_rows) if t == 0 else slice(pair_rows, 2 * pair_rows)
            sb = jnp.where(mask_prev, s_seq[t][prev_rows], s_seq[t + 1][0:pair_rows])
            if t == 0:
                sb = masked(sb, jnp.logical_or(mask_cur, i > 0))
            s_max = jnp.maximum(sm, sb)
            m2 = []
            sink_terms = []
            for hh in range(2):
                sink = sink_col(hh)
                m = jnp.maximum(jnp.max(s_max[:, hh * LANES:(hh + 1) * LANES], axis=1, keepdims=True), sink)
                m2.append(m)
                sink_terms.append(jnp.exp(sink - m))

            def probs(s):
                return jnp.concatenate([jnp.exp(s[:, hh * LANES:(hh + 1) * LANES] - m2[hh])
                                        for hh in range(2)], axis=1)

            pm, pb = probs(sm), probs(sb)
            pb16 = pb.astype(BF16)
            p_meta.append(pm.astype(BF16))
            p_prev.append(jnp.where(mask_prev, pb16, zero))
            p_cur.append(jnp.where(mask_prev, zero, pb16))
            l_all.append(_dot((pm + pb).astype(BF16), ones_blk)
                         + jnp.where(low2, sink_terms[0], sink_terms[1]))

        o_meta = _dot(jnp.concatenate(p_meta, axis=0), block_diag(vm_ref[:, sl]))
        o_seq = []
        for j in range(nblk + 1):
            parts = ([p_cur[j - 1]] if j >= 1 else []) + ([p_prev[j]] if j < nblk else [])
            o_seq.append(_dot(jnp.concatenate(parts, axis=0) if len(parts) > 1 else parts[0], v_seq[j]))
        for t in range(nblk):
            prev_rows = slice(0, pair_rows) if t == 0 else slice(pair_rows, 2 * pair_rows)
            o = (o_meta[t * pair_rows:(t + 1) * pair_rows] + o_seq[t][prev_rows]
                 + o_seq[t + 1][0:pair_rows]) / l_all[t]
            for pr in range(2):
                col = (2 * kvh + pr) * LANES
                o_ref[t * BLOCK:(t + 1) * BLOCK, col:col + LANES] = o[pr * BLOCK:(pr + 1) * BLOCK].astype(BF16)


def _attn_call(q, kd, vd, km, vm, sinks, batch, seq):
    step = ATTN_STEP_BLOCKS * BLOCK
    assert seq % step == 0
    qspec = pl.BlockSpec((None, step, ATTN_Q_W), lambda b, i, s: (b, i, 0))
    prev = pl.BlockSpec((None, BLOCK, ATTN_KV_DUP_W),
                        lambda b, i, s: (b, jnp.maximum(i * ATTN_STEP_BLOCKS - 1, 0), 0))
    cur = pl.BlockSpec((None, step, ATTN_KV_DUP_W), lambda b, i, s: (b, i, 0))
    meta = pl.BlockSpec((BLOCK, ATTN_KV_DUP_W), lambda b, i, s: (0, 0))
    grid_spec = pltpu.PrefetchScalarGridSpec(
        num_scalar_prefetch=1,
        grid=(batch, seq // step),
        in_specs=[qspec, prev, cur, meta, prev, cur, meta],
        out_specs=pl.BlockSpec((None, step, ATTN_Q_W), lambda b, i, s: (b, i, 0)),
    )
    return pl.pallas_call(
        _attn_body,
        grid_spec=grid_spec,
        out_shape=jax.ShapeDtypeStruct((batch, seq, ATTN_Q_W), BF16),
        compiler_params=pltpu.CompilerParams(dimension_semantics=("arbitrary", "arbitrary"),
                                             vmem_limit_bytes=VMEM_LIMIT),
        name="attn",
    )(sinks, q, kd, kd, km, vd, vd, vm)


GLA_STEP = 2 * GLA_CHUNK
GLA_PAIR_K = 2 * GLA_KEY_DIM
GLA_PAIR_V = 2 * GLA_VAL_DIM


def _gla_consts():
    r = lax.broadcasted_iota(jnp.int32, (GLA_STEP, 2 * GLA_STEP), 0)
    c = lax.broadcasted_iota(jnp.int32, (GLA_STEP, 2 * GLA_STEP), 1) & (GLA_STEP - 1)
    causal = jnp.logical_and((r >> 6) == (c >> 6), c <= r)
    tri = jnp.where(causal[:, :GLA_STEP], 1.0, 0.0).astype(BF16)
    return causal, tri


def _gla_step(q2, k2, v2, gk2, s_prev, causal, tri):
    row = lax.broadcasted_iota(jnp.int32, (GLA_STEP, GLA_PAIR_K), 0)
    lane = lax.broadcasted_iota(jnp.int32, (GLA_STEP, GLA_PAIR_K), 1)
    first = row < GLA_CHUNK
    low = lane < GLA_KEY_DIM
    g_hi = gk2.astype(BF16)
    g_r = gk2 - g_hi.astype(F32)
    g_mid = g_r.astype(BF16)
    g_lo = (g_r - g_mid.astype(F32)).astype(BF16)
    b = _dot(tri, g_hi) + _dot(tri, g_mid) + _dot(tri, g_lo)
    half = GLA_CHUNK // 2
    b_mid = jnp.where(first, b[half:half + 1, :], b[GLA_CHUNK + half:GLA_CHUNK + half + 1, :])
    qi = (q2 * jnp.exp(b - b_mid)).astype(BF16)
    ki = (k2 * jnp.exp(b_mid - b)).astype(BF16)
    qx = q2 * jnp.exp(b)
    zero = jnp.zeros((), BF16)
    kblk = jnp.concatenate([jnp.where(low, ki, zero), jnp.where(low, zero, ki)], axis=0)
    att = _dot_nt(qi, kblk)
    att = jnp.where(causal, att, 0.0).astype(BF16)
    zv = jnp.zeros((GLA_STEP, GLA_VAL_DIM), BF16)
    vblk = jnp.concatenate([jnp.concatenate([v2[:, :GLA_VAL_DIM], zv], axis=1),
                            jnp.concatenate([zv, v2[:, GLA_VAL_DIM:]], axis=1)], axis=0)
    o = _dot(att, vblk)
    b_t = b.T
    k_t = k2.T
    lane_t = lax.broadcasted_iota(jnp.int32, (GLA_PAIR_K, GLA_STEP), 1)
    first_t = lane_t < GLA_CHUNK
    bl_a = b_t[:, GLA_CHUNK - 1:GLA_CHUNK]
    bl_b = b_t[:, GLA_STEP - 1:GLA_STEP]
    kx_t = k_t * jnp.exp(jnp.where(first_t, bl_a, bl_b) - b_t)
    zf = jnp.zeros((), F32)
    kx_a = jnp.where(first_t, kx_t, zf).astype(BF16)
    kx_b = jnp.where(first_t, zf, kx_t).astype(BF16)
    srow = lax.broadcasted_iota(jnp.int32, (GLA_PAIR_K, GLA_PAIR_V), 0)
    slane = lax.broadcasted_iota(jnp.int32, (GLA_PAIR_K, GLA_PAIR_V), 1)
    diag = (srow < GLA_KEY_DIM) == (slane < GLA_VAL_DIM)
    s_a = jnp.exp(bl_a) * s_prev + jnp.where(diag, _dot(kx_a, v2), 0.0)
    s_b = jnp.exp(bl_b) * s_a + jnp.where(diag, _dot(kx_b, v2), 0.0)
    o = o + _dot(jnp.where(first, qx, zf).astype(BF16), s_prev.astype(BF16))
    o = o + _dot(jnp.where(first, zf, qx).astype(BF16), s_a.astype(BF16))
    return o, s_b


def _gla_init_body(k_ref, v_ref, gk_ref, s_out):
    causal, tri = _gla_consts()
    for p in range(GLA_HEADS // 2):
        k2 = k_ref[:, p * GLA_PAIR_K:(p + 1) * GLA_PAIR_K].astype(F32)
        v2 = v_ref[:, p * GLA_PAIR_V:(p + 1) * GLA_PAIR_V]
        gk2 = gk_ref[:, p * GLA_PAIR_K:(p + 1) * GLA_PAIR_K]
        s0 = jnp.zeros((GLA_PAIR_K, GLA_PAIR_V), F32)
        _, s = _gla_step(jnp.zeros_like(k2), k2, v2, gk2, s0, causal, tri)
        s_out[p] = s


def _gla_init_call(kg_m, vg_m, gk_m):
    return pl.pallas_call(
        _gla_init_body,
        out_shape=jax.ShapeDtypeStruct((GLA_HEADS // 2, GLA_PAIR_K, GLA_PAIR_V), F32),
        name="gla_init",
    )(kg_m, vg_m, gk_m)


def _gla_body(s0_ref, q_ref, k_ref, v_ref, gk_ref, rg_ref, nw_ref, o_ref, s_ref):
    t = pl.program_id(1)

    @pl.when(t == 0)
    def _():
        s_ref[...] = s0_ref[...]

    causal, tri = _gla_consts()
    nw = nw_ref[...]
    for p in range(GLA_HEADS // 2):
        ks = slice(p * GLA_PAIR_K, (p + 1) * GLA_PAIR_K)
        vs = slice(p * GLA_PAIR_V, (p + 1) * GLA_PAIR_V)
        s = s_ref[p]
        for u in range(GLA_GRID_STEPS):
            rows = slice(u * GLA_STEP, (u + 1) * GLA_STEP)
            o, s = _gla_step(q_ref[rows, ks].astype(F32), k_ref[rows, ks].astype(F32), v_ref[rows, vs],
                             gk_ref[rows, ks], s, causal, tri)
            outs = []
            for hh in range(2):
                oh = o[:, hh * GLA_VAL_DIM:(hh + 1) * GLA_VAL_DIM]
                ms = jnp.mean(oh * oh, axis=-1, keepdims=True)
                outs.append(oh * lax.rsqrt(ms + NORM_EPS) * nw)
            on = jnp.concatenate(outs, axis=1)
            o_ref[rows, vs] = (on * rg_ref[rows, vs].astype(F32)).astype(BF16)
        s_ref[p] = s


GLA_GRID_STEPS = 4


def _gla_call(s0, qg, kg, vg, gk, rg, nw, batch, seq):
    rows = GLA_GRID_STEPS * GLA_STEP
    assert seq % rows == 0
    steps = seq // rows
    spec = lambda width: pl.BlockSpec((None, rows, width), lambda b, t: (b, t, 0))
    return pl.pallas_call(
        _gla_body,
        grid=(batch, steps),
        in_specs=[_const_spec((GLA_HEADS // 2, GLA_PAIR_K, GLA_PAIR_V)),
                  spec(GLA_K_W), spec(GLA_K_W), spec(GLA_V_W), spec(GLA_K_W), spec(GLA_V_W),
                  _const_spec((1, GLA_VAL_DIM))],
        out_specs=spec(GLA_V_W),
        out_shape=jax.ShapeDtypeStruct((batch, seq, GLA_V_W), BF16),
        scratch_shapes=[pltpu.VMEM((GLA_HEADS // 2, GLA_PAIR_K, GLA_PAIR_V), F32)],
        compiler_params=pltpu.CompilerParams(dimension_semantics=("arbitrary", "arbitrary"),
                                             vmem_limit_bytes=VMEM_LIMIT),
        name="gla",
    )(s0, qg, kg, vg, gk, rg, nw)


ROUTER_W = LANES


RT_E0, RT_E1, RT_W0, RT_W1 = 0, 1, 2, 3


def _merge_body(x_ref, ao_ref, go_ref, sga_ref, sgg_ref, wab_ref, wgb_ref, wo_ref, n2_ref,
                wr2_ref, h_out, rt_out, rtt_out, cnt_out):
    ya = _dot(ao_ref[...], wab_ref[...])
    yg = _dot(go_ref[...], wgb_ref[...])
    merged = sga_ref[...].astype(F32) * ya + sgg_ref[...].astype(F32) * yg
    h = x_ref[...] + _dot(merged.astype(BF16), wo_ref[...])
    _store_slabs(h_out, h)
    xn = _rms_norm(h, n2_ref[...])

    xh, xl = _split_hi_lo(xn)
    wr2 = wr2_ref[...]
    lg2 = _dot(xh, wr2) + _dot(xl, wr2)
    lg = lg2[:, :ROUTER_W] + lg2[:, ROUTER_W:]
    lg_t = lg.T
    tm = lg.shape[0]
    neg = -jnp.inf
    big = jnp.int32(1 << 20)
    gl = lg_t[N_EXPERTS:N_EXPERTS + N_GROUPS, :]
    grow = lax.broadcasted_iota(jnp.int32, (N_GROUPS, tm), 0)
    gmax = jnp.max(gl, axis=0, keepdims=True)
    g_idx = jnp.min(jnp.where(gl == gmax, grow, big), axis=0, keepdims=True)
    g_w = 1.0 / jnp.sum(jnp.exp(gl - gmax), axis=0, keepdims=True)
    erow = lax.broadcasted_iota(jnp.int32, (N_EXPERTS, tm), 0)
    el = jnp.where((erow >> 3) == g_idx, lg_t[0:N_EXPERTS, :], neg)
    m1 = jnp.max(el, axis=0, keepdims=True)
    i1 = jnp.min(jnp.where(el == m1, erow, big), axis=0, keepdims=True)
    el2 = jnp.where(erow == i1, neg, el)
    m2 = jnp.max(el2, axis=0, keepdims=True)
    i2 = jnp.min(jnp.where(el2 == m2, erow, big), axis=0, keepdims=True)
    e2 = jnp.exp(m2 - m1)
    p1 = g_w / (1.0 + e2)
    p2 = g_w * e2 / (1.0 + e2)
    def record(rows):
        r = lax.broadcasted_iota(jnp.int32, (rows, tm), 0)
        return jnp.where(r == RT_E0, i1.astype(F32),
                         jnp.where(r == RT_E1, i2.astype(F32),
                                   jnp.where(r == RT_W0, p1, jnp.where(r == RT_W1, p2, 0.0))))
    rtt_out[...] = record(8)
    rt_out[...] = record(LANES).T
    picked = jnp.where(jnp.logical_or(erow == i1, erow == i2), 1.0, 0.0)

    @pl.when(pl.program_id(0) == 0)
    def _():
        cnt_out[...] = jnp.zeros_like(cnt_out)

    cnt_out[...] += jnp.sum(picked, axis=1, keepdims=True)


def _merge_call(x2d, ao, go, sga, sgg, w, tm):
    n = x2d.shape[0]
    row = lambda width: pl.BlockSpec((tm, width), lambda i: (i, 0))
    return pl.pallas_call(
        _merge_body,
        grid=(n // tm,),
        in_specs=[row(D_MODEL), row(ATTN_Q_W), row(GLA_V_W), row(D_MODEL), row(D_MODEL),
                  _const_spec((ATTN_Q_W, D_MODEL)), _const_spec((GLA_V_W, D_MODEL)),
                  _const_spec((D_MODEL, D_MODEL)), _const_spec((1, D_MODEL)),
                  _const_spec((D_MODEL, 2 * ROUTER_W))],
        out_specs=[pl.BlockSpec((ROW_TILES, tm, LANES), lambda i: (0, i, 0)), row(ROUTER_W),
                   pl.BlockSpec((8, tm), lambda i: (0, i)), _const_spec((N_EXPERTS, LANES))],
        out_shape=[jax.ShapeDtypeStruct((ROW_TILES, n, LANES), F32),
                   jax.ShapeDtypeStruct((n, ROUTER_W), F32),
                   jax.ShapeDtypeStruct((8, n), F32),
                   jax.ShapeDtypeStruct((N_EXPERTS, LANES), F32)],
        compiler_params=pltpu.CompilerParams(dimension_semantics=("arbitrary",),
                                             vmem_limit_bytes=VMEM_LIMIT),
        name="merge",
    )(x2d, ao, go, sga, sgg, w["wab"], w["wgb"], w["wo"], w["norm2"], w["wr2"])


EXPERT_TILE = 512
ROUTE_TILE = 512
VISIT_CAP = 256
VIS_EXPERT, VIS_TILE, VIS_LO, VIS_HI, VIS_COUNT = 0, 1, 2, 3, 4


def _num_visits(n_tokens):
    return (2 * n_tokens) // EXPERT_TILE + N_EXPERTS - 1


def _expert_cumsum_inclusive(x):
    row = lax.broadcasted_iota(jnp.int32, x.shape, 0)
    s = 1
    while s < N_EXPERTS:
        x = x + jnp.where(row >= s, pltpu.roll(x, s, 0), 0.0)
        s *= 2
    return x


def _route_body(rtt_ref, cnt_ref, pos_ref, vis_ref, carry_ref):
    i = pl.program_id(0)
    tm = rtt_ref.shape[1]
    cnt = cnt_ref[...]
    end = _expert_cumsum_inclusive(cnt)
    base = end - cnt

    @pl.when(i == 0)
    def _():
        carry_ref[...] = jnp.zeros_like(carry_ref)
        inv = 1.0 / EXPERT_TILE
        first = jnp.floor(base * inv)
        n_vis = jnp.where(cnt > 0.0, jnp.floor((end - 1.0) * inv) - first + 1.0, 0.0)
        vend = _expert_cumsum_inclusive(n_vis)
        vstart = vend - n_vis
        col = lambda a: a[:, 0:1]
        erow = lax.broadcasted_iota(jnp.int32, (N_EXPERTS, VISIT_CAP), 0).astype(F32)
        v = lax.broadcasted_iota(jnp.int32, (N_EXPERTS, VISIT_CAP), 1).astype(F32)
        ev = jnp.minimum(jnp.sum(jnp.where(col(vend) <= v, 1.0, 0.0), axis=0, keepdims=True),
                         N_EXPERTS - 1.0)
        mine = erow == ev
        pick = lambda a: jnp.sum(jnp.where(mine, a, 0.0), axis=0, keepdims=True)
        tile = pick(col(first) + v - col(vstart))
        lo = jnp.maximum(pick(col(base)) - tile * EXPERT_TILE, 0.0)
        hi = jnp.minimum(pick(col(end)) - tile * EXPERT_TILE, float(EXPERT_TILE))
        total = vend[N_EXPERTS - 1:N_EXPERTS, 0:1]
        r8 = lax.broadcasted_iota(jnp.int32, (8, VISIT_CAP), 0)
        rec = jnp.where(r8 == VIS_EXPERT, ev,
                        jnp.where(r8 == VIS_TILE, tile,
                                  jnp.where(r8 == VIS_LO, lo,
                                            jnp.where(r8 == VIS_HI, hi,
                                                      jnp.where(r8 == VIS_COUNT, total, 0.0)))))
        vis_ref[...] = rec.astype(jnp.int32)

    rtt = rtt_ref[...]
    erow = lax.broadcasted_iota(jnp.int32, (N_EXPERTS, tm), 0).astype(F32)
    oh0 = erow == rtt[RT_E0:RT_E0 + 1, :]
    oh1 = erow == rtt[RT_E1:RT_E1 + 1, :]
    picked = jnp.where(jnp.logical_or(oh0, oh1), 1.0, 0.0)
    r = lax.broadcasted_iota(jnp.int32, (tm, tm), 0)
    c = lax.broadcasted_iota(jnp.int32, (tm, tm), 1)
    earlier = jnp.where(r < c, 1.0, 0.0).astype(BF16)
    slot = _dot(picked.astype(BF16), earlier) + (carry_ref[:, 0:1] + base[:, 0:1])
    p0 = jnp.sum(jnp.where(oh0, slot, 0.0), axis=0, keepdims=True)
    p1 = jnp.sum(jnp.where(oh1, slot, 0.0), axis=0, keepdims=True)
    r8 = lax.broadcasted_iota(jnp.int32, (8, tm), 0)
    pos_ref[...] = jnp.where(r8 == 0, p0, jnp.where(r8 == 1, p1, 0.0)).astype(jnp.int32)
    carry_ref[...] += jnp.sum(picked, axis=1, keepdims=True)


def _route_call(rtt, cnt):
    n = rtt.shape[1]
    tm = _pick_tile(n, ROUTE_TILE)
    return pl.pallas_call(
        _route_body,
        grid=(n // tm,),
        in_specs=[pl.BlockSpec((8, tm), lambda i: (0, i)), _const_spec((N_EXPERTS, LANES))],
        out_specs=[pl.BlockSpec((8, tm), lambda i: (0, i)), _const_spec((8, VISIT_CAP))],
        out_shape=[jax.ShapeDtypeStruct((8, n), jnp.int32),
                   jax.ShapeDtypeStruct((8, VISIT_CAP), jnp.int32)],
        scratch_shapes=[pltpu.VMEM((N_EXPERTS, LANES), F32)],
        compiler_params=pltpu.CompilerParams(dimension_semantics=("arbitrary",)),
        name="route",
    )(rtt, cnt)


SCATTER_TILE = 512
ROW_TILES = D_MODEL // LANES


def _store_slabs(ref, val):
    for c in range(ROW_TILES):
        ref[c] = val[:, c * LANES:(c + 1) * LANES]


def _load_slabs(ref):
    return jnp.concatenate([ref[c] for c in range(ROW_TILES)], axis=1)


def _scatter_body(pos0_ref, pos1_ref, h_ref, hs_hbm, sem):
    tm = h_ref.shape[1]
    for r in range(tm):
        src = h_ref.at[:, r, :]
        pltpu.make_async_copy(src, hs_hbm.at[:, pos0_ref[r], :], sem).start(priority=0)
        pltpu.make_async_copy(src, hs_hbm.at[:, pos1_ref[r], :], sem).start(priority=1)
    for _ in range(2 * tm):
        pltpu.make_async_copy(h_ref.at[:, 0, :], hs_hbm.at[:, 0, :], sem).wait()


def _scatter_call(pos0, pos1, h_slabs, n_slots):
    n = h_slabs.shape[1]
    tm = _pick_tile(n, SCATTER_TILE)
    smem = lambda: pl.BlockSpec((tm,), lambda i: (i,), memory_space=pltpu.SMEM)
    return pl.pallas_call(
        _scatter_body,
        grid=(n // tm,),
        in_specs=[smem(), smem(), pl.BlockSpec((ROW_TILES, tm, LANES), lambda i: (0, i, 0))],
        out_specs=pl.BlockSpec(memory_space=pl.ANY),
        out_shape=jax.ShapeDtypeStruct((ROW_TILES, n_slots, LANES), F32),
        scratch_shapes=[pltpu.SemaphoreType.DMA],
        compiler_params=pltpu.CompilerParams(dimension_semantics=("arbitrary",)),
        name="scatter",
    )(pos0, pos1, h_slabs)


def _expert_body(vis_ref, h_ref, n2_ref, wg_ref, wu_ref, wd_ref, o_ref):
    j = pl.program_id(0)

    @pl.when(j < vis_ref[VIS_COUNT, 0])
    def _():
        lo_row = vis_ref[VIS_LO, j]
        hi_row = vis_ref[VIS_HI, j]
        x = _rms_norm(_load_slabs(h_ref), n2_ref[...]).astype(BF16)
        g = _dot(x, wg_ref[...].astype(BF16))
        u = _dot(x, wu_ref[...].astype(BF16))
        hg = (g * _sigmoid(g) * u).astype(BF16)
        new = _dot(hg, wd_ref[...].astype(BF16))
        row = lax.broadcasted_iota(jnp.int32, new.shape, 0)

        @pl.when(lo_row == 0)
        def _():
            _store_slabs(o_ref, jnp.where(row < hi_row, new, 0.0))

        @pl.when(lo_row > 0)
        def _():
            keep = jnp.logical_and(row >= lo_row, row < hi_row)
            _store_slabs(o_ref, jnp.where(keep, new, _load_slabs(o_ref)))


def _expert_call(vis, hs, n2, wg, wu, wd, n_visits):
    def visit(j, vis):
        return jnp.minimum(j, vis[VIS_COUNT, 0] - 1)

    def xmap(j, vis):
        return (0, vis[VIS_TILE, visit(j, vis)], 0)

    def wmap(j, vis):
        return (vis[VIS_EXPERT, visit(j, vis)], 0, 0)

    grid_spec = pltpu.PrefetchScalarGridSpec(
        num_scalar_prefetch=1,
        grid=(n_visits,),
        in_specs=[pl.BlockSpec((ROW_TILES, EXPERT_TILE, LANES), xmap),
                  pl.BlockSpec((1, D_MODEL), lambda j, vis: (0, 0)),
                  pl.BlockSpec((None, D_MODEL, EXPERT_FF), wmap),
                  pl.BlockSpec((None, D_MODEL, EXPERT_FF), wmap),
                  pl.BlockSpec((None, EXPERT_FF, D_MODEL), wmap)],
        out_specs=pl.BlockSpec((ROW_TILES, EXPERT_TILE, LANES), xmap),
    )
    return pl.pallas_call(
        _expert_body,
        grid_spec=grid_spec,
        out_shape=jax.ShapeDtypeStruct(hs.shape, F32),
        compiler_params=pltpu.CompilerParams(dimension_semantics=("arbitrary",),
                                             vmem_limit_bytes=VMEM_LIMIT),
        name="experts",
    )(vis, hs, n2, wg, wu, wd)


COMBINE_TILE = 256


def _combine_body(p0_ref, p1_ref, p0n_ref, p1n_ref, h_ref, rt_ref, y_hbm, o_ref, g_ref, sem):
    i = pl.program_id(0)
    tm = h_ref.shape[1]
    cur = i % 2

    def issue(q0_ref, q1_ref, buf):
        for r in range(tm):
            pltpu.make_async_copy(y_hbm.at[:, q0_ref[r], :], g_ref.at[buf, 0, :, r, :],
                                  sem.at[buf]).start(priority=0)
            pltpu.make_async_copy(y_hbm.at[:, q1_ref[r], :], g_ref.at[buf, 1, :, r, :],
                                  sem.at[buf]).start(priority=1)

    @pl.when(i == 0)
    def _():
        issue(p0_ref, p1_ref, 0)

    @pl.when(i + 1 < pl.num_programs(0))
    def _():
        issue(p0n_ref, p1n_ref, 1 - cur)

    for _ in range(2 * tm):
        pltpu.make_async_copy(y_hbm.at[:, 0, :], g_ref.at[cur, 0, :, 0, :], sem.at[cur]).wait()
    rt = rt_ref[...]
    o_ref[...] = (_load_slabs(h_ref) + rt[:, RT_W0:RT_W0 + 1] * _load_slabs(g_ref.at[cur, 0])
                  + rt[:, RT_W1:RT_W1 + 1] * _load_slabs(g_ref.at[cur, 1]))


def _combine_call(pos0, pos1, h_slabs, rt, y):
    n = h_slabs.shape[1]
    tm = _pick_tile(n, COMBINE_TILE)
    steps = n // tm
    smem_cur = lambda: pl.BlockSpec((tm,), lambda i: (i,), memory_space=pltpu.SMEM)
    smem_next = lambda: pl.BlockSpec((tm,), lambda i: (jnp.minimum(i + 1, steps - 1),),
                                     memory_space=pltpu.SMEM)
    row = lambda width: pl.BlockSpec((tm, width), lambda i: (i, 0))
    return pl.pallas_call(
        _combine_body,
        grid=(steps,),
        in_specs=[smem_cur(), smem_cur(), smem_next(), smem_next(),
                  pl.BlockSpec((ROW_TILES, tm, LANES), lambda i: (0, i, 0)), row(ROUTER_W),
                  pl.BlockSpec(memory_space=pl.ANY)],
        out_specs=row(D_MODEL),
        out_shape=jax.ShapeDtypeStruct((n, D_MODEL), F32),
        scratch_shapes=[pltpu.VMEM((2, 2, ROW_TILES, tm, LANES), F32), pltpu.SemaphoreType.DMA((2,))],
        compiler_params=pltpu.CompilerParams(dimension_semantics=("arbitrary",),
                                             vmem_limit_bytes=VMEM_LIMIT),
        name="combine",
    )(pos0, pos1, pos0, pos1, h_slabs, rt, y)


def _rope_tables(pos):
    half = HEAD_DIM // 2
    inv_freq = ROPE_THETA ** (-(jnp.arange(half, dtype=F32) * 2.0) / HEAD_DIM)
    ang = pos.astype(F32)[:, None] * inv_freq[None, :]
    cos = jnp.cos(ang)
    sin = jnp.sin(ang)
    cos64 = jnp.concatenate([cos, cos], axis=1)
    sin64 = jnp.concatenate([-sin, sin], axis=1)
    return jnp.tile(cos64, (1, LANES // HEAD_DIM)), jnp.tile(sin64, (1, LANES // HEAD_DIM))


def _prep_weights(norm1_w, w_in, q_norm_w, k_norm_w, gla_gate_up, gla_gate_bias, w_attn_branch,
                  w_gla_branch, w_out, norm2_w, router_group, router_expert):
    wb = w_in.astype(BF16)
    wqkv = jnp.zeros((D_MODEL, QKV_W), BF16).at[:, :QKV_GD_OFF].set(wb[:, :QKV_GD_OFF])
    wqkv = wqkv.at[:, QKV_GD_OFF:QKV_GD_OFF + GLA_GATE_RANK].set(wb[:, _OFF_GD:_OFF_GD + GLA_GATE_RANK])
    gup =jnp.zeros((LANES, GLA_K_W), BF16).at[:GLA_GATE_RANK].set(gla_gate_up.astype(BF16))
    wr = jnp.zeros((D_MODEL, ROUTER_W), F32)
    wr = wr.at[:, :N_EXPERTS].set(router_expert.transpose(1, 0, 2).reshape(D_MODEL, N_EXPERTS))
    wr = wr.at[:, N_EXPERTS:N_EXPERTS + N_GROUPS].set(router_group)
    wrh = wr.astype(BF16)
    wrl = (wr - wrh.astype(F32)).astype(BF16)
    return {
        "norm1": norm1_w.reshape(1, D_MODEL),
        "wqkv": wqkv,
        "wgla": wb[:, _OFF_GQ:_OFF_GQ + GLA_W],
        "wgate": wb[:, _OFF_GATE:_OFF_GATE + 2 * D_MODEL],
        "qn": jnp.tile(q_norm_w, ATTN_HEADS).reshape(1, ATTN_Q_W),
        "kn": jnp.tile(k_norm_w, ATTN_KV_HEADS).reshape(1, ATTN_KV_W),
        "gup": gup,
        "gb": gla_gate_bias.reshape(1, GLA_K_W),
        "wab": w_attn_branch.astype(BF16),
        "wgb": w_gla_branch.astype(BF16),
        "wo": w_out.astype(BF16),
        "norm2": norm2_w.reshape(1, D_MODEL),
        "wr2": jnp.concatenate([wrh, wrl], axis=1),
    }


def _pick_tile(n, want):
    t = want
    while n % t:
        t //= 2
    return t


def kernel(x, meta_tokens, norm1_w, w_in, q_norm_w, k_norm_w, attn_sinks, gla_gate_up, gla_gate_bias,
           gla_norm_w, w_attn_branch, w_gla_branch, w_out, norm2_w, router_group, router_expert,
           expert_w_gate, expert_w_up, expert_w_down):
    batch, seq, d = x.shape
    assert d == D_MODEL and seq % GLA_STEP == 0 and norm1_w.shape[0] == 1
    n = batch * seq
    w = _prep_weights(norm1_w[0], w_in[0], q_norm_w[0], k_norm_w[0], gla_gate_up[0], gla_gate_bias[0],
                      w_attn_branch[0], w_gla_branch[0], w_out[0], norm2_w[0], router_group[0],
                      router_expert[0])
    x2d = x.reshape(n, D_MODEL)

    cos_m, sin_m = _rope_tables(jnp.arange(N_META))
    cos_r, sin_r = _rope_tables(jnp.arange(seq) + N_META)
    meta = _proj_call(meta_tokens.astype(F32), N_META, 1, w, cos_m, sin_m)
    tm = _pick_tile(seq, 512)
    (q, kd, vd, qg, kg, vg, rg, gk, sga, sgg) = _proj_call(x2d, tm, seq // tm, w, cos_r, sin_r)

    pad_after = ((0, BLOCK - N_META), (0, 0))
    km = jnp.pad(meta[1], pad_after)
    vm = jnp.pad(meta[2], pad_after)
    ao = _attn_call(q.reshape(batch, seq, ATTN_Q_W), kd.reshape(batch, seq, ATTN_KV_DUP_W),
                    vd.reshape(batch, seq, ATTN_KV_DUP_W), km, vm, attn_sinks[0].astype(F32), batch, seq)

    pad_before = ((GLA_STEP - N_META, 0), (0, 0))
    s0 = _gla_init_call(jnp.pad(meta[4], pad_before), jnp.pad(meta[5], pad_before),
                        jnp.pad(meta[7], pad_before))
    go = _gla_call(s0, qg.reshape(batch, seq, GLA_K_W), kg.reshape(batch, seq, GLA_K_W),
                   vg.reshape(batch, seq, GLA_V_W), gk.reshape(batch, seq, GLA_K_W),
                   rg.reshape(batch, seq, GLA_V_W), gla_norm_w[0].reshape(1, GLA_VAL_DIM), batch, seq)

    h, rt, rtt, cnt = _merge_call(x2d, ao.reshape(n, ATTN_Q_W), go.reshape(n, GLA_V_W), sga, sgg, w,
                                  _pick_tile(n, 512))

    assert (2 * n) % EXPERT_TILE == 0 and _num_visits(n) <= VISIT_CAP
    pos, vis = _route_call(rtt, cnt)
    pos0, pos1 = pos[0], pos[1]
    hs = _scatter_call(pos0, pos1, h, 2 * n)
    wg = expert_w_gate[0].reshape(N_EXPERTS, D_MODEL, EXPERT_FF)
    wu = expert_w_up[0].reshape(N_EXPERTS, D_MODEL, EXPERT_FF)
    wd = expert_w_down[0].reshape(N_EXPERTS, EXPERT_FF, D_MODEL)
    y = _expert_call(vis, hs, w["norm2"], wg, wu, wd, _num_visits(n))
    out = _combine_call(pos0, pos1, h, rt, y)
    return out.reshape(batch, seq, D_MODEL)
```

```python
import functools
import math

import jax
import jax.numpy as jnp
from jax import lax
from jax.experimental import pallas as pl
from jax.experimental.pallas import tpu as pltpu

F32 = jnp.float32
BF16 = jnp.bfloat16

D_MODEL = 1024
N_META = 16
BLOCK = 128
ATTN_HEADS = 8
ATTN_KV_HEADS = 2
HEAD_DIM = 64
ROPE_THETA = 10000.0
ATTN_Q_W = ATTN_HEADS * HEAD_DIM
ATTN_KV_W = ATTN_KV_HEADS * HEAD_DIM
GLA_HEADS = 4
GLA_KEY_DIM = 64
GLA_VAL_DIM = 128
GLA_K_W = GLA_HEADS * GLA_KEY_DIM
GLA_V_W = GLA_HEADS * GLA_VAL_DIM
GLA_GATE_RANK = 16
GLA_GATE_NORM = 16.0
GLA_CHUNK = 64
N_GROUPS = 4
EXPERTS_PER_GROUP = 8
N_EXPERTS = N_GROUPS * EXPERTS_PER_GROUP
EXPERT_FF = 256
NORM_EPS = 1e-6
MASK_VALUE = -1e30

LANES = 128
ATTN_KV_DUP_W = 2 * ATTN_KV_W
VMEM_LIMIT = 56 * 1024 * 1024

_OFF_Q = 0
_OFF_K = _OFF_Q + ATTN_Q_W
_OFF_V = _OFF_K + ATTN_KV_W
_OFF_GQ = _OFF_V + ATTN_KV_W
_OFF_GD = _OFF_GQ + 2 * GLA_K_W + 2 * GLA_V_W
_OFF_GATE = _OFF_GD + GLA_GATE_RANK
GLA_W = 2 * GLA_K_W + 2 * GLA_V_W
QKV_GD_OFF = _OFF_GQ
QKV_W = D_MODEL


def _dot(a, b):
    return jnp.dot(a, b, preferred_element_type=F32)


def _dot_nt(a, b):
    return lax.dot_general(a, b, (((1,), (1,)), ((), ())), preferred_element_type=F32)


def _split_hi_lo(x):
    hi = x.astype(BF16)
    lo = (x - hi.astype(F32)).astype(BF16)
    return hi, lo


MXU_TILE = 256


def _group_mean_sq(x, group):
    w = x.shape[-1]
    slab = min(w, MXU_TILE)
    shift = int(math.log2(group))
    r = lax.broadcasted_iota(jnp.int32, (slab, slab), 0) >> shift
    c = lax.broadcasted_iota(jnp.int32, (slab, slab), 1) >> shift
    ones = jnp.where(r == c, 1.0, 0.0).astype(BF16)
    hi, lo = _split_hi_lo(x * x)
    sums = [_dot(hi[:, s:s + slab], ones) + _dot(lo[:, s:s + slab], ones) for s in range(0, w, slab)]
    return (sums[0] if len(sums) == 1 else jnp.concatenate(sums, axis=1)) * (1.0 / group)


def _rope(x, cos, sin_signed):
    w = x.shape[-1]
    reps = w // LANES
    if reps > 1:
        cos = jnp.concatenate([cos] * reps, axis=1)
        sin_signed = jnp.concatenate([sin_signed] * reps, axis=1)
    lane = lax.broadcasted_iota(jnp.int32, x.shape, 1)
    first_half = (lane & (HEAD_DIM // 2)) == 0
    swapped = jnp.where(first_half, pltpu.roll(x, w - HEAD_DIM // 2, 1), pltpu.roll(x, HEAD_DIM // 2, 1))
    return x * cos + swapped * sin_signed


def _rms_norm(x, w):
    ms = jnp.mean(x * x, axis=-1, keepdims=True)
    return x * lax.rsqrt(ms + NORM_EPS) * w


def _sigmoid(x):
    return 0.5 * jnp.tanh(0.5 * x) + 0.5


def _log_sigmoid(x):
    return jnp.minimum(x, 0.0) - jnp.log(1.0 + jnp.exp(-jnp.abs(x)))


def _dup_heads_on_lanes(x):
    low = lax.broadcasted_iota(jnp.int32, x.shape, 1) < HEAD_DIM
    swapped = pltpu.roll(x, HEAD_DIM, 1)
    return jnp.concatenate([jnp.where(low, x, swapped), jnp.where(low, swapped, x)], axis=1)


def _proj_body(x_ref, n1_ref, wqkv_ref, wgla_ref, wgate_ref,
               qn_ref, kn_ref, cos_ref, sin_ref, gup_ref, gb_ref,
               q_out, k_out, v_out, qg_out, kg_out, vg_out, rg_out, gk_out, sga_out, sgg_out):
    xn = _rms_norm(x_ref[...], n1_ref[...]).astype(BF16)
    cos = cos_ref[...]
    sin = sin_ref[...]

    a = _dot(xn, wqkv_ref[...])
    q = a[:, :ATTN_Q_W]
    q = q * lax.rsqrt(_group_mean_sq(q, HEAD_DIM) + NORM_EPS) * qn_ref[...]
    q_out[...] = (_rope(q, cos, sin) * (HEAD_DIM ** -0.5)).astype(BF16)

    k = a[:, _OFF_K:_OFF_K + ATTN_KV_W]
    k = k * lax.rsqrt(_group_mean_sq(k, HEAD_DIM) + NORM_EPS) * kn_ref[...]
    k_out[...] = _dup_heads_on_lanes(_rope(k, cos, sin)).astype(BF16)

    v_out[...] = _dup_heads_on_lanes(a[:, _OFF_V:_OFF_V + ATTN_KV_W]).astype(BF16)

    g = _dot(xn, wgla_ref[...])
    qg_out[...] = (g[:, :GLA_K_W] * (GLA_KEY_DIM ** -0.5)).astype(BF16)
    kg_out[...] = g[:, GLA_K_W:2 * GLA_K_W].astype(BF16)
    vg_out[...] = g[:, 2 * GLA_K_W:2 * GLA_K_W + GLA_V_W].astype(BF16)
    r = g[:, 2 * GLA_K_W + GLA_V_W:]
    rg_out[...] = (r * _sigmoid(r)).astype(BF16)

    gd = a[:, QKV_GD_OFF:QKV_GD_OFF + LANES].astype(BF16)
    z = _dot(gd, gup_ref[...]) + gb_ref[...]
    gk_out[...] = _log_sigmoid(z) * (1.0 / GLA_GATE_NORM)

    gates = _dot(xn, wgate_ref[...])
    sga_out[...] = _sigmoid(gates[:, :D_MODEL]).astype(BF16)
    sgg_out[...] = _sigmoid(gates[:, D_MODEL:]).astype(BF16)


def _const_spec(shape):
    nd = len(shape)
    return pl.BlockSpec(shape, lambda *_: (0,) * nd)


def _proj_call(x2d, tm, tiles_per_seq, w, cos, sin):
    n = x2d.shape[0]
    grid = (n // tm,)
    row = lambda width: pl.BlockSpec((tm, width), lambda i: (i, 0))
    tab = pl.BlockSpec((tm, LANES), lambda i: (i % tiles_per_seq, 0))
    in_specs = [
        row(D_MODEL), _const_spec((1, D_MODEL)),
        _const_spec((D_MODEL, QKV_W)), _const_spec((D_MODEL, GLA_W)),
        _const_spec((D_MODEL, 2 * D_MODEL)),
        _const_spec((1, ATTN_Q_W)), _const_spec((1, ATTN_KV_W)),
        tab, tab, _const_spec((LANES, GLA_K_W)), _const_spec((1, GLA_K_W)),
    ]
    out_widths = [ATTN_Q_W, ATTN_KV_DUP_W, ATTN_KV_DUP_W, GLA_K_W, GLA_K_W, GLA_V_W, GLA_V_W,
                  GLA_K_W, D_MODEL, D_MODEL]
    out_dtypes = [BF16] * 7 + [F32] + [BF16] * 2
    return pl.pallas_call(
        _proj_body,
        grid=grid,
        in_specs=in_specs,
        out_specs=[row(wd) for wd in out_widths],
        out_shape=[jax.ShapeDtypeStruct((n, wd), dt) for wd, dt in zip(out_widths, out_dtypes)],
        compiler_params=pltpu.CompilerParams(dimension_semantics=("arbitrary",),
                                             vmem_limit_bytes=VMEM_LIMIT),
        name="proj",
    )(x2d, w["norm1"], w["wqkv"], w["wgla"], w["wgate"],
      w["qn"], w["kn"], cos, sin, w["gup"], w["gb"])


ATTN_STEP_BLOCKS = 4


def _attn_body(sink_ref, q_ref, kp_ref, kc_ref, km_ref, vp_ref, vc_ref, vm_ref, o_ref):
    i = pl.program_id(1)
    nblk = ATTN_STEP_BLOCKS
    pair_rows = 2 * BLOCK
    lane = lax.broadcasted_iota(jnp.int32, (BLOCK, LANES), 1)
    low = lane < HEAD_DIM
    low2 = lax.broadcasted_iota(jnp.int32, (pair_rows, LANES), 1) < HEAD_DIM
    top2 = lax.broadcasted_iota(jnp.int32, (pair_rows, 1), 0) < BLOCK
    key = lax.broadcasted_iota(jnp.int32, (pair_rows, 2 * LANES), 1) & (LANES - 1)
    rowi = lax.broadcasted_iota(jnp.int32, (pair_rows, 2 * LANES), 0) & (BLOCK - 1)
    mask_meta = key < N_META
    mask_prev = key > rowi
    mask_cur = key <= rowi
    zero = jnp.zeros((), BF16)

    def block_diag(x):
        return jnp.concatenate([jnp.where(low, x, zero), jnp.where(low, zero, x)], axis=0)

    blk_row = lax.broadcasted_iota(jnp.int32, (2 * BLOCK, LANES), 0)
    blk_lane = lax.broadcasted_iota(jnp.int32, (2 * BLOCK, LANES), 1)
    ones_blk = jnp.where((blk_row < BLOCK) == (blk_lane < HEAD_DIM), 1.0, 0.0).astype(BF16)

    def masked(s, mask):
        return jnp.where(mask, s, MASK_VALUE)

    for kvh in range(ATTN_KV_HEADS):
        sl = slice(kvh * LANES, (kvh + 1) * LANES)
        k_seq = [block_diag(kp_ref[:, sl])]
        v_seq = [block_diag(vp_ref[:, sl])]
        for t in range(nblk):
            rows = slice(t * BLOCK, (t + 1) * BLOCK)
            k_seq.append(block_diag(kc_ref[rows, sl]))
            v_seq.append(block_diag(vc_ref[rows, sl]))
        q_all = jnp.concatenate(
            [q_ref[t * BLOCK:(t + 1) * BLOCK, (2 * kvh + pr) * LANES:(2 * kvh + pr + 1) * LANES]
             for t in range(nblk) for pr in range(2)], axis=0)
        s_meta = _dot_nt(q_all, block_diag(km_ref[:, sl]))
        s_seq = []
        for j in range(nblk + 1):
            lo_blk, hi_blk = max(j - 1, 0), min(j, nblk - 1)
            s_seq.append(_dot_nt(q_all[lo_blk * pair_rows:(hi_blk + 1) * pair_rows], k_seq[j]))

        def sink_col(hh):
            return jnp.where(top2, sink_ref[4 * kvh + hh], sink_ref[4 * kvh + 2 + hh])

        p_meta, p_prev, p_cur, l_all = [], [], [], []
        for t in range(nblk):
            sm = masked(s_meta[t * pair_rows:(t + 1) * pair_rows], mask_meta)
            prev_rows = slice(0, pair_rows) if t == 0 else slice(pair_rows, 2 * pair_rows)
            sb = jnp.where(mask_prev, s_seq[t][prev_rows], s_seq[t + 1][0:pair_rows])
            if t == 0:
                sb = masked(sb, jnp.logical_or(mask_cur, i > 0))
            s_max = jnp.maximum(sm, sb)
            m2 = []
            sink_terms = []
            for hh in range(2):
                sink = sink_col(hh)
                m = jnp.maximum(jnp.max(s_max[:, hh * LANES:(hh + 1) * LANES], axis=1, keepdims=True), sink)
                m2.append(m)
                sink_terms.append(jnp.exp(sink - m))

            def probs(s):
                return jnp.concatenate([jnp.exp(s[:, hh * LANES:(hh + 1) * LANES] - m2[hh])
                                        for hh in range(2)], axis=1)

            pm, pb = probs(sm), probs(sb)
            pb16 = pb.astype(BF16)
            p_meta.append(pm.astype(BF16))
            p_prev.append(jnp.where(mask_prev, pb16, zero))
            p_cur.append(jnp.where(mask_prev, zero, pb16))
            l_all.append(_dot((pm + pb).astype(BF16), ones_blk)
                         + jnp.where(low2, sink_terms[0], sink_terms[1]))

        o_meta = _dot(jnp.concatenate(p_meta, axis=0), block_diag(vm_ref[:, sl]))
        o_seq = []
        for j in range(nblk + 1):
            parts = ([p_cur[j - 1]] if j >= 1 else []) + ([p_prev[j]] if j < nblk else [])
            o_seq.append(_dot(jnp.concatenate(parts, axis=0) if len(parts) > 1 else parts[0], v_seq[j]))
        for t in range(nblk):
            prev_rows = slice(0, pair_rows) if t == 0 else slice(pair_rows, 2 * pair_rows)
            o = (o_meta[t * pair_rows:(t + 1) * pair_rows] + o_seq[t][prev_rows]
                 + o_seq[t + 1][0:pair_rows]) / l_all[t]
            for pr in range(2):
                col = (2 * kvh + pr) * LANES
                o_ref[t * BLOCK:(t + 1) * BLOCK, col:col + LANES] = o[pr * BLOCK:(pr + 1) * BLOCK].astype(BF16)


GLA_STEP = 2 * GLA_CHUNK
GLA_PAIR_K = 2 * GLA_KEY_DIM
GLA_PAIR_V = 2 * GLA_VAL_DIM


def _gla_consts():
    r = lax.broadcasted_iota(jnp.int32, (GLA_STEP, 2 * GLA_STEP), 0)
    c = lax.broadcasted_iota(jnp.int32, (GLA_STEP, 2 * GLA_STEP), 1) & (GLA_STEP - 1)
    causal = jnp.logical_and((r >> 6) == (c >> 6), c <= r)
    tri = jnp.where(causal[:, :GLA_STEP], 1.0, 0.0).astype(BF16)
    return causal, tri


def _gla_step(q2, k2, v2, gk2, s_prev, causal, tri):
    row = lax.broadcasted_iota(jnp.int32, (GLA_STEP, GLA_PAIR_K), 0)
    lane = lax.broadcasted_iota(jnp.int32, (GLA_STEP, GLA_PAIR_K), 1)
    first = row < GLA_CHUNK
    low = lane < GLA_KEY_DIM
    g_hi = gk2.astype(BF16)
    g_r = gk2 - g_hi.astype(F32)
    g_mid = g_r.astype(BF16)
    g_lo = (g_r - g_mid.astype(F32)).astype(BF16)
    b = _dot(tri, g_hi) + _dot(tri, g_mid) + _dot(tri, g_lo)
    half = GLA_CHUNK // 2
    b_mid = jnp.where(first, b[half:half + 1, :], b[GLA_CHUNK + half:GLA_CHUNK + half + 1, :])
    qi = (q2 * jnp.exp(b - b_mid)).astype(BF16)
    ki = (k2 * jnp.exp(b_mid - b)).astype(BF16)
    qx = q2 * jnp.exp(b)
    zero = jnp.zeros((), BF16)
    kblk = jnp.concatenate([jnp.where(low, ki, zero), jnp.where(low, zero, ki)], axis=0)
    att = _dot_nt(qi, kblk)
    att = jnp.where(causal, att, 0.0).astype(BF16)
    zv = jnp.zeros((GLA_STEP, GLA_VAL_DIM), BF16)
    vblk = jnp.concatenate([jnp.concatenate([v2[:, :GLA_VAL_DIM], zv], axis=1),
                            jnp.concatenate([zv, v2[:, GLA_VAL_DIM:]], axis=1)], axis=0)
    o = _dot(att, vblk)
    b_t = b.T
    k_t = k2.T
    lane_t = lax.broadcasted_iota(jnp.int32, (GLA_PAIR_K, GLA_STEP), 1)
    first_t = lane_t < GLA_CHUNK
    bl_a = b_t[:, GLA_CHUNK - 1:GLA_CHUNK]
    bl_b = b_t[:, GLA_STEP - 1:GLA_STEP]
    kx_t = k_t * jnp.exp(jnp.where(first_t, bl_a, bl_b) - b_t)
    zf = jnp.zeros((), F32)
    kx_a = jnp.where(first_t, kx_t, zf).astype(BF16)
    kx_b = jnp.where(first_t, zf, kx_t).astype(BF16)
    srow = lax.broadcasted_iota(jnp.int32, (GLA_PAIR_K, GLA_PAIR_V), 0)
    slane = lax.broadcasted_iota(jnp.int32, (GLA_PAIR_K, GLA_PAIR_V), 1)
    diag = (srow < GLA_KEY_DIM) == (slane < GLA_VAL_DIM)
    s_a = jnp.exp(bl_a) * s_prev + jnp.where(diag, _dot(kx_a, v2), 0.0)
    s_b = jnp.exp(bl_b) * s_a + jnp.where(diag, _dot(kx_b, v2), 0.0)
    o = o + _dot(jnp.where(first, qx, zf).astype(BF16), s_prev.astype(BF16))
    o = o + _dot(jnp.where(first, zf, qx).astype(BF16), s_a.astype(BF16))
    return o, s_b


def _gla_init_body(k_ref, v_ref, gk_ref, s_out):
    causal, tri = _gla_consts()
    for p in range(GLA_HEADS // 2):
        k2 = k_ref[:, p * GLA_PAIR_K:(p + 1) * GLA_PAIR_K].astype(F32)
        v2 = v_ref[:, p * GLA_PAIR_V:(p + 1) * GLA_PAIR_V]
        gk2 = gk_ref[:, p * GLA_PAIR_K:(p + 1) * GLA_PAIR_K]
        s0 = jnp.zeros((GLA_PAIR_K, GLA_PAIR_V), F32)
        _, s = _gla_step(jnp.zeros_like(k2), k2, v2, gk2, s0, causal, tri)
        s_out[p] = s


def _gla_init_call(kg_m, vg_m, gk_m):
    return pl.pallas_call(
        _gla_init_body,
        out_shape=jax.ShapeDtypeStruct((GLA_HEADS // 2, GLA_PAIR_K, GLA_PAIR_V), F32),
        name="gla_init",
    )(kg_m, vg_m, gk_m)


def _gla_body(s0_ref, q_ref, k_ref, v_ref, gk_ref, rg_ref, nw_ref, o_ref, s_ref):
    t = pl.program_id(1)

    @pl.when(t == 0)
    def _():
        s_ref[...] = s0_ref[...]

    causal, tri = _gla_consts()
    nw = nw_ref[...]
    for p in range(GLA_HEADS // 2):
        ks = slice(p * GLA_PAIR_K, (p + 1) * GLA_PAIR_K)
        vs = slice(p * GLA_PAIR_V, (p + 1) * GLA_PAIR_V)
        s = s_ref[p]
        for u in range(GLA_GRID_STEPS):
            rows = slice(u * GLA_STEP, (u + 1) * GLA_STEP)
            o, s = _gla_step(q_ref[rows, ks].astype(F32), k_ref[rows, ks].astype(F32), v_ref[rows, vs],
                             gk_ref[rows, ks], s, causal, tri)
            outs = []
            for hh in range(2):
                oh = o[:, hh * GLA_VAL_DIM:(hh + 1) * GLA_VAL_DIM]
                ms = jnp.mean(oh * oh, axis=-1, keepdims=True)
                outs.append(oh * lax.rsqrt(ms + NORM_EPS) * nw)
            on = jnp.concatenate(outs, axis=1)
            o_ref[rows, vs] = (on * rg_ref[rows, vs].astype(F32)).astype(BF16)
        s_ref[p] = s


GLA_GRID_STEPS = 4


def _mixers_body(sink_ref, q_ref, kp_ref, kc_ref, km_ref, vp_ref, vc_ref, vm_ref,
                 s0_ref, qg_ref, kg_ref, vg_ref, gk_ref, rg_ref, nw_ref, ao_ref, go_ref, s_ref):
    _attn_body(sink_ref, q_ref, kp_ref, kc_ref, km_ref, vp_ref, vc_ref, vm_ref, ao_ref)
    _gla_body(s0_ref, qg_ref, kg_ref, vg_ref, gk_ref, rg_ref, nw_ref, go_ref, s_ref)


def _mixers_call(q, kd, vd, km, vm, sinks, s0, qg, kg, vg, gk, rg, nw, batch, seq):
    step = ATTN_STEP_BLOCKS * BLOCK
    assert seq % step == 0 and step == GLA_GRID_STEPS * GLA_STEP
    seq_spec = lambda width: pl.BlockSpec((None, step, width), lambda b, i, s: (b, i, 0))
    prev = pl.BlockSpec((None, BLOCK, ATTN_KV_DUP_W),
                        lambda b, i, s: (b, jnp.maximum(i * ATTN_STEP_BLOCKS - 1, 0), 0))
    const = lambda shape: pl.BlockSpec(shape, lambda b, i, s: (0,) * len(shape))
    meta = const((BLOCK, ATTN_KV_DUP_W))
    grid_spec = pltpu.PrefetchScalarGridSpec(
        num_scalar_prefetch=1,
        grid=(batch, seq // step),
        in_specs=[seq_spec(ATTN_Q_W), prev, seq_spec(ATTN_KV_DUP_W), meta,
                  prev, seq_spec(ATTN_KV_DUP_W), meta,
                  const((GLA_HEADS // 2, GLA_PAIR_K, GLA_PAIR_V)),
                  seq_spec(GLA_K_W), seq_spec(GLA_K_W), seq_spec(GLA_V_W), seq_spec(GLA_K_W),
                  seq_spec(GLA_V_W), const((1, GLA_VAL_DIM))],
        out_specs=[seq_spec(ATTN_Q_W), seq_spec(GLA_V_W)],
        scratch_shapes=[pltpu.VMEM((GLA_HEADS // 2, GLA_PAIR_K, GLA_PAIR_V), F32)],
    )
    return pl.pallas_call(
        _mixers_body,
        grid_spec=grid_spec,
        out_shape=[jax.ShapeDtypeStruct((batch, seq, ATTN_Q_W), BF16),
                   jax.ShapeDtypeStruct((batch, seq, GLA_V_W), BF16)],
        compiler_params=pltpu.CompilerParams(dimension_semantics=("arbitrary", "arbitrary"),
                                             vmem_limit_bytes=VMEM_LIMIT),
        name="mixers",
    )(sinks, q, kd, kd, km, vd, vd, vm, s0, qg, kg, vg, gk, rg, nw)


ROUTER_W = LANES


RT_E0, RT_E1, RT_W0, RT_W1 = 0, 1, 2, 3


def _merge_body(x_ref, ao_ref, go_ref, sga_ref, sgg_ref, wab_ref, wgb_ref, wo_ref, n2_ref,
                wr2_ref, h_out, xn_out, rt_out, rtt_out, cnt_out):
    ya = _dot(ao_ref[...], wab_ref[...])
    yg = _dot(go_ref[...], wgb_ref[...])
    merged = sga_ref[...].astype(F32) * ya + sgg_ref[...].astype(F32) * yg
    h = x_ref[...] + _dot(merged.astype(BF16), wo_ref[...])
    _store_slabs(h_out, h)
    xn = _rms_norm(h, n2_ref[...])
    _store_slabs(xn_out, xn)

    xh, xl = _split_hi_lo(xn)
    wr2 = wr2_ref[...]
    lg2 = _dot(xh, wr2) + _dot(xl, wr2)
    lg = lg2[:, :ROUTER_W] + lg2[:, ROUTER_W:]
    lg_t = lg.T
    tm = lg.shape[0]
    neg = -jnp.inf
    big = jnp.int32(1 << 20)
    gl = lg_t[N_EXPERTS:N_EXPERTS + N_GROUPS, :]
    grow = lax.broadcasted_iota(jnp.int32, (N_GROUPS, tm), 0)
    gmax = jnp.max(gl, axis=0, keepdims=True)
    g_idx = jnp.min(jnp.where(gl == gmax, grow, big), axis=0, keepdims=True)
    g_w = 1.0 / jnp.sum(jnp.exp(gl - gmax), axis=0, keepdims=True)
    erow = lax.broadcasted_iota(jnp.int32, (N_EXPERTS, tm), 0)
    el = jnp.where((erow >> 3) == g_idx, lg_t[0:N_EXPERTS, :], neg)
    m1 = jnp.max(el, axis=0, keepdims=True)
    i1 = jnp.min(jnp.where(el == m1, erow, big), axis=0, keepdims=True)
    el2 = jnp.where(erow == i1, neg, el)
    m2 = jnp.max(el2, axis=0, keepdims=True)
    i2 = jnp.min(jnp.where(el2 == m2, erow, big), axis=0, keepdims=True)
    e2 = jnp.exp(m2 - m1)
    p1 = g_w / (1.0 + e2)
    p2 = g_w * e2 / (1.0 + e2)
    def record(rows):
        r = lax.broadcasted_iota(jnp.int32, (rows, tm), 0)
        return jnp.where(r == RT_E0, i1.astype(F32),
                         jnp.where(r == RT_E1, i2.astype(F32),
                                   jnp.where(r == RT_W0, p1, jnp.where(r == RT_W1, p2, 0.0))))
    rtt_out[...] = record(8)
    rt_out[...] = record(LANES).T
    picked = jnp.where(jnp.logical_or(erow == i1, erow == i2), 1.0, 0.0)

    @pl.when(pl.program_id(0) == 0)
    def _():
        cnt_out[...] = jnp.zeros_like(cnt_out)

    cnt_out[...] += jnp.sum(picked, axis=1, keepdims=True)


def _merge_call(x2d, ao, go, sga, sgg, w, tm):
    n = x2d.shape[0]
    row = lambda width: pl.BlockSpec((tm, width), lambda i: (i, 0))
    return pl.pallas_call(
        _merge_body,
        grid=(n // tm,),
        in_specs=[row(D_MODEL), row(ATTN_Q_W), row(GLA_V_W), row(D_MODEL), row(D_MODEL),
                  _const_spec((ATTN_Q_W, D_MODEL)), _const_spec((GLA_V_W, D_MODEL)),
                  _const_spec((D_MODEL, D_MODEL)), _const_spec((1, D_MODEL)),
                  _const_spec((D_MODEL, 2 * ROUTER_W))],
        out_specs=[pl.BlockSpec((ROW_TILES, tm, LANES), lambda i: (0, i, 0)),
                   pl.BlockSpec((ROW_TILES, tm, LANES), lambda i: (0, i, 0)), row(ROUTER_W),
                   pl.BlockSpec((8, tm), lambda i: (0, i)), _const_spec((N_EXPERTS, LANES))],
        out_shape=[jax.ShapeDtypeStruct((ROW_TILES, n, LANES), F32),
                   jax.ShapeDtypeStruct((ROW_TILES, n, LANES), F32),
                   jax.ShapeDtypeStruct((n, ROUTER_W), F32),
                   jax.ShapeDtypeStruct((8, n), F32),
                   jax.ShapeDtypeStruct((N_EXPERTS, LANES), F32)],
        compiler_params=pltpu.CompilerParams(dimension_semantics=("arbitrary",),
                                             vmem_limit_bytes=VMEM_LIMIT),
        name="merge",
    )(x2d, ao, go, sga, sgg, w["wab"], w["wgb"], w["wo"], w["norm2"], w["wr2"])


EXPERT_TILE = 512
ROUTE_TILE = 512
VISIT_CAP = 256
VIS_EXPERT, VIS_TILE, VIS_LO, VIS_HI, VIS_COUNT = 0, 1, 2, 3, 4


def _num_visits(n_tokens):
    return (2 * n_tokens) // EXPERT_TILE + N_EXPERTS - 1


def _expert_cumsum_inclusive(x):
    row = lax.broadcasted_iota(jnp.int32, x.shape, 0)
    s = 1
    while s < N_EXPERTS:
        x = x + jnp.where(row >= s, pltpu.roll(x, s, 0), 0.0)
        s *= 2
    return x


def _route_body(rtt_ref, cnt_ref, pos_ref, vis_ref, carry_ref):
    i = pl.program_id(0)
    tm = rtt_ref.shape[1]
    cnt = cnt_ref[...]
    end = _expert_cumsum_inclusive(cnt)
    base = end - cnt

    @pl.when(i == 0)
    def _():
        carry_ref[...] = jnp.zeros_like(carry_ref)
        inv = 1.0 / EXPERT_TILE
        first = jnp.floor(base * inv)
        n_vis = jnp.where(cnt > 0.0, jnp.floor((end - 1.0) * inv) - first + 1.0, 0.0)
        vend = _expert_cumsum_inclusive(n_vis)
        vstart = vend - n_vis
        col = lambda a: a[:, 0:1]
        erow = lax.broadcasted_iota(jnp.int32, (N_EXPERTS, VISIT_CAP), 0).astype(F32)
        v = lax.broadcasted_iota(jnp.int32, (N_EXPERTS, VISIT_CAP), 1).astype(F32)
        ev = jnp.minimum(jnp.sum(jnp.where(col(vend) <= v, 1.0, 0.0), axis=0, keepdims=True),
                         N_EXPERTS - 1.0)
        mine = erow == ev
        pick = lambda a: jnp.sum(jnp.where(mine, a, 0.0), axis=0, keepdims=True)
        tile = pick(col(first) + v - col(vstart))
        lo = jnp.maximum(pick(col(base)) - tile * EXPERT_TILE, 0.0)
        hi = jnp.minimum(pick(col(end)) - tile * EXPERT_TILE, float(EXPERT_TILE))
        total = vend[N_EXPERTS - 1:N_EXPERTS, 0:1]
        r8 = lax.broadcasted_iota(jnp.int32, (8, VISIT_CAP), 0)
        rec = jnp.where(r8 == VIS_EXPERT, ev,
                        jnp.where(r8 == VIS_TILE, tile,
                                  jnp.where(r8 == VIS_LO, lo,
                                            jnp.where(r8 == VIS_HI, hi,
                                                      jnp.where(r8 == VIS_COUNT, total, 0.0)))))
        vis_ref[...] = rec.astype(jnp.int32)

    rtt = rtt_ref[...]
    erow = lax.broadcasted_iota(jnp.int32, (N_EXPERTS, tm), 0).astype(F32)
    oh0 = erow == rtt[RT_E0:RT_E0 + 1, :]
    oh1 = erow == rtt[RT_E1:RT_E1 + 1, :]
    picked = jnp.where(jnp.logical_or(oh0, oh1), 1.0, 0.0)
    r = lax.broadcasted_iota(jnp.int32, (tm, tm), 0)
    c = lax.broadcasted_iota(jnp.int32, (tm, tm), 1)
    earlier = jnp.where(r < c, 1.0, 0.0).astype(BF16)
    slot = _dot(picked.astype(BF16), earlier) + (carry_ref[:, 0:1] + base[:, 0:1])
    p0 = jnp.sum(jnp.where(oh0, slot, 0.0), axis=0, keepdims=True)
    p1 = jnp.sum(jnp.where(oh1, slot, 0.0), axis=0, keepdims=True)
    r8 = lax.broadcasted_iota(jnp.int32, (8, tm), 0)
    pos_ref[...] = jnp.where(r8 == 0, p0, jnp.where(r8 == 1, p1, 0.0)).astype(jnp.int32)
    carry_ref[...] += jnp.sum(picked, axis=1, keepdims=True)


def _route_call(rtt, cnt):
    n = rtt.shape[1]
    tm = _pick_tile(n, ROUTE_TILE)
    return pl.pallas_call(
        _route_body,
        grid=(n // tm,),
        in_specs=[pl.BlockSpec((8, tm), lambda i: (0, i)), _const_spec((N_EXPERTS, LANES))],
        out_specs=[pl.BlockSpec((8, tm), lambda i: (0, i)), _const_spec((8, VISIT_CAP))],
        out_shape=[jax.ShapeDtypeStruct((8, n), jnp.int32),
                   jax.ShapeDtypeStruct((8, VISIT_CAP), jnp.int32)],
        scratch_shapes=[pltpu.VMEM((N_EXPERTS, LANES), F32)],
        compiler_params=pltpu.CompilerParams(dimension_semantics=("arbitrary",)),
        name="route",
    )(rtt, cnt)


SCATTER_TILE = 512
ROW_TILES = D_MODEL // LANES


def _store_slabs(ref, val):
    for c in range(ROW_TILES):
        ref[c] = val[:, c * LANES:(c + 1) * LANES]


def _load_slabs(ref):
    return jnp.concatenate([ref[c] for c in range(ROW_TILES)], axis=1)


def _scatter_body(pos0_ref, pos1_ref, h_ref, hs_hbm, sem):
    tm = h_ref.shape[1]
    for r in range(tm):
        src = h_ref.at[:, r, :]
        pltpu.make_async_copy(src, hs_hbm.at[:, pos0_ref[r], :], sem).start(priority=0)
        pltpu.make_async_copy(src, hs_hbm.at[:, pos1_ref[r], :], sem).start(priority=1)
    for _ in range(2 * tm):
        pltpu.make_async_copy(h_ref.at[:, 0, :], hs_hbm.at[:, 0, :], sem).wait()


def _scatter_call(pos0, pos1, h_slabs, n_slots):
    n = h_slabs.shape[1]
    tm = _pick_tile(n, SCATTER_TILE)
    smem = lambda: pl.BlockSpec((tm,), lambda i: (i,), memory_space=pltpu.SMEM)
    return pl.pallas_call(
        _scatter_body,
        grid=(n // tm,),
        in_specs=[smem(), smem(), pl.BlockSpec((ROW_TILES, tm, LANES), lambda i: (0, i, 0))],
        out_specs=pl.BlockSpec(memory_space=pl.ANY),
        out_shape=jax.ShapeDtypeStruct((ROW_TILES, n_slots, LANES), F32),
        scratch_shapes=[pltpu.SemaphoreType.DMA],
        compiler_params=pltpu.CompilerParams(dimension_semantics=("arbitrary",)),
        name="scatter",
    )(pos0, pos1, h_slabs)


def _expert_body(vis_ref, x_ref, wg_ref, wu_ref, wd_ref, o_ref):
    j = pl.program_id(0)

    @pl.when(j < vis_ref[VIS_COUNT, 0])
    def _():
        lo_row = vis_ref[VIS_LO, j]
        hi_row = vis_ref[VIS_HI, j]
        x = _load_slabs(x_ref).astype(BF16)
        g = _dot(x, wg_ref[...].astype(BF16))
        u = _dot(x, wu_ref[...].astype(BF16))
        hg = (g * _sigmoid(g) * u).astype(BF16)
        new = _dot(hg, wd_ref[...].astype(BF16))
        row = lax.broadcasted_iota(jnp.int32, new.shape, 0)

        @pl.when(lo_row == 0)
        def _():
            _store_slabs(o_ref, jnp.where(row < hi_row, new, 0.0))

        @pl.when(lo_row > 0)
        def _():
            keep = jnp.logical_and(row >= lo_row, row < hi_row)
            _store_slabs(o_ref, jnp.where(keep, new, _load_slabs(o_ref)))


def _expert_call(vis, hs, wg, wu, wd, n_visits):
    def visit(j, vis):
        return jnp.minimum(j, vis[VIS_COUNT, 0] - 1)

    def xmap(j, vis):
        return (0, vis[VIS_TILE, visit(j, vis)], 0)

    def wmap(j, vis):
        return (vis[VIS_EXPERT, visit(j, vis)], 0, 0)

    grid_spec = pltpu.PrefetchScalarGridSpec(
        num_scalar_prefetch=1,
        grid=(n_visits,),
        in_specs=[pl.BlockSpec((ROW_TILES, EXPERT_TILE, LANES), xmap),
                  pl.BlockSpec((None, D_MODEL, EXPERT_FF), wmap),
                  pl.BlockSpec((None, D_MODEL, EXPERT_FF), wmap),
                  pl.BlockSpec((None, EXPERT_FF, D_MODEL), wmap)],
        out_specs=pl.BlockSpec((ROW_TILES, EXPERT_TILE, LANES), xmap),
    )
    return pl.pallas_call(
        _expert_body,
        grid_spec=grid_spec,
        out_shape=jax.ShapeDtypeStruct(hs.shape, F32),
        compiler_params=pltpu.CompilerParams(dimension_semantics=("arbitrary",),
                                             vmem_limit_bytes=VMEM_LIMIT),
        name="experts",
    )(vis, hs, wg, wu, wd)


COMBINE_TILE = 256


def _combine_body(p0_ref, p1_ref, p0n_ref, p1n_ref, h_ref, rt_ref, y_hbm, o_ref, g_ref, sem):
    i = pl.program_id(0)
    tm = h_ref.shape[1]
    cur = i % 2

    def issue(q0_ref, q1_ref, buf):
        for r in range(tm):
            pltpu.make_async_copy(y_hbm.at[:, q0_ref[r], :], g_ref.at[buf, 0, :, r, :],
                                  sem.at[buf]).start(priority=0)
            pltpu.make_async_copy(y_hbm.at[:, q1_ref[r], :], g_ref.at[buf, 1, :, r, :],
                                  sem.at[buf]).start(priority=1)

    @pl.when(i == 0)
    def _():
        issue(p0_ref, p1_ref, 0)

    @pl.when(i + 1 < pl.num_programs(0))
    def _():
        issue(p0n_ref, p1n_ref, 1 - cur)

    for _ in range(2 * tm):
        pltpu.make_async_copy(y_hbm.at[:, 0, :], g_ref.at[cur, 0, :, 0, :], sem.at[cur]).wait()
    rt = rt_ref[...]
    o_ref[...] = (_load_slabs(h_ref) + rt[:, RT_W0:RT_W0 + 1] * _load_slabs(g_ref.at[cur, 0])
                  + rt[:, RT_W1:RT_W1 + 1] * _load_slabs(g_ref.at[cur, 1]))


def _combine_call(pos0, pos1, h_slabs, rt, y):
    n = h_slabs.shape[1]
    tm = _pick_tile(n, COMBINE_TILE)
    steps = n // tm
    smem_cur = lambda: pl.BlockSpec((tm,), lambda i: (i,), memory_space=pltpu.SMEM)
    smem_next = lambda: pl.BlockSpec((tm,), lambda i: (jnp.minimum(i + 1, steps - 1),),
                                     memory_space=pltpu.SMEM)
    row = lambda width: pl.BlockSpec((tm, width), lambda i: (i, 0))
    return pl.pallas_call(
        _combine_body,
        grid=(steps,),
        in_specs=[smem_cur(), smem_cur(), smem_next(), smem_next(),
                  pl.BlockSpec((ROW_TILES, tm, LANES), lambda i: (0, i, 0)), row(ROUTER_W),
                  pl.BlockSpec(memory_space=pl.ANY)],
        out_specs=row(D_MODEL),
        out_shape=jax.ShapeDtypeStruct((n, D_MODEL), F32),
        scratch_shapes=[pltpu.VMEM((2, 2, ROW_TILES, tm, LANES), F32), pltpu.SemaphoreType.DMA((2,))],
        compiler_params=pltpu.CompilerParams(dimension_semantics=("arbitrary",),
                                             vmem_limit_bytes=VMEM_LIMIT),
        name="combine",
    )(pos0, pos1, pos0, pos1, h_slabs, rt, y)


def _rope_tables(pos):
    half = HEAD_DIM // 2
    inv_freq = ROPE_THETA ** (-(jnp.arange(half, dtype=F32) * 2.0) / HEAD_DIM)
    ang = pos.astype(F32)[:, None] * inv_freq[None, :]
    cos = jnp.cos(ang)
    sin = jnp.sin(ang)
    cos64 = jnp.concatenate([cos, cos], axis=1)
    sin64 = jnp.concatenate([-sin, sin], axis=1)
    return jnp.tile(cos64, (1, LANES // HEAD_DIM)), jnp.tile(sin64, (1, LANES // HEAD_DIM))


def _prep_weights(norm1_w, w_in, q_norm_w, k_norm_w, gla_gate_up, gla_gate_bias, w_attn_branch,
                  w_gla_branch, w_out, norm2_w, router_group, router_expert):
    wb = w_in.astype(BF16)
    wqkv = jnp.zeros((D_MODEL, QKV_W), BF16).at[:, :QKV_GD_OFF].set(wb[:, :QKV_GD_OFF])
    wqkv = wqkv.at[:, QKV_GD_OFF:QKV_GD_OFF + GLA_GATE_RANK].set(wb[:, _OFF_GD:_OFF_GD + GLA_GATE_RANK])
    gup =jnp.zeros((LANES, GLA_K_W), BF16).at[:GLA_GATE_RANK].set(gla_gate_up.astype(BF16))
    wr = jnp.zeros((D_MODEL, ROUTER_W), F32)
    wr = wr.at[:, :N_EXPERTS].set(router_expert.transpose(1, 0, 2).reshape(D_MODEL, N_EXPERTS))
    wr = wr.at[:, N_EXPERTS:N_EXPERTS + N_GROUPS].set(router_group)
    wrh = wr.astype(BF16)
    wrl = (wr - wrh.astype(F32)).astype(BF16)
    return {
        "norm1": norm1_w.reshape(1, D_MODEL),
        "wqkv": wqkv,
        "wgla": wb[:, _OFF_GQ:_OFF_GQ + GLA_W],
        "wgate": wb[:, _OFF_GATE:_OFF_GATE + 2 * D_MODEL],
        "qn": jnp.tile(q_norm_w, ATTN_HEADS).reshape(1, ATTN_Q_W),
        "kn": jnp.tile(k_norm_w, ATTN_KV_HEADS).reshape(1, ATTN_KV_W),
        "gup": gup,
        "gb": gla_gate_bias.reshape(1, GLA_K_W),
        "wab": w_attn_branch.astype(BF16),
        "wgb": w_gla_branch.astype(BF16),
        "wo": w_out.astype(BF16),
        "norm2": norm2_w.reshape(1, D_MODEL),
        "wr2": jnp.concatenate([wrh, wrl], axis=1),
    }


def _pick_tile(n, want):
    t = want
    while n % t:
        t //= 2
    return t


def kernel(x, meta_tokens, norm1_w, w_in, q_norm_w, k_norm_w, attn_sinks, gla_gate_up, gla_gate_bias,
           gla_norm_w, w_attn_branch, w_gla_branch, w_out, norm2_w, router_group, router_expert,
           expert_w_gate, expert_w_up, expert_w_down):
    batch, seq, d = x.shape
    assert d == D_MODEL and seq % GLA_STEP == 0 and norm1_w.shape[0] == 1
    n = batch * seq
    w = _prep_weights(norm1_w[0], w_in[0], q_norm_w[0], k_norm_w[0], gla_gate_up[0], gla_gate_bias[0],
                      w_attn_branch[0], w_gla_branch[0], w_out[0], norm2_w[0], router_group[0],
                      router_expert[0])
    x2d = x.reshape(n, D_MODEL)

    cos_m, sin_m = _rope_tables(jnp.arange(N_META))
    cos_r, sin_r = _rope_tables(jnp.arange(seq) + N_META)
    meta = _proj_call(meta_tokens.astype(F32), N_META, 1, w, cos_m, sin_m)
    tm = _pick_tile(seq, 512)
    (q, kd, vd, qg, kg, vg, rg, gk, sga, sgg) = _proj_call(x2d, tm, seq // tm, w, cos_r, sin_r)

    pad_after = ((0, BLOCK - N_META), (0, 0))
    km = jnp.pad(meta[1], pad_after)
    vm = jnp.pad(meta[2], pad_after)
    pad_before = ((GLA_STEP - N_META, 0), (0, 0))
    s0 = _gla_init_call(jnp.pad(meta[4], pad_before), jnp.pad(meta[5], pad_before),
                        jnp.pad(meta[7], pad_before))
    ao, go = _mixers_call(q.reshape(batch, seq, ATTN_Q_W), kd.reshape(batch, seq, ATTN_KV_DUP_W),
                          vd.reshape(batch, seq, ATTN_KV_DUP_W), km, vm, attn_sinks[0].astype(F32),
                          s0, qg.reshape(batch, seq, GLA_K_W), kg.reshape(batch, seq, GLA_K_W),
                          vg.reshape(batch, seq, GLA_V_W), gk.reshape(batch, seq, GLA_K_W),
                          rg.reshape(batch, seq, GLA_V_W), gla_norm_w[0].reshape(1, GLA_VAL_DIM),
                          batch, seq)

    h, xn2, rt, rtt, cnt = _merge_call(x2d, ao.reshape(n, ATTN_Q_W), go.reshape(n, GLA_V_W), sga, sgg, w,
                                       _pick_tile(n, 512))

    assert (2 * n) % EXPERT_TILE == 0 and _num_visits(n) <= VISIT_CAP
    pos, vis = _route_call(rtt, cnt)
    pos0, pos1 = pos[0], pos[1]
    xs = _scatter_call(pos0, pos1, xn2, 2 * n)
    wg = expert_w_gate[0].reshape(N_EXPERTS, D_MODEL, EXPERT_FF)
    wu = expert_w_up[0].reshape(N_EXPERTS, D_MODEL, EXPERT_FF)
    wd = expert_w_down[0].reshape(N_EXPERTS, EXPERT_FF, D_MODEL)
    y = _expert_call(vis, xs, wg, wu, wd, _num_visits(n))
    out = _combine_call(pos0, pos1, h, rt, y)
    return out.reshape(batch, seq, D_MODEL)
```

```python
import functools
import math

import jax
import jax.numpy as jnp
from jax import lax
from jax.experimental import pallas as pl
from jax.experimental.pallas import tpu as pltpu

F32 = jnp.float32
BF16 = jnp.bfloat16

D_MODEL = 1024
N_META = 16
BLOCK = 128
ATTN_HEADS = 8
ATTN_KV_HEADS = 2
HEAD_DIM = 64
ROPE_THETA = 10000.0
ATTN_Q_W = ATTN_HEADS * HEAD_DIM
ATTN_KV_W = ATTN_KV_HEADS * HEAD_DIM
GLA_HEADS = 4
GLA_KEY_DIM = 64
GLA_VAL_DIM = 128
GLA_K_W = GLA_HEADS * GLA_KEY_DIM
GLA_V_W = GLA_HEADS * GLA_VAL_DIM
GLA_GATE_RANK = 16
GLA_GATE_NORM = 16.0
GLA_CHUNK = 64
N_GROUPS = 4
EXPERTS_PER_GROUP = 8
N_EXPERTS = N_GROUPS * EXPERTS_PER_GROUP
EXPERT_FF = 256
NORM_EPS = 1e-6
MASK_VALUE = -1e30

LANES = 128
ATTN_KV_DUP_W = 2 * ATTN_KV_W
VMEM_LIMIT = 56 * 1024 * 1024

_OFF_Q = 0
_OFF_K = _OFF_Q + ATTN_Q_W
_OFF_V = _OFF_K + ATTN_KV_W
_OFF_GQ = _OFF_V + ATTN_KV_W
_OFF_GD = _OFF_GQ + 2 * GLA_K_W + 2 * GLA_V_W
_OFF_GATE = _OFF_GD + GLA_GATE_RANK
GLA_W = 2 * GLA_K_W + 2 * GLA_V_W
QKV_GD_OFF = _OFF_GQ
QKV_W = D_MODEL


def _dot(a, b):
    return jnp.dot(a, b, preferred_element_type=F32)


def _dot_nt(a, b):
    return lax.dot_general(a, b, (((1,), (1,)), ((), ())), preferred_element_type=F32)


def _split_hi_lo(x):
    hi = x.astype(BF16)
    lo = (x - hi.astype(F32)).astype(BF16)
    return hi, lo


MXU_TILE = 256


def _group_mean_sq(x, group):
    w = x.shape[-1]
    slab = min(w, MXU_TILE)
    shift = int(math.log2(group))
    r = lax.broadcasted_iota(jnp.int32, (slab, slab), 0) >> shift
    c = lax.broadcasted_iota(jnp.int32, (slab, slab), 1) >> shift
    ones = jnp.where(r == c, 1.0, 0.0).astype(BF16)
    hi, lo = _split_hi_lo(x * x)
    sums = [_dot(hi[:, s:s + slab], ones) + _dot(lo[:, s:s + slab], ones) for s in range(0, w, slab)]
    return (sums[0] if len(sums) == 1 else jnp.concatenate(sums, axis=1)) * (1.0 / group)


def _rope(x, cos, sin_signed):
    w = x.shape[-1]
    reps = w // LANES
    if reps > 1:
        cos = jnp.concatenate([cos] * reps, axis=1)
        sin_signed = jnp.concatenate([sin_signed] * reps, axis=1)
    lane = lax.broadcasted_iota(jnp.int32, x.shape, 1)
    first_half = (lane & (HEAD_DIM // 2)) == 0
    swapped = jnp.where(first_half, pltpu.roll(x, w - HEAD_DIM // 2, 1), pltpu.roll(x, HEAD_DIM // 2, 1))
    return x * cos + swapped * sin_signed


def _rms_norm(x, w):
    ms = jnp.mean(x * x, axis=-1, keepdims=True)
    return x * lax.rsqrt(ms + NORM_EPS) * w


def _sigmoid(x):
    return 0.5 * jnp.tanh(0.5 * x) + 0.5


def _log_sigmoid(x):
    return jnp.minimum(x, 0.0) - jnp.log(1.0 + jnp.exp(-jnp.abs(x)))


def _dup_heads_on_lanes(x):
    low = lax.broadcasted_iota(jnp.int32, x.shape, 1) < HEAD_DIM
    swapped = pltpu.roll(x, HEAD_DIM, 1)
    return jnp.concatenate([jnp.where(low, x, swapped), jnp.where(low, swapped, x)], axis=1)


def _proj_body(x_ref, n1_ref, wqkv_ref, wgla_ref, wgate_ref,
               qn_ref, kn_ref, cos_ref, sin_ref, gup_ref, gb_ref,
               q_out, k_out, v_out, qg_out, kg_out, vg_out, rg_out, gk_out, sga_out, sgg_out):
    xn = _rms_norm(x_ref[...], n1_ref[...]).astype(BF16)
    cos = cos_ref[...]
    sin = sin_ref[...]

    a = _dot(xn, wqkv_ref[...])
    q = a[:, :ATTN_Q_W]
    q = q * lax.rsqrt(_group_mean_sq(q, HEAD_DIM) + NORM_EPS) * qn_ref[...]
    q_out[...] = (_rope(q, cos, sin) * (HEAD_DIM ** -0.5)).astype(BF16)

    k = a[:, _OFF_K:_OFF_K + ATTN_KV_W]
    k = k * lax.rsqrt(_group_mean_sq(k, HEAD_DIM) + NORM_EPS) * kn_ref[...]
    k_out[...] = _dup_heads_on_lanes(_rope(k, cos, sin)).astype(BF16)

    v_out[...] = _dup_heads_on_lanes(a[:, _OFF_V:_OFF_V + ATTN_KV_W]).astype(BF16)

    g = _dot(xn, wgla_ref[...])
    qg_out[...] = (g[:, :GLA_K_W] * (GLA_KEY_DIM ** -0.5)).astype(BF16)
    kg_out[...] = g[:, GLA_K_W:2 * GLA_K_W].astype(BF16)
    vg_out[...] = g[:, 2 * GLA_K_W:2 * GLA_K_W + GLA_V_W].astype(BF16)
    r = g[:, 2 * GLA_K_W + GLA_V_W:]
    rg_out[...] = (r * _sigmoid(r)).astype(BF16)

    gd = a[:, QKV_GD_OFF:QKV_GD_OFF + LANES].astype(BF16)
    z = _dot(gd, gup_ref[...]) + gb_ref[...]
    gk_out[...] = _log_sigmoid(z) * (1.0 / GLA_GATE_NORM)

    gates = _dot(xn, wgate_ref[...])
    sga_out[...] = _sigmoid(gates[:, :D_MODEL]).astype(BF16)
    sgg_out[...] = _sigmoid(gates[:, D_MODEL:]).astype(BF16)


def _const_spec(shape):
    nd = len(shape)
    return pl.BlockSpec(shape, lambda *_: (0,) * nd)


def _proj_call(x2d, tm, tiles_per_seq, w, cos, sin):
    n = x2d.shape[0]
    grid = (n // tm,)
    row = lambda width: pl.BlockSpec((tm, width), lambda i: (i, 0))
    tab = pl.BlockSpec((tm, LANES), lambda i: (i % tiles_per_seq, 0))
    in_specs = [
        row(D_MODEL), _const_spec((1, D_MODEL)),
        _const_spec((D_MODEL, QKV_W)), _const_spec((D_MODEL, GLA_W)),
        _const_spec((D_MODEL, 2 * D_MODEL)),
        _const_spec((1, ATTN_Q_W)), _const_spec((1, ATTN_KV_W)),
        tab, tab, _const_spec((LANES, GLA_K_W)), _const_spec((1, GLA_K_W)),
    ]
    out_widths = [ATTN_Q_W, ATTN_KV_DUP_W, ATTN_KV_DUP_W, GLA_K_W, GLA_K_W, GLA_V_W, GLA_V_W,
                  GLA_K_W, D_MODEL, D_MODEL]
    out_dtypes = [BF16] * 7 + [F32] + [BF16] * 2
    return pl.pallas_call(
        _proj_body,
        grid=grid,
        in_specs=in_specs,
        out_specs=[row(wd) for wd in out_widths],
        out_shape=[jax.ShapeDtypeStruct((n, wd), dt) for wd, dt in zip(out_widths, out_dtypes)],
        compiler_params=pltpu.CompilerParams(dimension_semantics=("arbitrary",),
                                             vmem_limit_bytes=VMEM_LIMIT),
        name="proj",
    )(x2d, w["norm1"], w["wqkv"], w["wgla"], w["wgate"],
      w["qn"], w["kn"], cos, sin, w["gup"], w["gb"])


ATTN_STEP_BLOCKS = 4


def _attn_body(sink_ref, q_ref, kp_ref, kc_ref, km_ref, vp_ref, vc_ref, vm_ref, o_ref):
    i = pl.program_id(1)
    nblk = ATTN_STEP_BLOCKS
    pair_rows = 2 * BLOCK
    lane = lax.broadcasted_iota(jnp.int32, (BLOCK, LANES), 1)
    low = lane < HEAD_DIM
    low2 = lax.broadcasted_iota(jnp.int32, (pair_rows, LANES), 1) < HEAD_DIM
    top2 = lax.broadcasted_iota(jnp.int32, (pair_rows, 1), 0) < BLOCK
    key = lax.broadcasted_iota(jnp.int32, (pair_rows, 2 * LANES), 1) & (LANES - 1)
    rowi = lax.broadcasted_iota(jnp.int32, (pair_rows, 2 * LANES), 0) & (BLOCK - 1)
    mask_meta = key < N_META
    mask_prev = key > rowi
    mask_cur = key <= rowi
    zero = jnp.zeros((), BF16)

    def block_diag(x):
        return jnp.concatenate([jnp.where(low, x, zero), jnp.where(low, zero, x)], axis=0)

    blk_row = lax.broadcasted_iota(jnp.int32, (2 * BLOCK, LANES), 0)
    blk_lane = lax.broadcasted_iota(jnp.int32, (2 * BLOCK, LANES), 1)
    ones_blk = jnp.where((blk_row < BLOCK) == (blk_lane < HEAD_DIM), 1.0, 0.0).astype(BF16)

    def masked(s, mask):
        return jnp.where(mask, s, MASK_VALUE)

    for kvh in range(ATTN_KV_HEADS):
        sl = slice(kvh * LANES, (kvh + 1) * LANES)
        k_seq = [block_diag(kp_ref[:, sl])]
        v_seq = [block_diag(vp_ref[:, sl])]
        for t in range(nblk):
            rows = slice(t * BLOCK, (t + 1) * BLOCK)
            k_seq.append(block_diag(kc_ref[rows, sl]))
            v_seq.append(block_diag(vc_ref[rows, sl]))
        q_all = jnp.concatenate(
            [q_ref[t * BLOCK:(t + 1) * BLOCK, (2 * kvh + pr) * LANES:(2 * kvh + pr + 1) * LANES]
             for t in range(nblk) for pr in range(2)], axis=0)
        s_meta = _dot_nt(q_all, block_diag(km_ref[:, sl]))
        s_seq = []
        for j in range(nblk + 1):
            lo_blk, hi_blk = max(j - 1, 0), min(j, nblk - 1)
            s_seq.append(_dot_nt(q_all[lo_blk * pair_rows:(hi_blk + 1) * pair_rows], k_seq[j]))

        def sink_col(hh):
            return jnp.where(top2, sink_ref[4 * kvh + hh], sink_ref[4 * kvh + 2 + hh])

        p_meta, p_prev, p_cur, l_all = [], [], [], []
        for t in range(nblk):
            sm = masked(s_meta[t * pair_rows:(t + 1) * pair_rows], mask_meta)
            prev_rows = slice(0, pair_rows) if t == 0 else slice(pair_rows, 2 * pair_rows)
            sb = jnp.where(mask_prev, s_seq[t][prev_rows], s_seq[t + 1][0:pair_rows])
            if t == 0:
                sb = masked(sb, jnp.logical_or(mask_cur, i > 0))
            s_max = jnp.maximum(sm, sb)
            m2 = []
            sink_terms = []
            for hh in range(2):
                sink = sink_col(hh)
                m = jnp.maximum(jnp.max(s_max[:, hh * LANES:(hh + 1) * LANES], axis=1, keepdims=True), sink)
                m2.append(m)
                sink_terms.append(jnp.exp(sink - m))

            def probs(s):
                return jnp.concatenate([jnp.exp(s[:, hh * LANES:(hh + 1) * LANES] - m2[hh])
                                        for hh in range(2)], axis=1)

            pm, pb = probs(sm), probs(sb)
            pb16 = pb.astype(BF16)
            p_meta.append(pm.astype(BF16))
            p_prev.append(jnp.where(mask_prev, pb16, zero))
            p_cur.append(jnp.where(mask_prev, zero, pb16))
            l_all.append(_dot((pm + pb).astype(BF16), ones_blk)
                         + jnp.where(low2, sink_terms[0], sink_terms[1]))

        o_meta = _dot(jnp.concatenate(p_meta, axis=0), block_diag(vm_ref[:, sl]))
        o_seq = []
        for j in range(nblk + 1):
            parts = ([p_cur[j - 1]] if j >= 1 else []) + ([p_prev[j]] if j < nblk else [])
            o_seq.append(_dot(jnp.concatenate(parts, axis=0) if len(parts) > 1 else parts[0], v_seq[j]))
        for t in range(nblk):
            prev_rows = slice(0, pair_rows) if t == 0 else slice(pair_rows, 2 * pair_rows)
            o = (o_meta[t * pair_rows:(t + 1) * pair_rows] + o_seq[t][prev_rows]
                 + o_seq[t + 1][0:pair_rows]) / l_all[t]
            for pr in range(2):
                col = (2 * kvh + pr) * LANES
                o_ref[t * BLOCK:(t + 1) * BLOCK, col:col + LANES] = o[pr * BLOCK:(pr + 1) * BLOCK].astype(BF16)


GLA_STEP = 2 * GLA_CHUNK
GLA_PAIR_K = 2 * GLA_KEY_DIM
GLA_PAIR_V = 2 * GLA_VAL_DIM


def _gla_consts():
    r = lax.broadcasted_iota(jnp.int32, (GLA_STEP, 2 * GLA_STEP), 0)
    c = lax.broadcasted_iota(jnp.int32, (GLA_STEP, 2 * GLA_STEP), 1) & (GLA_STEP - 1)
    causal = jnp.logical_and((r >> 6) == (c >> 6), c <= r)
    tri = jnp.where(causal[:, :GLA_STEP], 1.0, 0.0).astype(BF16)
    return causal, tri


def _gla_step(q2, k2, v2, gk2, s_prev, causal, tri):
    row = lax.broadcasted_iota(jnp.int32, (GLA_STEP, GLA_PAIR_K), 0)
    lane = lax.broadcasted_iota(jnp.int32, (GLA_STEP, GLA_PAIR_K), 1)
    first = row < GLA_CHUNK
    low = lane < GLA_KEY_DIM
    g_hi = gk2.astype(BF16)
    g_r = gk2 - g_hi.astype(F32)
    g_mid = g_r.astype(BF16)
    g_lo = (g_r - g_mid.astype(F32)).astype(BF16)
    b = _dot(tri, g_hi) + _dot(tri, g_mid) + _dot(tri, g_lo)
    half = GLA_CHUNK // 2
    b_mid = jnp.where(first, b[half:half + 1, :], b[GLA_CHUNK + half:GLA_CHUNK + half + 1, :])
    qi = (q2 * jnp.exp(b - b_mid)).astype(BF16)
    ki = (k2 * jnp.exp(b_mid - b)).astype(BF16)
    qx = q2 * jnp.exp(b)
    zero = jnp.zeros((), BF16)
    kblk = jnp.concatenate([jnp.where(low, ki, zero), jnp.where(low, zero, ki)], axis=0)
    att = _dot_nt(qi, kblk)
    att = jnp.where(causal, att, 0.0).astype(BF16)
    zv = jnp.zeros((GLA_STEP, GLA_VAL_DIM), BF16)
    vblk = jnp.concatenate([jnp.concatenate([v2[:, :GLA_VAL_DIM], zv], axis=1),
                            jnp.concatenate([zv, v2[:, GLA_VAL_DIM:]], axis=1)], axis=0)
    o = _dot(att, vblk)
    b_t = b.T
    k_t = k2.T
    lane_t = lax.broadcasted_iota(jnp.int32, (GLA_PAIR_K, GLA_STEP), 1)
    first_t = lane_t < GLA_CHUNK
    bl_a = b_t[:, GLA_CHUNK - 1:GLA_CHUNK]
    bl_b = b_t[:, GLA_STEP - 1:GLA_STEP]
    kx_t = k_t * jnp.exp(jnp.where(first_t, bl_a, bl_b) - b_t)
    zf = jnp.zeros((), F32)
    kx_a = jnp.where(first_t, kx_t, zf).astype(BF16)
    kx_b = jnp.where(first_t, zf, kx_t).astype(BF16)
    srow = lax.broadcasted_iota(jnp.int32, (GLA_PAIR_K, GLA_PAIR_V), 0)
    slane = lax.broadcasted_iota(jnp.int32, (GLA_PAIR_K, GLA_PAIR_V), 1)
    diag = (srow < GLA_KEY_DIM) == (slane < GLA_VAL_DIM)
    s_a = jnp.exp(bl_a) * s_prev + jnp.where(diag, _dot(kx_a, v2), 0.0)
    s_b = jnp.exp(bl_b) * s_a + jnp.where(diag, _dot(kx_b, v2), 0.0)
    o = o + _dot(jnp.where(first, qx, zf).astype(BF16), s_prev.astype(BF16))
    o = o + _dot(jnp.where(first, zf, qx).astype(BF16), s_a.astype(BF16))
    return o, s_b


def _gla_init_body(k_ref, v_ref, gk_ref, s_out):
    causal, tri = _gla_consts()
    for p in range(GLA_HEADS // 2):
        k2 = k_ref[:, p * GLA_PAIR_K:(p + 1) * GLA_PAIR_K].astype(F32)
        v2 = v_ref[:, p * GLA_PAIR_V:(p + 1) * GLA_PAIR_V]
        gk2 = gk_ref[:, p * GLA_PAIR_K:(p + 1) * GLA_PAIR_K]
        s0 = jnp.zeros((GLA_PAIR_K, GLA_PAIR_V), F32)
        _, s = _gla_step(jnp.zeros_like(k2), k2, v2, gk2, s0, causal, tri)
        s_out[p] = s


def _gla_init_call(kg_m, vg_m, gk_m):
    return pl.pallas_call(
        _gla_init_body,
        out_shape=jax.ShapeDtypeStruct((GLA_HEADS // 2, GLA_PAIR_K, GLA_PAIR_V), F32),
        name="gla_init",
    )(kg_m, vg_m, gk_m)


def _gla_body(s0_ref, q_ref, k_ref, v_ref, gk_ref, rg_ref, nw_ref, o_ref, s_ref):
    t = pl.program_id(1)

    @pl.when(t == 0)
    def _():
        s_ref[...] = s0_ref[...]

    causal, tri = _gla_consts()
    nw = nw_ref[...]
    for p in range(GLA_HEADS // 2):
        ks = slice(p * GLA_PAIR_K, (p + 1) * GLA_PAIR_K)
        vs = slice(p * GLA_PAIR_V, (p + 1) * GLA_PAIR_V)
        s = s_ref[p]
        for u in range(GLA_GRID_STEPS):
            rows = slice(u * GLA_STEP, (u + 1) * GLA_STEP)
            o, s = _gla_step(q_ref[rows, ks].astype(F32), k_ref[rows, ks].astype(F32), v_ref[rows, vs],
                             gk_ref[rows, ks], s, causal, tri)
            outs = []
            for hh in range(2):
                oh = o[:, hh * GLA_VAL_DIM:(hh + 1) * GLA_VAL_DIM]
                ms = jnp.mean(oh * oh, axis=-1, keepdims=True)
                outs.append(oh * lax.rsqrt(ms + NORM_EPS) * nw)
            on = jnp.concatenate(outs, axis=1)
            o_ref[rows, vs] = (on * rg_ref[rows, vs].astype(F32)).astype(BF16)
        s_ref[p] = s


GLA_GRID_STEPS = 4


def _mixers_body(sink_ref, q_ref, kp_ref, kc_ref, km_ref, vp_ref, vc_ref, vm_ref,
                 s0_ref, qg_ref, kg_ref, vg_ref, gk_ref, rg_ref, nw_ref, ao_ref, go_ref, s_ref):
    _attn_body(sink_ref, q_ref, kp_ref, kc_ref, km_ref, vp_ref, vc_ref, vm_ref, ao_ref)
    _gla_body(s0_ref, qg_ref, kg_ref, vg_ref, gk_ref, rg_ref, nw_ref, go_ref, s_ref)


def _mixers_call(q, kd, vd, km, vm, sinks, s0, qg, kg, vg, gk, rg, nw, batch, seq):
    step = ATTN_STEP_BLOCKS * BLOCK
    assert seq % step == 0 and step == GLA_GRID_STEPS * GLA_STEP
    seq_spec = lambda width: pl.BlockSpec((None, step, width), lambda b, i, s: (b, i, 0))
    prev = pl.BlockSpec((None, BLOCK, ATTN_KV_DUP_W),
                        lambda b, i, s: (b, jnp.maximum(i * ATTN_STEP_BLOCKS - 1, 0), 0))
    const = lambda shape: pl.BlockSpec(shape, lambda b, i, s: (0,) * len(shape))
    meta = const((BLOCK, ATTN_KV_DUP_W))
    grid_spec = pltpu.PrefetchScalarGridSpec(
        num_scalar_prefetch=1,
        grid=(batch, seq // step),
        in_specs=[seq_spec(ATTN_Q_W), prev, seq_spec(ATTN_KV_DUP_W), meta,
                  prev, seq_spec(ATTN_KV_DUP_W), meta,
                  const((GLA_HEADS // 2, GLA_PAIR_K, GLA_PAIR_V)),
                  seq_spec(GLA_K_W), seq_spec(GLA_K_W), seq_spec(GLA_V_W), seq_spec(GLA_K_W),
                  seq_spec(GLA_V_W), const((1, GLA_VAL_DIM))],
        out_specs=[seq_spec(ATTN_Q_W), seq_spec(GLA_V_W)],
        scratch_shapes=[pltpu.VMEM((GLA_HEADS // 2, GLA_PAIR_K, GLA_PAIR_V), F32)],
    )
    return pl.pallas_call(
        _mixers_body,
        grid_spec=grid_spec,
        out_shape=[jax.ShapeDtypeStruct((batch, seq, ATTN_Q_W), BF16),
                   jax.ShapeDtypeStruct((batch, seq, GLA_V_W), BF16)],
        compiler_params=pltpu.CompilerParams(dimension_semantics=("arbitrary", "arbitrary"),
                                             vmem_limit_bytes=VMEM_LIMIT),
        name="mixers",
    )(sinks, q, kd, kd, km, vd, vd, vm, s0, qg, kg, vg, gk, rg, nw)


ROUTER_W = LANES


RT_E0, RT_E1, RT_W0, RT_W1 = 0, 1, 2, 3


MERGE_SPLIT = 4


def _merge_body(x_ref, ao_ref, go_ref, sga_ref, sgg_ref, wab_ref, wgb_ref, wo_ref, n2_ref,
                wr2_ref, h_out, xn_out, rt_out, rtt_out, cnt_out):
    @pl.when(pl.program_id(0) == 0)
    def _():
        cnt_out[...] = jnp.zeros_like(cnt_out)

    rows_per = x_ref.shape[0] // MERGE_SPLIT
    for part in range(MERGE_SPLIT):
        _merge_rows(slice(part * rows_per, (part + 1) * rows_per), x_ref, ao_ref, go_ref, sga_ref, sgg_ref,
                    wab_ref, wgb_ref, wo_ref, n2_ref, wr2_ref, h_out, xn_out, rt_out, rtt_out, cnt_out)


def _merge_rows(rows, x_ref, ao_ref, go_ref, sga_ref, sgg_ref, wab_ref, wgb_ref, wo_ref, n2_ref,
                wr2_ref, h_out, xn_out, rt_out, rtt_out, cnt_out):
    ya = _dot(ao_ref[rows, :], wab_ref[...])
    yg = _dot(go_ref[rows, :], wgb_ref[...])
    merged = sga_ref[rows, :].astype(F32) * ya + sgg_ref[rows, :].astype(F32) * yg
    h = x_ref[rows, :] + _dot(merged.astype(BF16), wo_ref[...])
    xn = _rms_norm(h, n2_ref[...])
    for c in range(ROW_TILES):
        h_out[c, rows, :] = h[:, c * LANES:(c + 1) * LANES]
        xn_out[c, rows, :] = xn[:, c * LANES:(c + 1) * LANES]

    xh, xl = _split_hi_lo(xn)
    wr2 = wr2_ref[...]
    lg2 = _dot(xh, wr2) + _dot(xl, wr2)
    lg = lg2[:, :ROUTER_W] + lg2[:, ROUTER_W:]
    lg_t = lg.T
    tm = lg.shape[0]
    neg = -jnp.inf
    big = jnp.int32(1 << 20)
    gl = lg_t[N_EXPERTS:N_EXPERTS + N_GROUPS, :]
    grow = lax.broadcasted_iota(jnp.int32, (N_GROUPS, tm), 0)
    gmax = jnp.max(gl, axis=0, keepdims=True)
    g_idx = jnp.min(jnp.where(gl == gmax, grow, big), axis=0, keepdims=True)
    g_w = 1.0 / jnp.sum(jnp.exp(gl - gmax), axis=0, keepdims=True)
    erow = lax.broadcasted_iota(jnp.int32, (N_EXPERTS, tm), 0)
    el = jnp.where((erow >> 3) == g_idx, lg_t[0:N_EXPERTS, :], neg)
    m1 = jnp.max(el, axis=0, keepdims=True)
    i1 = jnp.min(jnp.where(el == m1, erow, big), axis=0, keepdims=True)
    el2 = jnp.where(erow == i1, neg, el)
    m2 = jnp.max(el2, axis=0, keepdims=True)
    i2 = jnp.min(jnp.where(el2 == m2, erow, big), axis=0, keepdims=True)
    e2 = jnp.exp(m2 - m1)
    p1 = g_w / (1.0 + e2)
    p2 = g_w * e2 / (1.0 + e2)
    def record(rows):
        r = lax.broadcasted_iota(jnp.int32, (rows, tm), 0)
        return jnp.where(r == RT_E0, i1.astype(F32),
                         jnp.where(r == RT_E1, i2.astype(F32),
                                   jnp.where(r == RT_W0, p1, jnp.where(r == RT_W1, p2, 0.0))))
    rtt_out[:, rows] = record(8)
    rt_out[rows, :] = record(LANES).T
    picked = jnp.where(jnp.logical_or(erow == i1, erow == i2), 1.0, 0.0)
    cnt_out[...] += jnp.sum(picked, axis=1, keepdims=True)


def _merge_call(x2d, ao, go, sga, sgg, w, tm):
    n = x2d.shape[0]
    row = lambda width: pl.BlockSpec((tm, width), lambda i: (i, 0))
    return pl.pallas_call(
        _merge_body,
        grid=(n // tm,),
        in_specs=[row(D_MODEL), row(ATTN_Q_W), row(GLA_V_W), row(D_MODEL), row(D_MODEL),
                  _const_spec((ATTN_Q_W, D_MODEL)), _const_spec((GLA_V_W, D_MODEL)),
                  _const_spec((D_MODEL, D_MODEL)), _const_spec((1, D_MODEL)),
                  _const_spec((D_MODEL, 2 * ROUTER_W))],
        out_specs=[pl.BlockSpec((ROW_TILES, tm, LANES), lambda i: (0, i, 0)),
                   pl.BlockSpec((ROW_TILES, tm, LANES), lambda i: (0, i, 0)), row(ROUTER_W),
                   pl.BlockSpec((8, tm), lambda i: (0, i)), _const_spec((N_EXPERTS, LANES))],
        out_shape=[jax.ShapeDtypeStruct((ROW_TILES, n, LANES), F32),
                   jax.ShapeDtypeStruct((ROW_TILES, n, LANES), F32),
                   jax.ShapeDtypeStruct((n, ROUTER_W), F32),
                   jax.ShapeDtypeStruct((8, n), F32),
                   jax.ShapeDtypeStruct((N_EXPERTS, LANES), F32)],
        compiler_params=pltpu.CompilerParams(dimension_semantics=("arbitrary",),
                                             vmem_limit_bytes=VMEM_LIMIT),
        name="merge",
    )(x2d, ao, go, sga, sgg, w["wab"], w["wgb"], w["wo"], w["norm2"], w["wr2"])


EXPERT_TILE = 512
ROUTE_TILE = 512
VISIT_CAP = 256
VIS_EXPERT, VIS_TILE, VIS_LO, VIS_HI, VIS_COUNT = 0, 1, 2, 3, 4


def _num_visits(n_tokens):
    return (2 * n_tokens) // EXPERT_TILE + N_EXPERTS - 1


def _expert_cumsum_inclusive(x):
    row = lax.broadcasted_iota(jnp.int32, x.shape, 0)
    s = 1
    while s < N_EXPERTS:
        x = x + jnp.where(row >= s, pltpu.roll(x, s, 0), 0.0)
        s *= 2
    return x


def _route_body(rtt_ref, cnt_ref, pos_ref, vis_ref, carry_ref):
    i = pl.program_id(0)
    tm = rtt_ref.shape[1]
    cnt = cnt_ref[...]
    end = _expert_cumsum_inclusive(cnt)
    base = end - cnt

    @pl.when(i == 0)
    def _():
        carry_ref[...] = jnp.zeros_like(carry_ref)
        inv = 1.0 / EXPERT_TILE
        first = jnp.floor(base * inv)
        n_vis = jnp.where(cnt > 0.0, jnp.floor((end - 1.0) * inv) - first + 1.0, 0.0)
        vend = _expert_cumsum_inclusive(n_vis)
        vstart = vend - n_vis
        col = lambda a: a[:, 0:1]
        erow = lax.broadcasted_iota(jnp.int32, (N_EXPERTS, VISIT_CAP), 0).astype(F32)
        v = lax.broadcasted_iota(jnp.int32, (N_EXPERTS, VISIT_CAP), 1).astype(F32)
        ev = jnp.minimum(jnp.sum(jnp.where(col(vend) <= v, 1.0, 0.0), axis=0, keepdims=True),
                         N_EXPERTS - 1.0)
        mine = erow == ev
        pick = lambda a: jnp.sum(jnp.where(mine, a, 0.0), axis=0, keepdims=True)
        tile = pick(col(first) + v - col(vstart))
        lo = jnp.maximum(pick(col(base)) - tile * EXPERT_TILE, 0.0)
        hi = jnp.minimum(pick(col(end)) - tile * EXPERT_TILE, float(EXPERT_TILE))
        total = vend[N_EXPERTS - 1:N_EXPERTS, 0:1]
        r8 = lax.broadcasted_iota(jnp.int32, (8, VISIT_CAP), 0)
        rec = jnp.where(r8 == VIS_EXPERT, ev,
                        jnp.where(r8 == VIS_TILE, tile,
                                  jnp.where(r8 == VIS_LO, lo,
                                            jnp.where(r8 == VIS_HI, hi,
                                                      jnp.where(r8 == VIS_COUNT, total, 0.0)))))
        vis_ref[...] = rec.astype(jnp.int32)

    rtt = rtt_ref[...]
    erow = lax.broadcasted_iota(jnp.int32, (N_EXPERTS, tm), 0).astype(F32)
    oh0 = erow == rtt[RT_E0:RT_E0 + 1, :]
    oh1 = erow == rtt[RT_E1:RT_E1 + 1, :]
    picked = jnp.where(jnp.logical_or(oh0, oh1), 1.0, 0.0)
    r = lax.broadcasted_iota(jnp.int32, (tm, tm), 0)
    c = lax.broadcasted_iota(jnp.int32, (tm, tm), 1)
    earlier = jnp.where(r < c, 1.0, 0.0).astype(BF16)
    slot = _dot(picked.astype(BF16), earlier) + (carry_ref[:, 0:1] + base[:, 0:1])
    p0 = jnp.sum(jnp.where(oh0, slot, 0.0), axis=0, keepdims=True)
    p1 = jnp.sum(jnp.where(oh1, slot, 0.0), axis=0, keepdims=True)
    r8 = lax.broadcasted_iota(jnp.int32, (8, tm), 0)
    pos_ref[...] = jnp.where(r8 == 0, p0, jnp.where(r8 == 1, p1, 0.0)).astype(jnp.int32)
    carry_ref[...] += jnp.sum(picked, axis=1, keepdims=True)


def _route_call(rtt, cnt):
    n = rtt.shape[1]
    tm = _pick_tile(n, ROUTE_TILE)
    return pl.pallas_call(
        _route_body,
        grid=(n // tm,),
        in_specs=[pl.BlockSpec((8, tm), lambda i: (0, i)), _const_spec((N_EXPERTS, LANES))],
        out_specs=[pl.BlockSpec((8, tm), lambda i: (0, i)), _const_spec((8, VISIT_CAP))],
        out_shape=[jax.ShapeDtypeStruct((8, n), jnp.int32),
                   jax.ShapeDtypeStruct((8, VISIT_CAP), jnp.int32)],
        scratch_shapes=[pltpu.VMEM((N_EXPERTS, LANES), F32)],
        compiler_params=pltpu.CompilerParams(dimension_semantics=("arbitrary",)),
        name="route",
    )(rtt, cnt)


SCATTER_TILE = 512
ROW_TILES = D_MODEL // LANES


def _store_slabs(ref, val):
    for c in range(ROW_TILES):
        ref[c] = val[:, c * LANES:(c + 1) * LANES]


def _load_slabs(ref):
    return jnp.concatenate([ref[c] for c in range(ROW_TILES)], axis=1)


def _scatter_body(pos0_ref, pos1_ref, h_ref, hs_hbm, sem):
    tm = h_ref.shape[1]
    for r in range(tm):
        src = h_ref.at[:, r, :]
        pltpu.make_async_copy(src, hs_hbm.at[:, pos0_ref[r], :], sem).start(priority=0)
        pltpu.make_async_copy(src, hs_hbm.at[:, pos1_ref[r], :], sem).start(priority=1)
    for _ in range(2 * tm):
        pltpu.make_async_copy(h_ref.at[:, 0, :], hs_hbm.at[:, 0, :], sem).wait()


def _scatter_call(pos0, pos1, h_slabs, n_slots):
    n = h_slabs.shape[1]
    tm = _pick_tile(n, SCATTER_TILE)
    smem = lambda: pl.BlockSpec((tm,), lambda i: (i,), memory_space=pltpu.SMEM)
    return pl.pallas_call(
        _scatter_body,
        grid=(n // tm,),
        in_specs=[smem(), smem(), pl.BlockSpec((ROW_TILES, tm, LANES), lambda i: (0, i, 0))],
        out_specs=pl.BlockSpec(memory_space=pl.ANY),
        out_shape=jax.ShapeDtypeStruct((ROW_TILES, n_slots, LANES), F32),
        scratch_shapes=[pltpu.SemaphoreType.DMA],
        compiler_params=pltpu.CompilerParams(dimension_semantics=("arbitrary",)),
        name="scatter",
    )(pos0, pos1, h_slabs)


def _expert_body(vis_ref, x_ref, wg_ref, wu_ref, wd_ref, o_ref):
    j = pl.program_id(0)

    @pl.when(j < vis_ref[VIS_COUNT, 0])
    def _():
        lo_row = vis_ref[VIS_LO, j]
        hi_row = vis_ref[VIS_HI, j]
        x = _load_slabs(x_ref).astype(BF16)
        g = _dot(x, wg_ref[...].astype(BF16))
        u = _dot(x, wu_ref[...].astype(BF16))
        hg = (g * _sigmoid(g) * u).astype(BF16)
        new = _dot(hg, wd_ref[...].astype(BF16))
        row = lax.broadcasted_iota(jnp.int32, new.shape, 0)

        @pl.when(lo_row == 0)
        def _():
            _store_slabs(o_ref, jnp.where(row < hi_row, new, 0.0))

        @pl.when(lo_row > 0)
        def _():
            keep = jnp.logical_and(row >= lo_row, row < hi_row)
            _store_slabs(o_ref, jnp.where(keep, new, _load_slabs(o_ref)))


def _expert_call(vis, hs, wg, wu, wd, n_visits):
    def visit(j, vis):
        return jnp.minimum(j, vis[VIS_COUNT, 0] - 1)

    def xmap(j, vis):
        return (0, vis[VIS_TILE, visit(j, vis)], 0)

    def wmap(j, vis):
        return (vis[VIS_EXPERT, visit(j, vis)], 0, 0)

    grid_spec = pltpu.PrefetchScalarGridSpec(
        num_scalar_prefetch=1,
        grid=(n_visits,),
        in_specs=[pl.BlockSpec((ROW_TILES, EXPERT_TILE, LANES), xmap),
                  pl.BlockSpec((None, D_MODEL, EXPERT_FF), wmap),
                  pl.BlockSpec((None, D_MODEL, EXPERT_FF), wmap),
                  pl.BlockSpec((None, EXPERT_FF, D_MODEL), wmap)],
        out_specs=pl.BlockSpec((ROW_TILES, EXPERT_TILE, LANES), xmap),
    )
    return pl.pallas_call(
        _expert_body,
        grid_spec=grid_spec,
        out_shape=jax.ShapeDtypeStruct(hs.shape, F32),
        compiler_params=pltpu.CompilerParams(dimension_semantics=("arbitrary",),
                                             vmem_limit_bytes=VMEM_LIMIT),
        name="experts",
    )(vis, hs, wg, wu, wd)


COMBINE_TILE = 256


def _combine_body(p0_ref, p1_ref, p0n_ref, p1n_ref, h_ref, rt_ref, y_hbm, o_ref, g_ref, sem):
    i = pl.program_id(0)
    tm = h_ref.shape[1]
    cur = i % 2

    def issue(q0_ref, q1_ref, buf):
        for r in range(tm):
            pltpu.make_async_copy(y_hbm.at[:, q0_ref[r], :], g_ref.at[buf, 0, :, r, :],
                                  sem.at[buf]).start(priority=0)
            pltpu.make_async_copy(y_hbm.at[:, q1_ref[r], :], g_ref.at[buf, 1, :, r, :],
                                  sem.at[buf]).start(priority=1)

    @pl.when(i == 0)
    def _():
        issue(p0_ref, p1_ref, 0)

    @pl.when(i + 1 < pl.num_programs(0))
    def _():
        issue(p0n_ref, p1n_ref, 1 - cur)

    for _ in range(2 * tm):
        pltpu.make_async_copy(y_hbm.at[:, 0, :], g_ref.at[cur, 0, :, 0, :], sem.at[cur]).wait()
    rt = rt_ref[...]
    o_ref[...] = (_load_slabs(h_ref) + rt[:, RT_W0:RT_W0 + 1] * _load_slabs(g_ref.at[cur, 0])
                  + rt[:, RT_W1:RT_W1 + 1] * _load_slabs(g_ref.at[cur, 1]))


def _combine_call(pos0, pos1, h_slabs, rt, y):
    n = h_slabs.shape[1]
    tm = _pick_tile(n, COMBINE_TILE)
    steps = n // tm
    smem_cur = lambda: pl.BlockSpec((tm,), lambda i: (i,), memory_space=pltpu.SMEM)
    smem_next = lambda: pl.BlockSpec((tm,), lambda i: (jnp.minimum(i + 1, steps - 1),),
                                     memory_space=pltpu.SMEM)
    row = lambda width: pl.BlockSpec((tm, width), lambda i: (i, 0))
    return pl.pallas_call(
        _combine_body,
        grid=(steps,),
        in_specs=[smem_cur(), smem_cur(), smem_next(), smem_next(),
                  pl.BlockSpec((ROW_TILES, tm, LANES), lambda i: (0, i, 0)), row(ROUTER_W),
                  pl.BlockSpec(memory_space=pl.ANY)],
        out_specs=row(D_MODEL),
        out_shape=jax.ShapeDtypeStruct((n, D_MODEL), F32),
        scratch_shapes=[pltpu.VMEM((2, 2, ROW_TILES, tm, LANES), F32), pltpu.SemaphoreType.DMA((2,))],
        compiler_params=pltpu.CompilerParams(dimension_semantics=("arbitrary",),
                                             vmem_limit_bytes=VMEM_LIMIT),
        name="combine",
    )(pos0, pos1, pos0, pos1, h_slabs, rt, y)


def _rope_tables(pos):
    half = HEAD_DIM // 2
    inv_freq = ROPE_THETA ** (-(jnp.arange(half, dtype=F32) * 2.0) / HEAD_DIM)
    ang = pos.astype(F32)[:, None] * inv_freq[None, :]
    cos = jnp.cos(ang)
    sin = jnp.sin(ang)
    cos64 = jnp.concatenate([cos, cos], axis=1)
    sin64 = jnp.concatenate([-sin, sin], axis=1)
    return jnp.tile(cos64, (1, LANES // HEAD_DIM)), jnp.tile(sin64, (1, LANES // HEAD_DIM))


WPREP_ROWS = 128


def _wprep_body(w_ref, qkv_out, gla_out, gate_out):
    w = w_ref[...]
    lane = lax.broadcasted_iota(jnp.int32, (WPREP_ROWS, LANES), 1)
    qkv_out[:, :QKV_GD_OFF] = w[:, :QKV_GD_OFF].astype(BF16)
    gd_tile = w[:, _OFF_GD:_OFF_GD + LANES]
    qkv_out[:, QKV_GD_OFF:QKV_GD_OFF + LANES] = jnp.where(lane < GLA_GATE_RANK, gd_tile, 0.0).astype(BF16)
    qkv_out[:, QKV_GD_OFF + LANES:] = jnp.zeros((WPREP_ROWS, QKV_W - QKV_GD_OFF - LANES), BF16)
    gla_out[...] = w[:, _OFF_GQ:_OFF_GQ + GLA_W].astype(BF16)
    keep = LANES - GLA_GATE_RANK
    n_tiles = 2 * D_MODEL // LANES
    rolled = [pltpu.roll(w[:, _OFF_GD + t * LANES:_OFF_GD + (t + 1) * LANES], keep, 1)
              for t in range(n_tiles)]
    tail = w[:, _OFF_GD + n_tiles * LANES:]
    tail = jnp.concatenate([tail, jnp.zeros((WPREP_ROWS, LANES - tail.shape[1]), F32)], axis=1)
    rolled.append(pltpu.roll(tail, keep, 1))
    for t in range(n_tiles):
        gate_out[:, t * LANES:(t + 1) * LANES] = jnp.where(lane < keep, rolled[t], rolled[t + 1]).astype(BF16)


def _wprep_call(w_in):
    in_total = w_in.shape[1]
    assert in_total == _OFF_GATE + 2 * D_MODEL and D_MODEL % WPREP_ROWS == 0
    row = lambda width: pl.BlockSpec((WPREP_ROWS, width), lambda i: (i, 0))
    return pl.pallas_call(
        _wprep_body,
        grid=(D_MODEL // WPREP_ROWS,),
        in_specs=[row(in_total)],
        out_specs=[row(QKV_W), row(GLA_W), row(2 * D_MODEL)],
        out_shape=[jax.ShapeDtypeStruct((D_MODEL, QKV_W), BF16),
                   jax.ShapeDtypeStruct((D_MODEL, GLA_W), BF16),
                   jax.ShapeDtypeStruct((D_MODEL, 2 * D_MODEL), BF16)],
        compiler_params=pltpu.CompilerParams(dimension_semantics=("arbitrary",)),
        name="wprep",
    )(w_in)


def _prep_weights(norm1_w, w_in, q_norm_w, k_norm_w, gla_gate_up, gla_gate_bias, w_attn_branch,
                  w_gla_branch, w_out, norm2_w, router_group, router_expert):
    wqkv, wgla, wgate = _wprep_call(w_in)
    gup = jnp.zeros((LANES, GLA_K_W), BF16).at[:GLA_GATE_RANK].set(gla_gate_up.astype(BF16))
    wr = jnp.zeros((D_MODEL, ROUTER_W), F32)
    wr = wr.at[:, :N_EXPERTS].set(router_expert.transpose(1, 0, 2).reshape(D_MODEL, N_EXPERTS))
    wr = wr.at[:, N_EXPERTS:N_EXPERTS + N_GROUPS].set(router_group)
    wrh = wr.astype(BF16)
    wrl = (wr - wrh.astype(F32)).astype(BF16)
    return {
        "norm1": norm1_w.reshape(1, D_MODEL),
        "wqkv": wqkv,
        "wgla": wgla,
        "wgate": wgate,
        "qn": jnp.tile(q_norm_w, ATTN_HEADS).reshape(1, ATTN_Q_W),
        "kn": jnp.tile(k_norm_w, ATTN_KV_HEADS).reshape(1, ATTN_KV_W),
        "gup": gup,
        "gb": gla_gate_bias.reshape(1, GLA_K_W),
        "wab": w_attn_branch.astype(BF16),
        "wgb": w_gla_branch.astype(BF16),
        "wo": w_out.astype(BF16),
        "norm2": norm2_w.reshape(1, D_MODEL),
        "wr2": jnp.concatenate([wrh, wrl], axis=1),
    }


def _pick_tile(n, want):
    t = want
    while n % t:
        t //= 2
    return t


def kernel(x, meta_tokens, norm1_w, w_in, q_norm_w, k_norm_w, attn_sinks, gla_gate_up, gla_gate_bias,
           gla_norm_w, w_attn_branch, w_gla_branch, w_out, norm2_w, router_group, router_expert,
           expert_w_gate, expert_w_up, expert_w_down):
    batch, seq, d = x.shape
    assert d == D_MODEL and seq % GLA_STEP == 0 and norm1_w.shape[0] == 1
    n = batch * seq
    w = _prep_weights(norm1_w[0], w_in[0], q_norm_w[0], k_norm_w[0], gla_gate_up[0], gla_gate_bias[0],
                      w_attn_branch[0], w_gla_branch[0], w_out[0], norm2_w[0], router_group[0],
                      router_expert[0])
    x2d = x.reshape(n, D_MODEL)

    cos_m, sin_m = _rope_tables(jnp.arange(N_META))
    cos_r, sin_r = _rope_tables(jnp.arange(seq) + N_META)
    meta = _proj_call(meta_tokens.astype(F32), N_META, 1, w, cos_m, sin_m)
    tm = _pick_tile(seq, 512)
    (q, kd, vd, qg, kg, vg, rg, gk, sga, sgg) = _proj_call(x2d, tm, seq // tm, w, cos_r, sin_r)

    pad_after = ((0, BLOCK - N_META), (0, 0))
    km = jnp.pad(meta[1], pad_after)
    vm = jnp.pad(meta[2], pad_after)
    pad_before = ((GLA_STEP - N_META, 0), (0, 0))
    s0 = _gla_init_call(jnp.pad(meta[4], pad_before), jnp.pad(meta[5], pad_before),
                        jnp.pad(meta[7], pad_before))
    ao, go = _mixers_call(q.reshape(batch, seq, ATTN_Q_W), kd.reshape(batch, seq, ATTN_KV_DUP_W),
                          vd.reshape(batch, seq, ATTN_KV_DUP_W), km, vm, attn_sinks[0].astype(F32),
                          s0, qg.reshape(batch, seq, GLA_K_W), kg.reshape(batch, seq, GLA_K_W),
                          vg.reshape(batch, seq, GLA_V_W), gk.reshape(batch, seq, GLA_K_W),
                          rg.reshape(batch, seq, GLA_V_W), gla_norm_w[0].reshape(1, GLA_VAL_DIM),
                          batch, seq)

    h, xn2, rt, rtt, cnt = _merge_call(x2d, ao.reshape(n, ATTN_Q_W), go.reshape(n, GLA_V_W), sga, sgg, w,
                                       _pick_tile(n, 1024))

    assert (2 * n) % EXPERT_TILE == 0 and _num_visits(n) <= VISIT_CAP
    pos, vis = _route_call(rtt, cnt)
    pos0, pos1 = pos[0], pos[1]
    xs = _scatter_call(pos0, pos1, xn2, 2 * n)
    wg = expert_w_gate[0].reshape(N_EXPERTS, D_MODEL, EXPERT_FF)
    wu = expert_w_up[0].reshape(N_EXPERTS, D_MODEL, EXPERT_FF)
    wd = expert_w_down[0].reshape(N_EXPERTS, EXPERT_FF, D_MODEL)
    y = _expert_call(vis, xs, wg, wu, wd, _num_visits(n))
    out = _combine_call(pos0, pos1, h, rt, y)
    return out.reshape(batch, seq, D_MODEL)
```

```python
import functools
import math

import jax
import jax.numpy as jnp
from jax import lax
from jax.experimental import pallas as pl
from jax.experimental.pallas import tpu as pltpu

F32 = jnp.float32
BF16 = jnp.bfloat16

D_MODEL = 1024
N_META = 16
BLOCK = 128
ATTN_HEADS = 8
ATTN_KV_HEADS = 2
HEAD_DIM = 64
ROPE_THETA = 10000.0
ATTN_Q_W = ATTN_HEADS * HEAD_DIM
ATTN_KV_W = ATTN_KV_HEADS * HEAD_DIM
GLA_HEADS = 4
GLA_KEY_DIM = 64
GLA_VAL_DIM = 128
GLA_K_W = GLA_HEADS * GLA_KEY_DIM
GLA_V_W = GLA_HEADS * GLA_VAL_DIM
GLA_GATE_RANK = 16
GLA_GATE_NORM = 16.0
GLA_CHUNK = 64
N_GROUPS = 4
EXPERTS_PER_GROUP = 8
N_EXPERTS = N_GROUPS * EXPERTS_PER_GROUP
EXPERT_FF = 256
NORM_EPS = 1e-6
MASK_VALUE = -1e30

LANES = 128
ATTN_KV_DUP_W = 2 * ATTN_KV_W
VMEM_LIMIT = 56 * 1024 * 1024

_OFF_Q = 0
_OFF_K = _OFF_Q + ATTN_Q_W
_OFF_V = _OFF_K + ATTN_KV_W
_OFF_GQ = _OFF_V + ATTN_KV_W
_OFF_GD = _OFF_GQ + 2 * GLA_K_W + 2 * GLA_V_W
_OFF_GATE = _OFF_GD + GLA_GATE_RANK
GLA_W = 2 * GLA_K_W + 2 * GLA_V_W
QKV_GD_OFF = _OFF_GQ
QKV_W = D_MODEL


def _dot(a, b):
    return jnp.dot(a, b, preferred_element_type=F32)


def _dot_nt(a, b):
    return lax.dot_general(a, b, (((1,), (1,)), ((), ())), preferred_element_type=F32)


def _split_hi_lo(x):
    hi = x.astype(BF16)
    lo = (x - hi.astype(F32)).astype(BF16)
    return hi, lo


MXU_TILE = 256


def _group_mean_sq(x, group):
    w = x.shape[-1]
    slab = min(w, MXU_TILE)
    shift = int(math.log2(group))
    r = lax.broadcasted_iota(jnp.int32, (slab, slab), 0) >> shift
    c = lax.broadcasted_iota(jnp.int32, (slab, slab), 1) >> shift
    ones = jnp.where(r == c, 1.0, 0.0).astype(BF16)
    hi, lo = _split_hi_lo(x * x)
    sums = [_dot(hi[:, s:s + slab], ones) + _dot(lo[:, s:s + slab], ones) for s in range(0, w, slab)]
    return (sums[0] if len(sums) == 1 else jnp.concatenate(sums, axis=1)) * (1.0 / group)


def _rope(x, cos, sin_signed):
    w = x.shape[-1]
    reps = w // LANES
    if reps > 1:
        cos = jnp.concatenate([cos] * reps, axis=1)
        sin_signed = jnp.concatenate([sin_signed] * reps, axis=1)
    lane = lax.broadcasted_iota(jnp.int32, x.shape, 1)
    first_half = (lane & (HEAD_DIM // 2)) == 0
    swapped = jnp.where(first_half, pltpu.roll(x, w - HEAD_DIM // 2, 1), pltpu.roll(x, HEAD_DIM // 2, 1))
    return x * cos + swapped * sin_signed


def _rms_norm(x, w):
    ms = jnp.mean(x * x, axis=-1, keepdims=True)
    return x * lax.rsqrt(ms + NORM_EPS) * w


def _sigmoid(x):
    return 0.5 * jnp.tanh(0.5 * x) + 0.5


def _log_sigmoid(x):
    return jnp.minimum(x, 0.0) - jnp.log(1.0 + jnp.exp(-jnp.abs(x)))


def _dup_heads_on_lanes(x):
    low = lax.broadcasted_iota(jnp.int32, x.shape, 1) < HEAD_DIM
    swapped = pltpu.roll(x, HEAD_DIM, 1)
    return jnp.concatenate([jnp.where(low, x, swapped), jnp.where(low, swapped, x)], axis=1)


def _proj_body(x_ref, n1_ref, wqkv_ref, wgla_ref, wgate_ref,
               qn_ref, kn_ref, cos_ref, sin_ref, gup_ref, gb_ref,
               q_out, k_out, v_out, qg_out, kg_out, vg_out, rg_out, gk_out, sga_out, sgg_out):
    xn = _rms_norm(x_ref[...], n1_ref[...]).astype(BF16)
    cos = cos_ref[...]
    sin = sin_ref[...]

    a = _dot(xn, wqkv_ref[...])
    q = a[:, :ATTN_Q_W]
    q = q * lax.rsqrt(_group_mean_sq(q, HEAD_DIM) + NORM_EPS) * qn_ref[...]
    q_out[...] = (_rope(q, cos, sin) * (HEAD_DIM ** -0.5)).astype(BF16)

    k = a[:, _OFF_K:_OFF_K + ATTN_KV_W]
    k = k * lax.rsqrt(_group_mean_sq(k, HEAD_DIM) + NORM_EPS) * kn_ref[...]
    k_out[...] = _dup_heads_on_lanes(_rope(k, cos, sin)).astype(BF16)

    v_out[...] = _dup_heads_on_lanes(a[:, _OFF_V:_OFF_V + ATTN_KV_W]).astype(BF16)

    g = _dot(xn, wgla_ref[...])
    qg_out[...] = (g[:, :GLA_K_W] * (GLA_KEY_DIM ** -0.5)).astype(BF16)
    kg_out[...] = g[:, GLA_K_W:2 * GLA_K_W].astype(BF16)
    vg_out[...] = g[:, 2 * GLA_K_W:2 * GLA_K_W + GLA_V_W].astype(BF16)
    r = g[:, 2 * GLA_K_W + GLA_V_W:]
    rg_out[...] = (r * _sigmoid(r)).astype(BF16)

    gd = a[:, QKV_GD_OFF:QKV_GD_OFF + LANES].astype(BF16)
    z = _dot(gd, gup_ref[...]) + gb_ref[...]
    gk_out[...] = _log_sigmoid(z) * (1.0 / GLA_GATE_NORM)

    gates = _dot(xn, wgate_ref[...])
    sga_out[...] = _sigmoid(gates[:, :D_MODEL]).astype(BF16)
    sgg_out[...] = _sigmoid(gates[:, D_MODEL:]).astype(BF16)


def _const_spec(shape):
    nd = len(shape)
    return pl.BlockSpec(shape, lambda *_: (0,) * nd)


def _proj_call(x2d, tm, tiles_per_seq, w, cos, sin):
    n = x2d.shape[0]
    grid = (n // tm,)
    row = lambda width: pl.BlockSpec((tm, width), lambda i: (i, 0))
    tab = pl.BlockSpec((tm, LANES), lambda i: (i % tiles_per_seq, 0))
    in_specs = [
        row(D_MODEL), _const_spec((1, D_MODEL)),
        _const_spec((D_MODEL, QKV_W)), _const_spec((D_MODEL, GLA_W)),
        _const_spec((D_MODEL, 2 * D_MODEL)),
        _const_spec((1, ATTN_Q_W)), _const_spec((1, ATTN_KV_W)),
        tab, tab, _const_spec((LANES, GLA_K_W)), _const_spec((1, GLA_K_W)),
    ]
    out_widths = [ATTN_Q_W, ATTN_KV_DUP_W, ATTN_KV_DUP_W, GLA_K_W, GLA_K_W, GLA_V_W, GLA_V_W,
                  GLA_K_W, D_MODEL, D_MODEL]
    out_dtypes = [BF16] * 7 + [F32] + [BF16] * 2
    return pl.pallas_call(
        _proj_body,
        grid=grid,
        in_specs=in_specs,
        out_specs=[row(wd) for wd in out_widths],
        out_shape=[jax.ShapeDtypeStruct((n, wd), dt) for wd, dt in zip(out_widths, out_dtypes)],
        compiler_params=pltpu.CompilerParams(dimension_semantics=("arbitrary",),
                                             vmem_limit_bytes=VMEM_LIMIT),
        name="proj",
    )(x2d, w["norm1"], w["wqkv"], w["wgla"], w["wgate"],
      w["qn"], w["kn"], cos, sin, w["gup"], w["gb"])


ATTN_STEP_BLOCKS = 4


def _attn_body(sink_ref, q_ref, kp_ref, kc_ref, km_ref, vp_ref, vc_ref, vm_ref, o_ref):
    i = pl.program_id(1)
    nblk = ATTN_STEP_BLOCKS
    pair_rows = 2 * BLOCK
    lane = lax.broadcasted_iota(jnp.int32, (BLOCK, LANES), 1)
    low = lane < HEAD_DIM
    low2 = lax.broadcasted_iota(jnp.int32, (pair_rows, LANES), 1) < HEAD_DIM
    top2 = lax.broadcasted_iota(jnp.int32, (pair_rows, 1), 0) < BLOCK
    key = lax.broadcasted_iota(jnp.int32, (pair_rows, 2 * LANES), 1) & (LANES - 1)
    rowi = lax.broadcasted_iota(jnp.int32, (pair_rows, 2 * LANES), 0) & (BLOCK - 1)
    mask_meta = key < N_META
    mask_prev = key > rowi
    mask_cur = key <= rowi
    zero = jnp.zeros((), BF16)

    def block_diag(x):
        return jnp.concatenate([jnp.where(low, x, zero), jnp.where(low, zero, x)], axis=0)

    blk_row = lax.broadcasted_iota(jnp.int32, (2 * BLOCK, LANES), 0)
    blk_lane = lax.broadcasted_iota(jnp.int32, (2 * BLOCK, LANES), 1)
    ones_blk = jnp.where((blk_row < BLOCK) == (blk_lane < HEAD_DIM), 1.0, 0.0).astype(BF16)

    def masked(s, mask):
        return jnp.where(mask, s, MASK_VALUE)

    kv_heads = range(ATTN_KV_HEADS)
    s_meta_all, s_seq_all, v_seq_all = [], [], []
    for kvh in kv_heads:
        sl = slice(kvh * LANES, (kvh + 1) * LANES)
        k_seq = [block_diag(kp_ref[:, sl])]
        v_seq = [block_diag(vp_ref[:, sl])]
        for t in range(nblk):
            rows = slice(t * BLOCK, (t + 1) * BLOCK)
            k_seq.append(block_diag(kc_ref[rows, sl]))
            v_seq.append(block_diag(vc_ref[rows, sl]))
        q_all = jnp.concatenate(
            [q_ref[t * BLOCK:(t + 1) * BLOCK, (2 * kvh + pr) * LANES:(2 * kvh + pr + 1) * LANES]
             for t in range(nblk) for pr in range(2)], axis=0)
        s_meta_all.append(_dot_nt(q_all, block_diag(km_ref[:, sl])))
        s_seq = []
        for j in range(nblk + 1):
            lo_blk, hi_blk = max(j - 1, 0), min(j, nblk - 1)
            s_seq.append(_dot_nt(q_all[lo_blk * pair_rows:(hi_blk + 1) * pair_rows], k_seq[j]))
        s_seq_all.append(s_seq)
        v_seq_all.append(v_seq)

    probs_all = []
    for kvh in kv_heads:
        s_meta, s_seq = s_meta_all[kvh], s_seq_all[kvh]

        def sink_col(hh):
            return jnp.where(top2, sink_ref[4 * kvh + hh], sink_ref[4 * kvh + 2 + hh])

        p_meta, p_prev, p_cur, l_all = [], [], [], []
        for t in range(nblk):
            sm = masked(s_meta[t * pair_rows:(t + 1) * pair_rows], mask_meta)
            prev_rows = slice(0, pair_rows) if t == 0 else slice(pair_rows, 2 * pair_rows)
            sb = jnp.where(mask_prev, s_seq[t][prev_rows], s_seq[t + 1][0:pair_rows])
            if t == 0:
                sb = masked(sb, jnp.logical_or(mask_cur, i > 0))
            s_max = jnp.maximum(sm, sb)
            m2 = []
            sink_terms = []
            for hh in range(2):
                sink = sink_col(hh)
                m = jnp.maximum(jnp.max(s_max[:, hh * LANES:(hh + 1) * LANES], axis=1, keepdims=True), sink)
                m2.append(m)
                sink_terms.append(jnp.exp(sink - m))

            def probs(s):
                return jnp.concatenate([jnp.exp(s[:, hh * LANES:(hh + 1) * LANES] - m2[hh])
                                        for hh in range(2)], axis=1)

            pm, pb = probs(sm), probs(sb)
            pb16 = pb.astype(BF16)
            p_meta.append(pm.astype(BF16))
            p_prev.append(jnp.where(mask_prev, pb16, zero))
            p_cur.append(jnp.where(mask_prev, zero, pb16))
            l_all.append(_dot((pm + pb).astype(BF16), ones_blk)
                         + jnp.where(low2, sink_terms[0], sink_terms[1]))
        probs_all.append((p_meta, p_prev, p_cur, l_all))

    for kvh in kv_heads:
        sl = slice(kvh * LANES, (kvh + 1) * LANES)
        p_meta, p_prev, p_cur, l_all = probs_all[kvh]
        v_seq = v_seq_all[kvh]
        o_meta = _dot(jnp.concatenate(p_meta, axis=0), block_diag(vm_ref[:, sl]))
        o_seq = []
        for j in range(nblk + 1):
            parts = ([p_cur[j - 1]] if j >= 1 else []) + ([p_prev[j]] if j < nblk else [])
            o_seq.append(_dot(jnp.concatenate(parts, axis=0) if len(parts) > 1 else parts[0], v_seq[j]))
        for t in range(nblk):
            prev_rows = slice(0, pair_rows) if t == 0 else slice(pair_rows, 2 * pair_rows)
            o = (o_meta[t * pair_rows:(t + 1) * pair_rows] + o_seq[t][prev_rows]
                 + o_seq[t + 1][0:pair_rows]) / l_all[t]
            for pr in range(2):
                col = (2 * kvh + pr) * LANES
                o_ref[t * BLOCK:(t + 1) * BLOCK, col:col + LANES] = o[pr * BLOCK:(pr + 1) * BLOCK].astype(BF16)


GLA_STEP = 2 * GLA_CHUNK
GLA_PAIR_K = 2 * GLA_KEY_DIM
GLA_PAIR_V = 2 * GLA_VAL_DIM


def _gla_consts():
    r = lax.broadcasted_iota(jnp.int32, (GLA_STEP, 2 * GLA_STEP), 0)
    c = lax.broadcasted_iota(jnp.int32, (GLA_STEP, 2 * GLA_STEP), 1) & (GLA_STEP - 1)
    causal = jnp.logical_and((r >> 6) == (c >> 6), c <= r)
    tri = jnp.where(causal[:, :GLA_STEP], 1.0, 0.0).astype(BF16)
    return causal, tri


def _gla_steps(units, states, causal, tri):
    row = lax.broadcasted_iota(jnp.int32, (GLA_STEP, GLA_PAIR_K), 0)
    lane = lax.broadcasted_iota(jnp.int32, (GLA_STEP, GLA_PAIR_K), 1)
    first = row < GLA_CHUNK
    low = lane < GLA_KEY_DIM
    lane_t = lax.broadcasted_iota(jnp.int32, (GLA_PAIR_K, GLA_STEP), 1)
    first_t = lane_t < GLA_CHUNK
    srow = lax.broadcasted_iota(jnp.int32, (GLA_PAIR_K, GLA_PAIR_V), 0)
    slane = lax.broadcasted_iota(jnp.int32, (GLA_PAIR_K, GLA_PAIR_V), 1)
    diag = (srow < GLA_KEY_DIM) == (slane < GLA_VAL_DIM)
    zero = jnp.zeros((), BF16)
    zf = jnp.zeros((), F32)
    zv = jnp.zeros((GLA_STEP, GLA_VAL_DIM), BF16)
    half = GLA_CHUNK // 2
    n = len(units)

    gk = jnp.concatenate([u[4] for u in units], axis=1) if n > 1 else units[0][4]
    g_hi = gk.astype(BF16)
    g_r = gk - g_hi.astype(F32)
    g_mid = g_r.astype(BF16)
    g_lo = (g_r - g_mid.astype(F32)).astype(BF16)
    b_all = _dot(tri, g_hi) + _dot(tri, g_mid) + _dot(tri, g_lo)

    prep = []
    for i, (_, q2, k2, v2, _) in enumerate(units):
        b = b_all[:, i * GLA_PAIR_K:(i + 1) * GLA_PAIR_K]
        b_mid = jnp.where(first, b[half:half + 1, :], b[GLA_CHUNK + half:GLA_CHUNK + half + 1, :])
        qi = (q2 * jnp.exp(b - b_mid)).astype(BF16)
        ki = (k2 * jnp.exp(b_mid - b)).astype(BF16)
        qx = q2 * jnp.exp(b)
        kblk = jnp.concatenate([jnp.where(low, ki, zero), jnp.where(low, zero, ki)], axis=0)
        vblk = jnp.concatenate([jnp.concatenate([v2[:, :GLA_VAL_DIM], zv], axis=1),
                                jnp.concatenate([zv, v2[:, GLA_VAL_DIM:]], axis=1)], axis=0)
        b_t = b.T
        k_t = k2.T
        bl_a = b_t[:, GLA_CHUNK - 1:GLA_CHUNK]
        bl_b = b_t[:, GLA_STEP - 1:GLA_STEP]
        kx_t = k_t * jnp.exp(jnp.where(first_t, bl_a, bl_b) - b_t)
        prep.append(dict(qi=qi, kblk=kblk, vblk=vblk,
                         qx_a=jnp.where(first, qx, zf).astype(BF16), qx_b=jnp.where(first, zf, qx).astype(BF16),
                         kx_a=jnp.where(first_t, kx_t, zf).astype(BF16),
                         kx_b=jnp.where(first_t, zf, kx_t).astype(BF16),
                         dec_a=jnp.exp(bl_a), dec_b=jnp.exp(bl_b)))

    atts = [jnp.where(causal, _dot_nt(p["qi"], p["kblk"]), 0.0).astype(BF16) for p in prep]
    kv_a = [jnp.where(diag, _dot(p["kx_a"], u[3]), 0.0) for p, u in zip(prep, units)]
    kv_b = [jnp.where(diag, _dot(p["kx_b"], u[3]), 0.0) for p, u in zip(prep, units)]
    outs = [_dot(a, p["vblk"]) for a, p in zip(atts, prep)]

    states = dict(states)
    for i, (pair, *_rest) in enumerate(units):
        p = prep[i]
        s_prev = states[pair]
        s_a = p["dec_a"] * s_prev + kv_a[i]
        outs[i] = outs[i] + _dot(p["qx_a"], s_prev.astype(BF16)) + _dot(p["qx_b"], s_a.astype(BF16))
        states[pair] = p["dec_b"] * s_a + kv_b[i]
    return outs, states


def _gla_init_body(k_ref, v_ref, gk_ref, s_out):
    causal, tri = _gla_consts()
    units = []
    for p in range(GLA_HEADS // 2):
        k2 = k_ref[:, p * GLA_PAIR_K:(p + 1) * GLA_PAIR_K].astype(F32)
        units.append((p, jnp.zeros_like(k2), k2, v_ref[:, p * GLA_PAIR_V:(p + 1) * GLA_PAIR_V],
                      gk_ref[:, p * GLA_PAIR_K:(p + 1) * GLA_PAIR_K]))
    zeros = {p: jnp.zeros((GLA_PAIR_K, GLA_PAIR_V), F32) for p in range(GLA_HEADS // 2)}
    _, states = _gla_steps(units, zeros, causal, tri)
    for p in range(GLA_HEADS // 2):
        s_out[p] = states[p]


def _gla_init_call(kg_m, vg_m, gk_m):
    return pl.pallas_call(
        _gla_init_body,
        out_shape=jax.ShapeDtypeStruct((GLA_HEADS // 2, GLA_PAIR_K, GLA_PAIR_V), F32),
        name="gla_init",
    )(kg_m, vg_m, gk_m)


def _gla_body(s0_ref, q_ref, k_ref, v_ref, gk_ref, rg_ref, nw_ref, o_ref, s_ref):
    t = pl.program_id(1)

    @pl.when(t == 0)
    def _():
        s_ref[...] = s0_ref[...]

    causal, tri = _gla_consts()
    nw = nw_ref[...]
    pairs = range(GLA_HEADS // 2)
    units = []
    where = []
    for u in range(GLA_GRID_STEPS):
        rows = slice(u * GLA_STEP, (u + 1) * GLA_STEP)
        for p in pairs:
            ks = slice(p * GLA_PAIR_K, (p + 1) * GLA_PAIR_K)
            vs = slice(p * GLA_PAIR_V, (p + 1) * GLA_PAIR_V)
            units.append((p, q_ref[rows, ks].astype(F32), k_ref[rows, ks].astype(F32), v_ref[rows, vs],
                          gk_ref[rows, ks]))
            where.append((rows, vs))
    outs, states = _gla_steps(units, {p: s_ref[p] for p in pairs}, causal, tri)
    for p in pairs:
        s_ref[p] = states[p]
    for o, (rows, vs) in zip(outs, where):
        heads = []
        for hh in range(2):
            oh = o[:, hh * GLA_VAL_DIM:(hh + 1) * GLA_VAL_DIM]
            ms = jnp.mean(oh * oh, axis=-1, keepdims=True)
            heads.append(oh * lax.rsqrt(ms + NORM_EPS) * nw)
        on = jnp.concatenate(heads, axis=1)
        o_ref[rows, vs] = (on * rg_ref[rows, vs].astype(F32)).astype(BF16)


GLA_GRID_STEPS = 4


def _mixers_body(sink_ref, q_ref, kp_ref, kc_ref, km_ref, vp_ref, vc_ref, vm_ref,
                 s0_ref, qg_ref, kg_ref, vg_ref, gk_ref, rg_ref, nw_ref, ao_ref, go_ref, s_ref):
    _gla_body(s0_ref, qg_ref, kg_ref, vg_ref, gk_ref, rg_ref, nw_ref, go_ref, s_ref)
    _attn_body(sink_ref, q_ref, kp_ref, kc_ref, km_ref, vp_ref, vc_ref, vm_ref, ao_ref)


def _mixers_call(q, kd, vd, km, vm, sinks, s0, qg, kg, vg, gk, rg, nw, batch, seq):
    step = ATTN_STEP_BLOCKS * BLOCK
    assert seq % step == 0 and step == GLA_GRID_STEPS * GLA_STEP
    seq_spec = lambda width: pl.BlockSpec((None, step, width), lambda b, i, s: (b, i, 0))
    prev = pl.BlockSpec((None, BLOCK, ATTN_KV_DUP_W),
                        lambda b, i, s: (b, jnp.maximum(i * ATTN_STEP_BLOCKS - 1, 0), 0))
    const = lambda shape: pl.BlockSpec(shape, lambda b, i, s: (0,) * len(shape))
    meta = const((BLOCK, ATTN_KV_DUP_W))
    grid_spec = pltpu.PrefetchScalarGridSpec(
        num_scalar_prefetch=1,
        grid=(batch, seq // step),
        in_specs=[seq_spec(ATTN_Q_W), prev, seq_spec(ATTN_KV_DUP_W), meta,
                  prev, seq_spec(ATTN_KV_DUP_W), meta,
                  const((GLA_HEADS // 2, GLA_PAIR_K, GLA_PAIR_V)),
                  seq_spec(GLA_K_W), seq_spec(GLA_K_W), seq_spec(GLA_V_W), seq_spec(GLA_K_W),
                  seq_spec(GLA_V_W), const((1, GLA_VAL_DIM))],
        out_specs=[seq_spec(ATTN_Q_W), seq_spec(GLA_V_W)],
        scratch_shapes=[pltpu.VMEM((GLA_HEADS // 2, GLA_PAIR_K, GLA_PAIR_V), F32)],
    )
    return pl.pallas_call(
        _mixers_body,
        grid_spec=grid_spec,
        out_shape=[jax.ShapeDtypeStruct((batch, seq, ATTN_Q_W), BF16),
                   jax.ShapeDtypeStruct((batch, seq, GLA_V_W), BF16)],
        compiler_params=pltpu.CompilerParams(dimension_semantics=("arbitrary", "arbitrary"),
                                             vmem_limit_bytes=VMEM_LIMIT),
        name="mixers",
    )(sinks, q, kd, kd, km, vd, vd, vm, s0, qg, kg, vg, gk, rg, nw)


ROUTER_W = LANES


RT_E0, RT_E1, RT_W0, RT_W1 = 0, 1, 2, 3


MERGE_SPLIT = 4


def _merge_body(x_ref, ao_ref, go_ref, sga_ref, sgg_ref, wab_ref, wgb_ref, wo_ref, n2_ref,
                wr2_ref, h_out, xn_out, rt_out, rtt_out, cnt_out):
    @pl.when(pl.program_id(0) == 0)
    def _():
        cnt_out[...] = jnp.zeros_like(cnt_out)

    rows_per = x_ref.shape[0] // MERGE_SPLIT
    for part in range(MERGE_SPLIT):
        _merge_rows(slice(part * rows_per, (part + 1) * rows_per), x_ref, ao_ref, go_ref, sga_ref, sgg_ref,
                    wab_ref, wgb_ref, wo_ref, n2_ref, wr2_ref, h_out, xn_out, rt_out, rtt_out, cnt_out)


def _merge_rows(rows, x_ref, ao_ref, go_ref, sga_ref, sgg_ref, wab_ref, wgb_ref, wo_ref, n2_ref,
                wr2_ref, h_out, xn_out, rt_out, rtt_out, cnt_out):
    ya = _dot(ao_ref[rows, :], wab_ref[...])
    yg = _dot(go_ref[rows, :], wgb_ref[...])
    merged = sga_ref[rows, :].astype(F32) * ya + sgg_ref[rows, :].astype(F32) * yg
    h = x_ref[rows, :] + _dot(merged.astype(BF16), wo_ref[...])
    xn = _rms_norm(h, n2_ref[...])
    for c in range(ROW_TILES):
        h_out[c, rows, :] = h[:, c * LANES:(c + 1) * LANES]
        xn_out[c, rows, :] = xn[:, c * LANES:(c + 1) * LANES]

    xh, xl = _split_hi_lo(xn)
    wr2 = wr2_ref[...]
    lg2 = _dot(xh, wr2) + _dot(xl, wr2)
    lg = lg2[:, :ROUTER_W] + lg2[:, ROUTER_W:]
    lg_t = lg.T
    tm = lg.shape[0]
    neg = -jnp.inf
    big = jnp.int32(1 << 20)
    gl = lg_t[N_EXPERTS:N_EXPERTS + N_GROUPS, :]
    grow = lax.broadcasted_iota(jnp.int32, (N_GROUPS, tm), 0)
    gmax = jnp.max(gl, axis=0, keepdims=True)
    g_idx = jnp.min(jnp.where(gl == gmax, grow, big), axis=0, keepdims=True)
    g_w = 1.0 / jnp.sum(jnp.exp(gl - gmax), axis=0, keepdims=True)
    erow = lax.broadcasted_iota(jnp.int32, (N_EXPERTS, tm), 0)
    el = jnp.where((erow >> 3) == g_idx, lg_t[0:N_EXPERTS, :], neg)
    m1 = jnp.max(el, axis=0, keepdims=True)
    i1 = jnp.min(jnp.where(el == m1, erow, big), axis=0, keepdims=True)
    el2 = jnp.where(erow == i1, neg, el)
    m2 = jnp.max(el2, axis=0, keepdims=True)
    i2 = jnp.min(jnp.where(el2 == m2, erow, big), axis=0, keepdims=True)
    e2 = jnp.exp(m2 - m1)
    p1 = g_w / (1.0 + e2)
    p2 = g_w * e2 / (1.0 + e2)
    def record(rows):
        r = lax.broadcasted_iota(jnp.int32, (rows, tm), 0)
        return jnp.where(r == RT_E0, i1.astype(F32),
                         jnp.where(r == RT_E1, i2.astype(F32),
                                   jnp.where(r == RT_W0, p1, jnp.where(r == RT_W1, p2, 0.0))))
    rtt_out[:, rows] = record(8)
    rt_out[rows, :] = record(LANES).T
    picked = jnp.where(jnp.logical_or(erow == i1, erow == i2), 1.0, 0.0)
    cnt_out[...] += jnp.sum(picked, axis=1, keepdims=True)


def _merge_call(x2d, ao, go, sga, sgg, w, tm):
    n = x2d.shape[0]
    row = lambda width: pl.BlockSpec((tm, width), lambda i: (i, 0))
    return pl.pallas_call(
        _merge_body,
        grid=(n // tm,),
        in_specs=[row(D_MODEL), row(ATTN_Q_W), row(GLA_V_W), row(D_MODEL), row(D_MODEL),
                  _const_spec((ATTN_Q_W, D_MODEL)), _const_spec((GLA_V_W, D_MODEL)),
                  _const_spec((D_MODEL, D_MODEL)), _const_spec((1, D_MODEL)),
                  _const_spec((D_MODEL, 2 * ROUTER_W))],
        out_specs=[pl.BlockSpec((ROW_TILES, tm, LANES), lambda i: (0, i, 0)),
                   pl.BlockSpec((ROW_TILES, tm, LANES), lambda i: (0, i, 0)), row(ROUTER_W),
                   pl.BlockSpec((8, tm), lambda i: (0, i)), _const_spec((N_EXPERTS, LANES))],
        out_shape=[jax.ShapeDtypeStruct((ROW_TILES, n, LANES), F32),
                   jax.ShapeDtypeStruct((ROW_TILES, n, LANES), F32),
                   jax.ShapeDtypeStruct((n, ROUTER_W), F32),
                   jax.ShapeDtypeStruct((8, n), F32),
                   jax.ShapeDtypeStruct((N_EXPERTS, LANES), F32)],
        compiler_params=pltpu.CompilerParams(dimension_semantics=("arbitrary",),
                                             vmem_limit_bytes=VMEM_LIMIT),
        name="merge",
    )(x2d, ao, go, sga, sgg, w["wab"], w["wgb"], w["wo"], w["norm2"], w["wr2"])


EXPERT_TILE = 512
ROUTE_TILE = 512
VISIT_CAP = 256
VIS_EXPERT, VIS_TILE, VIS_LO, VIS_HI, VIS_COUNT = 0, 1, 2, 3, 4


def _num_visits(n_tokens):
    return (2 * n_tokens) // EXPERT_TILE + N_EXPERTS - 1


def _expert_cumsum_inclusive(x):
    row = lax.broadcasted_iota(jnp.int32, x.shape, 0)
    s = 1
    while s < N_EXPERTS:
        x = x + jnp.where(row >= s, pltpu.roll(x, s, 0), 0.0)
        s *= 2
    return x


def _route_body(rtt_ref, cnt_ref, pos_ref, vis_ref, carry_ref):
    i = pl.program_id(0)
    tm = rtt_ref.shape[1]
    cnt = cnt_ref[...]
    end = _expert_cumsum_inclusive(cnt)
    base = end - cnt

    @pl.when(i == 0)
    def _():
        carry_ref[...] = jnp.zeros_like(carry_ref)
        inv = 1.0 / EXPERT_TILE
        first = jnp.floor(base * inv)
        n_vis = jnp.where(cnt > 0.0, jnp.floor((end - 1.0) * inv) - first + 1.0, 0.0)
        vend = _expert_cumsum_inclusive(n_vis)
        vstart = vend - n_vis
        col = lambda a: a[:, 0:1]
        erow = lax.broadcasted_iota(jnp.int32, (N_EXPERTS, VISIT_CAP), 0).astype(F32)
        v = lax.broadcasted_iota(jnp.int32, (N_EXPERTS, VISIT_CAP), 1).astype(F32)
        ev = jnp.minimum(jnp.sum(jnp.where(col(vend) <= v, 1.0, 0.0), axis=0, keepdims=True),
                         N_EXPERTS - 1.0)
        mine = erow == ev
        pick = lambda a: jnp.sum(jnp.where(mine, a, 0.0), axis=0, keepdims=True)
        tile = pick(col(first) + v - col(vstart))
        lo = jnp.maximum(pick(col(base)) - tile * EXPERT_TILE, 0.0)
        hi = jnp.minimum(pick(col(end)) - tile * EXPERT_TILE, float(EXPERT_TILE))
        total = vend[N_EXPERTS - 1:N_EXPERTS, 0:1]
        r8 = lax.broadcasted_iota(jnp.int32, (8, VISIT_CAP), 0)
        rec = jnp.where(r8 == VIS_EXPERT, ev,
                        jnp.where(r8 == VIS_TILE, tile,
                                  jnp.where(r8 == VIS_LO, lo,
                                            jnp.where(r8 == VIS_HI, hi,
                                                      jnp.where(r8 == VIS_COUNT, total, 0.0)))))
        vis_ref[...] = rec.astype(jnp.int32)

    rtt = rtt_ref[...]
    erow = lax.broadcasted_iota(jnp.int32, (N_EXPERTS, tm), 0).astype(F32)
    oh0 = erow == rtt[RT_E0:RT_E0 + 1, :]
    oh1 = erow == rtt[RT_E1:RT_E1 + 1, :]
    picked = jnp.where(jnp.logical_or(oh0, oh1), 1.0, 0.0)
    r = lax.broadcasted_iota(jnp.int32, (tm, tm), 0)
    c = lax.broadcasted_iota(jnp.int32, (tm, tm), 1)
    earlier = jnp.where(r < c, 1.0, 0.0).astype(BF16)
    slot = _dot(picked.astype(BF16), earlier) + (carry_ref[:, 0:1] + base[:, 0:1])
    p0 = jnp.sum(jnp.where(oh0, slot, 0.0), axis=0, keepdims=True)
    p1 = jnp.sum(jnp.where(oh1, slot, 0.0), axis=0, keepdims=True)
    r8 = lax.broadcasted_iota(jnp.int32, (8, tm), 0)
    pos_ref[...] = jnp.where(r8 == 0, p0, jnp.where(r8 == 1, p1, 0.0)).astype(jnp.int32)
    carry_ref[...] += jnp.sum(picked, axis=1, keepdims=True)


def _route_call(rtt, cnt):
    n = rtt.shape[1]
    tm = _pick_tile(n, ROUTE_TILE)
    return pl.pallas_call(
        _route_body,
        grid=(n // tm,),
        in_specs=[pl.BlockSpec((8, tm), lambda i: (0, i)), _const_spec((N_EXPERTS, LANES))],
        out_specs=[pl.BlockSpec((8, tm), lambda i: (0, i)), _const_spec((8, VISIT_CAP))],
        out_shape=[jax.ShapeDtypeStruct((8, n), jnp.int32),
                   jax.ShapeDtypeStruct((8, VISIT_CAP), jnp.int32)],
        scratch_shapes=[pltpu.VMEM((N_EXPERTS, LANES), F32)],
        compiler_params=pltpu.CompilerParams(dimension_semantics=("arbitrary",)),
        name="route",
    )(rtt, cnt)


SCATTER_TILE = 512
ROW_TILES = D_MODEL // LANES


def _store_slabs(ref, val):
    for c in range(ROW_TILES):
        ref[c] = val[:, c * LANES:(c + 1) * LANES]


def _load_slabs(ref):
    return jnp.concatenate([ref[c] for c in range(ROW_TILES)], axis=1)


def _scatter_body(pos0_ref, pos1_ref, h_ref, hs_hbm, sem):
    tm = h_ref.shape[1]
    for r in range(tm):
        src = h_ref.at[:, r, :]
        pltpu.make_async_copy(src, hs_hbm.at[:, pos0_ref[r], :], sem).start(priority=0)
        pltpu.make_async_copy(src, hs_hbm.at[:, pos1_ref[r], :], sem).start(priority=1)
    for _ in range(2 * tm):
        pltpu.make_async_copy(h_ref.at[:, 0, :], hs_hbm.at[:, 0, :], sem).wait()


def _scatter_call(pos0, pos1, h_slabs, n_slots):
    n = h_slabs.shape[1]
    tm = _pick_tile(n, SCATTER_TILE)
    smem = lambda: pl.BlockSpec((tm,), lambda i: (i,), memory_space=pltpu.SMEM)
    return pl.pallas_call(
        _scatter_body,
        grid=(n // tm,),
        in_specs=[smem(), smem(), pl.BlockSpec((ROW_TILES, tm, LANES), lambda i: (0, i, 0))],
        out_specs=pl.BlockSpec(memory_space=pl.ANY),
        out_shape=jax.ShapeDtypeStruct((ROW_TILES, n_slots, LANES), F32),
        scratch_shapes=[pltpu.SemaphoreType.DMA],
        compiler_params=pltpu.CompilerParams(dimension_semantics=("arbitrary",)),
        name="scatter",
    )(pos0, pos1, h_slabs)


def _expert_body(vis_ref, x_ref, wg_ref, wu_ref, wd_ref, o_ref):
    j = pl.program_id(0)

    @pl.when(j < vis_ref[VIS_COUNT, 0])
    def _():
        lo_row = vis_ref[VIS_LO, j]
        hi_row = vis_ref[VIS_HI, j]
        x = _load_slabs(x_ref).astype(BF16)
        g = _dot(x, wg_ref[...].astype(BF16))
        u = _dot(x, wu_ref[...].astype(BF16))
        hg = (g * _sigmoid(g) * u).astype(BF16)
        new = _dot(hg, wd_ref[...].astype(BF16))
        row = lax.broadcasted_iota(jnp.int32, new.shape, 0)

        @pl.when(lo_row == 0)
        def _():
            _store_slabs(o_ref, jnp.where(row < hi_row, new, 0.0))

        @pl.when(lo_row > 0)
        def _():
            keep = jnp.logical_and(row >= lo_row, row < hi_row)
            _store_slabs(o_ref, jnp.where(keep, new, _load_slabs(o_ref)))


def _expert_call(vis, hs, wg, wu, wd, n_visits):
    def visit(j, vis):
        return jnp.minimum(j, vis[VIS_COUNT, 0] - 1)

    def xmap(j, vis):
        return (0, vis[VIS_TILE, visit(j, vis)], 0)

    def wmap(j, vis):
        return (vis[VIS_EXPERT, visit(j, vis)], 0, 0)

    grid_spec = pltpu.PrefetchScalarGridSpec(
        num_scalar_prefetch=1,
        grid=(n_visits,),
        in_specs=[pl.BlockSpec((ROW_TILES, EXPERT_TILE, LANES), xmap),
                  pl.BlockSpec((None, D_MODEL, EXPERT_FF), wmap),
                  pl.BlockSpec((None, D_MODEL, EXPERT_FF), wmap),
                  pl.BlockSpec((None, EXPERT_FF, D_MODEL), wmap)],
        out_specs=pl.BlockSpec((ROW_TILES, EXPERT_TILE, LANES), xmap),
    )
    return pl.pallas_call(
        _expert_body,
        grid_spec=grid_spec,
        out_shape=jax.ShapeDtypeStruct(hs.shape, F32),
        compiler_params=pltpu.CompilerParams(dimension_semantics=("arbitrary",),
                                             vmem_limit_bytes=VMEM_LIMIT),
        name="experts",
    )(vis, hs, wg, wu, wd)


COMBINE_TILE = 256


def _combine_body(p0_ref, p1_ref, p0n_ref, p1n_ref, h_ref, rt_ref, y_hbm, o_ref, g_ref, sem):
    i = pl.program_id(0)
    tm = h_ref.shape[1]
    cur = i % 2

    def issue(q0_ref, q1_ref, buf):
        for r in range(tm):
            pltpu.make_async_copy(y_hbm.at[:, q0_ref[r], :], g_ref.at[buf, 0, :, r, :],
                                  sem.at[buf]).start(priority=0)
            pltpu.make_async_copy(y_hbm.at[:, q1_ref[r], :], g_ref.at[buf, 1, :, r, :],
                                  sem.at[buf]).start(priority=1)

    @pl.when(i == 0)
    def _():
        issue(p0_ref, p1_ref, 0)

    @pl.when(i + 1 < pl.num_programs(0))
    def _():
        issue(p0n_ref, p1n_ref, 1 - cur)

    for _ in range(2 * tm):
        pltpu.make_async_copy(y_hbm.at[:, 0, :], g_ref.at[cur, 0, :, 0, :], sem.at[cur]).wait()
    rt = rt_ref[...]
    o_ref[...] = (_load_slabs(h_ref) + rt[:, RT_W0:RT_W0 + 1] * _load_slabs(g_ref.at[cur, 0])
                  + rt[:, RT_W1:RT_W1 + 1] * _load_slabs(g_ref.at[cur, 1]))


def _combine_call(pos0, pos1, h_slabs, rt, y):
    n = h_slabs.shape[1]
    tm = _pick_tile(n, COMBINE_TILE)
    steps = n // tm
    smem_cur = lambda: pl.BlockSpec((tm,), lambda i: (i,), memory_space=pltpu.SMEM)
    smem_next = lambda: pl.BlockSpec((tm,), lambda i: (jnp.minimum(i + 1, steps - 1),),
                                     memory_space=pltpu.SMEM)
    row = lambda width: pl.BlockSpec((tm, width), lambda i: (i, 0))
    return pl.pallas_call(
        _combine_body,
        grid=(steps,),
        in_specs=[smem_cur(), smem_cur(), smem_next(), smem_next(),
                  pl.BlockSpec((ROW_TILES, tm, LANES), lambda i: (0, i, 0)), row(ROUTER_W),
                  pl.BlockSpec(memory_space=pl.ANY)],
        out_specs=row(D_MODEL),
        out_shape=jax.ShapeDtypeStruct((n, D_MODEL), F32),
        scratch_shapes=[pltpu.VMEM((2, 2, ROW_TILES, tm, LANES), F32), pltpu.SemaphoreType.DMA((2,))],
        compiler_params=pltpu.CompilerParams(dimension_semantics=("arbitrary",),
                                             vmem_limit_bytes=VMEM_LIMIT),
        name="combine",
    )(pos0, pos1, pos0, pos1, h_slabs, rt, y)


def _rope_tables(pos):
    half = HEAD_DIM // 2
    inv_freq = ROPE_THETA ** (-(jnp.arange(half, dtype=F32) * 2.0) / HEAD_DIM)
    ang = pos.astype(F32)[:, None] * inv_freq[None, :]
    cos = jnp.cos(ang)
    sin = jnp.sin(ang)
    cos64 = jnp.concatenate([cos, cos], axis=1)
    sin64 = jnp.concatenate([-sin, sin], axis=1)
    return jnp.tile(cos64, (1, LANES // HEAD_DIM)), jnp.tile(sin64, (1, LANES // HEAD_DIM))


WPREP_ROWS = 128


def _wprep_body(w_ref, qkv_out, gla_out, gate_out):
    w = w_ref[...]
    lane = lax.broadcasted_iota(jnp.int32, (WPREP_ROWS, LANES), 1)
    qkv_out[:, :QKV_GD_OFF] = w[:, :QKV_GD_OFF].astype(BF16)
    gd_tile = w[:, _OFF_GD:_OFF_GD + LANES]
    qkv_out[:, QKV_GD_OFF:QKV_GD_OFF + LANES] = jnp.where(lane < GLA_GATE_RANK, gd_tile, 0.0).astype(BF16)
    qkv_out[:, QKV_GD_OFF + LANES:] = jnp.zeros((WPREP_ROWS, QKV_W - QKV_GD_OFF - LANES), BF16)
    gla_out[...] = w[:, _OFF_GQ:_OFF_GQ + GLA_W].astype(BF16)
    keep = LANES - GLA_GATE_RANK
    n_tiles = 2 * D_MODEL // LANES
    rolled = [pltpu.roll(w[:, _OFF_GD + t * LANES:_OFF_GD + (t + 1) * LANES], keep, 1)
              for t in range(n_tiles)]
    tail = w[:, _OFF_GD + n_tiles * LANES:]
    tail = jnp.concatenate([tail, jnp.zeros((WPREP_ROWS, LANES - tail.shape[1]), F32)], axis=1)
    rolled.append(pltpu.roll(tail, keep, 1))
    for t in range(n_tiles):
        gate_out[:, t * LANES:(t + 1) * LANES] = jnp.where(lane < keep, rolled[t], rolled[t + 1]).astype(BF16)


def _wprep_call(w_in):
    in_total = w_in.shape[2]
    assert in_total == _OFF_GATE + 2 * D_MODEL and D_MODEL % WPREP_ROWS == 0
    row = lambda width: pl.BlockSpec((WPREP_ROWS, width), lambda i: (i, 0))
    return pl.pallas_call(
        _wprep_body,
        grid=(D_MODEL // WPREP_ROWS,),
        in_specs=[pl.BlockSpec((None, WPREP_ROWS, in_total), lambda i: (0, i, 0))],
        out_specs=[row(QKV_W), row(GLA_W), row(2 * D_MODEL)],
        out_shape=[jax.ShapeDtypeStruct((D_MODEL, QKV_W), BF16),
                   jax.ShapeDtypeStruct((D_MODEL, GLA_W), BF16),
                   jax.ShapeDtypeStruct((D_MODEL, 2 * D_MODEL), BF16)],
        compiler_params=pltpu.CompilerParams(dimension_semantics=("arbitrary",)),
        name="wprep",
    )(w_in)


def _prep_weights(norm1_w, w_in, q_norm_w, k_norm_w, gla_gate_up, gla_gate_bias, w_attn_branch,
                  w_gla_branch, w_out, norm2_w, router_group, router_expert):
    wqkv, wgla, wgate = _wprep_call(w_in)
    gup = jnp.zeros((LANES, GLA_K_W), BF16).at[:GLA_GATE_RANK].set(gla_gate_up.astype(BF16))
    wr = jnp.zeros((D_MODEL, ROUTER_W), F32)
    wr = wr.at[:, :N_EXPERTS].set(router_expert.transpose(1, 0, 2).reshape(D_MODEL, N_EXPERTS))
    wr = wr.at[:, N_EXPERTS:N_EXPERTS + N_GROUPS].set(router_group)
    wrh = wr.astype(BF16)
    wrl = (wr - wrh.astype(F32)).astype(BF16)
    return {
        "norm1": norm1_w.reshape(1, D_MODEL),
        "wqkv": wqkv,
        "wgla": wgla,
        "wgate": wgate,
        "qn": jnp.tile(q_norm_w, ATTN_HEADS).reshape(1, ATTN_Q_W),
        "kn": jnp.tile(k_norm_w, ATTN_KV_HEADS).reshape(1, ATTN_KV_W),
        "gup": gup,
        "gb": gla_gate_bias.reshape(1, GLA_K_W),
        "wab": w_attn_branch.astype(BF16),
        "wgb": w_gla_branch.astype(BF16),
        "wo": w_out.astype(BF16),
        "norm2": norm2_w.reshape(1, D_MODEL),
        "wr2": jnp.concatenate([wrh, wrl], axis=1),
    }


def _pick_tile(n, want):
    t = want
    while n % t:
        t //= 2
    return t


def kernel(x, meta_tokens, norm1_w, w_in, q_norm_w, k_norm_w, attn_sinks, gla_gate_up, gla_gate_bias,
           gla_norm_w, w_attn_branch, w_gla_branch, w_out, norm2_w, router_group, router_expert,
           expert_w_gate, expert_w_up, expert_w_down):
    batch, seq, d = x.shape
    assert d == D_MODEL and seq % GLA_STEP == 0 and norm1_w.shape[0] == 1
    n = batch * seq
    w = _prep_weights(norm1_w[0], w_in, q_norm_w[0], k_norm_w[0], gla_gate_up[0], gla_gate_bias[0],
                      w_attn_branch[0], w_gla_branch[0], w_out[0], norm2_w[0], router_group[0],
                      router_expert[0])
    x2d = x.reshape(n, D_MODEL)

    cos_m, sin_m = _rope_tables(jnp.arange(N_META))
    cos_r, sin_r = _rope_tables(jnp.arange(seq) + N_META)
    meta = _proj_call(meta_tokens.astype(F32), N_META, 1, w, cos_m, sin_m)
    tm = _pick_tile(seq, 512)
    (q, kd, vd, qg, kg, vg, rg, gk, sga, sgg) = _proj_call(x2d, tm, seq // tm, w, cos_r, sin_r)

    pad_after = ((0, BLOCK - N_META), (0, 0))
    km = jnp.pad(meta[1], pad_after)
    vm = jnp.pad(meta[2], pad_after)
    pad_before = ((GLA_STEP - N_META, 0), (0, 0))
    s0 = _gla_init_call(jnp.pad(meta[4], pad_before), jnp.pad(meta[5], pad_before),
                        jnp.pad(meta[7], pad_before))
    ao, go = _mixers_call(q.reshape(batch, seq, ATTN_Q_W), kd.reshape(batch, seq, ATTN_KV_DUP_W),
                          vd.reshape(batch, seq, ATTN_KV_DUP_W), km, vm, attn_sinks[0].astype(F32),
                          s0, qg.reshape(batch, seq, GLA_K_W), kg.reshape(batch, seq, GLA_K_W),
                          vg.reshape(batch, seq, GLA_V_W), gk.reshape(batch, seq, GLA_K_W),
                          rg.reshape(batch, seq, GLA_V_W), gla_norm_w[0].reshape(1, GLA_VAL_DIM),
                          batch, seq)

    h, xn2, rt, rtt, cnt = _merge_call(x2d, ao.reshape(n, ATTN_Q_W), go.reshape(n, GLA_V_W), sga, sgg, w,
                                       _pick_tile(n, 1024))

    assert (2 * n) % EXPERT_TILE == 0 and _num_visits(n) <= VISIT_CAP
    pos, vis = _route_call(rtt, cnt)
    pos0, pos1 = pos[0], pos[1]
    xs = _scatter_call(pos0, pos1, xn2, 2 * n)
    wg = expert_w_gate[0].reshape(N_EXPERTS, D_MODEL, EXPERT_FF)
    wu = expert_w_up[0].reshape(N_EXPERTS, D_MODEL, EXPERT_FF)
    wd = expert_w_down[0].reshape(N_EXPERTS, EXPERT_FF, D_MODEL)
    y = _expert_call(vis, xs, wg, wu, wd, _num_visits(n))
    out = _combine_call(pos0, pos1, h, rt, y)
    return out.reshape(batch, seq, D_MODEL)
```

```python
import functools
import math

import jax
import jax.numpy as jnp
from jax import lax
from jax.experimental import pallas as pl
from jax.experimental.pallas import tpu as pltpu

F32 = jnp.float32
BF16 = jnp.bfloat16

D_MODEL = 1024
N_META = 16
BLOCK = 128
ATTN_HEADS = 8
ATTN_KV_HEADS = 2
HEAD_DIM = 64
ROPE_THETA = 10000.0
ATTN_Q_W = ATTN_HEADS * HEAD_DIM
ATTN_KV_W = ATTN_KV_HEADS * HEAD_DIM
GLA_HEADS = 4
GLA_KEY_DIM = 64
GLA_VAL_DIM = 128
GLA_K_W = GLA_HEADS * GLA_KEY_DIM
GLA_V_W = GLA_HEADS * GLA_VAL_DIM
GLA_GATE_RANK = 16
GLA_GATE_NORM = 16.0
GLA_CHUNK = 64
N_GROUPS = 4
EXPERTS_PER_GROUP = 8
N_EXPERTS = N_GROUPS * EXPERTS_PER_GROUP
EXPERT_FF = 256
NORM_EPS = 1e-6
MASK_VALUE = -1e30

LANES = 128
ATTN_KV_DUP_W = 2 * ATTN_KV_W
VMEM_LIMIT = 56 * 1024 * 1024

_OFF_Q = 0
_OFF_K = _OFF_Q + ATTN_Q_W
_OFF_V = _OFF_K + ATTN_KV_W
_OFF_GQ = _OFF_V + ATTN_KV_W
_OFF_GD = _OFF_GQ + 2 * GLA_K_W + 2 * GLA_V_W
_OFF_GATE = _OFF_GD + GLA_GATE_RANK
GLA_W = 2 * GLA_K_W + 2 * GLA_V_W
QKV_GD_OFF = _OFF_GQ
QKV_W = D_MODEL


def _dot(a, b):
    return jnp.dot(a, b, preferred_element_type=F32)


def _dot_nt(a, b):
    return lax.dot_general(a, b, (((1,), (1,)), ((), ())), preferred_element_type=F32)


def _split_hi_lo(x):
    hi = x.astype(BF16)
    lo = (x - hi.astype(F32)).astype(BF16)
    return hi, lo


MXU_TILE = 256


def _group_mean_sq(x, group):
    w = x.shape[-1]
    slab = min(w, MXU_TILE)
    shift = int(math.log2(group))
    r = lax.broadcasted_iota(jnp.int32, (slab, slab), 0) >> shift
    c = lax.broadcasted_iota(jnp.int32, (slab, slab), 1) >> shift
    ones = jnp.where(r == c, 1.0, 0.0).astype(BF16)
    hi, lo = _split_hi_lo(x * x)
    sums = [_dot(hi[:, s:s + slab], ones) + _dot(lo[:, s:s + slab], ones) for s in range(0, w, slab)]
    return (sums[0] if len(sums) == 1 else jnp.concatenate(sums, axis=1)) * (1.0 / group)


def _rope(x, cos, sin_signed):
    w = x.shape[-1]
    reps = w // LANES
    if reps > 1:
        cos = jnp.concatenate([cos] * reps, axis=1)
        sin_signed = jnp.concatenate([sin_signed] * reps, axis=1)
    lane = lax.broadcasted_iota(jnp.int32, x.shape, 1)
    first_half = (lane & (HEAD_DIM // 2)) == 0
    swapped = jnp.where(first_half, pltpu.roll(x, w - HEAD_DIM // 2, 1), pltpu.roll(x, HEAD_DIM // 2, 1))
    return x * cos + swapped * sin_signed


def _rms_norm(x, w):
    ms = jnp.mean(x * x, axis=-1, keepdims=True)
    return x * lax.rsqrt(ms + NORM_EPS) * w


def _sigmoid(x):
    return 0.5 * jnp.tanh(0.5 * x) + 0.5


def _log_sigmoid(x):
    return jnp.minimum(x, 0.0) - jnp.log(1.0 + jnp.exp(-jnp.abs(x)))


def _dup_heads_on_lanes(x):
    low = lax.broadcasted_iota(jnp.int32, x.shape, 1) < HEAD_DIM
    swapped = pltpu.roll(x, HEAD_DIM, 1)
    return jnp.concatenate([jnp.where(low, x, swapped), jnp.where(low, swapped, x)], axis=1)


def _proj_body(x_ref, n1_ref, wqkv_ref, wgla_ref, wgate_ref,
               qn_ref, kn_ref, cos_ref, sin_ref, gup_ref, gb_ref,
               q_out, k_out, v_out, qg_out, kg_out, vg_out, rg_out, gk_out, sga_out, sgg_out):
    xn = _rms_norm(x_ref[...], n1_ref[...]).astype(BF16)
    cos = cos_ref[...]
    sin = sin_ref[...]

    a = _dot(xn, wqkv_ref[...])
    q = a[:, :ATTN_Q_W]
    q = q * lax.rsqrt(_group_mean_sq(q, HEAD_DIM) + NORM_EPS) * qn_ref[...]
    q_out[...] = (_rope(q, cos, sin) * (HEAD_DIM ** -0.5)).astype(BF16)

    k = a[:, _OFF_K:_OFF_K + ATTN_KV_W]
    k = k * lax.rsqrt(_group_mean_sq(k, HEAD_DIM) + NORM_EPS) * kn_ref[...]
    k_out[...] = _dup_heads_on_lanes(_rope(k, cos, sin)).astype(BF16)

    v_out[...] = _dup_heads_on_lanes(a[:, _OFF_V:_OFF_V + ATTN_KV_W]).astype(BF16)

    g = _dot(xn, wgla_ref[...])
    qg_out[...] = (g[:, :GLA_K_W] * (GLA_KEY_DIM ** -0.5)).astype(BF16)
    kg_out[...] = g[:, GLA_K_W:2 * GLA_K_W].astype(BF16)
    vg_out[...] = g[:, 2 * GLA_K_W:2 * GLA_K_W + GLA_V_W].astype(BF16)
    r = g[:, 2 * GLA_K_W + GLA_V_W:]
    rg_out[...] = (r * _sigmoid(r)).astype(BF16)

    gd = a[:, QKV_GD_OFF:QKV_GD_OFF + LANES].astype(BF16)
    z = _dot(gd, gup_ref[...]) + gb_ref[...]
    gk_out[...] = _log_sigmoid(z) * (1.0 / GLA_GATE_NORM)

    gates = _dot(xn, wgate_ref[...])
    sga_out[...] = _sigmoid(gates[:, :D_MODEL]).astype(BF16)
    sgg_out[...] = _sigmoid(gates[:, D_MODEL:]).astype(BF16)


def _const_spec(shape):
    nd = len(shape)
    return pl.BlockSpec(shape, lambda *_: (0,) * nd)


def _proj_call(x2d, tm, tiles_per_seq, w, cos, sin):
    n = x2d.shape[0]
    grid = (n // tm,)
    row = lambda width: pl.BlockSpec((tm, width), lambda i: (i, 0))
    tab = pl.BlockSpec((tm, LANES), lambda i: (i % tiles_per_seq, 0))
    in_specs = [
        row(D_MODEL), _const_spec((1, D_MODEL)),
        _const_spec((D_MODEL, QKV_W)), _const_spec((D_MODEL, GLA_W)),
        _const_spec((D_MODEL, 2 * D_MODEL)),
        _const_spec((1, ATTN_Q_W)), _const_spec((1, ATTN_KV_W)),
        tab, tab, _const_spec((LANES, GLA_K_W)), _const_spec((1, GLA_K_W)),
    ]
    out_widths = [ATTN_Q_W, ATTN_KV_DUP_W, ATTN_KV_DUP_W, GLA_K_W, GLA_K_W, GLA_V_W, GLA_V_W,
                  GLA_K_W, D_MODEL, D_MODEL]
    out_dtypes = [BF16] * 7 + [F32] + [BF16] * 2
    return pl.pallas_call(
        _proj_body,
        grid=grid,
        in_specs=in_specs,
        out_specs=[row(wd) for wd in out_widths],
        out_shape=[jax.ShapeDtypeStruct((n, wd), dt) for wd, dt in zip(out_widths, out_dtypes)],
        compiler_params=pltpu.CompilerParams(dimension_semantics=("arbitrary",),
                                             vmem_limit_bytes=VMEM_LIMIT),
        name="proj",
    )(x2d, w["norm1"], w["wqkv"], w["wgla"], w["wgate"],
      w["qn"], w["kn"], cos, sin, w["gup"], w["gb"])


ATTN_STEP_BLOCKS = 4


def _attn_body(sink_ref, q_ref, kp_ref, kc_ref, km_ref, vp_ref, vc_ref, vm_ref, o_ref):
    i = pl.program_id(1)
    nblk = ATTN_STEP_BLOCKS
    pair_rows = 2 * BLOCK
    lane = lax.broadcasted_iota(jnp.int32, (BLOCK, LANES), 1)
    low = lane < HEAD_DIM
    low2 = lax.broadcasted_iota(jnp.int32, (pair_rows, LANES), 1) < HEAD_DIM
    top2 = lax.broadcasted_iota(jnp.int32, (pair_rows, 1), 0) < BLOCK
    key = lax.broadcasted_iota(jnp.int32, (pair_rows, 2 * LANES), 1) & (LANES - 1)
    rowi = lax.broadcasted_iota(jnp.int32, (pair_rows, 2 * LANES), 0) & (BLOCK - 1)
    mask_meta = key < N_META
    mask_prev = key > rowi
    mask_cur = key <= rowi
    zero = jnp.zeros((), BF16)

    def block_diag(x):
        return jnp.concatenate([jnp.where(low, x, zero), jnp.where(low, zero, x)], axis=0)

    blk_row = lax.broadcasted_iota(jnp.int32, (2 * BLOCK, LANES), 0)
    blk_lane = lax.broadcasted_iota(jnp.int32, (2 * BLOCK, LANES), 1)
    ones_blk = jnp.where((blk_row < BLOCK) == (blk_lane < HEAD_DIM), 1.0, 0.0).astype(BF16)

    def masked(s, mask):
        return jnp.where(mask, s, MASK_VALUE)

    kv_heads = range(ATTN_KV_HEADS)
    s_meta_all, s_seq_all, v_seq_all = [], [], []
    for kvh in kv_heads:
        sl = slice(kvh * LANES, (kvh + 1) * LANES)
        k_seq = [block_diag(kp_ref[:, sl])]
        v_seq = [block_diag(vp_ref[:, sl])]
        for t in range(nblk):
            rows = slice(t * BLOCK, (t + 1) * BLOCK)
            k_seq.append(block_diag(kc_ref[rows, sl]))
            v_seq.append(block_diag(vc_ref[rows, sl]))
        q_all = jnp.concatenate(
            [q_ref[t * BLOCK:(t + 1) * BLOCK, (2 * kvh + pr) * LANES:(2 * kvh + pr + 1) * LANES]
             for t in range(nblk) for pr in range(2)], axis=0)
        s_meta_all.append(_dot_nt(q_all, block_diag(km_ref[:, sl])))
        s_seq = []
        for j in range(nblk + 1):
            lo_blk, hi_blk = max(j - 1, 0), min(j, nblk - 1)
            s_seq.append(_dot_nt(q_all[lo_blk * pair_rows:(hi_blk + 1) * pair_rows], k_seq[j]))
        s_seq_all.append(s_seq)
        v_seq_all.append(v_seq)

    probs_all = []
    for kvh in kv_heads:
        s_meta, s_seq = s_meta_all[kvh], s_seq_all[kvh]

        def sink_col(hh):
            return jnp.where(top2, sink_ref[4 * kvh + hh], sink_ref[4 * kvh + 2 + hh])

        p_meta, p_prev, p_cur, l_all = [], [], [], []
        for t in range(nblk):
            sm = masked(s_meta[t * pair_rows:(t + 1) * pair_rows], mask_meta)
            prev_rows = slice(0, pair_rows) if t == 0 else slice(pair_rows, 2 * pair_rows)
            sb = jnp.where(mask_prev, s_seq[t][prev_rows], s_seq[t + 1][0:pair_rows])
            if t == 0:
                sb = masked(sb, jnp.logical_or(mask_cur, i > 0))
            s_max = jnp.maximum(sm, sb)
            m2 = []
            sink_terms = []
            for hh in range(2):
                sink = sink_col(hh)
                m = jnp.maximum(jnp.max(s_max[:, hh * LANES:(hh + 1) * LANES], axis=1, keepdims=True), sink)
                m2.append(m)
                sink_terms.append(jnp.exp(sink - m))

            def probs(s):
                return jnp.concatenate([jnp.exp(s[:, hh * LANES:(hh + 1) * LANES] - m2[hh])
                                        for hh in range(2)], axis=1)

            pm, pb = probs(sm), probs(sb)
            pb16 = pb.astype(BF16)
            p_meta.append(pm.astype(BF16))
            p_prev.append(jnp.where(mask_prev, pb16, zero))
            p_cur.append(jnp.where(mask_prev, zero, pb16))
            l_all.append(_dot((pm + pb).astype(BF16), ones_blk)
                         + jnp.where(low2, sink_terms[0], sink_terms[1]))
        probs_all.append((p_meta, p_prev, p_cur, l_all))

    for kvh in kv_heads:
        sl = slice(kvh * LANES, (kvh + 1) * LANES)
        p_meta, p_prev, p_cur, l_all = probs_all[kvh]
        v_seq = v_seq_all[kvh]
        o_meta = _dot(jnp.concatenate(p_meta, axis=0), block_diag(vm_ref[:, sl]))
        o_seq = []
        for j in range(nblk + 1):
            parts = ([p_cur[j - 1]] if j >= 1 else []) + ([p_prev[j]] if j < nblk else [])
            o_seq.append(_dot(jnp.concatenate(parts, axis=0) if len(parts) > 1 else parts[0], v_seq[j]))
        for t in range(nblk):
            prev_rows = slice(0, pair_rows) if t == 0 else slice(pair_rows, 2 * pair_rows)
            o = (o_meta[t * pair_rows:(t + 1) * pair_rows] + o_seq[t][prev_rows]
                 + o_seq[t + 1][0:pair_rows]) / l_all[t]
            for pr in range(2):
                col = (2 * kvh + pr) * LANES
                o_ref[t * BLOCK:(t + 1) * BLOCK, col:col + LANES] = o[pr * BLOCK:(pr + 1) * BLOCK].astype(BF16)


GLA_STEP = 2 * GLA_CHUNK
GLA_PAIR_K = 2 * GLA_KEY_DIM
GLA_PAIR_V = 2 * GLA_VAL_DIM


def _gla_consts():
    r = lax.broadcasted_iota(jnp.int32, (GLA_STEP, 2 * GLA_STEP), 0)
    c = lax.broadcasted_iota(jnp.int32, (GLA_STEP, 2 * GLA_STEP), 1) & (GLA_STEP - 1)
    causal = jnp.logical_and((r >> 6) == (c >> 6), c <= r)
    tri = jnp.where(causal[:, :GLA_STEP], 1.0, 0.0).astype(BF16)
    return causal, tri


def _gla_steps(units, states, causal, tri):
    row = lax.broadcasted_iota(jnp.int32, (GLA_STEP, GLA_PAIR_K), 0)
    lane = lax.broadcasted_iota(jnp.int32, (GLA_STEP, GLA_PAIR_K), 1)
    first = row < GLA_CHUNK
    low = lane < GLA_KEY_DIM
    lane_t = lax.broadcasted_iota(jnp.int32, (GLA_PAIR_K, GLA_STEP), 1)
    first_t = lane_t < GLA_CHUNK
    srow = lax.broadcasted_iota(jnp.int32, (GLA_PAIR_K, GLA_PAIR_V), 0)
    slane = lax.broadcasted_iota(jnp.int32, (GLA_PAIR_K, GLA_PAIR_V), 1)
    diag = (srow < GLA_KEY_DIM) == (slane < GLA_VAL_DIM)
    zero = jnp.zeros((), BF16)
    zf = jnp.zeros((), F32)
    zv = jnp.zeros((GLA_STEP, GLA_VAL_DIM), BF16)
    half = GLA_CHUNK // 2
    n = len(units)

    gk = jnp.concatenate([u[4] for u in units], axis=1) if n > 1 else units[0][4]
    g_hi = gk.astype(BF16)
    g_r = gk - g_hi.astype(F32)
    g_mid = g_r.astype(BF16)
    g_lo = (g_r - g_mid.astype(F32)).astype(BF16)
    b_all = _dot(tri, g_hi) + _dot(tri, g_mid) + _dot(tri, g_lo)

    prep = []
    for i, (_, q2, k2, v2, _) in enumerate(units):
        b = b_all[:, i * GLA_PAIR_K:(i + 1) * GLA_PAIR_K]
        b_mid = jnp.where(first, b[half:half + 1, :], b[GLA_CHUNK + half:GLA_CHUNK + half + 1, :])
        qi = (q2 * jnp.exp(b - b_mid)).astype(BF16)
        ki = (k2 * jnp.exp(b_mid - b)).astype(BF16)
        qx = q2 * jnp.exp(b)
        kblk = jnp.concatenate([jnp.where(low, ki, zero), jnp.where(low, zero, ki)], axis=0)
        vblk = jnp.concatenate([jnp.concatenate([v2[:, :GLA_VAL_DIM], zv], axis=1),
                                jnp.concatenate([zv, v2[:, GLA_VAL_DIM:]], axis=1)], axis=0)
        b_t = b.T
        k_t = k2.T
        bl_a = b_t[:, GLA_CHUNK - 1:GLA_CHUNK]
        bl_b = b_t[:, GLA_STEP - 1:GLA_STEP]
        kx_t = k_t * jnp.exp(jnp.where(first_t, bl_a, bl_b) - b_t)
        prep.append(dict(qi=qi, kblk=kblk, vblk=vblk,
                         qx_a=jnp.where(first, qx, zf).astype(BF16), qx_b=jnp.where(first, zf, qx).astype(BF16),
                         kx_a=jnp.where(first_t, kx_t, zf).astype(BF16),
                         kx_b=jnp.where(first_t, zf, kx_t).astype(BF16),
                         dec_a=jnp.exp(bl_a), dec_b=jnp.exp(bl_b)))

    atts = [jnp.where(causal, _dot_nt(p["qi"], p["kblk"]), 0.0).astype(BF16) for p in prep]
    kv_a = [jnp.where(diag, _dot(p["kx_a"], u[3]), 0.0) for p, u in zip(prep, units)]
    kv_b = [jnp.where(diag, _dot(p["kx_b"], u[3]), 0.0) for p, u in zip(prep, units)]
    outs = [_dot(a, p["vblk"]) for a, p in zip(atts, prep)]

    states = dict(states)
    for i, (pair, *_rest) in enumerate(units):
        p = prep[i]
        s_prev = states[pair]
        s_a = p["dec_a"] * s_prev + kv_a[i]
        outs[i] = outs[i] + _dot(p["qx_a"], s_prev.astype(BF16)) + _dot(p["qx_b"], s_a.astype(BF16))
        states[pair] = p["dec_b"] * s_a + kv_b[i]
    return outs, states


def _gla_init_body(k_ref, v_ref, gk_ref, s_out):
    causal, tri = _gla_consts()
    units = []
    for p in range(GLA_HEADS // 2):
        k2 = k_ref[:, p * GLA_PAIR_K:(p + 1) * GLA_PAIR_K].astype(F32)
        units.append((p, jnp.zeros_like(k2), k2, v_ref[:, p * GLA_PAIR_V:(p + 1) * GLA_PAIR_V],
                      gk_ref[:, p * GLA_PAIR_K:(p + 1) * GLA_PAIR_K]))
    zeros = {p: jnp.zeros((GLA_PAIR_K, GLA_PAIR_V), F32) for p in range(GLA_HEADS // 2)}
    _, states = _gla_steps(units, zeros, causal, tri)
    for p in range(GLA_HEADS // 2):
        s_out[p] = states[p]


def _gla_init_call(kg_m, vg_m, gk_m):
    return pl.pallas_call(
        _gla_init_body,
        out_shape=jax.ShapeDtypeStruct((GLA_HEADS // 2, GLA_PAIR_K, GLA_PAIR_V), F32),
        name="gla_init",
    )(kg_m, vg_m, gk_m)


def _gla_body(s0_ref, q_ref, k_ref, v_ref, gk_ref, rg_ref, nw_ref, o_ref, s_ref):
    t = pl.program_id(1)

    @pl.when(t == 0)
    def _():
        s_ref[...] = s0_ref[...]

    causal, tri = _gla_consts()
    nw = nw_ref[...]
    pairs = range(GLA_HEADS // 2)
    units = []
    where = []
    for u in range(GLA_GRID_STEPS):
        rows = slice(u * GLA_STEP, (u + 1) * GLA_STEP)
        for p in pairs:
            ks = slice(p * GLA_PAIR_K, (p + 1) * GLA_PAIR_K)
            vs = slice(p * GLA_PAIR_V, (p + 1) * GLA_PAIR_V)
            units.append((p, q_ref[rows, ks].astype(F32), k_ref[rows, ks].astype(F32), v_ref[rows, vs],
                          gk_ref[rows, ks]))
            where.append((rows, vs))
    outs, states = _gla_steps(units, {p: s_ref[p] for p in pairs}, causal, tri)
    for p in pairs:
        s_ref[p] = states[p]
    for o, (rows, vs) in zip(outs, where):
        heads = []
        for hh in range(2):
            oh = o[:, hh * GLA_VAL_DIM:(hh + 1) * GLA_VAL_DIM]
            ms = jnp.mean(oh * oh, axis=-1, keepdims=True)
            heads.append(oh * lax.rsqrt(ms + NORM_EPS) * nw)
        on = jnp.concatenate(heads, axis=1)
        o_ref[rows, vs] = (on * rg_ref[rows, vs].astype(F32)).astype(BF16)


GLA_GRID_STEPS = 4


def _mixers_body(sink_ref, q_ref, kp_ref, kc_ref, km_ref, vp_ref, vc_ref, vm_ref,
                 s0_ref, qg_ref, kg_ref, vg_ref, gk_ref, rg_ref, nw_ref, ao_ref, go_ref, s_ref):
    _gla_body(s0_ref, qg_ref, kg_ref, vg_ref, gk_ref, rg_ref, nw_ref, go_ref, s_ref)
    _attn_body(sink_ref, q_ref, kp_ref, kc_ref, km_ref, vp_ref, vc_ref, vm_ref, ao_ref)


def _mixers_call(q, kd, vd, km, vm, sinks, s0, qg, kg, vg, gk, rg, nw, batch, seq):
    step = ATTN_STEP_BLOCKS * BLOCK
    assert seq % step == 0 and step == GLA_GRID_STEPS * GLA_STEP
    seq_spec = lambda width: pl.BlockSpec((None, step, width), lambda b, i, s: (b, i, 0))
    prev = pl.BlockSpec((None, BLOCK, ATTN_KV_DUP_W),
                        lambda b, i, s: (b, jnp.maximum(i * ATTN_STEP_BLOCKS - 1, 0), 0))
    const = lambda shape: pl.BlockSpec(shape, lambda b, i, s: (0,) * len(shape))
    meta = const((BLOCK, ATTN_KV_DUP_W))
    grid_spec = pltpu.PrefetchScalarGridSpec(
        num_scalar_prefetch=1,
        grid=(batch, seq // step),
        in_specs=[seq_spec(ATTN_Q_W), prev, seq_spec(ATTN_KV_DUP_W), meta,
                  prev, seq_spec(ATTN_KV_DUP_W), meta,
                  const((GLA_HEADS // 2, GLA_PAIR_K, GLA_PAIR_V)),
                  seq_spec(GLA_K_W), seq_spec(GLA_K_W), seq_spec(GLA_V_W), seq_spec(GLA_K_W),
                  seq_spec(GLA_V_W), const((1, GLA_VAL_DIM))],
        out_specs=[seq_spec(ATTN_Q_W), seq_spec(GLA_V_W)],
        scratch_shapes=[pltpu.VMEM((GLA_HEADS // 2, GLA_PAIR_K, GLA_PAIR_V), F32)],
    )
    return pl.pallas_call(
        _mixers_body,
        grid_spec=grid_spec,
        out_shape=[jax.ShapeDtypeStruct((batch, seq, ATTN_Q_W), BF16),
                   jax.ShapeDtypeStruct((batch, seq, GLA_V_W), BF16)],
        compiler_params=pltpu.CompilerParams(dimension_semantics=("arbitrary", "arbitrary"),
                                             vmem_limit_bytes=VMEM_LIMIT),
        name="mixers",
    )(sinks, q, kd, kd, km, vd, vd, vm, s0, qg, kg, vg, gk, rg, nw)


ROUTER_W = LANES


RT_E0, RT_E1, RT_W0, RT_W1 = 0, 1, 2, 3


MERGE_SPLIT = 4


def _merge_body(x_ref, ao_ref, go_ref, sga_ref, sgg_ref, wab_ref, wgb_ref, wo_ref, n2_ref,
                wr2_ref, h_out, xn_out, rt_out, rtt_out, cnt_out):
    @pl.when(pl.program_id(0) == 0)
    def _():
        cnt_out[...] = jnp.zeros_like(cnt_out)

    rows_per = x_ref.shape[0] // MERGE_SPLIT
    for part in range(MERGE_SPLIT):
        _merge_rows(slice(part * rows_per, (part + 1) * rows_per), x_ref, ao_ref, go_ref, sga_ref, sgg_ref,
                    wab_ref, wgb_ref, wo_ref, n2_ref, wr2_ref, h_out, xn_out, rt_out, rtt_out, cnt_out)


def _merge_rows(rows, x_ref, ao_ref, go_ref, sga_ref, sgg_ref, wab_ref, wgb_ref, wo_ref, n2_ref,
                wr2_ref, h_out, xn_out, rt_out, rtt_out, cnt_out):
    ya = _dot(ao_ref[rows, :], wab_ref[...])
    yg = _dot(go_ref[rows, :], wgb_ref[...])
    merged = sga_ref[rows, :].astype(F32) * ya + sgg_ref[rows, :].astype(F32) * yg
    h = x_ref[rows, :] + _dot(merged.astype(BF16), wo_ref[...])
    xn = _rms_norm(h, n2_ref[...])
    for c in range(ROW_TILES):
        h_out[c, rows, :] = h[:, c * LANES:(c + 1) * LANES]
        xn_out[c, rows, :] = xn[:, c * LANES:(c + 1) * LANES]

    xh, xl = _split_hi_lo(xn)
    wr2 = wr2_ref[...]
    lg2 = _dot(xh, wr2) + _dot(xl, wr2)
    lg = lg2[:, :ROUTER_W] + lg2[:, ROUTER_W:]
    lg_t = lg.T
    tm = lg.shape[0]
    neg = -jnp.inf
    big = jnp.int32(1 << 20)
    gl = lg_t[N_EXPERTS:N_EXPERTS + N_GROUPS, :]
    grow = lax.broadcasted_iota(jnp.int32, (N_GROUPS, tm), 0)
    gmax = jnp.max(gl, axis=0, keepdims=True)
    g_idx = jnp.min(jnp.where(gl == gmax, grow, big), axis=0, keepdims=True)
    g_w = 1.0 / jnp.sum(jnp.exp(gl - gmax), axis=0, keepdims=True)
    erow = lax.broadcasted_iota(jnp.int32, (N_EXPERTS, tm), 0)
    el = jnp.where((erow >> 3) == g_idx, lg_t[0:N_EXPERTS, :], neg)
    m1 = jnp.max(el, axis=0, keepdims=True)
    i1 = jnp.min(jnp.where(el == m1, erow, big), axis=0, keepdims=True)
    el2 = jnp.where(erow == i1, neg, el)
    m2 = jnp.max(el2, axis=0, keepdims=True)
    i2 = jnp.min(jnp.where(el2 == m2, erow, big), axis=0, keepdims=True)
    e2 = jnp.exp(m2 - m1)
    p1 = g_w / (1.0 + e2)
    p2 = g_w * e2 / (1.0 + e2)
    def record(rows):
        r = lax.broadcasted_iota(jnp.int32, (rows, tm), 0)
        return jnp.where(r == RT_E0, i1.astype(F32),
                         jnp.where(r == RT_E1, i2.astype(F32),
                                   jnp.where(r == RT_W0, p1, jnp.where(r == RT_W1, p2, 0.0))))
    rtt_out[:, rows] = record(8)
    rt_out[rows, :] = record(LANES).T
    picked = jnp.where(jnp.logical_or(erow == i1, erow == i2), 1.0, 0.0)
    cnt_out[...] += jnp.sum(picked, axis=1, keepdims=True)


def _merge_call(x2d, ao, go, sga, sgg, w, tm):
    n = x2d.shape[0]
    row = lambda width: pl.BlockSpec((tm, width), lambda i: (i, 0))
    return pl.pallas_call(
        _merge_body,
        grid=(n // tm,),
        in_specs=[row(D_MODEL), row(ATTN_Q_W), row(GLA_V_W), row(D_MODEL), row(D_MODEL),
                  _const_spec((ATTN_Q_W, D_MODEL)), _const_spec((GLA_V_W, D_MODEL)),
                  _const_spec((D_MODEL, D_MODEL)), _const_spec((1, D_MODEL)),
                  _const_spec((D_MODEL, 2 * ROUTER_W))],
        out_specs=[pl.BlockSpec((ROW_TILES, tm, LANES), lambda i: (0, i, 0)),
                   pl.BlockSpec((ROW_TILES, tm, LANES), lambda i: (0, i, 0)), row(ROUTER_W),
                   pl.BlockSpec((8, tm), lambda i: (0, i)), _const_spec((N_EXPERTS, LANES))],
        out_shape=[jax.ShapeDtypeStruct((ROW_TILES, n, LANES), F32),
                   jax.ShapeDtypeStruct((ROW_TILES, n, LANES), F32),
                   jax.ShapeDtypeStruct((n, ROUTER_W), F32),
                   jax.ShapeDtypeStruct((8, n), F32),
                   jax.ShapeDtypeStruct((N_EXPERTS, LANES), F32)],
        compiler_params=pltpu.CompilerParams(dimension_semantics=("arbitrary",),
                                             vmem_limit_bytes=VMEM_LIMIT),
        name="merge",
    )(x2d, ao, go, sga, sgg, w["wab"], w["wgb"], w["wo"], w["norm2"], w["wr2"])


EXPERT_TILE = 512
EXPERT_SPLIT = 2
ROUTE_TILE = 512
VISIT_CAP = 256
VIS_EXPERT, VIS_TILE, VIS_LO, VIS_HI, VIS_COUNT = 0, 1, 2, 3, 4


def _num_visits(n_tokens):
    return (2 * n_tokens) // EXPERT_TILE + N_EXPERTS - 1


def _expert_cumsum_inclusive(x):
    row = lax.broadcasted_iota(jnp.int32, x.shape, 0)
    s = 1
    while s < N_EXPERTS:
        x = x + jnp.where(row >= s, pltpu.roll(x, s, 0), 0.0)
        s *= 2
    return x


def _route_body(rtt_ref, cnt_ref, pos_ref, vis_ref, carry_ref):
    i = pl.program_id(0)
    tm = rtt_ref.shape[1]
    cnt = cnt_ref[...]
    end = _expert_cumsum_inclusive(cnt)
    base = end - cnt

    @pl.when(i == 0)
    def _():
        carry_ref[...] = jnp.zeros_like(carry_ref)
        inv = 1.0 / EXPERT_TILE
        first = jnp.floor(base * inv)
        n_vis = jnp.where(cnt > 0.0, jnp.floor((end - 1.0) * inv) - first + 1.0, 0.0)
        vend = _expert_cumsum_inclusive(n_vis)
        vstart = vend - n_vis
        col = lambda a: a[:, 0:1]
        erow = lax.broadcasted_iota(jnp.int32, (N_EXPERTS, VISIT_CAP), 0).astype(F32)
        v = lax.broadcasted_iota(jnp.int32, (N_EXPERTS, VISIT_CAP), 1).astype(F32)
        ev = jnp.minimum(jnp.sum(jnp.where(col(vend) <= v, 1.0, 0.0), axis=0, keepdims=True),
                         N_EXPERTS - 1.0)
        mine = erow == ev
        pick = lambda a: jnp.sum(jnp.where(mine, a, 0.0), axis=0, keepdims=True)
        tile = pick(col(first) + v - col(vstart))
        lo = jnp.maximum(pick(col(base)) - tile * EXPERT_TILE, 0.0)
        hi = jnp.minimum(pick(col(end)) - tile * EXPERT_TILE, float(EXPERT_TILE))
        total = vend[N_EXPERTS - 1:N_EXPERTS, 0:1]
        r8 = lax.broadcasted_iota(jnp.int32, (8, VISIT_CAP), 0)
        rec = jnp.where(r8 == VIS_EXPERT, ev,
                        jnp.where(r8 == VIS_TILE, tile,
                                  jnp.where(r8 == VIS_LO, lo,
                                            jnp.where(r8 == VIS_HI, hi,
                                                      jnp.where(r8 == VIS_COUNT, total, 0.0)))))
        vis_ref[...] = rec.astype(jnp.int32)

    rtt = rtt_ref[...]
    erow = lax.broadcasted_iota(jnp.int32, (N_EXPERTS, tm), 0).astype(F32)
    oh0 = erow == rtt[RT_E0:RT_E0 + 1, :]
    oh1 = erow == rtt[RT_E1:RT_E1 + 1, :]
    picked = jnp.where(jnp.logical_or(oh0, oh1), 1.0, 0.0)
    r = lax.broadcasted_iota(jnp.int32, (tm, tm), 0)
    c = lax.broadcasted_iota(jnp.int32, (tm, tm), 1)
    earlier = jnp.where(r < c, 1.0, 0.0).astype(BF16)
    slot = _dot(picked.astype(BF16), earlier) + (carry_ref[:, 0:1] + base[:, 0:1])
    p0 = jnp.sum(jnp.where(oh0, slot, 0.0), axis=0, keepdims=True)
    p1 = jnp.sum(jnp.where(oh1, slot, 0.0), axis=0, keepdims=True)
    r8 = lax.broadcasted_iota(jnp.int32, (8, tm), 0)
    pos_ref[...] = jnp.where(r8 == 0, p0, jnp.where(r8 == 1, p1, 0.0)).astype(jnp.int32)
    carry_ref[...] += jnp.sum(picked, axis=1, keepdims=True)


def _route_call(rtt, cnt):
    n = rtt.shape[1]
    tm = _pick_tile(n, ROUTE_TILE)
    return pl.pallas_call(
        _route_body,
        grid=(n // tm,),
        in_specs=[pl.BlockSpec((8, tm), lambda i: (0, i)), _const_spec((N_EXPERTS, LANES))],
        out_specs=[pl.BlockSpec((8, tm), lambda i: (0, i)), _const_spec((8, VISIT_CAP))],
        out_shape=[jax.ShapeDtypeStruct((8, n), jnp.int32),
                   jax.ShapeDtypeStruct((8, VISIT_CAP), jnp.int32)],
        scratch_shapes=[pltpu.VMEM((N_EXPERTS, LANES), F32)],
        compiler_params=pltpu.CompilerParams(dimension_semantics=("arbitrary",)),
        name="route",
    )(rtt, cnt)


SCATTER_TILE = 512
ROW_TILES = D_MODEL // LANES


def _store_slabs(ref, val):
    for c in range(ROW_TILES):
        ref[c] = val[:, c * LANES:(c + 1) * LANES]


def _load_slabs(ref):
    return jnp.concatenate([ref[c] for c in range(ROW_TILES)], axis=1)


def _scatter_body(pos0_ref, pos1_ref, h_ref, hs_hbm, sem):
    tm = h_ref.shape[1]
    for r in range(tm):
        src = h_ref.at[:, r, :]
        pltpu.make_async_copy(src, hs_hbm.at[:, pos0_ref[r], :], sem).start(priority=0)
        pltpu.make_async_copy(src, hs_hbm.at[:, pos1_ref[r], :], sem).start(priority=1)
    for _ in range(2 * tm):
        pltpu.make_async_copy(h_ref.at[:, 0, :], hs_hbm.at[:, 0, :], sem).wait()


def _scatter_call(pos0, pos1, h_slabs, n_slots):
    n = h_slabs.shape[1]
    tm = _pick_tile(n, SCATTER_TILE)
    smem = lambda: pl.BlockSpec((tm,), lambda i: (i,), memory_space=pltpu.SMEM)
    return pl.pallas_call(
        _scatter_body,
        grid=(n // tm,),
        in_specs=[smem(), smem(), pl.BlockSpec((ROW_TILES, tm, LANES), lambda i: (0, i, 0))],
        out_specs=pl.BlockSpec(memory_space=pl.ANY),
        out_shape=jax.ShapeDtypeStruct((ROW_TILES, n_slots, LANES), F32),
        scratch_shapes=[pltpu.SemaphoreType.DMA],
        compiler_params=pltpu.CompilerParams(dimension_semantics=("arbitrary",)),
        name="scatter",
    )(pos0, pos1, h_slabs)


def _expert_body(vis_ref, x_ref, wg_ref, wu_ref, wd_ref, o_ref):
    j = pl.program_id(0)

    @pl.when(j < vis_ref[VIS_COUNT, 0])
    def _():
        lo_row = vis_ref[VIS_LO, j]
        hi_row = vis_ref[VIS_HI, j]

        @pl.when(lo_row == 0)
        def _():
            o_ref[...] = jnp.zeros_like(o_ref)

        wg = wg_ref[...].astype(BF16)
        wu = wu_ref[...].astype(BF16)
        wd = wd_ref[...].astype(BF16)
        rows_per = EXPERT_TILE // EXPERT_SPLIT
        for part in range(EXPERT_SPLIT):
            rows = slice(part * rows_per, (part + 1) * rows_per)
            x = jnp.concatenate([x_ref[c, rows, :] for c in range(ROW_TILES)], axis=1).astype(BF16)
            g = _dot(x, wg)
            u = _dot(x, wu)
            hg = (g * _sigmoid(g) * u).astype(BF16)
            new = _dot(hg, wd)
            row = lax.broadcasted_iota(jnp.int32, (rows_per, LANES), 0) + part * rows_per
            keep = jnp.logical_and(row >= lo_row, row < hi_row)
            for c in range(ROW_TILES):
                o_ref[c, rows, :] = jnp.where(keep, new[:, c * LANES:(c + 1) * LANES], o_ref[c, rows, :])


def _expert_call(vis, hs, wg, wu, wd, n_visits):
    def visit(j, vis):
        return jnp.minimum(j, vis[VIS_COUNT, 0] - 1)

    def xmap(j, vis):
        return (0, vis[VIS_TILE, visit(j, vis)], 0)

    def wmap(j, vis):
        return (vis[VIS_EXPERT, visit(j, vis)], 0, 0)

    grid_spec = pltpu.PrefetchScalarGridSpec(
        num_scalar_prefetch=1,
        grid=(n_visits,),
        in_specs=[pl.BlockSpec((ROW_TILES, EXPERT_TILE, LANES), xmap),
                  pl.BlockSpec((None, D_MODEL, EXPERT_FF), wmap),
                  pl.BlockSpec((None, D_MODEL, EXPERT_FF), wmap),
                  pl.BlockSpec((None, EXPERT_FF, D_MODEL), wmap)],
        out_specs=pl.BlockSpec((ROW_TILES, EXPERT_TILE, LANES), xmap),
    )
    return pl.pallas_call(
        _expert_body,
        grid_spec=grid_spec,
        out_shape=jax.ShapeDtypeStruct(hs.shape, F32),
        compiler_params=pltpu.CompilerParams(dimension_semantics=("arbitrary",),
                                             vmem_limit_bytes=VMEM_LIMIT),
        name="experts",
    )(vis, hs, wg, wu, wd)


COMBINE_TILE = 256


def _combine_body(p0_ref, p1_ref, p0n_ref, p1n_ref, h_ref, rt_ref, y_hbm, o_ref, g_ref, sem):
    i = pl.program_id(0)
    tm = h_ref.shape[1]
    cur = i % 2

    def issue(q0_ref, q1_ref, buf):
        for r in range(tm):
            pltpu.make_async_copy(y_hbm.at[:, q0_ref[r], :], g_ref.at[buf, 0, :, r, :],
                                  sem.at[buf]).start(priority=0)
            pltpu.make_async_copy(y_hbm.at[:, q1_ref[r], :], g_ref.at[buf, 1, :, r, :],
                                  sem.at[buf]).start(priority=1)

    @pl.when(i == 0)
    def _():
        issue(p0_ref, p1_ref, 0)

    @pl.when(i + 1 < pl.num_programs(0))
    def _():
        issue(p0n_ref, p1n_ref, 1 - cur)

    for _ in range(2 * tm):
        pltpu.make_async_copy(y_hbm.at[:, 0, :], g_ref.at[cur, 0, :, 0, :], sem.at[cur]).wait()
    rt = rt_ref[...]
    o_ref[...] = (_load_slabs(h_ref) + rt[:, RT_W0:RT_W0 + 1] * _load_slabs(g_ref.at[cur, 0])
                  + rt[:, RT_W1:RT_W1 + 1] * _load_slabs(g_ref.at[cur, 1]))


def _combine_call(pos0, pos1, h_slabs, rt, y):
    n = h_slabs.shape[1]
    tm = _pick_tile(n, COMBINE_TILE)
    steps = n // tm
    smem_cur = lambda: pl.BlockSpec((tm,), lambda i: (i,), memory_space=pltpu.SMEM)
    smem_next = lambda: pl.BlockSpec((tm,), lambda i: (jnp.minimum(i + 1, steps - 1),),
                                     memory_space=pltpu.SMEM)
    row = lambda width: pl.BlockSpec((tm, width), lambda i: (i, 0))
    return pl.pallas_call(
        _combine_body,
        grid=(steps,),
        in_specs=[smem_cur(), smem_cur(), smem_next(), smem_next(),
                  pl.BlockSpec((ROW_TILES, tm, LANES), lambda i: (0, i, 0)), row(ROUTER_W),
                  pl.BlockSpec(memory_space=pl.ANY)],
        out_specs=row(D_MODEL),
        out_shape=jax.ShapeDtypeStruct((n, D_MODEL), F32),
        scratch_shapes=[pltpu.VMEM((2, 2, ROW_TILES, tm, LANES), F32), pltpu.SemaphoreType.DMA((2,))],
        compiler_params=pltpu.CompilerParams(dimension_semantics=("arbitrary",),
                                             vmem_limit_bytes=VMEM_LIMIT),
        name="combine",
    )(pos0, pos1, pos0, pos1, h_slabs, rt, y)


def _rope_tables(pos):
    half = HEAD_DIM // 2
    inv_freq = ROPE_THETA ** (-(jnp.arange(half, dtype=F32) * 2.0) / HEAD_DIM)
    ang = pos.astype(F32)[:, None] * inv_freq[None, :]
    cos = jnp.cos(ang)
    sin = jnp.sin(ang)
    cos64 = jnp.concatenate([cos, cos], axis=1)
    sin64 = jnp.concatenate([-sin, sin], axis=1)
    return jnp.tile(cos64, (1, LANES // HEAD_DIM)), jnp.tile(sin64, (1, LANES // HEAD_DIM))


WPREP_ROWS = 128


def _wprep_body(w_ref, qkv_out, gla_out, gate_out):
    w = w_ref[...]
    lane = lax.broadcasted_iota(jnp.int32, (WPREP_ROWS, LANES), 1)
    qkv_out[:, :QKV_GD_OFF] = w[:, :QKV_GD_OFF].astype(BF16)
    gd_tile = w[:, _OFF_GD:_OFF_GD + LANES]
    qkv_out[:, QKV_GD_OFF:QKV_GD_OFF + LANES] = jnp.where(lane < GLA_GATE_RANK, gd_tile, 0.0).astype(BF16)
    qkv_out[:, QKV_GD_OFF + LANES:] = jnp.zeros((WPREP_ROWS, QKV_W - QKV_GD_OFF - LANES), BF16)
    gla_out[...] = w[:, _OFF_GQ:_OFF_GQ + GLA_W].astype(BF16)
    keep = LANES - GLA_GATE_RANK
    n_tiles = 2 * D_MODEL // LANES
    rolled = [pltpu.roll(w[:, _OFF_GD + t * LANES:_OFF_GD + (t + 1) * LANES], keep, 1)
              for t in range(n_tiles)]
    tail = w[:, _OFF_GD + n_tiles * LANES:]
    tail = jnp.concatenate([tail, jnp.zeros((WPREP_ROWS, LANES - tail.shape[1]), F32)], axis=1)
    rolled.append(pltpu.roll(tail, keep, 1))
    for t in range(n_tiles):
        gate_out[:, t * LANES:(t + 1) * LANES] = jnp.where(lane < keep, rolled[t], rolled[t + 1]).astype(BF16)


def _wprep_call(w_in):
    in_total = w_in.shape[2]
    assert in_total == _OFF_GATE + 2 * D_MODEL and D_MODEL % WPREP_ROWS == 0
    row = lambda width: pl.BlockSpec((WPREP_ROWS, width), lambda i: (i, 0))
    return pl.pallas_call(
        _wprep_body,
        grid=(D_MODEL // WPREP_ROWS,),
        in_specs=[pl.BlockSpec((None, WPREP_ROWS, in_total), lambda i: (0, i, 0))],
        out_specs=[row(QKV_W), row(GLA_W), row(2 * D_MODEL)],
        out_shape=[jax.ShapeDtypeStruct((D_MODEL, QKV_W), BF16),
                   jax.ShapeDtypeStruct((D_MODEL, GLA_W), BF16),
                   jax.ShapeDtypeStruct((D_MODEL, 2 * D_MODEL), BF16)],
        compiler_params=pltpu.CompilerParams(dimension_semantics=("arbitrary",)),
        name="wprep",
    )(w_in)


def _prep_weights(norm1_w, w_in, q_norm_w, k_norm_w, gla_gate_up, gla_gate_bias, w_attn_branch,
                  w_gla_branch, w_out, norm2_w, router_group, router_expert):
    wqkv, wgla, wgate = _wprep_call(w_in)
    gup = jnp.zeros((LANES, GLA_K_W), BF16).at[:GLA_GATE_RANK].set(gla_gate_up.astype(BF16))
    wr = jnp.zeros((D_MODEL, ROUTER_W), F32)
    wr = wr.at[:, :N_EXPERTS].set(router_expert.transpose(1, 0, 2).reshape(D_MODEL, N_EXPERTS))
    wr = wr.at[:, N_EXPERTS:N_EXPERTS + N_GROUPS].set(router_group)
    wrh = wr.astype(BF16)
    wrl = (wr - wrh.astype(F32)).astype(BF16)
    return {
        "norm1": norm1_w.reshape(1, D_MODEL),
        "wqkv": wqkv,
        "wgla": wgla,
        "wgate": wgate,
        "qn": jnp.tile(q_norm_w, ATTN_HEADS).reshape(1, ATTN_Q_W),
        "kn": jnp.tile(k_norm_w, ATTN_KV_HEADS).reshape(1, ATTN_KV_W),
        "gup": gup,
        "gb": gla_gate_bias.reshape(1, GLA_K_W),
        "wab": w_attn_branch.astype(BF16),
        "wgb": w_gla_branch.astype(BF16),
        "wo": w_out.astype(BF16),
        "norm2": norm2_w.reshape(1, D_MODEL),
        "wr2": jnp.concatenate([wrh, wrl], axis=1),
    }


def _pick_tile(n, want):
    t = want
    while n % t:
        t //= 2
    return t


def kernel(x, meta_tokens, norm1_w, w_in, q_norm_w, k_norm_w, attn_sinks, gla_gate_up, gla_gate_bias,
           gla_norm_w, w_attn_branch, w_gla_branch, w_out, norm2_w, router_group, router_expert,
           expert_w_gate, expert_w_up, expert_w_down):
    batch, seq, d = x.shape
    assert d == D_MODEL and seq % GLA_STEP == 0 and norm1_w.shape[0] == 1
    n = batch * seq
    w = _prep_weights(norm1_w[0], w_in, q_norm_w[0], k_norm_w[0], gla_gate_up[0], gla_gate_bias[0],
                      w_attn_branch[0], w_gla_branch[0], w_out[0], norm2_w[0], router_group[0],
                      router_expert[0])
    x2d = x.reshape(n, D_MODEL)

    cos_m, sin_m = _rope_tables(jnp.arange(N_META))
    cos_r, sin_r = _rope_tables(jnp.arange(seq) + N_META)
    meta = _proj_call(meta_tokens.astype(F32), N_META, 1, w, cos_m, sin_m)
    tm = _pick_tile(seq, 512)
    (q, kd, vd, qg, kg, vg, rg, gk, sga, sgg) = _proj_call(x2d, tm, seq // tm, w, cos_r, sin_r)

    pad_after = ((0, BLOCK - N_META), (0, 0))
    km = jnp.pad(meta[1], pad_after)
    vm = jnp.pad(meta[2], pad_after)
    pad_before = ((GLA_STEP - N_META, 0), (0, 0))
    s0 = _gla_init_call(jnp.pad(meta[4], pad_before), jnp.pad(meta[5], pad_before),
                        jnp.pad(meta[7], pad_before))
    ao, go = _mixers_call(q.reshape(batch, seq, ATTN_Q_W), kd.reshape(batch, seq, ATTN_KV_DUP_W),
                          vd.reshape(batch, seq, ATTN_KV_DUP_W), km, vm, attn_sinks[0].astype(F32),
                          s0, qg.reshape(batch, seq, GLA_K_W), kg.reshape(batch, seq, GLA_K_W),
                          vg.reshape(batch, seq, GLA_V_W), gk.reshape(batch, seq, GLA_K_W),
                          rg.reshape(batch, seq, GLA_V_W), gla_norm_w[0].reshape(1, GLA_VAL_DIM),
                          batch, seq)

    h, xn2, rt, rtt, cnt = _merge_call(x2d, ao.reshape(n, ATTN_Q_W), go.reshape(n, GLA_V_W), sga, sgg, w,
                                       _pick_tile(n, 1024))

    assert (2 * n) % EXPERT_TILE == 0 and _num_visits(n) <= VISIT_CAP
    pos, vis = _route_call(rtt, cnt)
    pos0, pos1 = pos[0], pos[1]
    xs = _scatter_call(pos0, pos1, xn2, 2 * n)
    wg = expert_w_gate[0].reshape(N_EXPERTS, D_MODEL, EXPERT_FF)
    wu = expert_w_up[0].reshape(N_EXPERTS, D_MODEL, EXPERT_FF)
    wd = expert_w_down[0].reshape(N_EXPERTS, EXPERT_FF, D_MODEL)
    y = _expert_call(vis, xs, wg, wu, wd, _num_visits(n))
    out = _combine_call(pos0, pos1, h, rt, y)
    return out.reshape(batch, seq, D_MODEL)
```

```python
import functools
import math

import jax
import jax.numpy as jnp
from jax import lax
from jax.experimental import pallas as pl
from jax.experimental.pallas import tpu as pltpu

F32 = jnp.float32
BF16 = jnp.bfloat16

D_MODEL = 1024
N_META = 16
BLOCK = 128
ATTN_HEADS = 8
ATTN_KV_HEADS = 2
HEAD_DIM = 64
ROPE_THETA = 10000.0
ATTN_Q_W = ATTN_HEADS * HEAD_DIM
ATTN_KV_W = ATTN_KV_HEADS * HEAD_DIM
GLA_HEADS = 4
GLA_KEY_DIM = 64
GLA_VAL_DIM = 128
GLA_K_W = GLA_HEADS * GLA_KEY_DIM
GLA_V_W = GLA_HEADS * GLA_VAL_DIM
GLA_GATE_RANK = 16
GLA_GATE_NORM = 16.0
GLA_CHUNK = 64
N_GROUPS = 4
EXPERTS_PER_GROUP = 8
N_EXPERTS = N_GROUPS * EXPERTS_PER_GROUP
EXPERT_FF = 256
NORM_EPS = 1e-6
MASK_VALUE = -1e30

LANES = 128
ATTN_KV_DUP_W = 2 * ATTN_KV_W
VMEM_LIMIT = 56 * 1024 * 1024

_OFF_Q = 0
_OFF_K = _OFF_Q + ATTN_Q_W
_OFF_V = _OFF_K + ATTN_KV_W
_OFF_GQ = _OFF_V + ATTN_KV_W
_OFF_GD = _OFF_GQ + 2 * GLA_K_W + 2 * GLA_V_W
_OFF_GATE = _OFF_GD + GLA_GATE_RANK
GLA_W = 2 * GLA_K_W + 2 * GLA_V_W
QKV_GD_OFF = _OFF_GQ
QKV_W = D_MODEL


def _dot(a, b):
    return jnp.dot(a, b, preferred_element_type=F32)


def _dot_nt(a, b):
    return lax.dot_general(a, b, (((1,), (1,)), ((), ())), preferred_element_type=F32)


def _split_hi_lo(x):
    hi = x.astype(BF16)
    lo = (x - hi.astype(F32)).astype(BF16)
    return hi, lo


MXU_TILE = 256


def _group_mean_sq(x, group):
    w = x.shape[-1]
    slab = min(w, MXU_TILE)
    shift = int(math.log2(group))
    r = lax.broadcasted_iota(jnp.int32, (slab, slab), 0) >> shift
    c = lax.broadcasted_iota(jnp.int32, (slab, slab), 1) >> shift
    ones = jnp.where(r == c, 1.0, 0.0).astype(BF16)
    hi, lo = _split_hi_lo(x * x)
    sums = [_dot(hi[:, s:s + slab], ones) + _dot(lo[:, s:s + slab], ones) for s in range(0, w, slab)]
    return (sums[0] if len(sums) == 1 else jnp.concatenate(sums, axis=1)) * (1.0 / group)


def _rope(x, cos, sin_signed):
    w = x.shape[-1]
    reps = w // LANES
    if reps > 1:
        cos = jnp.concatenate([cos] * reps, axis=1)
        sin_signed = jnp.concatenate([sin_signed] * reps, axis=1)
    lane = lax.broadcasted_iota(jnp.int32, x.shape, 1)
    first_half = (lane & (HEAD_DIM // 2)) == 0
    swapped = jnp.where(first_half, pltpu.roll(x, w - HEAD_DIM // 2, 1), pltpu.roll(x, HEAD_DIM // 2, 1))
    return x * cos + swapped * sin_signed


def _rms_norm(x, w):
    ms = jnp.mean(x * x, axis=-1, keepdims=True)
    return x * lax.rsqrt(ms + NORM_EPS) * w


def _sigmoid(x):
    return 0.5 * jnp.tanh(0.5 * x) + 0.5


def _log_sigmoid(x):
    return jnp.minimum(x, 0.0) - jnp.log(1.0 + jnp.exp(-jnp.abs(x)))


def _dup_heads_on_lanes(x):
    low = lax.broadcasted_iota(jnp.int32, x.shape, 1) < HEAD_DIM
    swapped = pltpu.roll(x, HEAD_DIM, 1)
    return jnp.concatenate([jnp.where(low, x, swapped), jnp.where(low, swapped, x)], axis=1)


def _proj_body(x_ref, n1_ref, wqkv_ref, wgla_ref, wgate_ref,
               qn_ref, kn_ref, cos_ref, sin_ref, gup_ref, gb_ref,
               q_out, k_out, v_out, qg_out, kg_out, vg_out, rg_out, gk_out, sga_out, sgg_out):
    xn = _rms_norm(x_ref[...], n1_ref[...]).astype(BF16)
    cos = cos_ref[...]
    sin = sin_ref[...]

    a = _dot(xn, wqkv_ref[...])
    q = a[:, :ATTN_Q_W]
    q = q * lax.rsqrt(_group_mean_sq(q, HEAD_DIM) + NORM_EPS) * qn_ref[...]
    q_out[...] = (_rope(q, cos, sin) * (HEAD_DIM ** -0.5)).astype(BF16)

    k = a[:, _OFF_K:_OFF_K + ATTN_KV_W]
    k = k * lax.rsqrt(_group_mean_sq(k, HEAD_DIM) + NORM_EPS) * kn_ref[...]
    k_out[...] = _dup_heads_on_lanes(_rope(k, cos, sin)).astype(BF16)

    v_out[...] = _dup_heads_on_lanes(a[:, _OFF_V:_OFF_V + ATTN_KV_W]).astype(BF16)

    g = _dot(xn, wgla_ref[...])
    qg_out[...] = (g[:, :GLA_K_W] * (GLA_KEY_DIM ** -0.5)).astype(BF16)
    kg_out[...] = g[:, GLA_K_W:2 * GLA_K_W].astype(BF16)
    vg_out[...] = g[:, 2 * GLA_K_W:2 * GLA_K_W + GLA_V_W].astype(BF16)
    r = g[:, 2 * GLA_K_W + GLA_V_W:]
    rg_out[...] = (r * _sigmoid(r)).astype(BF16)

    gd = a[:, QKV_GD_OFF:QKV_GD_OFF + LANES].astype(BF16)
    z = _dot(gd, gup_ref[...]) + gb_ref[...]
    gk_out[...] = _log_sigmoid(z) * (1.0 / GLA_GATE_NORM)

    gates = _dot(xn, wgate_ref[...])
    sga_out[...] = _sigmoid(gates[:, :D_MODEL]).astype(BF16)
    sgg_out[...] = _sigmoid(gates[:, D_MODEL:]).astype(BF16)


def _const_spec(shape):
    nd = len(shape)
    return pl.BlockSpec(shape, lambda *_: (0,) * nd)


def _proj_call(x2d, tm, tiles_per_seq, w, cos, sin):
    n = x2d.shape[0]
    grid = (n // tm,)
    row = lambda width: pl.BlockSpec((tm, width), lambda i: (i, 0))
    tab = pl.BlockSpec((tm, LANES), lambda i: (i % tiles_per_seq, 0))
    in_specs = [
        row(D_MODEL), _const_spec((1, D_MODEL)),
        _const_spec((D_MODEL, QKV_W)), _const_spec((D_MODEL, GLA_W)),
        _const_spec((D_MODEL, 2 * D_MODEL)),
        _const_spec((1, ATTN_Q_W)), _const_spec((1, ATTN_KV_W)),
        tab, tab, _const_spec((LANES, GLA_K_W)), _const_spec((1, GLA_K_W)),
    ]
    out_widths = [ATTN_Q_W, ATTN_KV_DUP_W, ATTN_KV_DUP_W, GLA_K_W, GLA_K_W, GLA_V_W, GLA_V_W,
                  GLA_K_W, D_MODEL, D_MODEL]
    out_dtypes = [BF16] * 7 + [F32] + [BF16] * 2
    return pl.pallas_call(
        _proj_body,
        grid=grid,
        in_specs=in_specs,
        out_specs=[row(wd) for wd in out_widths],
        out_shape=[jax.ShapeDtypeStruct((n, wd), dt) for wd, dt in zip(out_widths, out_dtypes)],
        compiler_params=pltpu.CompilerParams(dimension_semantics=("arbitrary",),
                                             vmem_limit_bytes=VMEM_LIMIT),
        name="proj",
    )(x2d, w["norm1"], w["wqkv"], w["wgla"], w["wgate"],
      w["qn"], w["kn"], cos, sin, w["gup"], w["gb"])


ATTN_STEP_BLOCKS = 4


def _attn_body(sink_ref, q_ref, kp_ref, kc_ref, km_ref, vp_ref, vc_ref, vm_ref, o_ref):
    i = pl.program_id(1)
    nblk = ATTN_STEP_BLOCKS
    pair_rows = 2 * BLOCK
    lane = lax.broadcasted_iota(jnp.int32, (BLOCK, LANES), 1)
    low = lane < HEAD_DIM
    low2 = lax.broadcasted_iota(jnp.int32, (pair_rows, LANES), 1) < HEAD_DIM
    top2 = lax.broadcasted_iota(jnp.int32, (pair_rows, 1), 0) < BLOCK
    key = lax.broadcasted_iota(jnp.int32, (pair_rows, 2 * LANES), 1) & (LANES - 1)
    rowi = lax.broadcasted_iota(jnp.int32, (pair_rows, 2 * LANES), 0) & (BLOCK - 1)
    mask_meta = key < N_META
    mask_prev = key > rowi
    mask_cur = key <= rowi
    zero = jnp.zeros((), BF16)

    def block_diag(x):
        return jnp.concatenate([jnp.where(low, x, zero), jnp.where(low, zero, x)], axis=0)

    blk_row = lax.broadcasted_iota(jnp.int32, (2 * BLOCK, LANES), 0)
    blk_lane = lax.broadcasted_iota(jnp.int32, (2 * BLOCK, LANES), 1)
    ones_blk = jnp.where((blk_row < BLOCK) == (blk_lane < HEAD_DIM), 1.0, 0.0).astype(BF16)

    def masked(s, mask):
        return jnp.where(mask, s, MASK_VALUE)

    kv_heads = range(ATTN_KV_HEADS)
    s_meta_all, s_seq_all, v_seq_all = [], [], []
    for kvh in kv_heads:
        sl = slice(kvh * LANES, (kvh + 1) * LANES)
        k_seq = [block_diag(kp_ref[:, sl])]
        v_seq = [block_diag(vp_ref[:, sl])]
        for t in range(nblk):
            rows = slice(t * BLOCK, (t + 1) * BLOCK)
            k_seq.append(block_diag(kc_ref[rows, sl]))
            v_seq.append(block_diag(vc_ref[rows, sl]))
        q_all = jnp.concatenate(
            [q_ref[t * BLOCK:(t + 1) * BLOCK, (2 * kvh + pr) * LANES:(2 * kvh + pr + 1) * LANES]
             for t in range(nblk) for pr in range(2)], axis=0)
        s_meta_all.append(_dot_nt(q_all, block_diag(km_ref[:, sl])))
        s_seq = []
        for j in range(nblk + 1):
            lo_blk, hi_blk = max(j - 1, 0), min(j, nblk - 1)
            s_seq.append(_dot_nt(q_all[lo_blk * pair_rows:(hi_blk + 1) * pair_rows], k_seq[j]))
        s_seq_all.append(s_seq)
        v_seq_all.append(v_seq)

    probs_all = []
    for kvh in kv_heads:
        s_meta, s_seq = s_meta_all[kvh], s_seq_all[kvh]

        def sink_col(hh):
            return jnp.where(top2, sink_ref[4 * kvh + hh], sink_ref[4 * kvh + 2 + hh])

        p_meta, p_prev, p_cur, l_all = [], [], [], []
        for t in range(nblk):
            sm = masked(s_meta[t * pair_rows:(t + 1) * pair_rows], mask_meta)
            prev_rows = slice(0, pair_rows) if t == 0 else slice(pair_rows, 2 * pair_rows)
            sb = jnp.where(mask_prev, s_seq[t][prev_rows], s_seq[t + 1][0:pair_rows])
            if t == 0:
                sb = masked(sb, jnp.logical_or(mask_cur, i > 0))
            s_max = jnp.maximum(sm, sb)
            m2 = []
            sink_terms = []
            for hh in range(2):
                sink = sink_col(hh)
                m = jnp.maximum(jnp.max(s_max[:, hh * LANES:(hh + 1) * LANES], axis=1, keepdims=True), sink)
                m2.append(m)
                sink_terms.append(jnp.exp(sink - m))

            def probs(s):
                return jnp.concatenate([jnp.exp(s[:, hh * LANES:(hh + 1) * LANES] - m2[hh])
                                        for hh in range(2)], axis=1)

            pm, pb = probs(sm), probs(sb)
            pb16 = pb.astype(BF16)
            p_meta.append(pm.astype(BF16))
            p_prev.append(jnp.where(mask_prev, pb16, zero))
            p_cur.append(jnp.where(mask_prev, zero, pb16))
            l_all.append(_dot((pm + pb).astype(BF16), ones_blk)
                         + jnp.where(low2, sink_terms[0], sink_terms[1]))
        probs_all.append((p_meta, p_prev, p_cur, l_all))

    for kvh in kv_heads:
        sl = slice(kvh * LANES, (kvh + 1) * LANES)
        p_meta, p_prev, p_cur, l_all = probs_all[kvh]
        v_seq = v_seq_all[kvh]
        o_meta = _dot(jnp.concatenate(p_meta, axis=0), block_diag(vm_ref[:, sl]))
        o_seq = []
        for j in range(nblk + 1):
            parts = ([p_cur[j - 1]] if j >= 1 else []) + ([p_prev[j]] if j < nblk else [])
            o_seq.append(_dot(jnp.concatenate(parts, axis=0) if len(parts) > 1 else parts[0], v_seq[j]))
        for t in range(nblk):
            prev_rows = slice(0, pair_rows) if t == 0 else slice(pair_rows, 2 * pair_rows)
            o = (o_meta[t * pair_rows:(t + 1) * pair_rows] + o_seq[t][prev_rows]
                 + o_seq[t + 1][0:pair_rows]) / l_all[t]
            for pr in range(2):
                col = (2 * kvh + pr) * LANES
                o_ref[t * BLOCK:(t + 1) * BLOCK, col:col + LANES] = o[pr * BLOCK:(pr + 1) * BLOCK].astype(BF16)


GLA_STEP = 2 * GLA_CHUNK
GLA_PAIR_K = 2 * GLA_KEY_DIM
GLA_PAIR_V = 2 * GLA_VAL_DIM


def _gla_consts():
    r = lax.broadcasted_iota(jnp.int32, (GLA_STEP, 2 * GLA_STEP), 0)
    c = lax.broadcasted_iota(jnp.int32, (GLA_STEP, 2 * GLA_STEP), 1) & (GLA_STEP - 1)
    causal = jnp.logical_and((r >> 6) == (c >> 6), c <= r)
    tri = jnp.where(causal[:, :GLA_STEP], 1.0, 0.0).astype(BF16)
    return causal, tri


def _gla_steps(units, states, causal, tri):
    row = lax.broadcasted_iota(jnp.int32, (GLA_STEP, GLA_PAIR_K), 0)
    lane = lax.broadcasted_iota(jnp.int32, (GLA_STEP, GLA_PAIR_K), 1)
    first = row < GLA_CHUNK
    low = lane < GLA_KEY_DIM
    lane_t = lax.broadcasted_iota(jnp.int32, (GLA_PAIR_K, GLA_STEP), 1)
    first_t = lane_t < GLA_CHUNK
    srow = lax.broadcasted_iota(jnp.int32, (GLA_PAIR_K, GLA_PAIR_V), 0)
    slane = lax.broadcasted_iota(jnp.int32, (GLA_PAIR_K, GLA_PAIR_V), 1)
    diag = (srow < GLA_KEY_DIM) == (slane < GLA_VAL_DIM)
    zero = jnp.zeros((), BF16)
    zf = jnp.zeros((), F32)
    zv = jnp.zeros((GLA_STEP, GLA_VAL_DIM), BF16)
    half = GLA_CHUNK // 2
    n = len(units)

    gk = jnp.concatenate([u[4] for u in units], axis=1) if n > 1 else units[0][4]
    g_hi = gk.astype(BF16)
    g_r = gk - g_hi.astype(F32)
    g_mid = g_r.astype(BF16)
    g_lo = (g_r - g_mid.astype(F32)).astype(BF16)
    b_all = _dot(tri, g_hi) + _dot(tri, g_mid) + _dot(tri, g_lo)

    prep = []
    for i, (_, q2, k2, v2, _) in enumerate(units):
        b = b_all[:, i * GLA_PAIR_K:(i + 1) * GLA_PAIR_K]
        b_mid = jnp.where(first, b[half:half + 1, :], b[GLA_CHUNK + half:GLA_CHUNK + half + 1, :])
        qi = (q2 * jnp.exp(b - b_mid)).astype(BF16)
        ki = (k2 * jnp.exp(b_mid - b)).astype(BF16)
        qx = q2 * jnp.exp(b)
        kblk = jnp.concatenate([jnp.where(low, ki, zero), jnp.where(low, zero, ki)], axis=0)
        vblk = jnp.concatenate([jnp.concatenate([v2[:, :GLA_VAL_DIM], zv], axis=1),
                                jnp.concatenate([zv, v2[:, GLA_VAL_DIM:]], axis=1)], axis=0)
        b_t = b.T
        k_t = k2.T
        bl_a = b_t[:, GLA_CHUNK - 1:GLA_CHUNK]
        bl_b = b_t[:, GLA_STEP - 1:GLA_STEP]
        kx_t = k_t * jnp.exp(jnp.where(first_t, bl_a, bl_b) - b_t)
        prep.append(dict(qi=qi, kblk=kblk, vblk=vblk,
                         qx_a=jnp.where(first, qx, zf).astype(BF16), qx_b=jnp.where(first, zf, qx).astype(BF16),
                         kx_a=jnp.where(first_t, kx_t, zf).astype(BF16),
                         kx_b=jnp.where(first_t, zf, kx_t).astype(BF16),
                         dec_a=jnp.exp(bl_a), dec_b=jnp.exp(bl_b)))

    atts = [jnp.where(causal, _dot_nt(p["qi"], p["kblk"]), 0.0).astype(BF16) for p in prep]
    kv_a = [jnp.where(diag, _dot(p["kx_a"], u[3]), 0.0) for p, u in zip(prep, units)]
    kv_b = [jnp.where(diag, _dot(p["kx_b"], u[3]), 0.0) for p, u in zip(prep, units)]
    outs = [_dot(a, p["vblk"]) for a, p in zip(atts, prep)]

    states = dict(states)
    for i, (pair, *_rest) in enumerate(units):
        p = prep[i]
        s_prev = states[pair]
        s_a = p["dec_a"] * s_prev + kv_a[i]
        outs[i] = outs[i] + _dot(p["qx_a"], s_prev.astype(BF16)) + _dot(p["qx_b"], s_a.astype(BF16))
        states[pair] = p["dec_b"] * s_a + kv_b[i]
    return outs, states


def _gla_init_body(k_ref, v_ref, gk_ref, s_out):
    causal, tri = _gla_consts()
    units = []
    for p in range(GLA_HEADS // 2):
        k2 = k_ref[:, p * GLA_PAIR_K:(p + 1) * GLA_PAIR_K].astype(F32)
        units.append((p, jnp.zeros_like(k2), k2, v_ref[:, p * GLA_PAIR_V:(p + 1) * GLA_PAIR_V],
                      gk_ref[:, p * GLA_PAIR_K:(p + 1) * GLA_PAIR_K]))
    zeros = {p: jnp.zeros((GLA_PAIR_K, GLA_PAIR_V), F32) for p in range(GLA_HEADS // 2)}
    _, states = _gla_steps(units, zeros, causal, tri)
    for p in range(GLA_HEADS // 2):
        s_out[p] = states[p]


def _gla_init_call(kg_m, vg_m, gk_m):
    return pl.pallas_call(
        _gla_init_body,
        out_shape=jax.ShapeDtypeStruct((GLA_HEADS // 2, GLA_PAIR_K, GLA_PAIR_V), F32),
        name="gla_init",
    )(kg_m, vg_m, gk_m)


def _gla_body(s0_ref, q_ref, k_ref, v_ref, gk_ref, rg_ref, nw_ref, o_ref, s_ref):
    t = pl.program_id(1)

    @pl.when(t == 0)
    def _():
        s_ref[...] = s0_ref[...]

    causal, tri = _gla_consts()
    nw = nw_ref[...]
    pairs = range(GLA_HEADS // 2)
    units = []
    where = []
    for u in range(GLA_GRID_STEPS):
        rows = slice(u * GLA_STEP, (u + 1) * GLA_STEP)
        for p in pairs:
            ks = slice(p * GLA_PAIR_K, (p + 1) * GLA_PAIR_K)
            vs = slice(p * GLA_PAIR_V, (p + 1) * GLA_PAIR_V)
            units.append((p, q_ref[rows, ks].astype(F32), k_ref[rows, ks].astype(F32), v_ref[rows, vs],
                          gk_ref[rows, ks]))
            where.append((rows, vs))
    outs, states = _gla_steps(units, {p: s_ref[p] for p in pairs}, causal, tri)
    for p in pairs:
        s_ref[p] = states[p]
    for o, (rows, vs) in zip(outs, where):
        heads = []
        for hh in range(2):
            oh = o[:, hh * GLA_VAL_DIM:(hh + 1) * GLA_VAL_DIM]
            ms = jnp.mean(oh * oh, axis=-1, keepdims=True)
            heads.append(oh * lax.rsqrt(ms + NORM_EPS) * nw)
        on = jnp.concatenate(heads, axis=1)
        o_ref[rows, vs] = (on * rg_ref[rows, vs].astype(F32)).astype(BF16)


GLA_GRID_STEPS = 4


def _mixers_body(sink_ref, q_ref, kp_ref, kc_ref, km_ref, vp_ref, vc_ref, vm_ref,
                 s0_ref, qg_ref, kg_ref, vg_ref, gk_ref, rg_ref, nw_ref, ao_ref, go_ref, s_ref):
    _gla_body(s0_ref, qg_ref, kg_ref, vg_ref, gk_ref, rg_ref, nw_ref, go_ref, s_ref)
    _attn_body(sink_ref, q_ref, kp_ref, kc_ref, km_ref, vp_ref, vc_ref, vm_ref, ao_ref)


def _mixers_call(q, kd, vd, km, vm, sinks, s0, qg, kg, vg, gk, rg, nw, batch, seq):
    step = ATTN_STEP_BLOCKS * BLOCK
    assert seq % step == 0 and step == GLA_GRID_STEPS * GLA_STEP
    seq_spec = lambda width: pl.BlockSpec((None, step, width), lambda b, i, s: (b, i, 0))
    prev = pl.BlockSpec((None, BLOCK, ATTN_KV_DUP_W),
                        lambda b, i, s: (b, jnp.maximum(i * ATTN_STEP_BLOCKS - 1, 0), 0))
    const = lambda shape: pl.BlockSpec(shape, lambda b, i, s: (0,) * len(shape))
    meta = const((BLOCK, ATTN_KV_DUP_W))
    grid_spec = pltpu.PrefetchScalarGridSpec(
        num_scalar_prefetch=1,
        grid=(batch, seq // step),
        in_specs=[seq_spec(ATTN_Q_W), prev, seq_spec(ATTN_KV_DUP_W), meta,
                  prev, seq_spec(ATTN_KV_DUP_W), meta,
                  const((GLA_HEADS // 2, GLA_PAIR_K, GLA_PAIR_V)),
                  seq_spec(GLA_K_W), seq_spec(GLA_K_W), seq_spec(GLA_V_W), seq_spec(GLA_K_W),
                  seq_spec(GLA_V_W), const((1, GLA_VAL_DIM))],
        out_specs=[seq_spec(ATTN_Q_W), seq_spec(GLA_V_W)],
        scratch_shapes=[pltpu.VMEM((GLA_HEADS // 2, GLA_PAIR_K, GLA_PAIR_V), F32)],
    )
    return pl.pallas_call(
        _mixers_body,
        grid_spec=grid_spec,
        out_shape=[jax.ShapeDtypeStruct((batch, seq, ATTN_Q_W), BF16),
                   jax.ShapeDtypeStruct((batch, seq, GLA_V_W), BF16)],
        compiler_params=pltpu.CompilerParams(dimension_semantics=("arbitrary", "arbitrary"),
                                             vmem_limit_bytes=VMEM_LIMIT),
        name="mixers",
    )(sinks, q, kd, kd, km, vd, vd, vm, s0, qg, kg, vg, gk, rg, nw)


ROUTER_W = LANES


RT_E0, RT_E1, RT_W0, RT_W1 = 0, 1, 2, 3


MERGE_SPLIT = 4


def _merge_body(x_ref, ao_ref, go_ref, sga_ref, sgg_ref, wab_ref, wgb_ref, wo_ref, n2_ref,
                wr2_ref, h_out, xn_out, rt_out, rtt_out, cnt_out):
    @pl.when(pl.program_id(0) == 0)
    def _():
        cnt_out[...] = jnp.zeros_like(cnt_out)

    rows_per = x_ref.shape[0] // MERGE_SPLIT
    for part in range(MERGE_SPLIT):
        _merge_rows(slice(part * rows_per, (part + 1) * rows_per), x_ref, ao_ref, go_ref, sga_ref, sgg_ref,
                    wab_ref, wgb_ref, wo_ref, n2_ref, wr2_ref, h_out, xn_out, rt_out, rtt_out, cnt_out)


def _merge_rows(rows, x_ref, ao_ref, go_ref, sga_ref, sgg_ref, wab_ref, wgb_ref, wo_ref, n2_ref,
                wr2_ref, h_out, xn_out, rt_out, rtt_out, cnt_out):
    ya = _dot(ao_ref[rows, :], wab_ref[...])
    yg = _dot(go_ref[rows, :], wgb_ref[...])
    merged = sga_ref[rows, :].astype(F32) * ya + sgg_ref[rows, :].astype(F32) * yg
    h = x_ref[rows, :] + _dot(merged.astype(BF16), wo_ref[...])
    xn = _rms_norm(h, n2_ref[...])
    for c in range(ROW_TILES):
        h_out[c, rows, :] = h[:, c * LANES:(c + 1) * LANES]
        xn_out[c, rows, :] = xn[:, c * LANES:(c + 1) * LANES]

    xh, xl = _split_hi_lo(xn)
    wr2 = wr2_ref[...]
    lg2 = _dot(xh, wr2) + _dot(xl, wr2)
    lg = lg2[:, :ROUTER_W] + lg2[:, ROUTER_W:]
    lg_t = lg.T
    tm = lg.shape[0]
    neg = -jnp.inf
    big = jnp.int32(1 << 20)
    gl = lg_t[N_EXPERTS:N_EXPERTS + N_GROUPS, :]
    grow = lax.broadcasted_iota(jnp.int32, (N_GROUPS, tm), 0)
    gmax = jnp.max(gl, axis=0, keepdims=True)
    g_idx = jnp.min(jnp.where(gl == gmax, grow, big), axis=0, keepdims=True)
    g_w = 1.0 / jnp.sum(jnp.exp(gl - gmax), axis=0, keepdims=True)
    erow = lax.broadcasted_iota(jnp.int32, (N_EXPERTS, tm), 0)
    el = jnp.where((erow >> 3) == g_idx, lg_t[0:N_EXPERTS, :], neg)
    m1 = jnp.max(el, axis=0, keepdims=True)
    i1 = jnp.min(jnp.where(el == m1, erow, big), axis=0, keepdims=True)
    el2 = jnp.where(erow == i1, neg, el)
    m2 = jnp.max(el2, axis=0, keepdims=True)
    i2 = jnp.min(jnp.where(el2 == m2, erow, big), axis=0, keepdims=True)
    e2 = jnp.exp(m2 - m1)
    p1 = g_w / (1.0 + e2)
    p2 = g_w * e2 / (1.0 + e2)
    def record(rows):
        r = lax.broadcasted_iota(jnp.int32, (rows, tm), 0)
        return jnp.where(r == RT_E0, i1.astype(F32),
                         jnp.where(r == RT_E1, i2.astype(F32),
                                   jnp.where(r == RT_W0, p1, jnp.where(r == RT_W1, p2, 0.0))))
    rtt_out[:, rows] = record(8)
    rt_out[rows, :] = record(LANES).T
    picked = jnp.where(jnp.logical_or(erow == i1, erow == i2), 1.0, 0.0)
    cnt_out[...] += jnp.sum(picked, axis=1, keepdims=True)


def _merge_call(x2d, ao, go, sga, sgg, w, tm):
    n = x2d.shape[0]
    row = lambda width: pl.BlockSpec((tm, width), lambda i: (i, 0))
    return pl.pallas_call(
        _merge_body,
        grid=(n // tm,),
        in_specs=[row(D_MODEL), row(ATTN_Q_W), row(GLA_V_W), row(D_MODEL), row(D_MODEL),
                  _const_spec((ATTN_Q_W, D_MODEL)), _const_spec((GLA_V_W, D_MODEL)),
                  _const_spec((D_MODEL, D_MODEL)), _const_spec((1, D_MODEL)),
                  _const_spec((D_MODEL, 2 * ROUTER_W))],
        out_specs=[pl.BlockSpec((ROW_TILES, tm, LANES), lambda i: (0, i, 0)),
                   pl.BlockSpec((ROW_TILES, tm, LANES), lambda i: (0, i, 0)), row(ROUTER_W),
                   pl.BlockSpec((8, tm), lambda i: (0, i)), _const_spec((N_EXPERTS, LANES))],
        out_shape=[jax.ShapeDtypeStruct((ROW_TILES, n, LANES), F32),
                   jax.ShapeDtypeStruct((ROW_TILES, n, LANES), F32),
                   jax.ShapeDtypeStruct((n, ROUTER_W), F32),
                   jax.ShapeDtypeStruct((8, n), F32),
                   jax.ShapeDtypeStruct((N_EXPERTS, LANES), F32)],
        compiler_params=pltpu.CompilerParams(dimension_semantics=("arbitrary",),
                                             vmem_limit_bytes=VMEM_LIMIT),
        name="merge",
    )(x2d, ao, go, sga, sgg, w["wab"], w["wgb"], w["wo"], w["norm2"], w["wr2"])


EXPERT_TILE = 512
EXPERT_SPLIT = 2
ROUTE_TILE = 512
VISIT_CAP = 256
VIS_EXPERT, VIS_TILE, VIS_LO, VIS_HI, VIS_COUNT = 0, 1, 2, 3, 4


def _num_visits(n_tokens):
    return (2 * n_tokens) // EXPERT_TILE + N_EXPERTS - 1


def _expert_cumsum_inclusive(x):
    row = lax.broadcasted_iota(jnp.int32, x.shape, 0)
    s = 1
    while s < N_EXPERTS:
        x = x + jnp.where(row >= s, pltpu.roll(x, s, 0), 0.0)
        s *= 2
    return x


def _route_body(rtt_ref, cnt_ref, pos_ref, vis_ref, carry_ref):
    i = pl.program_id(0)
    tm = rtt_ref.shape[1]
    cnt = cnt_ref[...]
    end = _expert_cumsum_inclusive(cnt)
    base = end - cnt

    @pl.when(i == 0)
    def _():
        carry_ref[...] = jnp.zeros_like(carry_ref)
        inv = 1.0 / EXPERT_TILE
        first = jnp.floor(base * inv)
        n_vis = jnp.where(cnt > 0.0, jnp.floor((end - 1.0) * inv) - first + 1.0, 0.0)
        vend = _expert_cumsum_inclusive(n_vis)
        vstart = vend - n_vis
        col = lambda a: a[:, 0:1]
        erow = lax.broadcasted_iota(jnp.int32, (N_EXPERTS, VISIT_CAP), 0).astype(F32)
        v = lax.broadcasted_iota(jnp.int32, (N_EXPERTS, VISIT_CAP), 1).astype(F32)
        ev = jnp.minimum(jnp.sum(jnp.where(col(vend) <= v, 1.0, 0.0), axis=0, keepdims=True),
                         N_EXPERTS - 1.0)
        mine = erow == ev
        pick = lambda a: jnp.sum(jnp.where(mine, a, 0.0), axis=0, keepdims=True)
        tile = pick(col(first) + v - col(vstart))
        lo = jnp.maximum(pick(col(base)) - tile * EXPERT_TILE, 0.0)
        hi = jnp.minimum(pick(col(end)) - tile * EXPERT_TILE, float(EXPERT_TILE))
        total = vend[N_EXPERTS - 1:N_EXPERTS, 0:1]
        r8 = lax.broadcasted_iota(jnp.int32, (8, VISIT_CAP), 0)
        rec = jnp.where(r8 == VIS_EXPERT, ev,
                        jnp.where(r8 == VIS_TILE, tile,
                                  jnp.where(r8 == VIS_LO, lo,
                                            jnp.where(r8 == VIS_HI, hi,
                                                      jnp.where(r8 == VIS_COUNT, total, 0.0)))))
        vis_ref[...] = rec.astype(jnp.int32)

    rtt = rtt_ref[...]
    erow = lax.broadcasted_iota(jnp.int32, (N_EXPERTS, tm), 0).astype(F32)
    oh0 = erow == rtt[RT_E0:RT_E0 + 1, :]
    oh1 = erow == rtt[RT_E1:RT_E1 + 1, :]
    picked = jnp.where(jnp.logical_or(oh0, oh1), 1.0, 0.0)
    r = lax.broadcasted_iota(jnp.int32, (tm, tm), 0)
    c = lax.broadcasted_iota(jnp.int32, (tm, tm), 1)
    earlier = jnp.where(r < c, 1.0, 0.0).astype(BF16)
    slot = _dot(picked.astype(BF16), earlier) + (carry_ref[:, 0:1] + base[:, 0:1])
    p0 = jnp.sum(jnp.where(oh0, slot, 0.0), axis=0, keepdims=True)
    p1 = jnp.sum(jnp.where(oh1, slot, 0.0), axis=0, keepdims=True)
    r8 = lax.broadcasted_iota(jnp.int32, (8, tm), 0)
    pos_ref[...] = jnp.where(r8 == 0, p0, jnp.where(r8 == 1, p1, 0.0)).astype(jnp.int32)
    carry_ref[...] += jnp.sum(picked, axis=1, keepdims=True)


def _route_call(rtt, cnt):
    n = rtt.shape[1]
    tm = _pick_tile(n, ROUTE_TILE)
    return pl.pallas_call(
        _route_body,
        grid=(n // tm,),
        in_specs=[pl.BlockSpec((8, tm), lambda i: (0, i)), _const_spec((N_EXPERTS, LANES))],
        out_specs=[pl.BlockSpec((8, tm), lambda i: (0, i)), _const_spec((8, VISIT_CAP))],
        out_shape=[jax.ShapeDtypeStruct((8, n), jnp.int32),
                   jax.ShapeDtypeStruct((8, VISIT_CAP), jnp.int32)],
        scratch_shapes=[pltpu.VMEM((N_EXPERTS, LANES), F32)],
        compiler_params=pltpu.CompilerParams(dimension_semantics=("arbitrary",)),
        name="route",
    )(rtt, cnt)


SCATTER_TILE = 512
ROW_TILES = D_MODEL // LANES


def _store_slabs(ref, val):
    for c in range(ROW_TILES):
        ref[c] = val[:, c * LANES:(c + 1) * LANES]


def _load_slabs(ref):
    return jnp.concatenate([ref[c] for c in range(ROW_TILES)], axis=1)


def _scatter_body(pos0_ref, pos1_ref, h_ref, hs_hbm, sem):
    tm = h_ref.shape[1]
    for r in range(tm):
        src = h_ref.at[:, r, :]
        pltpu.make_async_copy(src, hs_hbm.at[:, pos0_ref[r], :], sem).start(priority=0)
        pltpu.make_async_copy(src, hs_hbm.at[:, pos1_ref[r], :], sem).start(priority=1)
    for _ in range(2 * tm):
        pltpu.make_async_copy(h_ref.at[:, 0, :], hs_hbm.at[:, 0, :], sem).wait()


def _scatter_call(pos0, pos1, h_slabs, n_slots):
    n = h_slabs.shape[1]
    tm = _pick_tile(n, SCATTER_TILE)
    smem = lambda: pl.BlockSpec((tm,), lambda i: (i,), memory_space=pltpu.SMEM)
    return pl.pallas_call(
        _scatter_body,
        grid=(n // tm,),
        in_specs=[smem(), smem(), pl.BlockSpec((ROW_TILES, tm, LANES), lambda i: (0, i, 0))],
        out_specs=pl.BlockSpec(memory_space=pl.ANY),
        out_shape=jax.ShapeDtypeStruct((ROW_TILES, n_slots, LANES), F32),
        scratch_shapes=[pltpu.SemaphoreType.DMA],
        compiler_params=pltpu.CompilerParams(dimension_semantics=("arbitrary",)),
        name="scatter",
    )(pos0, pos1, h_slabs)


EXPERT_RING = 3


def _expert_body(vis_ref, xs_hbm, wg_ref, wu_ref, wd_ref, o_ref, x_ring, sem):
    j = pl.program_id(0)
    count = vis_ref[VIS_COUNT, 0]

    def tile_copy(v, slot):
        start = pl.multiple_of(vis_ref[VIS_TILE, v] * EXPERT_TILE, EXPERT_TILE)
        return pltpu.make_async_copy(xs_hbm.at[:, pl.ds(start, EXPERT_TILE), :], x_ring.at[slot],
                                     sem.at[slot])

    @pl.when(j == 0)
    def _():
        for v in range(EXPERT_RING - 1):
            @pl.when(v < count)
            def _():
                tile_copy(v, v).start()

    ahead = j + EXPERT_RING - 1

    @pl.when(ahead < count)
    def _():
        tile_copy(ahead, ahead % EXPERT_RING).start()

    @pl.when(j < count)
    def _():
        slot = j % EXPERT_RING
        tile_copy(j, slot).wait()
        x_ref = x_ring.at[slot]
        lo_row = vis_ref[VIS_LO, j]
        hi_row = vis_ref[VIS_HI, j]

        @pl.when(lo_row == 0)
        def _():
            o_ref[...] = jnp.zeros_like(o_ref)

        wg = wg_ref[...].astype(BF16)
        wu = wu_ref[...].astype(BF16)
        wd = wd_ref[...].astype(BF16)
        rows_per = EXPERT_TILE // EXPERT_SPLIT
        for part in range(EXPERT_SPLIT):
            rows = slice(part * rows_per, (part + 1) * rows_per)
            x = jnp.concatenate([x_ref[c, rows, :] for c in range(ROW_TILES)], axis=1).astype(BF16)
            g = _dot(x, wg)
            u = _dot(x, wu)
            hg = (g * _sigmoid(g) * u).astype(BF16)
            new = _dot(hg, wd)
            row = lax.broadcasted_iota(jnp.int32, (rows_per, LANES), 0) + part * rows_per
            keep = jnp.logical_and(row >= lo_row, row < hi_row)
            for c in range(ROW_TILES):
                o_ref[c, rows, :] = jnp.where(keep, new[:, c * LANES:(c + 1) * LANES], o_ref[c, rows, :])


def _expert_call(vis, hs, wg, wu, wd, n_visits):
    def visit(j, vis):
        return jnp.minimum(j, vis[VIS_COUNT, 0] - 1)

    def xmap(j, vis):
        return (0, vis[VIS_TILE, visit(j, vis)], 0)

    def wmap(j, vis):
        return (vis[VIS_EXPERT, visit(j, vis)], 0, 0)

    grid_spec = pltpu.PrefetchScalarGridSpec(
        num_scalar_prefetch=1,
        grid=(n_visits,),
        in_specs=[pl.BlockSpec(memory_space=pl.ANY),
                  pl.BlockSpec((None, D_MODEL, EXPERT_FF), wmap),
                  pl.BlockSpec((None, D_MODEL, EXPERT_FF), wmap),
                  pl.BlockSpec((None, EXPERT_FF, D_MODEL), wmap)],
        out_specs=pl.BlockSpec((ROW_TILES, EXPERT_TILE, LANES), xmap),
        scratch_shapes=[pltpu.VMEM((EXPERT_RING, ROW_TILES, EXPERT_TILE, LANES), F32),
                        pltpu.SemaphoreType.DMA((EXPERT_RING,))],
    )
    return pl.pallas_call(
        _expert_body,
        grid_spec=grid_spec,
        out_shape=jax.ShapeDtypeStruct(hs.shape, F32),
        compiler_params=pltpu.CompilerParams(dimension_semantics=("arbitrary",),
                                             vmem_limit_bytes=VMEM_LIMIT),
        name="experts",
    )(vis, hs, wg, wu, wd)


COMBINE_TILE = 256


def _combine_body(p0_ref, p1_ref, p0n_ref, p1n_ref, h_ref, rt_ref, y_hbm, o_ref, g_ref, sem):
    i = pl.program_id(0)
    tm = h_ref.shape[1]
    cur = i % 2

    def issue(q0_ref, q1_ref, buf):
        for r in range(tm):
            pltpu.make_async_copy(y_hbm.at[:, q0_ref[r], :], g_ref.at[buf, 0, :, r, :],
                                  sem.at[buf]).start(priority=0)
            pltpu.make_async_copy(y_hbm.at[:, q1_ref[r], :], g_ref.at[buf, 1, :, r, :],
                                  sem.at[buf]).start(priority=1)

    @pl.when(i == 0)
    def _():
        issue(p0_ref, p1_ref, 0)

    @pl.when(i + 1 < pl.num_programs(0))
    def _():
        issue(p0n_ref, p1n_ref, 1 - cur)

    for _ in range(2 * tm):
        pltpu.make_async_copy(y_hbm.at[:, 0, :], g_ref.at[cur, 0, :, 0, :], sem.at[cur]).wait()
    rt = rt_ref[...]
    o_ref[...] = (_load_slabs(h_ref) + rt[:, RT_W0:RT_W0 + 1] * _load_slabs(g_ref.at[cur, 0])
                  + rt[:, RT_W1:RT_W1 + 1] * _load_slabs(g_ref.at[cur, 1]))


def _combine_call(pos0, pos1, h_slabs, rt, y):
    n = h_slabs.shape[1]
    tm = _pick_tile(n, COMBINE_TILE)
    steps = n // tm
    smem_cur = lambda: pl.BlockSpec((tm,), lambda i: (i,), memory_space=pltpu.SMEM)
    smem_next = lambda: pl.BlockSpec((tm,), lambda i: (jnp.minimum(i + 1, steps - 1),),
                                     memory_space=pltpu.SMEM)
    row = lambda width: pl.BlockSpec((tm, width), lambda i: (i, 0))
    return pl.pallas_call(
        _combine_body,
        grid=(steps,),
        in_specs=[smem_cur(), smem_cur(), smem_next(), smem_next(),
                  pl.BlockSpec((ROW_TILES, tm, LANES), lambda i: (0, i, 0)), row(ROUTER_W),
                  pl.BlockSpec(memory_space=pl.ANY)],
        out_specs=row(D_MODEL),
        out_shape=jax.ShapeDtypeStruct((n, D_MODEL), F32),
        scratch_shapes=[pltpu.VMEM((2, 2, ROW_TILES, tm, LANES), F32), pltpu.SemaphoreType.DMA((2,))],
        compiler_params=pltpu.CompilerParams(dimension_semantics=("arbitrary",),
                                             vmem_limit_bytes=VMEM_LIMIT),
        name="combine",
    )(pos0, pos1, pos0, pos1, h_slabs, rt, y)


def _rope_tables(pos):
    half = HEAD_DIM // 2
    inv_freq = ROPE_THETA ** (-(jnp.arange(half, dtype=F32) * 2.0) / HEAD_DIM)
    ang = pos.astype(F32)[:, None] * inv_freq[None, :]
    cos = jnp.cos(ang)
    sin = jnp.sin(ang)
    cos64 = jnp.concatenate([cos, cos], axis=1)
    sin64 = jnp.concatenate([-sin, sin], axis=1)
    return jnp.tile(cos64, (1, LANES // HEAD_DIM)), jnp.tile(sin64, (1, LANES // HEAD_DIM))


WPREP_ROWS = 128


def _wprep_body(w_ref, qkv_out, gla_out, gate_out):
    w = w_ref[...]
    lane = lax.broadcasted_iota(jnp.int32, (WPREP_ROWS, LANES), 1)
    qkv_out[:, :QKV_GD_OFF] = w[:, :QKV_GD_OFF].astype(BF16)
    gd_tile = w[:, _OFF_GD:_OFF_GD + LANES]
    qkv_out[:, QKV_GD_OFF:QKV_GD_OFF + LANES] = jnp.where(lane < GLA_GATE_RANK, gd_tile, 0.0).astype(BF16)
    qkv_out[:, QKV_GD_OFF + LANES:] = jnp.zeros((WPREP_ROWS, QKV_W - QKV_GD_OFF - LANES), BF16)
    gla_out[...] = w[:, _OFF_GQ:_OFF_GQ + GLA_W].astype(BF16)
    keep = LANES - GLA_GATE_RANK
    n_tiles = 2 * D_MODEL // LANES
    rolled = [pltpu.roll(w[:, _OFF_GD + t * LANES:_OFF_GD + (t + 1) * LANES], keep, 1)
              for t in range(n_tiles)]
    tail = w[:, _OFF_GD + n_tiles * LANES:]
    tail = jnp.concatenate([tail, jnp.zeros((WPREP_ROWS, LANES - tail.shape[1]), F32)], axis=1)
    rolled.append(pltpu.roll(tail, keep, 1))
    for t in range(n_tiles):
        gate_out[:, t * LANES:(t + 1) * LANES] = jnp.where(lane < keep, rolled[t], rolled[t + 1]).astype(BF16)


def _wprep_call(w_in):
    in_total = w_in.shape[2]
    assert in_total == _OFF_GATE + 2 * D_MODEL and D_MODEL % WPREP_ROWS == 0
    row = lambda width: pl.BlockSpec((WPREP_ROWS, width), lambda i: (i, 0))
    return pl.pallas_call(
        _wprep_body,
        grid=(D_MODEL // WPREP_ROWS,),
        in_specs=[pl.BlockSpec((None, WPREP_ROWS, in_total), lambda i: (0, i, 0))],
        out_specs=[row(QKV_W), row(GLA_W), row(2 * D_MODEL)],
        out_shape=[jax.ShapeDtypeStruct((D_MODEL, QKV_W), BF16),
                   jax.ShapeDtypeStruct((D_MODEL, GLA_W), BF16),
                   jax.ShapeDtypeStruct((D_MODEL, 2 * D_MODEL), BF16)],
        compiler_params=pltpu.CompilerParams(dimension_semantics=("arbitrary",)),
        name="wprep",
    )(w_in)


def _prep_weights(norm1_w, w_in, q_norm_w, k_norm_w, gla_gate_up, gla_gate_bias, w_attn_branch,
                  w_gla_branch, w_out, norm2_w, router_group, router_expert):
    wqkv, wgla, wgate = _wprep_call(w_in)
    gup = jnp.zeros((LANES, GLA_K_W), BF16).at[:GLA_GATE_RANK].set(gla_gate_up.astype(BF16))
    wr = jnp.zeros((D_MODEL, ROUTER_W), F32)
    wr = wr.at[:, :N_EXPERTS].set(router_expert.transpose(1, 0, 2).reshape(D_MODEL, N_EXPERTS))
    wr = wr.at[:, N_EXPERTS:N_EXPERTS + N_GROUPS].set(router_group)
    wrh = wr.astype(BF16)
    wrl = (wr - wrh.astype(F32)).astype(BF16)
    return {
        "norm1": norm1_w.reshape(1, D_MODEL),
        "wqkv": wqkv,
        "wgla": wgla,
        "wgate": wgate,
        "qn": jnp.tile(q_norm_w, ATTN_HEADS).reshape(1, ATTN_Q_W),
        "kn": jnp.tile(k_norm_w, ATTN_KV_HEADS).reshape(1, ATTN_KV_W),
        "gup": gup,
        "gb": gla_gate_bias.reshape(1, GLA_K_W),
        "wab": w_attn_branch.astype(BF16),
        "wgb": w_gla_branch.astype(BF16),
        "wo": w_out.astype(BF16),
        "norm2": norm2_w.reshape(1, D_MODEL),
        "wr2": jnp.concatenate([wrh, wrl], axis=1),
    }


def _pick_tile(n, want):
    t = want
    while n % t:
        t //= 2
    return t


def kernel(x, meta_tokens, norm1_w, w_in, q_norm_w, k_norm_w, attn_sinks, gla_gate_up, gla_gate_bias,
           gla_norm_w, w_attn_branch, w_gla_branch, w_out, norm2_w, router_group, router_expert,
           expert_w_gate, expert_w_up, expert_w_down):
    batch, seq, d = x.shape
    assert d == D_MODEL and seq % GLA_STEP == 0 and norm1_w.shape[0] == 1
    n = batch * seq
    w = _prep_weights(norm1_w[0], w_in, q_norm_w[0], k_norm_w[0], gla_gate_up[0], gla_gate_bias[0],
                      w_attn_branch[0], w_gla_branch[0], w_out[0], norm2_w[0], router_group[0],
                      router_expert[0])
    x2d = x.reshape(n, D_MODEL)

    cos_m, sin_m = _rope_tables(jnp.arange(N_META))
    cos_r, sin_r = _rope_tables(jnp.arange(seq) + N_META)
    meta = _proj_call(meta_tokens.astype(F32), N_META, 1, w, cos_m, sin_m)
    tm = _pick_tile(seq, 512)
    (q, kd, vd, qg, kg, vg, rg, gk, sga, sgg) = _proj_call(x2d, tm, seq // tm, w, cos_r, sin_r)

    pad_after = ((0, BLOCK - N_META), (0, 0))
    km = jnp.pad(meta[1], pad_after)
    vm = jnp.pad(meta[2], pad_after)
    pad_before = ((GLA_STEP - N_META, 0), (0, 0))
    s0 = _gla_init_call(jnp.pad(meta[4], pad_before), jnp.pad(meta[5], pad_before),
                        jnp.pad(meta[7], pad_before))
    ao, go = _mixers_call(q.reshape(batch, seq, ATTN_Q_W), kd.reshape(batch, seq, ATTN_KV_DUP_W),
                          vd.reshape(batch, seq, ATTN_KV_DUP_W), km, vm, attn_sinks[0].astype(F32),
                          s0, qg.reshape(batch, seq, GLA_K_W), kg.reshape(batch, seq, GLA_K_W),
                          vg.reshape(batch, seq, GLA_V_W), gk.reshape(batch, seq, GLA_K_W),
                          rg.reshape(batch, seq, GLA_V_W), gla_norm_w[0].reshape(1, GLA_VAL_DIM),
                          batch, seq)

    h, xn2, rt, rtt, cnt = _merge_call(x2d, ao.reshape(n, ATTN_Q_W), go.reshape(n, GLA_V_W), sga, sgg, w,
                                       _pick_tile(n, 1024))

    assert (2 * n) % EXPERT_TILE == 0 and _num_visits(n) <= VISIT_CAP
    pos, vis = _route_call(rtt, cnt)
    pos0, pos1 = pos[0], pos[1]
    xs = _scatter_call(pos0, pos1, xn2, 2 * n)
    wg = expert_w_gate[0].reshape(N_EXPERTS, D_MODEL, EXPERT_FF)
    wu = expert_w_up[0].reshape(N_EXPERTS, D_MODEL, EXPERT_FF)
    wd = expert_w_down[0].reshape(N_EXPERTS, EXPERT_FF, D_MODEL)
    y = _expert_call(vis, xs, wg, wu, wd, _num_visits(n))
    out = _combine_call(pos0, pos1, h, rt, y)
    return out.reshape(batch, seq, D_MODEL)
```

```python
import math

import jax
import jax.numpy as jnp
from jax import lax
from jax.experimental import pallas as pl
from jax.experimental.pallas import tpu as pltpu

F32 = jnp.float32
BF16 = jnp.bfloat16

D_MODEL = 1024
N_META = 16
BLOCK = 128
ATTN_HEADS = 8
ATTN_KV_HEADS = 2
HEAD_DIM = 64
ROPE_THETA = 10000.0
ATTN_Q_W = ATTN_HEADS * HEAD_DIM
ATTN_KV_W = ATTN_KV_HEADS * HEAD_DIM
GLA_HEADS = 4
GLA_KEY_DIM = 64
GLA_VAL_DIM = 128
GLA_K_W = GLA_HEADS * GLA_KEY_DIM
GLA_V_W = GLA_HEADS * GLA_VAL_DIM
GLA_GATE_RANK = 16
GLA_GATE_NORM = 16.0
GLA_CHUNK = 64
N_GROUPS = 4
EXPERTS_PER_GROUP = 8
N_EXPERTS = N_GROUPS * EXPERTS_PER_GROUP
EXPERT_FF = 256
NORM_EPS = 1e-6
MASK_VALUE = -1e30

LANES = 128
ATTN_KV_DUP_W = 2 * ATTN_KV_W
VMEM_LIMIT = 56 * 1024 * 1024

_OFF_Q = 0
_OFF_K = _OFF_Q + ATTN_Q_W
_OFF_V = _OFF_K + ATTN_KV_W
_OFF_GQ = _OFF_V + ATTN_KV_W
_OFF_GD = _OFF_GQ + 2 * GLA_K_W + 2 * GLA_V_W
_OFF_GATE = _OFF_GD + GLA_GATE_RANK
GLA_W = 2 * GLA_K_W + 2 * GLA_V_W
QKV_GD_OFF = _OFF_GQ
QKV_W = D_MODEL


def _dot(a, b):
    return jnp.dot(a, b, preferred_element_type=F32)


def _dot_nt(a, b):
    return lax.dot_general(a, b, (((1,), (1,)), ((), ())), preferred_element_type=F32)


def _split_hi_lo(x):
    hi = x.astype(BF16)
    lo = (x - hi.astype(F32)).astype(BF16)
    return hi, lo


MXU_TILE = 256


def _group_mean_sq(x, group):
    w = x.shape[-1]
    slab = min(w, MXU_TILE)
    shift = int(math.log2(group))
    r = lax.broadcasted_iota(jnp.int32, (slab, slab), 0) >> shift
    c = lax.broadcasted_iota(jnp.int32, (slab, slab), 1) >> shift
    ones = jnp.where(r == c, 1.0, 0.0).astype(BF16)
    hi, lo = _split_hi_lo(x * x)
    sums = [_dot(hi[:, s:s + slab], ones) + _dot(lo[:, s:s + slab], ones) for s in range(0, w, slab)]
    return (sums[0] if len(sums) == 1 else jnp.concatenate(sums, axis=1)) * (1.0 / group)


def _rope(x, cos, sin_signed):
    w = x.shape[-1]
    reps = w // LANES
    if reps > 1:
        cos = jnp.concatenate([cos] * reps, axis=1)
        sin_signed = jnp.concatenate([sin_signed] * reps, axis=1)
    lane = lax.broadcasted_iota(jnp.int32, x.shape, 1)
    first_half = (lane & (HEAD_DIM // 2)) == 0
    swapped = jnp.where(first_half, pltpu.roll(x, w - HEAD_DIM // 2, 1), pltpu.roll(x, HEAD_DIM // 2, 1))
    return x * cos + swapped * sin_signed


def _rms_norm(x, w):
    ms = jnp.mean(x * x, axis=-1, keepdims=True)
    return x * lax.rsqrt(ms + NORM_EPS) * w


def _sigmoid(x):
    return 0.5 * jnp.tanh(0.5 * x) + 0.5


def _log_sigmoid(x):
    return jnp.minimum(x, 0.0) - jnp.log(1.0 + jnp.exp(-jnp.abs(x)))


def _dup_heads_on_lanes(x):
    low = lax.broadcasted_iota(jnp.int32, x.shape, 1) < HEAD_DIM
    swapped = pltpu.roll(x, HEAD_DIM, 1)
    return jnp.concatenate([jnp.where(low, x, swapped), jnp.where(low, swapped, x)], axis=1)


def _proj_body(x_ref, n1_ref, wqkv_ref, wgla_ref, wgate_ref,
               qn_ref, kn_ref, cos_ref, sin_ref, gup_ref, gb_ref,
               q_out, k_out, v_out, qg_out, kg_out, vg_out, rg_out, gk_out, sga_out, sgg_out):
    xn = _rms_norm(x_ref[...], n1_ref[...]).astype(BF16)
    cos = cos_ref[...]
    sin = sin_ref[...]

    a = _dot(xn, wqkv_ref[...])
    q = a[:, :ATTN_Q_W]
    q = q * lax.rsqrt(_group_mean_sq(q, HEAD_DIM) + NORM_EPS) * qn_ref[...]
    q_out[...] = (_rope(q, cos, sin) * (HEAD_DIM ** -0.5)).astype(BF16)

    k = a[:, _OFF_K:_OFF_K + ATTN_KV_W]
    k = k * lax.rsqrt(_group_mean_sq(k, HEAD_DIM) + NORM_EPS) * kn_ref[...]
    k_out[...] = _dup_heads_on_lanes(_rope(k, cos, sin)).astype(BF16)

    v_out[...] = _dup_heads_on_lanes(a[:, _OFF_V:_OFF_V + ATTN_KV_W]).astype(BF16)

    g = _dot(xn, wgla_ref[...])
    qg_out[...] = (g[:, :GLA_K_W] * (GLA_KEY_DIM ** -0.5)).astype(BF16)
    kg_out[...] = g[:, GLA_K_W:2 * GLA_K_W].astype(BF16)
    vg_out[...] = g[:, 2 * GLA_K_W:2 * GLA_K_W + GLA_V_W].astype(BF16)
    r = g[:, 2 * GLA_K_W + GLA_V_W:]
    rg_out[...] = (r * _sigmoid(r)).astype(BF16)

    gd = a[:, QKV_GD_OFF:QKV_GD_OFF + LANES].astype(BF16)
    z = _dot(gd, gup_ref[...]) + gb_ref[...]
    gk_out[...] = _log_sigmoid(z) * (1.0 / GLA_GATE_NORM)

    gates = _dot(xn, wgate_ref[...])
    sga_out[...] = _sigmoid(gates[:, :D_MODEL]).astype(BF16)
    sgg_out[...] = _sigmoid(gates[:, D_MODEL:]).astype(BF16)


def _const_spec(shape):
    nd = len(shape)
    return pl.BlockSpec(shape, lambda *_: (0,) * nd)


def _proj_call(x2d, tm, tiles_per_seq, w, cos, sin):
    n = x2d.shape[0]
    grid = (n // tm,)
    row = lambda width: pl.BlockSpec((tm, width), lambda i: (i, 0))
    tab = pl.BlockSpec((tm, LANES), lambda i: (i % tiles_per_seq, 0))
    in_specs = [
        row(D_MODEL), _const_spec((1, D_MODEL)),
        _const_spec((D_MODEL, QKV_W)), _const_spec((D_MODEL, GLA_W)),
        _const_spec((D_MODEL, 2 * D_MODEL)),
        _const_spec((1, ATTN_Q_W)), _const_spec((1, ATTN_KV_W)),
        tab, tab, _const_spec((LANES, GLA_K_W)), _const_spec((1, GLA_K_W)),
    ]
    out_widths = [ATTN_Q_W, ATTN_KV_DUP_W, ATTN_KV_DUP_W, GLA_K_W, GLA_K_W, GLA_V_W, GLA_V_W,
                  GLA_K_W, D_MODEL, D_MODEL]
    out_dtypes = [BF16] * 7 + [F32] + [BF16] * 2
    return pl.pallas_call(
        _proj_body,
        grid=grid,
        in_specs=in_specs,
        out_specs=[row(wd) for wd in out_widths],
        out_shape=[jax.ShapeDtypeStruct((n, wd), dt) for wd, dt in zip(out_widths, out_dtypes)],
        compiler_params=pltpu.CompilerParams(dimension_semantics=("arbitrary",),
                                             vmem_limit_bytes=VMEM_LIMIT),
        name="proj",
    )(x2d, w["norm1"], w["wqkv"], w["wgla"], w["wgate"],
      w["qn"], w["kn"], cos, sin, w["gup"], w["gb"])


ATTN_STEP_BLOCKS = 8


def _attn_body(sink_ref, q_ref, kp_ref, kc_ref, km_ref, vp_ref, vc_ref, vm_ref, o_ref):
    i = pl.program_id(1)
    nblk = ATTN_STEP_BLOCKS
    pair_rows = 2 * BLOCK
    lane = lax.broadcasted_iota(jnp.int32, (BLOCK, LANES), 1)
    low = lane < HEAD_DIM
    low2 = lax.broadcasted_iota(jnp.int32, (pair_rows, LANES), 1) < HEAD_DIM
    top2 = lax.broadcasted_iota(jnp.int32, (pair_rows, 1), 0) < BLOCK
    key = lax.broadcasted_iota(jnp.int32, (pair_rows, 2 * LANES), 1) & (LANES - 1)
    rowi = lax.broadcasted_iota(jnp.int32, (pair_rows, 2 * LANES), 0) & (BLOCK - 1)
    mask_meta = key < N_META
    mask_prev = key > rowi
    mask_cur = key <= rowi
    zero = jnp.zeros((), BF16)

    def block_diag(x):
        return jnp.concatenate([jnp.where(low, x, zero), jnp.where(low, zero, x)], axis=0)

    blk_row = lax.broadcasted_iota(jnp.int32, (2 * BLOCK, LANES), 0)
    blk_lane = lax.broadcasted_iota(jnp.int32, (2 * BLOCK, LANES), 1)
    ones_blk = jnp.where((blk_row < BLOCK) == (blk_lane < HEAD_DIM), 1.0, 0.0).astype(BF16)

    def masked(s, mask):
        return jnp.where(mask, s, MASK_VALUE)

    kv_heads = range(ATTN_KV_HEADS)
    s_meta_all, s_seq_all, v_seq_all = [], [], []
    for kvh in kv_heads:
        sl = slice(kvh * LANES, (kvh + 1) * LANES)
        k_seq = [block_diag(kp_ref[:, sl])]
        v_seq = [block_diag(vp_ref[:, sl])]
        for t in range(nblk):
            rows = slice(t * BLOCK, (t + 1) * BLOCK)
            k_seq.append(block_diag(kc_ref[rows, sl]))
            v_seq.append(block_diag(vc_ref[rows, sl]))
        q_all = jnp.concatenate(
            [q_ref[t * BLOCK:(t + 1) * BLOCK, (2 * kvh + pr) * LANES:(2 * kvh + pr + 1) * LANES]
             for t in range(nblk) for pr in range(2)], axis=0)
        s_meta_all.append(_dot_nt(q_all, block_diag(km_ref[:, sl])))
        s_seq = []
        for j in range(nblk + 1):
            lo_blk, hi_blk = max(j - 1, 0), min(j, nblk - 1)
            s_seq.append(_dot_nt(q_all[lo_blk * pair_rows:(hi_blk + 1) * pair_rows], k_seq[j]))
        s_seq_all.append(s_seq)
        v_seq_all.append(v_seq)

    probs_all = []
    for kvh in kv_heads:
        s_meta, s_seq = s_meta_all[kvh], s_seq_all[kvh]

        def sink_col(hh):
            return jnp.where(top2, sink_ref[4 * kvh + hh], sink_ref[4 * kvh + 2 + hh])

        p_meta, p_prev, p_cur, l_all = [], [], [], []
        for t in range(nblk):
            sm = masked(s_meta[t * pair_rows:(t + 1) * pair_rows], mask_meta)
            prev_rows = slice(0, pair_rows) if t == 0 else slice(pair_rows, 2 * pair_rows)
            sb = jnp.where(mask_prev, s_seq[t][prev_rows], s_seq[t + 1][0:pair_rows])
            if t == 0:
                sb = masked(sb, jnp.logical_or(mask_cur, i > 0))
            s_max = jnp.maximum(sm, sb)
            m2 = []
            sink_terms = []
            for hh in range(2):
                sink = sink_col(hh)
                m = jnp.maximum(jnp.max(s_max[:, hh * LANES:(hh + 1) * LANES], axis=1, keepdims=True), sink)
                m2.append(m)
                sink_terms.append(jnp.exp(sink - m))

            def probs(s):
                return jnp.concatenate([jnp.exp(s[:, hh * LANES:(hh + 1) * LANES] - m2[hh])
                                        for hh in range(2)], axis=1)

            pm, pb = probs(sm), probs(sb)
            pb16 = pb.astype(BF16)
            p_meta.append(pm.astype(BF16))
            p_prev.append(jnp.where(mask_prev, pb16, zero))
            p_cur.append(jnp.where(mask_prev, zero, pb16))
            l_all.append(_dot((pm + pb).astype(BF16), ones_blk)
                         + jnp.where(low2, sink_terms[0], sink_terms[1]))
        probs_all.append((p_meta, p_prev, p_cur, l_all))

    for kvh in kv_heads:
        sl = slice(kvh * LANES, (kvh + 1) * LANES)
        p_meta, p_prev, p_cur, l_all = probs_all[kvh]
        v_seq = v_seq_all[kvh]
        o_meta = _dot(jnp.concatenate(p_meta, axis=0), block_diag(vm_ref[:, sl]))
        o_seq = []
        for j in range(nblk + 1):
            parts = ([p_cur[j - 1]] if j >= 1 else []) + ([p_prev[j]] if j < nblk else [])
            o_seq.append(_dot(jnp.concatenate(parts, axis=0) if len(parts) > 1 else parts[0], v_seq[j]))
        for t in range(nblk):
            prev_rows = slice(0, pair_rows) if t == 0 else slice(pair_rows, 2 * pair_rows)
            o = (o_meta[t * pair_rows:(t + 1) * pair_rows] + o_seq[t][prev_rows]
                 + o_seq[t + 1][0:pair_rows]) / l_all[t]
            for pr in range(2):
                col = (2 * kvh + pr) * LANES
                o_ref[t * BLOCK:(t + 1) * BLOCK, col:col + LANES] = o[pr * BLOCK:(pr + 1) * BLOCK].astype(BF16)


GLA_STEP = 2 * GLA_CHUNK
GLA_PAIR_K = 2 * GLA_KEY_DIM
GLA_PAIR_V = 2 * GLA_VAL_DIM


def _gla_consts():
    r = lax.broadcasted_iota(jnp.int32, (GLA_STEP, 2 * GLA_STEP), 0)
    c = lax.broadcasted_iota(jnp.int32, (GLA_STEP, 2 * GLA_STEP), 1) & (GLA_STEP - 1)
    causal = jnp.logical_and((r >> 6) == (c >> 6), c <= r)
    tri = jnp.where(causal[:, :GLA_STEP], 1.0, 0.0).astype(BF16)
    return causal, tri


def _gla_steps(units, states, causal, tri):
    row = lax.broadcasted_iota(jnp.int32, (GLA_STEP, GLA_PAIR_K), 0)
    lane = lax.broadcasted_iota(jnp.int32, (GLA_STEP, GLA_PAIR_K), 1)
    first = row < GLA_CHUNK
    low = lane < GLA_KEY_DIM
    lane_t = lax.broadcasted_iota(jnp.int32, (GLA_PAIR_K, GLA_STEP), 1)
    first_t = lane_t < GLA_CHUNK
    srow = lax.broadcasted_iota(jnp.int32, (GLA_PAIR_K, GLA_PAIR_V), 0)
    slane = lax.broadcasted_iota(jnp.int32, (GLA_PAIR_K, GLA_PAIR_V), 1)
    diag = (srow < GLA_KEY_DIM) == (slane < GLA_VAL_DIM)
    zero = jnp.zeros((), BF16)
    zf = jnp.zeros((), F32)
    zv = jnp.zeros((GLA_STEP, GLA_VAL_DIM), BF16)
    half = GLA_CHUNK // 2
    n = len(units)

    gk = jnp.concatenate([u[4] for u in units], axis=1) if n > 1 else units[0][4]
    g_hi = gk.astype(BF16)
    g_r = gk - g_hi.astype(F32)
    g_mid = g_r.astype(BF16)
    g_lo = (g_r - g_mid.astype(F32)).astype(BF16)
    b_all = _dot(tri, g_hi) + _dot(tri, g_mid) + _dot(tri, g_lo)

    prep = []
    for i, (_, q2, k2, v2, _) in enumerate(units):
        b = b_all[:, i * GLA_PAIR_K:(i + 1) * GLA_PAIR_K]
        b_mid = jnp.where(first, b[half:half + 1, :], b[GLA_CHUNK + half:GLA_CHUNK + half + 1, :])
        qi = (q2 * jnp.exp(b - b_mid)).astype(BF16)
        ki = (k2 * jnp.exp(b_mid - b)).astype(BF16)
        qx = q2 * jnp.exp(b)
        kblk = jnp.concatenate([jnp.where(low, ki, zero), jnp.where(low, zero, ki)], axis=0)
        vblk = jnp.concatenate([jnp.concatenate([v2[:, :GLA_VAL_DIM], zv], axis=1),
                                jnp.concatenate([zv, v2[:, GLA_VAL_DIM:]], axis=1)], axis=0)
        b_t = b.T
        k_t = k2.T
        bl_a = b_t[:, GLA_CHUNK - 1:GLA_CHUNK]
        bl_b = b_t[:, GLA_STEP - 1:GLA_STEP]
        kx_t = k_t * jnp.exp(jnp.where(first_t, bl_a, bl_b) - b_t)
        prep.append(dict(qi=qi, kblk=kblk, vblk=vblk,
                         qx_a=jnp.where(first, qx, zf).astype(BF16), qx_b=jnp.where(first, zf, qx).astype(BF16),
                         kx_a=jnp.where(first_t, kx_t, zf).astype(BF16),
                         kx_b=jnp.where(first_t, zf, kx_t).astype(BF16),
                         dec_a=jnp.exp(bl_a), dec_b=jnp.exp(bl_b)))

    atts = [jnp.where(causal, _dot_nt(p["qi"], p["kblk"]), 0.0).astype(BF16) for p in prep]
    kv_a = [jnp.where(diag, _dot(p["kx_a"], u[3]), 0.0) for p, u in zip(prep, units)]
    kv_b = [jnp.where(diag, _dot(p["kx_b"], u[3]), 0.0) for p, u in zip(prep, units)]
    outs = [_dot(a, p["vblk"]) for a, p in zip(atts, prep)]

    states = dict(states)
    for i, (pair, *_rest) in enumerate(units):
        p = prep[i]
        s_prev = states[pair]
        s_a = p["dec_a"] * s_prev + kv_a[i]
        outs[i] = outs[i] + _dot(p["qx_a"], s_prev.astype(BF16)) + _dot(p["qx_b"], s_a.astype(BF16))
        states[pair] = p["dec_b"] * s_a + kv_b[i]
    return outs, states


def _gla_init_body(k_ref, v_ref, gk_ref, s_out):
    causal, tri = _gla_consts()
    units = []
    for p in range(GLA_HEADS // 2):
        k2 = k_ref[:, p * GLA_PAIR_K:(p + 1) * GLA_PAIR_K].astype(F32)
        units.append((p, jnp.zeros_like(k2), k2, v_ref[:, p * GLA_PAIR_V:(p + 1) * GLA_PAIR_V],
                      gk_ref[:, p * GLA_PAIR_K:(p + 1) * GLA_PAIR_K]))
    zeros = {p: jnp.zeros((GLA_PAIR_K, GLA_PAIR_V), F32) for p in range(GLA_HEADS // 2)}
    _, states = _gla_steps(units, zeros, causal, tri)
    for p in range(GLA_HEADS // 2):
        s_out[p] = states[p]


def _gla_init_call(kg_m, vg_m, gk_m):
    return pl.pallas_call(
        _gla_init_body,
        out_shape=jax.ShapeDtypeStruct((GLA_HEADS // 2, GLA_PAIR_K, GLA_PAIR_V), F32),
        name="gla_init",
    )(kg_m, vg_m, gk_m)


def _gla_body(s0_ref, q_ref, k_ref, v_ref, gk_ref, rg_ref, nw_ref, o_ref, s_ref):
    t = pl.program_id(1)

    @pl.when(t == 0)
    def _():
        s_ref[...] = s0_ref[...]

    causal, tri = _gla_consts()
    nw = nw_ref[...]
    pairs = range(GLA_HEADS // 2)
    units = []
    where = []
    for u in range(GLA_GRID_STEPS):
        rows = slice(u * GLA_STEP, (u + 1) * GLA_STEP)
        for p in pairs:
            ks = slice(p * GLA_PAIR_K, (p + 1) * GLA_PAIR_K)
            vs = slice(p * GLA_PAIR_V, (p + 1) * GLA_PAIR_V)
            units.append((p, q_ref[rows, ks].astype(F32), k_ref[rows, ks].astype(F32), v_ref[rows, vs],
                          gk_ref[rows, ks]))
            where.append((rows, vs))
    outs, states = _gla_steps(units, {p: s_ref[p] for p in pairs}, causal, tri)
    for p in pairs:
        s_ref[p] = states[p]
    for o, (rows, vs) in zip(outs, where):
        heads = []
        for hh in range(2):
            oh = o[:, hh * GLA_VAL_DIM:(hh + 1) * GLA_VAL_DIM]
            ms = jnp.mean(oh * oh, axis=-1, keepdims=True)
            heads.append(oh * lax.rsqrt(ms + NORM_EPS) * nw)
        on = jnp.concatenate(heads, axis=1)
        o_ref[rows, vs] = (on * rg_ref[rows, vs].astype(F32)).astype(BF16)


GLA_GRID_STEPS = 8


def _mixers_body(sink_ref, q_ref, kp_ref, kc_ref, km_ref, vp_ref, vc_ref, vm_ref,
                 s0_ref, qg_ref, kg_ref, vg_ref, gk_ref, rg_ref, nw_ref, ao_ref, go_ref, s_ref):
    _gla_body(s0_ref, qg_ref, kg_ref, vg_ref, gk_ref, rg_ref, nw_ref, go_ref, s_ref)
    _attn_body(sink_ref, q_ref, kp_ref, kc_ref, km_ref, vp_ref, vc_ref, vm_ref, ao_ref)


def _mixers_call(q, kd, vd, km, vm, sinks, s0, qg, kg, vg, gk, rg, nw, batch, seq):
    step = ATTN_STEP_BLOCKS * BLOCK
    assert seq % step == 0 and step == GLA_GRID_STEPS * GLA_STEP
    seq_spec = lambda width: pl.BlockSpec((None, step, width), lambda b, i, s: (b, i, 0))
    prev = pl.BlockSpec((None, BLOCK, ATTN_KV_DUP_W),
                        lambda b, i, s: (b, jnp.maximum(i * ATTN_STEP_BLOCKS - 1, 0), 0))
    const = lambda shape: pl.BlockSpec(shape, lambda b, i, s: (0,) * len(shape))
    meta = const((BLOCK, ATTN_KV_DUP_W))
    grid_spec = pltpu.PrefetchScalarGridSpec(
        num_scalar_prefetch=1,
        grid=(batch, seq // step),
        in_specs=[seq_spec(ATTN_Q_W), prev, seq_spec(ATTN_KV_DUP_W), meta,
                  prev, seq_spec(ATTN_KV_DUP_W), meta,
                  const((GLA_HEADS // 2, GLA_PAIR_K, GLA_PAIR_V)),
                  seq_spec(GLA_K_W), seq_spec(GLA_K_W), seq_spec(GLA_V_W), seq_spec(GLA_K_W),
                  seq_spec(GLA_V_W), const((1, GLA_VAL_DIM))],
        out_specs=[seq_spec(ATTN_Q_W), seq_spec(GLA_V_W)],
        scratch_shapes=[pltpu.VMEM((GLA_HEADS // 2, GLA_PAIR_K, GLA_PAIR_V), F32)],
    )
    return pl.pallas_call(
        _mixers_body,
        grid_spec=grid_spec,
        out_shape=[jax.ShapeDtypeStruct((batch, seq, ATTN_Q_W), BF16),
                   jax.ShapeDtypeStruct((batch, seq, GLA_V_W), BF16)],
        compiler_params=pltpu.CompilerParams(dimension_semantics=("arbitrary", "arbitrary"),
                                             vmem_limit_bytes=VMEM_LIMIT),
        name="mixers",
    )(sinks, q, kd, kd, km, vd, vd, vm, s0, qg, kg, vg, gk, rg, nw)


ROUTER_W = LANES


RT_E0, RT_E1, RT_W0, RT_W1 = 0, 1, 2, 3


MERGE_SPLIT = 4


def _merge_body(x_ref, ao_ref, go_ref, sga_ref, sgg_ref, wab_ref, wgb_ref, wo_ref, n2_ref,
                wr2_ref, h_out, xn_out, rt_out, rtt_out, cnt_out):
    @pl.when(pl.program_id(0) == 0)
    def _():
        cnt_out[...] = jnp.zeros_like(cnt_out)

    rows_per = x_ref.shape[0] // MERGE_SPLIT
    for part in range(MERGE_SPLIT):
        _merge_rows(slice(part * rows_per, (part + 1) * rows_per), x_ref, ao_ref, go_ref, sga_ref, sgg_ref,
                    wab_ref, wgb_ref, wo_ref, n2_ref, wr2_ref, h_out, xn_out, rt_out, rtt_out, cnt_out)


def _merge_rows(rows, x_ref, ao_ref, go_ref, sga_ref, sgg_ref, wab_ref, wgb_ref, wo_ref, n2_ref,
                wr2_ref, h_out, xn_out, rt_out, rtt_out, cnt_out):
    ya = _dot(ao_ref[rows, :], wab_ref[...])
    yg = _dot(go_ref[rows, :], wgb_ref[...])
    merged = sga_ref[rows, :].astype(F32) * ya + sgg_ref[rows, :].astype(F32) * yg
    h = x_ref[rows, :] + _dot(merged.astype(BF16), wo_ref[...])
    xn = _rms_norm(h, n2_ref[...])
    for c in range(ROW_TILES):
        h_out[c, rows, :] = h[:, c * LANES:(c + 1) * LANES]
        xn_out[c, rows, :] = xn[:, c * LANES:(c + 1) * LANES]

    xh, xl = _split_hi_lo(xn)
    wr2 = wr2_ref[...]
    lg2 = _dot(xh, wr2) + _dot(xl, wr2)
    lg = lg2[:, :ROUTER_W] + lg2[:, ROUTER_W:]
    lg_t = lg.T
    tm = lg.shape[0]
    neg = -jnp.inf
    big = jnp.int32(1 << 20)
    gl = lg_t[N_EXPERTS:N_EXPERTS + N_GROUPS, :]
    grow = lax.broadcasted_iota(jnp.int32, (N_GROUPS, tm), 0)
    gmax = jnp.max(gl, axis=0, keepdims=True)
    g_idx = jnp.min(jnp.where(gl == gmax, grow, big), axis=0, keepdims=True)
    g_w = 1.0 / jnp.sum(jnp.exp(gl - gmax), axis=0, keepdims=True)
    erow = lax.broadcasted_iota(jnp.int32, (N_EXPERTS, tm), 0)
    el = jnp.where((erow >> 3) == g_idx, lg_t[0:N_EXPERTS, :], neg)
    m1 = jnp.max(el, axis=0, keepdims=True)
    i1 = jnp.min(jnp.where(el == m1, erow, big), axis=0, keepdims=True)
    el2 = jnp.where(erow == i1, neg, el)
    m2 = jnp.max(el2, axis=0, keepdims=True)
    i2 = jnp.min(jnp.where(el2 == m2, erow, big), axis=0, keepdims=True)
    e2 = jnp.exp(m2 - m1)
    p1 = g_w / (1.0 + e2)
    p2 = g_w * e2 / (1.0 + e2)
    def record(rows):
        r = lax.broadcasted_iota(jnp.int32, (rows, tm), 0)
        return jnp.where(r == RT_E0, i1.astype(F32),
                         jnp.where(r == RT_E1, i2.astype(F32),
                                   jnp.where(r == RT_W0, p1, jnp.where(r == RT_W1, p2, 0.0))))
    rtt_out[:, rows] = record(8)
    rt_out[rows, :] = record(LANES).T
    picked = jnp.where(jnp.logical_or(erow == i1, erow == i2), 1.0, 0.0)
    cnt_out[...] += jnp.sum(picked, axis=1, keepdims=True)


def _merge_call(x2d, ao, go, sga, sgg, w, tm):
    n = x2d.shape[0]
    row = lambda width: pl.BlockSpec((tm, width), lambda i: (i, 0))
    return pl.pallas_call(
        _merge_body,
        grid=(n // tm,),
        in_specs=[row(D_MODEL), row(ATTN_Q_W), row(GLA_V_W), row(D_MODEL), row(D_MODEL),
                  _const_spec((ATTN_Q_W, D_MODEL)), _const_spec((GLA_V_W, D_MODEL)),
                  _const_spec((D_MODEL, D_MODEL)), _const_spec((1, D_MODEL)),
                  _const_spec((D_MODEL, 2 * ROUTER_W))],
        out_specs=[pl.BlockSpec((ROW_TILES, tm, LANES), lambda i: (0, i, 0)),
                   pl.BlockSpec((ROW_TILES, tm, LANES), lambda i: (0, i, 0)), row(ROUTER_W),
                   pl.BlockSpec((8, tm), lambda i: (0, i)), _const_spec((N_EXPERTS, LANES))],
        out_shape=[jax.ShapeDtypeStruct((ROW_TILES, n, LANES), F32),
                   jax.ShapeDtypeStruct((ROW_TILES, n, LANES), F32),
                   jax.ShapeDtypeStruct((n, ROUTER_W), F32),
                   jax.ShapeDtypeStruct((8, n), F32),
                   jax.ShapeDtypeStruct((N_EXPERTS, LANES), F32)],
        compiler_params=pltpu.CompilerParams(dimension_semantics=("arbitrary",),
                                             vmem_limit_bytes=VMEM_LIMIT),
        name="merge",
    )(x2d, ao, go, sga, sgg, w["wab"], w["wgb"], w["wo"], w["norm2"], w["wr2"])


EXPERT_TILE = 512
EXPERT_SPLIT = 2
ROUTE_TILE = 512
VISIT_CAP = 256
VIS_EXPERT, VIS_TILE, VIS_LO, VIS_HI, VIS_COUNT = 0, 1, 2, 3, 4


def _num_visits(n_tokens):
    return (2 * n_tokens) // EXPERT_TILE + N_EXPERTS - 1


def _expert_cumsum_inclusive(x):
    row = lax.broadcasted_iota(jnp.int32, x.shape, 0)
    s = 1
    while s < N_EXPERTS:
        x = x + jnp.where(row >= s, pltpu.roll(x, s, 0), 0.0)
        s *= 2
    return x


def _route_body(rtt_ref, cnt_ref, pos_ref, vis_ref, carry_ref):
    i = pl.program_id(0)
    tm = rtt_ref.shape[1]
    cnt = cnt_ref[...]
    end = _expert_cumsum_inclusive(cnt)
    base = end - cnt

    @pl.when(i == 0)
    def _():
        carry_ref[...] = jnp.zeros_like(carry_ref)
        inv = 1.0 / EXPERT_TILE
        first = jnp.floor(base * inv)
        n_vis = jnp.where(cnt > 0.0, jnp.floor((end - 1.0) * inv) - first + 1.0, 0.0)
        vend = _expert_cumsum_inclusive(n_vis)
        vstart = vend - n_vis
        col = lambda a: a[:, 0:1]
        erow = lax.broadcasted_iota(jnp.int32, (N_EXPERTS, VISIT_CAP), 0).astype(F32)
        v = lax.broadcasted_iota(jnp.int32, (N_EXPERTS, VISIT_CAP), 1).astype(F32)
        ev = jnp.minimum(jnp.sum(jnp.where(col(vend) <= v, 1.0, 0.0), axis=0, keepdims=True),
                         N_EXPERTS - 1.0)
        mine = erow == ev
        pick = lambda a: jnp.sum(jnp.where(mine, a, 0.0), axis=0, keepdims=True)
        tile = pick(col(first) + v - col(vstart))
        lo = jnp.maximum(pick(col(base)) - tile * EXPERT_TILE, 0.0)
        hi = jnp.minimum(pick(col(end)) - tile * EXPERT_TILE, float(EXPERT_TILE))
        total = vend[N_EXPERTS - 1:N_EXPERTS, 0:1]
        r8 = lax.broadcasted_iota(jnp.int32, (8, VISIT_CAP), 0)
        rec = jnp.where(r8 == VIS_EXPERT, ev,
                        jnp.where(r8 == VIS_TILE, tile,
                                  jnp.where(r8 == VIS_LO, lo,
                                            jnp.where(r8 == VIS_HI, hi,
                                                      jnp.where(r8 == VIS_COUNT, total, 0.0)))))
        vis_ref[...] = rec.astype(jnp.int32)

    rtt = rtt_ref[...]
    erow = lax.broadcasted_iota(jnp.int32, (N_EXPERTS, tm), 0).astype(F32)
    oh0 = erow == rtt[RT_E0:RT_E0 + 1, :]
    oh1 = erow == rtt[RT_E1:RT_E1 + 1, :]
    picked = jnp.where(jnp.logical_or(oh0, oh1), 1.0, 0.0)
    r = lax.broadcasted_iota(jnp.int32, (tm, tm), 0)
    c = lax.broadcasted_iota(jnp.int32, (tm, tm), 1)
    earlier = jnp.where(r < c, 1.0, 0.0).astype(BF16)
    slot = _dot(picked.astype(BF16), earlier) + (carry_ref[:, 0:1] + base[:, 0:1])
    p0 = jnp.sum(jnp.where(oh0, slot, 0.0), axis=0, keepdims=True)
    p1 = jnp.sum(jnp.where(oh1, slot, 0.0), axis=0, keepdims=True)
    r8 = lax.broadcasted_iota(jnp.int32, (8, tm), 0)
    pos_ref[...] = jnp.where(r8 == 0, p0, jnp.where(r8 == 1, p1, 0.0)).astype(jnp.int32)
    carry_ref[...] += jnp.sum(picked, axis=1, keepdims=True)


def _route_call(rtt, cnt):
    n = rtt.shape[1]
    tm = _pick_tile(n, ROUTE_TILE)
    return pl.pallas_call(
        _route_body,
        grid=(n // tm,),
        in_specs=[pl.BlockSpec((8, tm), lambda i: (0, i)), _const_spec((N_EXPERTS, LANES))],
        out_specs=[pl.BlockSpec((8, tm), lambda i: (0, i)), _const_spec((8, VISIT_CAP))],
        out_shape=[jax.ShapeDtypeStruct((8, n), jnp.int32),
                   jax.ShapeDtypeStruct((8, VISIT_CAP), jnp.int32)],
        scratch_shapes=[pltpu.VMEM((N_EXPERTS, LANES), F32)],
        compiler_params=pltpu.CompilerParams(dimension_semantics=("arbitrary",)),
        name="route",
    )(rtt, cnt)


SCATTER_TILE = 512
ROW_TILES = D_MODEL // LANES


def _store_slabs(ref, val):
    for c in range(ROW_TILES):
        ref[c] = val[:, c * LANES:(c + 1) * LANES]


def _load_slabs(ref):
    return jnp.concatenate([ref[c] for c in range(ROW_TILES)], axis=1)


def _scatter_body(pos0_ref, pos1_ref, h_ref, hs_hbm, sem):
    tm = h_ref.shape[1]
    for r in range(tm):
        src = h_ref.at[:, r, :]
        pltpu.make_async_copy(src, hs_hbm.at[:, pos0_ref[r], :], sem).start(priority=0)
        pltpu.make_async_copy(src, hs_hbm.at[:, pos1_ref[r], :], sem).start(priority=1)
    for _ in range(2 * tm):
        pltpu.make_async_copy(h_ref.at[:, 0, :], hs_hbm.at[:, 0, :], sem).wait()


def _scatter_call(pos0, pos1, h_slabs, n_slots):
    n = h_slabs.shape[1]
    tm = _pick_tile(n, SCATTER_TILE)
    smem = lambda: pl.BlockSpec((tm,), lambda i: (i,), memory_space=pltpu.SMEM)
    return pl.pallas_call(
        _scatter_body,
        grid=(n // tm,),
        in_specs=[smem(), smem(), pl.BlockSpec((ROW_TILES, tm, LANES), lambda i: (0, i, 0))],
        out_specs=pl.BlockSpec(memory_space=pl.ANY),
        out_shape=jax.ShapeDtypeStruct((ROW_TILES, n_slots, LANES), F32),
        scratch_shapes=[pltpu.SemaphoreType.DMA],
        compiler_params=pltpu.CompilerParams(dimension_semantics=("arbitrary",)),
        name="scatter",
    )(pos0, pos1, h_slabs)


EXPERT_RING = 3


def _expert_body(vis_ref, xs_hbm, wg_ref, wu_ref, wd_ref, o_ref, x_ring, sem):
    j = pl.program_id(0)
    count = vis_ref[VIS_COUNT, 0]

    def tile_copy(v, slot):
        start = pl.multiple_of(vis_ref[VIS_TILE, v] * EXPERT_TILE, EXPERT_TILE)
        return pltpu.make_async_copy(xs_hbm.at[:, pl.ds(start, EXPERT_TILE), :], x_ring.at[slot],
                                     sem.at[slot])

    @pl.when(j == 0)
    def _():
        for v in range(EXPERT_RING - 1):
            @pl.when(v < count)
            def _():
                tile_copy(v, v).start()

    ahead = j + EXPERT_RING - 1

    @pl.when(ahead < count)
    def _():
        tile_copy(ahead, ahead % EXPERT_RING).start()

    @pl.when(j < count)
    def _():
        slot = j % EXPERT_RING
        tile_copy(j, slot).wait()
        x_ref = x_ring.at[slot]
        lo_row = vis_ref[VIS_LO, j]
        hi_row = vis_ref[VIS_HI, j]

        @pl.when(lo_row == 0)
        def _():
            o_ref[...] = jnp.zeros_like(o_ref)

        wg = wg_ref[...].astype(BF16)
        wu = wu_ref[...].astype(BF16)
        wd = wd_ref[...].astype(BF16)
        rows_per = EXPERT_TILE // EXPERT_SPLIT
        for part in range(EXPERT_SPLIT):
            rows = slice(part * rows_per, (part + 1) * rows_per)
            x = jnp.concatenate([x_ref[c, rows, :] for c in range(ROW_TILES)], axis=1).astype(BF16)
            g = _dot(x, wg)
            u = _dot(x, wu)
            hg = (g * _sigmoid(g) * u).astype(BF16)
            new = _dot(hg, wd)
            row = lax.broadcasted_iota(jnp.int32, (rows_per, LANES), 0) + part * rows_per
            keep = jnp.logical_and(row >= lo_row, row < hi_row)
            for c in range(ROW_TILES):
                o_ref[c, rows, :] = jnp.where(keep, new[:, c * LANES:(c + 1) * LANES], o_ref[c, rows, :])


def _expert_call(vis, hs, wg, wu, wd, n_visits):
    def visit(j, vis):
        return jnp.minimum(j, vis[VIS_COUNT, 0] - 1)

    def xmap(j, vis):
        return (0, vis[VIS_TILE, visit(j, vis)], 0)

    def wmap(j, vis):
        return (vis[VIS_EXPERT, visit(j, vis)], 0, 0)

    grid_spec = pltpu.PrefetchScalarGridSpec(
        num_scalar_prefetch=1,
        grid=(n_visits,),
        in_specs=[pl.BlockSpec(memory_space=pl.ANY),
                  pl.BlockSpec((None, D_MODEL, EXPERT_FF), wmap),
                  pl.BlockSpec((None, D_MODEL, EXPERT_FF), wmap),
                  pl.BlockSpec((None, EXPERT_FF, D_MODEL), wmap)],
        out_specs=pl.BlockSpec((ROW_TILES, EXPERT_TILE, LANES), xmap),
        scratch_shapes=[pltpu.VMEM((EXPERT_RING, ROW_TILES, EXPERT_TILE, LANES), F32),
                        pltpu.SemaphoreType.DMA((EXPERT_RING,))],
    )
    return pl.pallas_call(
        _expert_body,
        grid_spec=grid_spec,
        out_shape=jax.ShapeDtypeStruct(hs.shape, F32),
        compiler_params=pltpu.CompilerParams(dimension_semantics=("arbitrary",),
                                             vmem_limit_bytes=VMEM_LIMIT),
        name="experts",
    )(vis, hs, wg, wu, wd)


COMBINE_TILE = 256


def _combine_body(p0_ref, p1_ref, p0n_ref, p1n_ref, h_ref, rt_ref, y_hbm, o_ref, g_ref, sem):
    i = pl.program_id(0)
    tm = h_ref.shape[1]
    cur = i % 2

    def issue(q0_ref, q1_ref, buf):
        for r in range(tm):
            pltpu.make_async_copy(y_hbm.at[:, q0_ref[r], :], g_ref.at[buf, 0, :, r, :],
                                  sem.at[buf]).start(priority=0)
            pltpu.make_async_copy(y_hbm.at[:, q1_ref[r], :], g_ref.at[buf, 1, :, r, :],
                                  sem.at[buf]).start(priority=1)

    @pl.when(i == 0)
    def _():
        issue(p0_ref, p1_ref, 0)

    @pl.when(i + 1 < pl.num_programs(0))
    def _():
        issue(p0n_ref, p1n_ref, 1 - cur)

    for _ in range(2 * tm):
        pltpu.make_async_copy(y_hbm.at[:, 0, :], g_ref.at[cur, 0, :, 0, :], sem.at[cur]).wait()
    rt = rt_ref[...]
    o_ref[...] = (_load_slabs(h_ref) + rt[:, RT_W0:RT_W0 + 1] * _load_slabs(g_ref.at[cur, 0])
                  + rt[:, RT_W1:RT_W1 + 1] * _load_slabs(g_ref.at[cur, 1]))


def _combine_call(pos0, pos1, h_slabs, rt, y):
    n = h_slabs.shape[1]
    tm = _pick_tile(n, COMBINE_TILE)
    steps = n // tm
    smem_cur = lambda: pl.BlockSpec((tm,), lambda i: (i,), memory_space=pltpu.SMEM)
    smem_next = lambda: pl.BlockSpec((tm,), lambda i: (jnp.minimum(i + 1, steps - 1),),
                                     memory_space=pltpu.SMEM)
    row = lambda width: pl.BlockSpec((tm, width), lambda i: (i, 0))
    return pl.pallas_call(
        _combine_body,
        grid=(steps,),
        in_specs=[smem_cur(), smem_cur(), smem_next(), smem_next(),
                  pl.BlockSpec((ROW_TILES, tm, LANES), lambda i: (0, i, 0)), row(ROUTER_W),
                  pl.BlockSpec(memory_space=pl.ANY)],
        out_specs=row(D_MODEL),
        out_shape=jax.ShapeDtypeStruct((n, D_MODEL), F32),
        scratch_shapes=[pltpu.VMEM((2, 2, ROW_TILES, tm, LANES), F32), pltpu.SemaphoreType.DMA((2,))],
        compiler_params=pltpu.CompilerParams(dimension_semantics=("arbitrary",),
                                             vmem_limit_bytes=VMEM_LIMIT),
        name="combine",
    )(pos0, pos1, pos0, pos1, h_slabs, rt, y)


def _rope_tables(pos):
    half = HEAD_DIM // 2
    inv_freq = ROPE_THETA ** (-(jnp.arange(half, dtype=F32) * 2.0) / HEAD_DIM)
    ang = pos.astype(F32)[:, None] * inv_freq[None, :]
    cos = jnp.cos(ang)
    sin = jnp.sin(ang)
    cos64 = jnp.concatenate([cos, cos], axis=1)
    sin64 = jnp.concatenate([-sin, sin], axis=1)
    return jnp.tile(cos64, (1, LANES // HEAD_DIM)), jnp.tile(sin64, (1, LANES // HEAD_DIM))


WPREP_ROWS = 128


def _wprep_body(w_ref, qkv_out, gla_out, gate_out):
    w = w_ref[...]
    lane = lax.broadcasted_iota(jnp.int32, (WPREP_ROWS, LANES), 1)
    qkv_out[:, :QKV_GD_OFF] = w[:, :QKV_GD_OFF].astype(BF16)
    gd_tile = w[:, _OFF_GD:_OFF_GD + LANES]
    qkv_out[:, QKV_GD_OFF:QKV_GD_OFF + LANES] = jnp.where(lane < GLA_GATE_RANK, gd_tile, 0.0).astype(BF16)
    qkv_out[:, QKV_GD_OFF + LANES:] = jnp.zeros((WPREP_ROWS, QKV_W - QKV_GD_OFF - LANES), BF16)
    gla_out[...] = w[:, _OFF_GQ:_OFF_GQ + GLA_W].astype(BF16)
    keep = LANES - GLA_GATE_RANK
    n_tiles = 2 * D_MODEL // LANES
    rolled = [pltpu.roll(w[:, _OFF_GD + t * LANES:_OFF_GD + (t + 1) * LANES], keep, 1)
              for t in range(n_tiles)]
    tail = w[:, _OFF_GD + n_tiles * LANES:]
    tail = jnp.concatenate([tail, jnp.zeros((WPREP_ROWS, LANES - tail.shape[1]), F32)], axis=1)
    rolled.append(pltpu.roll(tail, keep, 1))
    for t in range(n_tiles):
        gate_out[:, t * LANES:(t + 1) * LANES] = jnp.where(lane < keep, rolled[t], rolled[t + 1]).astype(BF16)


def _wprep_call(w_in):
    in_total = w_in.shape[2]
    assert in_total == _OFF_GATE + 2 * D_MODEL and D_MODEL % WPREP_ROWS == 0
    row = lambda width: pl.BlockSpec((WPREP_ROWS, width), lambda i: (i, 0))
    return pl.pallas_call(
        _wprep_body,
        grid=(D_MODEL // WPREP_ROWS,),
        in_specs=[pl.BlockSpec((None, WPREP_ROWS, in_total), lambda i: (0, i, 0))],
        out_specs=[row(QKV_W), row(GLA_W), row(2 * D_MODEL)],
        out_shape=[jax.ShapeDtypeStruct((D_MODEL, QKV_W), BF16),
                   jax.ShapeDtypeStruct((D_MODEL, GLA_W), BF16),
                   jax.ShapeDtypeStruct((D_MODEL, 2 * D_MODEL), BF16)],
        compiler_params=pltpu.CompilerParams(dimension_semantics=("arbitrary",)),
        name="wprep",
    )(w_in)


def _prep_weights(norm1_w, w_in, q_norm_w, k_norm_w, gla_gate_up, gla_gate_bias, w_attn_branch,
                  w_gla_branch, w_out, norm2_w, router_group, router_expert):
    wqkv, wgla, wgate = _wprep_call(w_in)
    gup = jnp.zeros((LANES, GLA_K_W), BF16).at[:GLA_GATE_RANK].set(gla_gate_up.astype(BF16))
    wr = jnp.zeros((D_MODEL, ROUTER_W), F32)
    wr = wr.at[:, :N_EXPERTS].set(router_expert.transpose(1, 0, 2).reshape(D_MODEL, N_EXPERTS))
    wr = wr.at[:, N_EXPERTS:N_EXPERTS + N_GROUPS].set(router_group)
    wrh = wr.astype(BF16)
    wrl = (wr - wrh.astype(F32)).astype(BF16)
    return {
        "norm1": norm1_w.reshape(1, D_MODEL),
        "wqkv": wqkv,
        "wgla": wgla,
        "wgate": wgate,
        "qn": jnp.tile(q_norm_w, ATTN_HEADS).reshape(1, ATTN_Q_W),
        "kn": jnp.tile(k_norm_w, ATTN_KV_HEADS).reshape(1, ATTN_KV_W),
        "gup": gup,
        "gb": gla_gate_bias.reshape(1, GLA_K_W),
        "wab": w_attn_branch.astype(BF16),
        "wgb": w_gla_branch.astype(BF16),
        "wo": w_out.astype(BF16),
        "norm2": norm2_w.reshape(1, D_MODEL),
        "wr2": jnp.concatenate([wrh, wrl], axis=1),
    }


def _pick_tile(n, want):
    t = want
    while n % t:
        t //= 2
    return t


def kernel(x, meta_tokens, norm1_w, w_in, q_norm_w, k_norm_w, attn_sinks, gla_gate_up, gla_gate_bias,
           gla_norm_w, w_attn_branch, w_gla_branch, w_out, norm2_w, router_group, router_expert,
           expert_w_gate, expert_w_up, expert_w_down):
    batch, seq, d = x.shape
    assert d == D_MODEL and seq % GLA_STEP == 0 and norm1_w.shape[0] == 1
    n = batch * seq
    w = _prep_weights(norm1_w[0], w_in, q_norm_w[0], k_norm_w[0], gla_gate_up[0], gla_gate_bias[0],
                      w_attn_branch[0], w_gla_branch[0], w_out[0], norm2_w[0], router_group[0],
                      router_expert[0])
    x2d = x.reshape(n, D_MODEL)

    cos_m, sin_m = _rope_tables(jnp.arange(N_META))
    cos_r, sin_r = _rope_tables(jnp.arange(seq) + N_META)
    meta = _proj_call(meta_tokens.astype(F32), N_META, 1, w, cos_m, sin_m)
    tm = _pick_tile(seq, 512)
    (q, kd, vd, qg, kg, vg, rg, gk, sga, sgg) = _proj_call(x2d, tm, seq // tm, w, cos_r, sin_r)

    pad_after = ((0, BLOCK - N_META), (0, 0))
    km = jnp.pad(meta[1], pad_after)
    vm = jnp.pad(meta[2], pad_after)
    pad_before = ((GLA_STEP - N_META, 0), (0, 0))
    s0 = _gla_init_call(jnp.pad(meta[4], pad_before), jnp.pad(meta[5], pad_before),
                        jnp.pad(meta[7], pad_before))
    ao, go = _mixers_call(q.reshape(batch, seq, ATTN_Q_W), kd.reshape(batch, seq, ATTN_KV_DUP_W),
                          vd.reshape(batch, seq, ATTN_KV_DUP_W), km, vm, attn_sinks[0].astype(F32),
                          s0, qg.reshape(batch, seq, GLA_K_W), kg.reshape(batch, seq, GLA_K_W),
                          vg.reshape(batch, seq, GLA_V_W), gk.reshape(batch, seq, GLA_K_W),
                          rg.reshape(batch, seq, GLA_V_W), gla_norm_w[0].reshape(1, GLA_VAL_DIM),
                          batch, seq)

    h, xn2, rt, rtt, cnt = _merge_call(x2d, ao.reshape(n, ATTN_Q_W), go.reshape(n, GLA_V_W), sga, sgg, w,
                                       _pick_tile(n, 1024))

    assert (2 * n) % EXPERT_TILE == 0 and _num_visits(n) <= VISIT_CAP
    pos, vis = _route_call(rtt, cnt)
    pos0, pos1 = pos[0], pos[1]
    xs = _scatter_call(pos0, pos1, xn2, 2 * n)
    wg = expert_w_gate[0].reshape(N_EXPERTS, D_MODEL, EXPERT_FF)
    wu = expert_w_up[0].reshape(N_EXPERTS, D_MODEL, EXPERT_FF)
    wd = expert_w_down[0].reshape(N_EXPERTS, EXPERT_FF, D_MODEL)
    y = _expert_call(vis, xs, wg, wu, wd, _num_visits(n))
    out = _combine_call(pos0, pos1, h, rt, y)
    return out.reshape(batch, seq, D_MODEL)
```

```python
import math

import jax
import jax.numpy as jnp
from jax import lax
from jax.experimental import pallas as pl
from jax.experimental.pallas import tpu as pltpu

F32 = jnp.float32
BF16 = jnp.bfloat16

D_MODEL = 1024
N_META = 16
BLOCK = 128
ATTN_HEADS = 8
ATTN_KV_HEADS = 2
HEAD_DIM = 64
ROPE_THETA = 10000.0
ATTN_Q_W = ATTN_HEADS * HEAD_DIM
ATTN_KV_W = ATTN_KV_HEADS * HEAD_DIM
GLA_HEADS = 4
GLA_KEY_DIM = 64
GLA_VAL_DIM = 128
GLA_K_W = GLA_HEADS * GLA_KEY_DIM
GLA_V_W = GLA_HEADS * GLA_VAL_DIM
GLA_GATE_RANK = 16
GLA_GATE_NORM = 16.0
GLA_CHUNK = 64
N_GROUPS = 4
EXPERTS_PER_GROUP = 8
N_EXPERTS = N_GROUPS * EXPERTS_PER_GROUP
EXPERT_FF = 256
NORM_EPS = 1e-6
MASK_VALUE = -1e30
assert GLA_CHUNK & (GLA_CHUNK - 1) == 0 and EXPERTS_PER_GROUP & (EXPERTS_PER_GROUP - 1) == 0

LANES = 128
ATTN_KV_DUP_W = 2 * ATTN_KV_W
VMEM_LIMIT = 56 * 1024 * 1024

_OFF_Q = 0
_OFF_K = _OFF_Q + ATTN_Q_W
_OFF_V = _OFF_K + ATTN_KV_W
_OFF_GQ = _OFF_V + ATTN_KV_W
_OFF_GD = _OFF_GQ + 2 * GLA_K_W + 2 * GLA_V_W
_OFF_GATE = _OFF_GD + GLA_GATE_RANK
GLA_W = 2 * GLA_K_W + 2 * GLA_V_W
QKV_GD_OFF = _OFF_GQ
QKV_W = D_MODEL


def _dot(a, b):
    return jnp.dot(a, b, preferred_element_type=F32)


def _dot_nt(a, b):
    return lax.dot_general(a, b, (((1,), (1,)), ((), ())), preferred_element_type=F32)


def _split_hi_lo(x):
    hi = x.astype(BF16)
    lo = (x - hi.astype(F32)).astype(BF16)
    return hi, lo


MXU_TILE = 256


def _group_mean_sq(x, group):
    w = x.shape[-1]
    slab = min(w, MXU_TILE)
    shift = int(math.log2(group))
    r = lax.broadcasted_iota(jnp.int32, (slab, slab), 0) >> shift
    c = lax.broadcasted_iota(jnp.int32, (slab, slab), 1) >> shift
    ones = jnp.where(r == c, 1.0, 0.0).astype(BF16)
    hi, lo = _split_hi_lo(x * x)
    sums = [_dot(hi[:, s:s + slab], ones) + _dot(lo[:, s:s + slab], ones) for s in range(0, w, slab)]
    return (sums[0] if len(sums) == 1 else jnp.concatenate(sums, axis=1)) * (1.0 / group)


def _rope(x, cos, sin_signed):
    w = x.shape[-1]
    reps = w // LANES
    if reps > 1:
        cos = jnp.concatenate([cos] * reps, axis=1)
        sin_signed = jnp.concatenate([sin_signed] * reps, axis=1)
    lane = lax.broadcasted_iota(jnp.int32, x.shape, 1)
    first_half = (lane & (HEAD_DIM // 2)) == 0
    swapped = jnp.where(first_half, pltpu.roll(x, w - HEAD_DIM // 2, 1), pltpu.roll(x, HEAD_DIM // 2, 1))
    return x * cos + swapped * sin_signed


def _rms_norm(x, w):
    ms = jnp.mean(x * x, axis=-1, keepdims=True)
    return x * lax.rsqrt(ms + NORM_EPS) * w


def _sigmoid(x):
    return 0.5 * jnp.tanh(0.5 * x) + 0.5


def _log_sigmoid(x):
    return jnp.minimum(x, 0.0) - jnp.log(1.0 + jnp.exp(-jnp.abs(x)))


def _dup_heads_on_lanes(x):
    low = lax.broadcasted_iota(jnp.int32, x.shape, 1) < HEAD_DIM
    swapped = pltpu.roll(x, HEAD_DIM, 1)
    return jnp.concatenate([jnp.where(low, x, swapped), jnp.where(low, swapped, x)], axis=1)


def _proj_body(x_ref, n1_ref, wqkv_ref, wgla_ref, wgate_ref,
               qn_ref, kn_ref, cos_ref, sin_ref, gup_ref, gb_ref,
               q_out, k_out, v_out, qg_out, kg_out, vg_out, rg_out, gk_out, sga_out, sgg_out):
    xn = _rms_norm(x_ref[...], n1_ref[...]).astype(BF16)
    cos = cos_ref[...]
    sin = sin_ref[...]

    a = _dot(xn, wqkv_ref[...])
    q = a[:, :ATTN_Q_W]
    q = q * lax.rsqrt(_group_mean_sq(q, HEAD_DIM) + NORM_EPS) * qn_ref[...]
    q_out[...] = (_rope(q, cos, sin) * (HEAD_DIM ** -0.5)).astype(BF16)

    k = a[:, _OFF_K:_OFF_K + ATTN_KV_W]
    k = k * lax.rsqrt(_group_mean_sq(k, HEAD_DIM) + NORM_EPS) * kn_ref[...]
    k_out[...] = _dup_heads_on_lanes(_rope(k, cos, sin)).astype(BF16)

    v_out[...] = _dup_heads_on_lanes(a[:, _OFF_V:_OFF_V + ATTN_KV_W]).astype(BF16)

    g = _dot(xn, wgla_ref[...])
    qg_out[...] = (g[:, :GLA_K_W] * (GLA_KEY_DIM ** -0.5)).astype(BF16)
    kg_out[...] = g[:, GLA_K_W:2 * GLA_K_W].astype(BF16)
    vg_out[...] = g[:, 2 * GLA_K_W:2 * GLA_K_W + GLA_V_W].astype(BF16)
    r = g[:, 2 * GLA_K_W + GLA_V_W:]
    rg_out[...] = (r * _sigmoid(r)).astype(BF16)

    gd = a[:, QKV_GD_OFF:QKV_GD_OFF + LANES].astype(BF16)
    z = _dot(gd, gup_ref[...]) + gb_ref[...]
    gk_out[...] = _log_sigmoid(z) * (1.0 / GLA_GATE_NORM)

    gates = _dot(xn, wgate_ref[...])
    sga_out[...] = _sigmoid(gates[:, :D_MODEL]).astype(BF16)
    sgg_out[...] = _sigmoid(gates[:, D_MODEL:]).astype(BF16)


def _const_spec(shape):
    nd = len(shape)
    return pl.BlockSpec(shape, lambda *_: (0,) * nd)


def _proj_call(x2d, tm, tiles_per_seq, w, cos, sin):
    n = x2d.shape[0]
    grid = (n // tm,)
    row = lambda width: pl.BlockSpec((tm, width), lambda i: (i, 0))
    tab = pl.BlockSpec((tm, LANES), lambda i: (i % tiles_per_seq, 0))
    in_specs = [
        row(D_MODEL), _const_spec((1, D_MODEL)),
        _const_spec((D_MODEL, QKV_W)), _const_spec((D_MODEL, GLA_W)),
        _const_spec((D_MODEL, 2 * D_MODEL)),
        _const_spec((1, ATTN_Q_W)), _const_spec((1, ATTN_KV_W)),
        tab, tab, _const_spec((LANES, GLA_K_W)), _const_spec((1, GLA_K_W)),
    ]
    out_widths = [ATTN_Q_W, ATTN_KV_DUP_W, ATTN_KV_DUP_W, GLA_K_W, GLA_K_W, GLA_V_W, GLA_V_W,
                  GLA_K_W, D_MODEL, D_MODEL]
    out_dtypes = [BF16] * 7 + [F32] + [BF16] * 2
    return pl.pallas_call(
        _proj_body,
        grid=grid,
        in_specs=in_specs,
        out_specs=[row(wd) for wd in out_widths],
        out_shape=[jax.ShapeDtypeStruct((n, wd), dt) for wd, dt in zip(out_widths, out_dtypes)],
        compiler_params=pltpu.CompilerParams(dimension_semantics=("arbitrary",),
                                             vmem_limit_bytes=VMEM_LIMIT),
        name="proj",
    )(x2d, w["norm1"], w["wqkv"], w["wgla"], w["wgate"],
      w["qn"], w["kn"], cos, sin, w["gup"], w["gb"])


ATTN_STEP_BLOCKS = 8


def _attn_body(sink_ref, q_ref, kp_ref, kc_ref, km_ref, vp_ref, vc_ref, vm_ref, o_ref):
    i = pl.program_id(1)
    nblk = ATTN_STEP_BLOCKS
    pair_rows = 2 * BLOCK
    lane = lax.broadcasted_iota(jnp.int32, (BLOCK, LANES), 1)
    low = lane < HEAD_DIM
    low2 = lax.broadcasted_iota(jnp.int32, (pair_rows, LANES), 1) < HEAD_DIM
    top2 = lax.broadcasted_iota(jnp.int32, (pair_rows, 1), 0) < BLOCK
    key = lax.broadcasted_iota(jnp.int32, (pair_rows, 2 * LANES), 1) & (LANES - 1)
    rowi = lax.broadcasted_iota(jnp.int32, (pair_rows, 2 * LANES), 0) & (BLOCK - 1)
    mask_meta = key < N_META
    mask_prev = key > rowi
    mask_cur = key <= rowi
    zero = jnp.zeros((), BF16)

    def block_diag(x):
        return jnp.concatenate([jnp.where(low, x, zero), jnp.where(low, zero, x)], axis=0)

    blk_row = lax.broadcasted_iota(jnp.int32, (2 * BLOCK, LANES), 0)
    blk_lane = lax.broadcasted_iota(jnp.int32, (2 * BLOCK, LANES), 1)
    ones_blk = jnp.where((blk_row < BLOCK) == (blk_lane < HEAD_DIM), 1.0, 0.0).astype(BF16)

    def masked(s, mask):
        return jnp.where(mask, s, MASK_VALUE)

    kv_heads = range(ATTN_KV_HEADS)
    s_meta_all, s_seq_all, v_seq_all = [], [], []
    for kvh in kv_heads:
        sl = slice(kvh * LANES, (kvh + 1) * LANES)
        k_seq = [block_diag(kp_ref[:, sl])]
        v_seq = [block_diag(vp_ref[:, sl])]
        for t in range(nblk):
            rows = slice(t * BLOCK, (t + 1) * BLOCK)
            k_seq.append(block_diag(kc_ref[rows, sl]))
            v_seq.append(block_diag(vc_ref[rows, sl]))
        q_all = jnp.concatenate(
            [q_ref[t * BLOCK:(t + 1) * BLOCK, (2 * kvh + pr) * LANES:(2 * kvh + pr + 1) * LANES]
             for t in range(nblk) for pr in range(2)], axis=0)
        s_meta_all.append(_dot_nt(q_all, block_diag(km_ref[:, sl])))
        s_seq = []
        for j in range(nblk + 1):
            lo_blk, hi_blk = max(j - 1, 0), min(j, nblk - 1)
            s_seq.append(_dot_nt(q_all[lo_blk * pair_rows:(hi_blk + 1) * pair_rows], k_seq[j]))
        s_seq_all.append(s_seq)
        v_seq_all.append(v_seq)

    probs_all = []
    for kvh in kv_heads:
        s_meta, s_seq = s_meta_all[kvh], s_seq_all[kvh]

        def sink_col(hh):
            return jnp.where(top2, sink_ref[4 * kvh + hh], sink_ref[4 * kvh + 2 + hh])

        p_meta, p_prev, p_cur, l_all = [], [], [], []
        for t in range(nblk):
            sm = masked(s_meta[t * pair_rows:(t + 1) * pair_rows], mask_meta)
            prev_rows = slice(0, pair_rows) if t == 0 else slice(pair_rows, 2 * pair_rows)
            sb = jnp.where(mask_prev, s_seq[t][prev_rows], s_seq[t + 1][0:pair_rows])
            if t == 0:
                sb = masked(sb, jnp.logical_or(mask_cur, i > 0))
            s_max = jnp.maximum(sm, sb)
            m2 = []
            sink_terms = []
            for hh in range(2):
                sink = sink_col(hh)
                m = jnp.maximum(jnp.max(s_max[:, hh * LANES:(hh + 1) * LANES], axis=1, keepdims=True), sink)
                m2.append(m)
                sink_terms.append(jnp.exp(sink - m))

            def probs(s):
                return jnp.concatenate([jnp.exp(s[:, hh * LANES:(hh + 1) * LANES] - m2[hh])
                                        for hh in range(2)], axis=1)

            pm, pb = probs(sm), probs(sb)
            pb16 = pb.astype(BF16)
            p_meta.append(pm.astype(BF16))
            p_prev.append(jnp.where(mask_prev, pb16, zero))
            p_cur.append(jnp.where(mask_prev, zero, pb16))
            l_all.append(_dot((pm + pb).astype(BF16), ones_blk)
                         + jnp.where(low2, sink_terms[0], sink_terms[1]))
        probs_all.append((p_meta, p_prev, p_cur, l_all))

    for kvh in kv_heads:
        sl = slice(kvh * LANES, (kvh + 1) * LANES)
        p_meta, p_prev, p_cur, l_all = probs_all[kvh]
        v_seq = v_seq_all[kvh]
        o_meta = _dot(jnp.concatenate(p_meta, axis=0), block_diag(vm_ref[:, sl]))
        o_seq = []
        for j in range(nblk + 1):
            parts = ([p_cur[j - 1]] if j >= 1 else []) + ([p_prev[j]] if j < nblk else [])
            o_seq.append(_dot(jnp.concatenate(parts, axis=0) if len(parts) > 1 else parts[0], v_seq[j]))
        for t in range(nblk):
            prev_rows = slice(0, pair_rows) if t == 0 else slice(pair_rows, 2 * pair_rows)
            o = (o_meta[t * pair_rows:(t + 1) * pair_rows] + o_seq[t][prev_rows]
                 + o_seq[t + 1][0:pair_rows]) / l_all[t]
            for pr in range(2):
                col = (2 * kvh + pr) * LANES
                o_ref[t * BLOCK:(t + 1) * BLOCK, col:col + LANES] = o[pr * BLOCK:(pr + 1) * BLOCK].astype(BF16)


GLA_STEP = 2 * GLA_CHUNK
GLA_PAIR_K = 2 * GLA_KEY_DIM
GLA_PAIR_V = 2 * GLA_VAL_DIM


def _gla_consts():
    r = lax.broadcasted_iota(jnp.int32, (GLA_STEP, 2 * GLA_STEP), 0)
    c = lax.broadcasted_iota(jnp.int32, (GLA_STEP, 2 * GLA_STEP), 1) & (GLA_STEP - 1)
    chunk_shift = GLA_CHUNK.bit_length() - 1
    causal = jnp.logical_and((r >> chunk_shift) == (c >> chunk_shift), c <= r)
    tri = jnp.where(causal[:, :GLA_STEP], 1.0, 0.0).astype(BF16)
    return causal, tri


def _gla_steps(units, states, causal, tri):
    row = lax.broadcasted_iota(jnp.int32, (GLA_STEP, GLA_PAIR_K), 0)
    lane = lax.broadcasted_iota(jnp.int32, (GLA_STEP, GLA_PAIR_K), 1)
    first = row < GLA_CHUNK
    low = lane < GLA_KEY_DIM
    lane_t = lax.broadcasted_iota(jnp.int32, (GLA_PAIR_K, GLA_STEP), 1)
    first_t = lane_t < GLA_CHUNK
    srow = lax.broadcasted_iota(jnp.int32, (GLA_PAIR_K, GLA_PAIR_V), 0)
    slane = lax.broadcasted_iota(jnp.int32, (GLA_PAIR_K, GLA_PAIR_V), 1)
    diag = (srow < GLA_KEY_DIM) == (slane < GLA_VAL_DIM)
    zero = jnp.zeros((), BF16)
    zf = jnp.zeros((), F32)
    zv = jnp.zeros((GLA_STEP, GLA_VAL_DIM), BF16)
    half = GLA_CHUNK // 2
    n = len(units)

    gk = jnp.concatenate([u[4] for u in units], axis=1) if n > 1 else units[0][4]
    g_hi = gk.astype(BF16)
    g_r = gk - g_hi.astype(F32)
    g_mid = g_r.astype(BF16)
    g_lo = (g_r - g_mid.astype(F32)).astype(BF16)
    b_all = _dot(tri, g_hi) + _dot(tri, g_mid) + _dot(tri, g_lo)

    prep = []
    for i, (_, q2, k2, v2, _) in enumerate(units):
        b = b_all[:, i * GLA_PAIR_K:(i + 1) * GLA_PAIR_K]
        b_mid = jnp.where(first, b[half:half + 1, :], b[GLA_CHUNK + half:GLA_CHUNK + half + 1, :])
        qi = (q2 * jnp.exp(b - b_mid)).astype(BF16)
        ki = (k2 * jnp.exp(b_mid - b)).astype(BF16)
        qx = q2 * jnp.exp(b)
        kblk = jnp.concatenate([jnp.where(low, ki, zero), jnp.where(low, zero, ki)], axis=0)
        vblk = jnp.concatenate([jnp.concatenate([v2[:, :GLA_VAL_DIM], zv], axis=1),
                                jnp.concatenate([zv, v2[:, GLA_VAL_DIM:]], axis=1)], axis=0)
        b_t = b.T
        k_t = k2.T
        bl_a = b_t[:, GLA_CHUNK - 1:GLA_CHUNK]
        bl_b = b_t[:, GLA_STEP - 1:GLA_STEP]
        kx_t = k_t * jnp.exp(jnp.where(first_t, bl_a, bl_b) - b_t)
        prep.append(dict(qi=qi, kblk=kblk, vblk=vblk,
                         qx_a=jnp.where(first, qx, zf).astype(BF16), qx_b=jnp.where(first, zf, qx).astype(BF16),
                         kx_a=jnp.where(first_t, kx_t, zf).astype(BF16),
                         kx_b=jnp.where(first_t, zf, kx_t).astype(BF16),
                         dec_a=jnp.exp(bl_a), dec_b=jnp.exp(bl_b)))

    atts = [jnp.where(causal, _dot_nt(p["qi"], p["kblk"]), 0.0).astype(BF16) for p in prep]
    kv_a = [jnp.where(diag, _dot(p["kx_a"], u[3]), 0.0) for p, u in zip(prep, units)]
    kv_b = [jnp.where(diag, _dot(p["kx_b"], u[3]), 0.0) for p, u in zip(prep, units)]
    outs = [_dot(a, p["vblk"]) for a, p in zip(atts, prep)]

    states = dict(states)
    for i, (pair, *_rest) in enumerate(units):
        p = prep[i]
        s_prev = states[pair]
        s_a = p["dec_a"] * s_prev + kv_a[i]
        outs[i] = outs[i] + _dot(p["qx_a"], s_prev.astype(BF16)) + _dot(p["qx_b"], s_a.astype(BF16))
        states[pair] = p["dec_b"] * s_a + kv_b[i]
    return outs, states


def _gla_init_body(k_ref, v_ref, gk_ref, s_out):
    causal, tri = _gla_consts()
    units = []
    for p in range(GLA_HEADS // 2):
        k2 = k_ref[:, p * GLA_PAIR_K:(p + 1) * GLA_PAIR_K].astype(F32)
        units.append((p, jnp.zeros_like(k2), k2, v_ref[:, p * GLA_PAIR_V:(p + 1) * GLA_PAIR_V],
                      gk_ref[:, p * GLA_PAIR_K:(p + 1) * GLA_PAIR_K]))
    zeros = {p: jnp.zeros((GLA_PAIR_K, GLA_PAIR_V), F32) for p in range(GLA_HEADS // 2)}
    _, states = _gla_steps(units, zeros, causal, tri)
    for p in range(GLA_HEADS // 2):
        s_out[p] = states[p]


def _gla_init_call(kg_m, vg_m, gk_m):
    return pl.pallas_call(
        _gla_init_body,
        out_shape=jax.ShapeDtypeStruct((GLA_HEADS // 2, GLA_PAIR_K, GLA_PAIR_V), F32),
        name="gla_init",
    )(kg_m, vg_m, gk_m)


def _gla_body(s0_ref, q_ref, k_ref, v_ref, gk_ref, rg_ref, nw_ref, o_ref, s_ref):
    t = pl.program_id(1)

    @pl.when(t == 0)
    def _():
        s_ref[...] = s0_ref[...]

    causal, tri = _gla_consts()
    nw = nw_ref[...]
    pairs = range(GLA_HEADS // 2)
    units = []
    where = []
    for u in range(GLA_GRID_STEPS):
        rows = slice(u * GLA_STEP, (u + 1) * GLA_STEP)
        for p in pairs:
            ks = slice(p * GLA_PAIR_K, (p + 1) * GLA_PAIR_K)
            vs = slice(p * GLA_PAIR_V, (p + 1) * GLA_PAIR_V)
            units.append((p, q_ref[rows, ks].astype(F32), k_ref[rows, ks].astype(F32), v_ref[rows, vs],
                          gk_ref[rows, ks]))
            where.append((rows, vs))
    outs, states = _gla_steps(units, {p: s_ref[p] for p in pairs}, causal, tri)
    for p in pairs:
        s_ref[p] = states[p]
    for o, (rows, vs) in zip(outs, where):
        heads = []
        for hh in range(2):
            oh = o[:, hh * GLA_VAL_DIM:(hh + 1) * GLA_VAL_DIM]
            ms = jnp.mean(oh * oh, axis=-1, keepdims=True)
            heads.append(oh * lax.rsqrt(ms + NORM_EPS) * nw)
        on = jnp.concatenate(heads, axis=1)
        o_ref[rows, vs] = (on * rg_ref[rows, vs].astype(F32)).astype(BF16)


GLA_GRID_STEPS = 8


def _mixers_body(sink_ref, q_ref, kp_ref, kc_ref, km_ref, vp_ref, vc_ref, vm_ref,
                 s0_ref, qg_ref, kg_ref, vg_ref, gk_ref, rg_ref, nw_ref, ao_ref, go_ref, s_ref):
    _gla_body(s0_ref, qg_ref, kg_ref, vg_ref, gk_ref, rg_ref, nw_ref, go_ref, s_ref)
    _attn_body(sink_ref, q_ref, kp_ref, kc_ref, km_ref, vp_ref, vc_ref, vm_ref, ao_ref)


def _mixers_call(q, kd, vd, km, vm, sinks, s0, qg, kg, vg, gk, rg, nw, batch, seq):
    step = ATTN_STEP_BLOCKS * BLOCK
    assert seq % step == 0 and step == GLA_GRID_STEPS * GLA_STEP
    seq_spec = lambda width: pl.BlockSpec((None, step, width), lambda b, i, s: (b, i, 0))
    prev = pl.BlockSpec((None, BLOCK, ATTN_KV_DUP_W),
                        lambda b, i, s: (b, jnp.maximum(i * ATTN_STEP_BLOCKS - 1, 0), 0))
    const = lambda shape: pl.BlockSpec(shape, lambda b, i, s: (0,) * len(shape))
    meta = const((BLOCK, ATTN_KV_DUP_W))
    grid_spec = pltpu.PrefetchScalarGridSpec(
        num_scalar_prefetch=1,
        grid=(batch, seq // step),
        in_specs=[seq_spec(ATTN_Q_W), prev, seq_spec(ATTN_KV_DUP_W), meta,
                  prev, seq_spec(ATTN_KV_DUP_W), meta,
                  const((GLA_HEADS // 2, GLA_PAIR_K, GLA_PAIR_V)),
                  seq_spec(GLA_K_W), seq_spec(GLA_K_W), seq_spec(GLA_V_W), seq_spec(GLA_K_W),
                  seq_spec(GLA_V_W), const((1, GLA_VAL_DIM))],
        out_specs=[seq_spec(ATTN_Q_W), seq_spec(GLA_V_W)],
        scratch_shapes=[pltpu.VMEM((GLA_HEADS // 2, GLA_PAIR_K, GLA_PAIR_V), F32)],
    )
    return pl.pallas_call(
        _mixers_body,
        grid_spec=grid_spec,
        out_shape=[jax.ShapeDtypeStruct((batch, seq, ATTN_Q_W), BF16),
                   jax.ShapeDtypeStruct((batch, seq, GLA_V_W), BF16)],
        compiler_params=pltpu.CompilerParams(dimension_semantics=("arbitrary", "arbitrary"),
                                             vmem_limit_bytes=VMEM_LIMIT),
        name="mixers",
    )(sinks, q, kd, kd, km, vd, vd, vm, s0, qg, kg, vg, gk, rg, nw)


ROUTER_W = LANES


RT_E0, RT_E1, RT_W0, RT_W1 = 0, 1, 2, 3


MERGE_SPLIT = 4


def _merge_body(x_ref, ao_ref, go_ref, sga_ref, sgg_ref, wab_ref, wgb_ref, wo_ref, n2_ref,
                wr2_ref, h_out, xn_out, rt_out, rtt_out, cnt_out):
    @pl.when(pl.program_id(0) == 0)
    def _():
        cnt_out[...] = jnp.zeros_like(cnt_out)

    rows_per = x_ref.shape[0] // MERGE_SPLIT
    for part in range(MERGE_SPLIT):
        _merge_rows(slice(part * rows_per, (part + 1) * rows_per), x_ref, ao_ref, go_ref, sga_ref, sgg_ref,
                    wab_ref, wgb_ref, wo_ref, n2_ref, wr2_ref, h_out, xn_out, rt_out, rtt_out, cnt_out)


def _merge_rows(rows, x_ref, ao_ref, go_ref, sga_ref, sgg_ref, wab_ref, wgb_ref, wo_ref, n2_ref,
                wr2_ref, h_out, xn_out, rt_out, rtt_out, cnt_out):
    ya = _dot(ao_ref[rows, :], wab_ref[...])
    yg = _dot(go_ref[rows, :], wgb_ref[...])
    merged = sga_ref[rows, :].astype(F32) * ya + sgg_ref[rows, :].astype(F32) * yg
    h = x_ref[rows, :] + _dot(merged.astype(BF16), wo_ref[...])
    xn = _rms_norm(h, n2_ref[...])
    for c in range(ROW_TILES):
        h_out[c, rows, :] = h[:, c * LANES:(c + 1) * LANES]
        xn_out[c, rows, :] = xn[:, c * LANES:(c + 1) * LANES]

    xh, xl = _split_hi_lo(xn)
    wr2 = wr2_ref[...]
    lg2 = _dot(xh, wr2) + _dot(xl, wr2)
    lg = lg2[:, :ROUTER_W] + lg2[:, ROUTER_W:]
    lg_t = lg.T
    tm = lg.shape[0]
    neg = -jnp.inf
    big = jnp.int32(LANES)
    gl = lg_t[N_EXPERTS:N_EXPERTS + N_GROUPS, :]
    grow = lax.broadcasted_iota(jnp.int32, (N_GROUPS, tm), 0)
    gmax = jnp.max(gl, axis=0, keepdims=True)
    g_idx = jnp.min(jnp.where(gl == gmax, grow, big), axis=0, keepdims=True)
    g_w = 1.0 / jnp.sum(jnp.exp(gl - gmax), axis=0, keepdims=True)
    erow = lax.broadcasted_iota(jnp.int32, (N_EXPERTS, tm), 0)
    group_shift = EXPERTS_PER_GROUP.bit_length() - 1
    el = jnp.where((erow >> group_shift) == g_idx, lg_t[0:N_EXPERTS, :], neg)
    m1 = jnp.max(el, axis=0, keepdims=True)
    i1 = jnp.min(jnp.where(el == m1, erow, big), axis=0, keepdims=True)
    el2 = jnp.where(erow == i1, neg, el)
    m2 = jnp.max(el2, axis=0, keepdims=True)
    i2 = jnp.min(jnp.where(el2 == m2, erow, big), axis=0, keepdims=True)
    e2 = jnp.exp(m2 - m1)
    p1 = g_w / (1.0 + e2)
    p2 = g_w * e2 / (1.0 + e2)
    def record(rows):
        r = lax.broadcasted_iota(jnp.int32, (rows, tm), 0)
        return jnp.where(r == RT_E0, i1.astype(F32),
                         jnp.where(r == RT_E1, i2.astype(F32),
                                   jnp.where(r == RT_W0, p1, jnp.where(r == RT_W1, p2, 0.0))))
    rtt_out[:, rows] = record(8)
    rt_out[rows, :] = record(LANES).T
    picked = jnp.where(jnp.logical_or(erow == i1, erow == i2), 1.0, 0.0)
    cnt_out[...] += jnp.sum(picked, axis=1, keepdims=True)


def _merge_call(x2d, ao, go, sga, sgg, w, tm):
    n = x2d.shape[0]
    row = lambda width: pl.BlockSpec((tm, width), lambda i: (i, 0))
    return pl.pallas_call(
        _merge_body,
        grid=(n // tm,),
        in_specs=[row(D_MODEL), row(ATTN_Q_W), row(GLA_V_W), row(D_MODEL), row(D_MODEL),
                  _const_spec((ATTN_Q_W, D_MODEL)), _const_spec((GLA_V_W, D_MODEL)),
                  _const_spec((D_MODEL, D_MODEL)), _const_spec((1, D_MODEL)),
                  _const_spec((D_MODEL, 2 * ROUTER_W))],
        out_specs=[pl.BlockSpec((ROW_TILES, tm, LANES), lambda i: (0, i, 0)),
                   pl.BlockSpec((ROW_TILES, tm, LANES), lambda i: (0, i, 0)), row(ROUTER_W),
                   pl.BlockSpec((8, tm), lambda i: (0, i)), _const_spec((N_EXPERTS, LANES))],
        out_shape=[jax.ShapeDtypeStruct((ROW_TILES, n, LANES), F32),
                   jax.ShapeDtypeStruct((ROW_TILES, n, LANES), F32),
                   jax.ShapeDtypeStruct((n, ROUTER_W), F32),
                   jax.ShapeDtypeStruct((8, n), F32),
                   jax.ShapeDtypeStruct((N_EXPERTS, LANES), F32)],
        compiler_params=pltpu.CompilerParams(dimension_semantics=("arbitrary",),
                                             vmem_limit_bytes=VMEM_LIMIT),
        name="merge",
    )(x2d, ao, go, sga, sgg, w["wab"], w["wgb"], w["wo"], w["norm2"], w["wr2"])


EXPERT_TILE = 512
EXPERT_SPLIT = 2
ROUTE_TILE = 512
VISIT_CAP = 256
VIS_EXPERT, VIS_TILE, VIS_LO, VIS_HI, VIS_COUNT = 0, 1, 2, 3, 4


def _num_visits(n_tokens):
    return (2 * n_tokens) // EXPERT_TILE + N_EXPERTS - 1


def _expert_cumsum_inclusive(x):
    row = lax.broadcasted_iota(jnp.int32, x.shape, 0)
    s = 1
    while s < N_EXPERTS:
        x = x + jnp.where(row >= s, pltpu.roll(x, s, 0), 0.0)
        s *= 2
    return x


def _route_body(rtt_ref, cnt_ref, pos_ref, vis_ref, carry_ref):
    i = pl.program_id(0)
    tm = rtt_ref.shape[1]
    cnt = cnt_ref[...]
    end = _expert_cumsum_inclusive(cnt)
    base = end - cnt

    @pl.when(i == 0)
    def _():
        carry_ref[...] = jnp.zeros_like(carry_ref)
        inv = 1.0 / EXPERT_TILE
        first = jnp.floor(base * inv)
        n_vis = jnp.where(cnt > 0.0, jnp.floor((end - 1.0) * inv) - first + 1.0, 0.0)
        vend = _expert_cumsum_inclusive(n_vis)
        vstart = vend - n_vis
        col = lambda a: a[:, 0:1]
        erow = lax.broadcasted_iota(jnp.int32, (N_EXPERTS, VISIT_CAP), 0).astype(F32)
        v = lax.broadcasted_iota(jnp.int32, (N_EXPERTS, VISIT_CAP), 1).astype(F32)
        ev = jnp.minimum(jnp.sum(jnp.where(col(vend) <= v, 1.0, 0.0), axis=0, keepdims=True),
                         N_EXPERTS - 1.0)
        mine = erow == ev
        pick = lambda a: jnp.sum(jnp.where(mine, a, 0.0), axis=0, keepdims=True)
        tile = pick(col(first) + v - col(vstart))
        lo = jnp.maximum(pick(col(base)) - tile * EXPERT_TILE, 0.0)
        hi = jnp.minimum(pick(col(end)) - tile * EXPERT_TILE, float(EXPERT_TILE))
        total = vend[N_EXPERTS - 1:N_EXPERTS, 0:1]
        r8 = lax.broadcasted_iota(jnp.int32, (8, VISIT_CAP), 0)
        rec = jnp.where(r8 == VIS_EXPERT, ev,
                        jnp.where(r8 == VIS_TILE, tile,
                                  jnp.where(r8 == VIS_LO, lo,
                                            jnp.where(r8 == VIS_HI, hi,
                                                      jnp.where(r8 == VIS_COUNT, total, 0.0)))))
        vis_ref[...] = rec.astype(jnp.int32)

    rtt = rtt_ref[...]
    erow = lax.broadcasted_iota(jnp.int32, (N_EXPERTS, tm), 0).astype(F32)
    oh0 = erow == rtt[RT_E0:RT_E0 + 1, :]
    oh1 = erow == rtt[RT_E1:RT_E1 + 1, :]
    picked = jnp.where(jnp.logical_or(oh0, oh1), 1.0, 0.0)
    r = lax.broadcasted_iota(jnp.int32, (tm, tm), 0)
    c = lax.broadcasted_iota(jnp.int32, (tm, tm), 1)
    earlier = jnp.where(r < c, 1.0, 0.0).astype(BF16)
    slot = _dot(picked.astype(BF16), earlier) + (carry_ref[:, 0:1] + base[:, 0:1])
    p0 = jnp.sum(jnp.where(oh0, slot, 0.0), axis=0, keepdims=True)
    p1 = jnp.sum(jnp.where(oh1, slot, 0.0), axis=0, keepdims=True)
    r8 = lax.broadcasted_iota(jnp.int32, (8, tm), 0)
    pos_ref[...] = jnp.where(r8 == 0, p0, jnp.where(r8 == 1, p1, 0.0)).astype(jnp.int32)
    carry_ref[...] += jnp.sum(picked, axis=1, keepdims=True)


def _route_call(rtt, cnt):
    n = rtt.shape[1]
    tm = _pick_tile(n, ROUTE_TILE)
    return pl.pallas_call(
        _route_body,
        grid=(n // tm,),
        in_specs=[pl.BlockSpec((8, tm), lambda i: (0, i)), _const_spec((N_EXPERTS, LANES))],
        out_specs=[pl.BlockSpec((8, tm), lambda i: (0, i)), _const_spec((8, VISIT_CAP))],
        out_shape=[jax.ShapeDtypeStruct((8, n), jnp.int32),
                   jax.ShapeDtypeStruct((8, VISIT_CAP), jnp.int32)],
        scratch_shapes=[pltpu.VMEM((N_EXPERTS, LANES), F32)],
        compiler_params=pltpu.CompilerParams(dimension_semantics=("arbitrary",)),
        name="route",
    )(rtt, cnt)


SCATTER_TILE = 512
ROW_TILES = D_MODEL // LANES


def _store_slabs(ref, val):
    for c in range(ROW_TILES):
        ref[c] = val[:, c * LANES:(c + 1) * LANES]


def _load_slabs(ref):
    return jnp.concatenate([ref[c] for c in range(ROW_TILES)], axis=1)


def _scatter_body(pos0_ref, pos1_ref, h_ref, hs_hbm, sem):
    tm = h_ref.shape[1]
    for r in range(tm):
        src = h_ref.at[:, r, :]
        pltpu.make_async_copy(src, hs_hbm.at[:, pos0_ref[r], :], sem).start(priority=0)
        pltpu.make_async_copy(src, hs_hbm.at[:, pos1_ref[r], :], sem).start(priority=1)
    for _ in range(2 * tm):
        pltpu.make_async_copy(h_ref.at[:, 0, :], hs_hbm.at[:, 0, :], sem).wait()


def _scatter_call(pos0, pos1, h_slabs, n_slots):
    n = h_slabs.shape[1]
    tm = _pick_tile(n, SCATTER_TILE)
    smem = lambda: pl.BlockSpec((tm,), lambda i: (i,), memory_space=pltpu.SMEM)
    return pl.pallas_call(
        _scatter_body,
        grid=(n // tm,),
        in_specs=[smem(), smem(), pl.BlockSpec((ROW_TILES, tm, LANES), lambda i: (0, i, 0))],
        out_specs=pl.BlockSpec(memory_space=pl.ANY),
        out_shape=jax.ShapeDtypeStruct((ROW_TILES, n_slots, LANES), F32),
        scratch_shapes=[pltpu.SemaphoreType.DMA],
        compiler_params=pltpu.CompilerParams(dimension_semantics=("arbitrary",)),
        name="scatter",
    )(pos0, pos1, h_slabs)


EXPERT_RING = 3


def _expert_body(vis_ref, xs_hbm, wg_ref, wu_ref, wd_ref, o_ref, x_ring, sem):
    j = pl.program_id(0)
    count = vis_ref[VIS_COUNT, 0]

    def tile_copy(v, slot):
        start = pl.multiple_of(vis_ref[VIS_TILE, v] * EXPERT_TILE, EXPERT_TILE)
        return pltpu.make_async_copy(xs_hbm.at[:, pl.ds(start, EXPERT_TILE), :], x_ring.at[slot],
                                     sem.at[slot])

    @pl.when(j == 0)
    def _():
        for v in range(EXPERT_RING - 1):
            @pl.when(v < count)
            def _():
                tile_copy(v, v).start()

    ahead = j + EXPERT_RING - 1

    @pl.when(ahead < count)
    def _():
        tile_copy(ahead, ahead % EXPERT_RING).start()

    @pl.when(j < count)
    def _():
        slot = j % EXPERT_RING
        tile_copy(j, slot).wait()
        x_ref = x_ring.at[slot]
        lo_row = vis_ref[VIS_LO, j]
        hi_row = vis_ref[VIS_HI, j]

        @pl.when(lo_row == 0)
        def _():
            o_ref[...] = jnp.zeros_like(o_ref)

        wg = wg_ref[...].astype(BF16)
        wu = wu_ref[...].astype(BF16)
        wd = wd_ref[...].astype(BF16)
        rows_per = EXPERT_TILE // EXPERT_SPLIT
        for part in range(EXPERT_SPLIT):
            rows = slice(part * rows_per, (part + 1) * rows_per)
            x = jnp.concatenate([x_ref[c, rows, :] for c in range(ROW_TILES)], axis=1).astype(BF16)
            g = _dot(x, wg)
            u = _dot(x, wu)
            hg = (g * _sigmoid(g) * u).astype(BF16)
            new = _dot(hg, wd)
            row = lax.broadcasted_iota(jnp.int32, (rows_per, LANES), 0) + part * rows_per
            keep = jnp.logical_and(row >= lo_row, row < hi_row)
            for c in range(ROW_TILES):
                o_ref[c, rows, :] = jnp.where(keep, new[:, c * LANES:(c + 1) * LANES], o_ref[c, rows, :])


def _expert_call(vis, hs, wg, wu, wd, n_visits):
    def visit(j, vis):
        return jnp.minimum(j, vis[VIS_COUNT, 0] - 1)

    def xmap(j, vis):
        return (0, vis[VIS_TILE, visit(j, vis)], 0)

    def wmap(j, vis):
        return (vis[VIS_EXPERT, visit(j, vis)], 0, 0)

    grid_spec = pltpu.PrefetchScalarGridSpec(
        num_scalar_prefetch=1,
        grid=(n_visits,),
        in_specs=[pl.BlockSpec(memory_space=pl.ANY),
                  pl.BlockSpec((None, D_MODEL, EXPERT_FF), wmap),
                  pl.BlockSpec((None, D_MODEL, EXPERT_FF), wmap),
                  pl.BlockSpec((None, EXPERT_FF, D_MODEL), wmap)],
        out_specs=pl.BlockSpec((ROW_TILES, EXPERT_TILE, LANES), xmap),
        scratch_shapes=[pltpu.VMEM((EXPERT_RING, ROW_TILES, EXPERT_TILE, LANES), F32),
                        pltpu.SemaphoreType.DMA((EXPERT_RING,))],
    )
    return pl.pallas_call(
        _expert_body,
        grid_spec=grid_spec,
        out_shape=jax.ShapeDtypeStruct(hs.shape, F32),
        compiler_params=pltpu.CompilerParams(dimension_semantics=("arbitrary",),
                                             vmem_limit_bytes=VMEM_LIMIT),
        name="experts",
    )(vis, hs, wg, wu, wd)


COMBINE_TILE = 256


def _combine_body(p0_ref, p1_ref, p0n_ref, p1n_ref, h_ref, rt_ref, y_hbm, o_ref, g_ref, sem):
    i = pl.program_id(0)
    tm = h_ref.shape[1]
    cur = i % 2

    def issue(q0_ref, q1_ref, buf):
        for r in range(tm):
            pltpu.make_async_copy(y_hbm.at[:, q0_ref[r], :], g_ref.at[buf, 0, :, r, :],
                                  sem.at[buf]).start(priority=0)
            pltpu.make_async_copy(y_hbm.at[:, q1_ref[r], :], g_ref.at[buf, 1, :, r, :],
                                  sem.at[buf]).start(priority=1)

    @pl.when(i == 0)
    def _():
        issue(p0_ref, p1_ref, 0)

    @pl.when(i + 1 < pl.num_programs(0))
    def _():
        issue(p0n_ref, p1n_ref, 1 - cur)

    for _ in range(2 * tm):
        pltpu.make_async_copy(y_hbm.at[:, 0, :], g_ref.at[cur, 0, :, 0, :], sem.at[cur]).wait()
    rt = rt_ref[...]
    o_ref[...] = (_load_slabs(h_ref) + rt[:, RT_W0:RT_W0 + 1] * _load_slabs(g_ref.at[cur, 0])
                  + rt[:, RT_W1:RT_W1 + 1] * _load_slabs(g_ref.at[cur, 1]))


def _combine_call(pos0, pos1, h_slabs, rt, y):
    n = h_slabs.shape[1]
    tm = _pick_tile(n, COMBINE_TILE)
    steps = n // tm
    smem_cur = lambda: pl.BlockSpec((tm,), lambda i: (i,), memory_space=pltpu.SMEM)
    smem_next = lambda: pl.BlockSpec((tm,), lambda i: (jnp.minimum(i + 1, steps - 1),),
                                     memory_space=pltpu.SMEM)
    row = lambda width: pl.BlockSpec((tm, width), lambda i: (i, 0))
    return pl.pallas_call(
        _combine_body,
        grid=(steps,),
        in_specs=[smem_cur(), smem_cur(), smem_next(), smem_next(),
                  pl.BlockSpec((ROW_TILES, tm, LANES), lambda i: (0, i, 0)), row(ROUTER_W),
                  pl.BlockSpec(memory_space=pl.ANY)],
        out_specs=row(D_MODEL),
        out_shape=jax.ShapeDtypeStruct((n, D_MODEL), F32),
        scratch_shapes=[pltpu.VMEM((2, 2, ROW_TILES, tm, LANES), F32), pltpu.SemaphoreType.DMA((2,))],
        compiler_params=pltpu.CompilerParams(dimension_semantics=("arbitrary",),
                                             vmem_limit_bytes=VMEM_LIMIT),
        name="combine",
    )(pos0, pos1, pos0, pos1, h_slabs, rt, y)


def _rope_tables(pos):
    half = HEAD_DIM // 2
    inv_freq = ROPE_THETA ** (-(jnp.arange(half, dtype=F32) * 2.0) / HEAD_DIM)
    ang = pos.astype(F32)[:, None] * inv_freq[None, :]
    cos = jnp.cos(ang)
    sin = jnp.sin(ang)
    cos64 = jnp.concatenate([cos, cos], axis=1)
    sin64 = jnp.concatenate([-sin, sin], axis=1)
    return jnp.tile(cos64, (1, LANES // HEAD_DIM)), jnp.tile(sin64, (1, LANES // HEAD_DIM))


WPREP_ROWS = 128


def _wprep_body(w_ref, qkv_out, gla_out, gate_out):
    w = w_ref[...]
    lane = lax.broadcasted_iota(jnp.int32, (WPREP_ROWS, LANES), 1)
    qkv_out[:, :QKV_GD_OFF] = w[:, :QKV_GD_OFF].astype(BF16)
    gd_tile = w[:, _OFF_GD:_OFF_GD + LANES]
    qkv_out[:, QKV_GD_OFF:QKV_GD_OFF + LANES] = jnp.where(lane < GLA_GATE_RANK, gd_tile, 0.0).astype(BF16)
    qkv_out[:, QKV_GD_OFF + LANES:] = jnp.zeros((WPREP_ROWS, QKV_W - QKV_GD_OFF - LANES), BF16)
    gla_out[...] = w[:, _OFF_GQ:_OFF_GQ + GLA_W].astype(BF16)
    keep = LANES - GLA_GATE_RANK
    n_tiles = 2 * D_MODEL // LANES
    rolled = [pltpu.roll(w[:, _OFF_GD + t * LANES:_OFF_GD + (t + 1) * LANES], keep, 1)
              for t in range(n_tiles)]
    tail = w[:, _OFF_GD + n_tiles * LANES:]
    tail = jnp.concatenate([tail, jnp.zeros((WPREP_ROWS, LANES - tail.shape[1]), F32)], axis=1)
    rolled.append(pltpu.roll(tail, keep, 1))
    for t in range(n_tiles):
        gate_out[:, t * LANES:(t + 1) * LANES] = jnp.where(lane < keep, rolled[t], rolled[t + 1]).astype(BF16)


def _wprep_call(w_in):
    in_total = w_in.shape[2]
    assert in_total == _OFF_GATE + 2 * D_MODEL and D_MODEL % WPREP_ROWS == 0
    row = lambda width: pl.BlockSpec((WPREP_ROWS, width), lambda i: (i, 0))
    return pl.pallas_call(
        _wprep_body,
        grid=(D_MODEL // WPREP_ROWS,),
        in_specs=[pl.BlockSpec((None, WPREP_ROWS, in_total), lambda i: (0, i, 0))],
        out_specs=[row(QKV_W), row(GLA_W), row(2 * D_MODEL)],
        out_shape=[jax.ShapeDtypeStruct((D_MODEL, QKV_W), BF16),
                   jax.ShapeDtypeStruct((D_MODEL, GLA_W), BF16),
                   jax.ShapeDtypeStruct((D_MODEL, 2 * D_MODEL), BF16)],
        compiler_params=pltpu.CompilerParams(dimension_semantics=("arbitrary",)),
        name="wprep",
    )(w_in)


def _prep_weights(norm1_w, w_in, q_norm_w, k_norm_w, gla_gate_up, gla_gate_bias, w_attn_branch,
                  w_gla_branch, w_out, norm2_w, router_group, router_expert):
    wqkv, wgla, wgate = _wprep_call(w_in)
    gup = jnp.zeros((LANES, GLA_K_W), BF16).at[:GLA_GATE_RANK].set(gla_gate_up.astype(BF16))
    wr = jnp.zeros((D_MODEL, ROUTER_W), F32)
    wr = wr.at[:, :N_EXPERTS].set(router_expert.transpose(1, 0, 2).reshape(D_MODEL, N_EXPERTS))
    wr = wr.at[:, N_EXPERTS:N_EXPERTS + N_GROUPS].set(router_group)
    wrh = wr.astype(BF16)
    wrl = (wr - wrh.astype(F32)).astype(BF16)
    return {
        "norm1": norm1_w.reshape(1, D_MODEL),
        "wqkv": wqkv,
        "wgla": wgla,
        "wgate": wgate,
        "qn": jnp.tile(q_norm_w, ATTN_HEADS).reshape(1, ATTN_Q_W),
        "kn": jnp.tile(k_norm_w, ATTN_KV_HEADS).reshape(1, ATTN_KV_W),
        "gup": gup,
        "gb": gla_gate_bias.reshape(1, GLA_K_W),
        "wab": w_attn_branch.astype(BF16),
        "wgb": w_gla_branch.astype(BF16),
        "wo": w_out.astype(BF16),
        "norm2": norm2_w.reshape(1, D_MODEL),
        "wr2": jnp.concatenate([wrh, wrl], axis=1),
    }


def _pick_tile(n, want):
    t = want
    while n % t:
        t //= 2
    return t


def kernel(x, meta_tokens, norm1_w, w_in, q_norm_w, k_norm_w, attn_sinks, gla_gate_up, gla_gate_bias,
           gla_norm_w, w_attn_branch, w_gla_branch, w_out, norm2_w, router_group, router_expert,
           expert_w_gate, expert_w_up, expert_w_down):
    batch, seq, d = x.shape
    assert d == D_MODEL and seq % GLA_STEP == 0 and norm1_w.shape[0] == 1
    n = batch * seq
    w = _prep_weights(norm1_w[0], w_in, q_norm_w[0], k_norm_w[0], gla_gate_up[0], gla_gate_bias[0],
                      w_attn_branch[0], w_gla_branch[0], w_out[0], norm2_w[0], router_group[0],
                      router_expert[0])
    x2d = x.reshape(n, D_MODEL)

    cos_m, sin_m = _rope_tables(jnp.arange(N_META))
    cos_r, sin_r = _rope_tables(jnp.arange(seq) + N_META)
    meta = _proj_call(meta_tokens.astype(F32), N_META, 1, w, cos_m, sin_m)
    tm = _pick_tile(seq, 512)
    (q, kd, vd, qg, kg, vg, rg, gk, sga, sgg) = _proj_call(x2d, tm, seq // tm, w, cos_r, sin_r)

    pad_after = ((0, BLOCK - N_META), (0, 0))
    km = jnp.pad(meta[1], pad_after)
    vm = jnp.pad(meta[2], pad_after)
    pad_before = ((GLA_STEP - N_META, 0), (0, 0))
    s0 = _gla_init_call(jnp.pad(meta[4], pad_before), jnp.pad(meta[5], pad_before),
                        jnp.pad(meta[7], pad_before))
    ao, go = _mixers_call(q.reshape(batch, seq, ATTN_Q_W), kd.reshape(batch, seq, ATTN_KV_DUP_W),
                          vd.reshape(batch, seq, ATTN_KV_DUP_W), km, vm, attn_sinks[0].astype(F32),
                          s0, qg.reshape(batch, seq, GLA_K_W), kg.reshape(batch, seq, GLA_K_W),
                          vg.reshape(batch, seq, GLA_V_W), gk.reshape(batch, seq, GLA_K_W),
                          rg.reshape(batch, seq, GLA_V_W), gla_norm_w[0].reshape(1, GLA_VAL_DIM),
                          batch, seq)

    h, xn2, rt, rtt, cnt = _merge_call(x2d, ao.reshape(n, ATTN_Q_W), go.reshape(n, GLA_V_W), sga, sgg, w,
                                       _pick_tile(n, 1024))

    assert (2 * n) % EXPERT_TILE == 0 and _num_visits(n) <= VISIT_CAP
    pos, vis = _route_call(rtt, cnt)
    pos0, pos1 = pos[0], pos[1]
    xs = _scatter_call(pos0, pos1, xn2, 2 * n)
    wg = expert_w_gate[0].reshape(N_EXPERTS, D_MODEL, EXPERT_FF)
    wu = expert_w_up[0].reshape(N_EXPERTS, D_MODEL, EXPERT_FF)
    wd = expert_w_down[0].reshape(N_EXPERTS, EXPERT_FF, D_MODEL)
    y = _expert_call(vis, xs, wg, wu, wd, _num_visits(n))
    out = _combine_call(pos0, pos1, h, rt, y)
    return out.reshape(batch, seq, D_MODEL)
```

```python
import math

import jax
import jax.numpy as jnp
from jax import lax
from jax.experimental import pallas as pl
from jax.experimental.pallas import tpu as pltpu

F32 = jnp.float32
BF16 = jnp.bfloat16

D_MODEL = 1024
N_META = 16
BLOCK = 128
ATTN_HEADS = 8
ATTN_KV_HEADS = 2
HEAD_DIM = 64
ROPE_THETA = 10000.0
ATTN_Q_W = ATTN_HEADS * HEAD_DIM
ATTN_KV_W = ATTN_KV_HEADS * HEAD_DIM
GLA_HEADS = 4
GLA_KEY_DIM = 64
GLA_VAL_DIM = 128
GLA_K_W = GLA_HEADS * GLA_KEY_DIM
GLA_V_W = GLA_HEADS * GLA_VAL_DIM
GLA_GATE_RANK = 16
GLA_GATE_NORM = 16.0
GLA_CHUNK = 64
N_GROUPS = 4
EXPERTS_PER_GROUP = 8
N_EXPERTS = N_GROUPS * EXPERTS_PER_GROUP
EXPERT_FF = 256
NORM_EPS = 1e-6
MASK_VALUE = -1e30
assert GLA_CHUNK & (GLA_CHUNK - 1) == 0 and EXPERTS_PER_GROUP & (EXPERTS_PER_GROUP - 1) == 0

LANES = 128
ATTN_KV_DUP_W = 2 * ATTN_KV_W
VMEM_LIMIT = 56 * 1024 * 1024

_OFF_Q = 0
_OFF_K = _OFF_Q + ATTN_Q_W
_OFF_V = _OFF_K + ATTN_KV_W
_OFF_GQ = _OFF_V + ATTN_KV_W
_OFF_GD = _OFF_GQ + 2 * GLA_K_W + 2 * GLA_V_W
_OFF_GATE = _OFF_GD + GLA_GATE_RANK
GLA_W = 2 * GLA_K_W + 2 * GLA_V_W
QKV_GD_OFF = _OFF_GQ
QKV_W = D_MODEL


def _dot(a, b):
    return jnp.dot(a, b, preferred_element_type=F32)


def _dot_nt(a, b):
    return lax.dot_general(a, b, (((1,), (1,)), ((), ())), preferred_element_type=F32)


def _split_hi_lo(x):
    hi = x.astype(BF16)
    lo = (x - hi.astype(F32)).astype(BF16)
    return hi, lo


MXU_TILE = 256


def _group_mean_sq(x, group):
    w = x.shape[-1]
    slab = min(w, MXU_TILE)
    shift = int(math.log2(group))
    r = lax.broadcasted_iota(jnp.int32, (slab, slab), 0) >> shift
    c = lax.broadcasted_iota(jnp.int32, (slab, slab), 1) >> shift
    ones = jnp.where(r == c, 1.0, 0.0).astype(BF16)
    hi, lo = _split_hi_lo(x * x)
    sums = [_dot(hi[:, s:s + slab], ones) + _dot(lo[:, s:s + slab], ones) for s in range(0, w, slab)]
    return (sums[0] if len(sums) == 1 else jnp.concatenate(sums, axis=1)) * (1.0 / group)


def _rope(x, cos, sin_signed):
    w = x.shape[-1]
    reps = w // LANES
    if reps > 1:
        cos = jnp.concatenate([cos] * reps, axis=1)
        sin_signed = jnp.concatenate([sin_signed] * reps, axis=1)
    lane = lax.broadcasted_iota(jnp.int32, x.shape, 1)
    first_half = (lane & (HEAD_DIM // 2)) == 0
    swapped = jnp.where(first_half, pltpu.roll(x, w - HEAD_DIM // 2, 1), pltpu.roll(x, HEAD_DIM // 2, 1))
    return x * cos + swapped * sin_signed


def _rms_norm(x, w):
    ms = jnp.mean(x * x, axis=-1, keepdims=True)
    return x * lax.rsqrt(ms + NORM_EPS) * w


def _sigmoid(x):
    return 0.5 * jnp.tanh(0.5 * x) + 0.5


def _log_sigmoid(x):
    return jnp.minimum(x, 0.0) - jnp.log(1.0 + jnp.exp(-jnp.abs(x)))


def _dup_heads_on_lanes(x):
    low = lax.broadcasted_iota(jnp.int32, x.shape, 1) < HEAD_DIM
    swapped = pltpu.roll(x, HEAD_DIM, 1)
    return jnp.concatenate([jnp.where(low, x, swapped), jnp.where(low, swapped, x)], axis=1)


def _proj_body(x_ref, n1_ref, wqkv_ref, wgla_ref, wgate_ref,
               qn_ref, kn_ref, cos_ref, sin_ref, gup_ref, gb_ref,
               q_out, k_out, v_out, qg_out, kg_out, vg_out, rg_out, gk_out, sga_out, sgg_out):
    xn = _rms_norm(x_ref[...], n1_ref[...]).astype(BF16)
    cos = cos_ref[...]
    sin = sin_ref[...]

    a = _dot(xn, wqkv_ref[...])
    q = a[:, :ATTN_Q_W]
    q = q * lax.rsqrt(_group_mean_sq(q, HEAD_DIM) + NORM_EPS) * qn_ref[...]
    q_out[...] = (_rope(q, cos, sin) * (HEAD_DIM ** -0.5)).astype(BF16)

    k = a[:, _OFF_K:_OFF_K + ATTN_KV_W]
    k = k * lax.rsqrt(_group_mean_sq(k, HEAD_DIM) + NORM_EPS) * kn_ref[...]
    k_out[...] = _dup_heads_on_lanes(_rope(k, cos, sin)).astype(BF16)

    v_out[...] = _dup_heads_on_lanes(a[:, _OFF_V:_OFF_V + ATTN_KV_W]).astype(BF16)

    g = _dot(xn, wgla_ref[...])
    qg_out[...] = (g[:, :GLA_K_W] * (GLA_KEY_DIM ** -0.5)).astype(BF16)
    kg_out[...] = g[:, GLA_K_W:2 * GLA_K_W].astype(BF16)
    vg_out[...] = g[:, 2 * GLA_K_W:2 * GLA_K_W + GLA_V_W].astype(BF16)
    r = g[:, 2 * GLA_K_W + GLA_V_W:]
    rg_out[...] = (r * _sigmoid(r)).astype(BF16)

    gd = a[:, QKV_GD_OFF:QKV_GD_OFF + LANES].astype(BF16)
    z = _dot(gd, gup_ref[...]) + gb_ref[...]
    gk_out[...] = _log_sigmoid(z) * (1.0 / GLA_GATE_NORM)

    gates = _dot(xn, wgate_ref[...])
    sga_out[...] = _sigmoid(gates[:, :D_MODEL]).astype(BF16)
    sgg_out[...] = _sigmoid(gates[:, D_MODEL:]).astype(BF16)


def _const_spec(shape):
    nd = len(shape)
    return pl.BlockSpec(shape, lambda *_: (0,) * nd)


def _proj_call(x2d, tm, tiles_per_seq, w, cos, sin):
    n = x2d.shape[0]
    grid = (n // tm,)
    row = lambda width: pl.BlockSpec((tm, width), lambda i: (i, 0))
    tab = pl.BlockSpec((tm, LANES), lambda i: (i % tiles_per_seq, 0))
    in_specs = [
        row(D_MODEL), _const_spec((1, D_MODEL)),
        _const_spec((D_MODEL, QKV_W)), _const_spec((D_MODEL, GLA_W)),
        _const_spec((D_MODEL, 2 * D_MODEL)),
        _const_spec((1, ATTN_Q_W)), _const_spec((1, ATTN_KV_W)),
        tab, tab, _const_spec((LANES, GLA_K_W)), _const_spec((1, GLA_K_W)),
    ]
    out_widths = [ATTN_Q_W, ATTN_KV_DUP_W, ATTN_KV_DUP_W, GLA_K_W, GLA_K_W, GLA_V_W, GLA_V_W,
                  GLA_K_W, D_MODEL, D_MODEL]
    out_dtypes = [BF16] * 7 + [F32] + [BF16] * 2
    return pl.pallas_call(
        _proj_body,
        grid=grid,
        in_specs=in_specs,
        out_specs=[row(wd) for wd in out_widths],
        out_shape=[jax.ShapeDtypeStruct((n, wd), dt) for wd, dt in zip(out_widths, out_dtypes)],
        compiler_params=pltpu.CompilerParams(dimension_semantics=("arbitrary",),
                                             vmem_limit_bytes=VMEM_LIMIT),
        name="proj",
    )(x2d, w["norm1"], w["wqkv"], w["wgla"], w["wgate"],
      w["qn"], w["kn"], cos, sin, w["gup"], w["gb"])


ATTN_STEP_BLOCKS = 8


def _attn_body(sink_ref, q_ref, kp_ref, kc_ref, km_ref, vp_ref, vc_ref, vm_ref, o_ref):
    i = pl.program_id(1)
    nblk = ATTN_STEP_BLOCKS
    pair_rows = 2 * BLOCK
    lane = lax.broadcasted_iota(jnp.int32, (BLOCK, LANES), 1)
    low = lane < HEAD_DIM
    low2 = lax.broadcasted_iota(jnp.int32, (pair_rows, LANES), 1) < HEAD_DIM
    top2 = lax.broadcasted_iota(jnp.int32, (pair_rows, 1), 0) < BLOCK
    key = lax.broadcasted_iota(jnp.int32, (pair_rows, 2 * LANES), 1) & (LANES - 1)
    rowi = lax.broadcasted_iota(jnp.int32, (pair_rows, 2 * LANES), 0) & (BLOCK - 1)
    mask_meta = key < N_META
    mask_prev = key > rowi
    mask_cur = key <= rowi
    zero = jnp.zeros((), BF16)

    def block_diag(x):
        return jnp.concatenate([jnp.where(low, x, zero), jnp.where(low, zero, x)], axis=0)

    blk_row = lax.broadcasted_iota(jnp.int32, (2 * BLOCK, LANES), 0)
    blk_lane = lax.broadcasted_iota(jnp.int32, (2 * BLOCK, LANES), 1)
    ones_blk = jnp.where((blk_row < BLOCK) == (blk_lane < HEAD_DIM), 1.0, 0.0).astype(BF16)

    def masked(s, mask):
        return jnp.where(mask, s, MASK_VALUE)

    kv_heads = range(ATTN_KV_HEADS)
    s_meta_all, s_seq_all, v_seq_all = [], [], []
    for kvh in kv_heads:
        sl = slice(kvh * LANES, (kvh + 1) * LANES)
        k_seq = [block_diag(kp_ref[:, sl])]
        v_seq = [block_diag(vp_ref[:, sl])]
        for t in range(nblk):
            rows = slice(t * BLOCK, (t + 1) * BLOCK)
            k_seq.append(block_diag(kc_ref[rows, sl]))
            v_seq.append(block_diag(vc_ref[rows, sl]))
        q_all = jnp.concatenate(
            [q_ref[t * BLOCK:(t + 1) * BLOCK, (2 * kvh + pr) * LANES:(2 * kvh + pr + 1) * LANES]
             for t in range(nblk) for pr in range(2)], axis=0)
        s_meta_all.append(_dot_nt(q_all, block_diag(km_ref[:, sl])))
        s_seq = []
        for j in range(nblk + 1):
            lo_blk, hi_blk = max(j - 1, 0), min(j, nblk - 1)
            s_seq.append(_dot_nt(q_all[lo_blk * pair_rows:(hi_blk + 1) * pair_rows], k_seq[j]))
        s_seq_all.append(s_seq)
        v_seq_all.append(v_seq)

    probs_all = []
    for kvh in kv_heads:
        s_meta, s_seq = s_meta_all[kvh], s_seq_all[kvh]

        def sink_col(hh):
            return jnp.where(top2, sink_ref[4 * kvh + hh], sink_ref[4 * kvh + 2 + hh])

        p_meta, p_prev, p_cur, l_all = [], [], [], []
        for t in range(nblk):
            sm = masked(s_meta[t * pair_rows:(t + 1) * pair_rows], mask_meta)
            prev_rows = slice(0, pair_rows) if t == 0 else slice(pair_rows, 2 * pair_rows)
            sb = jnp.where(mask_prev, s_seq[t][prev_rows], s_seq[t + 1][0:pair_rows])
            if t == 0:
                sb = masked(sb, jnp.logical_or(mask_cur, i > 0))
            s_max = jnp.maximum(sm, sb)
            m2 = []
            sink_terms = []
            for hh in range(2):
                sink = sink_col(hh)
                m = jnp.maximum(jnp.max(s_max[:, hh * LANES:(hh + 1) * LANES], axis=1, keepdims=True), sink)
                m2.append(m)
                sink_terms.append(jnp.exp(sink - m))

            def probs(s):
                return jnp.concatenate([jnp.exp(s[:, hh * LANES:(hh + 1) * LANES] - m2[hh])
                                        for hh in range(2)], axis=1)

            pm, pb = probs(sm), probs(sb)
            pb16 = pb.astype(BF16)
            p_meta.append(pm.astype(BF16))
            p_prev.append(jnp.where(mask_prev, pb16, zero))
            p_cur.append(jnp.where(mask_prev, zero, pb16))
            l_all.append(_dot((pm + pb).astype(BF16), ones_blk)
                         + jnp.where(low2, sink_terms[0], sink_terms[1]))
        probs_all.append((p_meta, p_prev, p_cur, l_all))

    for kvh in kv_heads:
        sl = slice(kvh * LANES, (kvh + 1) * LANES)
        p_meta, p_prev, p_cur, l_all = probs_all[kvh]
        v_seq = v_seq_all[kvh]
        o_meta = _dot(jnp.concatenate(p_meta, axis=0), block_diag(vm_ref[:, sl]))
        o_seq = []
        for j in range(nblk + 1):
            parts = ([p_cur[j - 1]] if j >= 1 else []) + ([p_prev[j]] if j < nblk else [])
            o_seq.append(_dot(jnp.concatenate(parts, axis=0) if len(parts) > 1 else parts[0], v_seq[j]))
        for t in range(nblk):
            prev_rows = slice(0, pair_rows) if t == 0 else slice(pair_rows, 2 * pair_rows)
            o = (o_meta[t * pair_rows:(t + 1) * pair_rows] + o_seq[t][prev_rows]
                 + o_seq[t + 1][0:pair_rows]) / l_all[t]
            for pr in range(2):
                col = (2 * kvh + pr) * LANES
                o_ref[t * BLOCK:(t + 1) * BLOCK, col:col + LANES] = o[pr * BLOCK:(pr + 1) * BLOCK].astype(BF16)


GLA_STEP = 2 * GLA_CHUNK
GLA_PAIR_K = 2 * GLA_KEY_DIM
GLA_PAIR_V = 2 * GLA_VAL_DIM


def _gla_consts():
    r = lax.broadcasted_iota(jnp.int32, (GLA_STEP, 2 * GLA_STEP), 0)
    c = lax.broadcasted_iota(jnp.int32, (GLA_STEP, 2 * GLA_STEP), 1) & (GLA_STEP - 1)
    chunk_shift = GLA_CHUNK.bit_length() - 1
    causal = jnp.logical_and((r >> chunk_shift) == (c >> chunk_shift), c <= r)
    tri = jnp.where(causal[:, :GLA_STEP], 1.0, 0.0).astype(BF16)
    return causal, tri


def _gla_steps(units, states, causal, tri):
    row = lax.broadcasted_iota(jnp.int32, (GLA_STEP, GLA_PAIR_K), 0)
    lane = lax.broadcasted_iota(jnp.int32, (GLA_STEP, GLA_PAIR_K), 1)
    first = row < GLA_CHUNK
    low = lane < GLA_KEY_DIM
    lane_t = lax.broadcasted_iota(jnp.int32, (GLA_PAIR_K, GLA_STEP), 1)
    first_t = lane_t < GLA_CHUNK
    srow = lax.broadcasted_iota(jnp.int32, (GLA_PAIR_K, GLA_PAIR_V), 0)
    slane = lax.broadcasted_iota(jnp.int32, (GLA_PAIR_K, GLA_PAIR_V), 1)
    diag = (srow < GLA_KEY_DIM) == (slane < GLA_VAL_DIM)
    zero = jnp.zeros((), BF16)
    zf = jnp.zeros((), F32)
    zv = jnp.zeros((GLA_STEP, GLA_VAL_DIM), BF16)
    half = GLA_CHUNK // 2
    n = len(units)

    gk = jnp.concatenate([u[4] for u in units], axis=1) if n > 1 else units[0][4]
    g_hi = gk.astype(BF16)
    g_r = gk - g_hi.astype(F32)
    g_mid = g_r.astype(BF16)
    g_lo = (g_r - g_mid.astype(F32)).astype(BF16)
    b_all = _dot(tri, g_hi) + _dot(tri, g_mid) + _dot(tri, g_lo)

    prep = []
    for i, (_, q2, k2, v2, _) in enumerate(units):
        b = b_all[:, i * GLA_PAIR_K:(i + 1) * GLA_PAIR_K]
        b_mid = jnp.where(first, b[half:half + 1, :], b[GLA_CHUNK + half:GLA_CHUNK + half + 1, :])
        qi = (q2 * jnp.exp(b - b_mid)).astype(BF16)
        ki = (k2 * jnp.exp(b_mid - b)).astype(BF16)
        qx = q2 * jnp.exp(b)
        kblk = jnp.concatenate([jnp.where(low, ki, zero), jnp.where(low, zero, ki)], axis=0)
        vblk = jnp.concatenate([jnp.concatenate([v2[:, :GLA_VAL_DIM], zv], axis=1),
                                jnp.concatenate([zv, v2[:, GLA_VAL_DIM:]], axis=1)], axis=0)
        b_t = b.T
        k_t = k2.T
        bl_a = b_t[:, GLA_CHUNK - 1:GLA_CHUNK]
        bl_b = b_t[:, GLA_STEP - 1:GLA_STEP]
        kx_t = k_t * jnp.exp(jnp.where(first_t, bl_a, bl_b) - b_t)
        prep.append(dict(qi=qi, kblk=kblk, vblk=vblk,
                         qx_a=jnp.where(first, qx, zf).astype(BF16), qx_b=jnp.where(first, zf, qx).astype(BF16),
                         kx_a=jnp.where(first_t, kx_t, zf).astype(BF16),
                         kx_b=jnp.where(first_t, zf, kx_t).astype(BF16),
                         dec_a=jnp.exp(bl_a), dec_b=jnp.exp(bl_b)))

    atts = [jnp.where(causal, _dot_nt(p["qi"], p["kblk"]), 0.0).astype(BF16) for p in prep]
    kv_a = [jnp.where(diag, _dot(p["kx_a"], u[3]), 0.0) for p, u in zip(prep, units)]
    kv_b = [jnp.where(diag, _dot(p["kx_b"], u[3]), 0.0) for p, u in zip(prep, units)]
    outs = [_dot(a, p["vblk"]) for a, p in zip(atts, prep)]

    states = dict(states)
    for i, (pair, *_rest) in enumerate(units):
        p = prep[i]
        s_prev = states[pair]
        s_a = p["dec_a"] * s_prev + kv_a[i]
        outs[i] = outs[i] + _dot(p["qx_a"], s_prev.astype(BF16)) + _dot(p["qx_b"], s_a.astype(BF16))
        states[pair] = p["dec_b"] * s_a + kv_b[i]
    return outs, states


def _gla_init_body(k_ref, v_ref, gk_ref, s_out):
    causal, tri = _gla_consts()
    units = []
    for p in range(GLA_HEADS // 2):
        k2 = k_ref[:, p * GLA_PAIR_K:(p + 1) * GLA_PAIR_K].astype(F32)
        units.append((p, jnp.zeros_like(k2), k2, v_ref[:, p * GLA_PAIR_V:(p + 1) * GLA_PAIR_V],
                      gk_ref[:, p * GLA_PAIR_K:(p + 1) * GLA_PAIR_K]))
    zeros = {p: jnp.zeros((GLA_PAIR_K, GLA_PAIR_V), F32) for p in range(GLA_HEADS // 2)}
    _, states = _gla_steps(units, zeros, causal, tri)
    for p in range(GLA_HEADS // 2):
        s_out[p] = states[p]


def _gla_init_call(kg_m, vg_m, gk_m):
    return pl.pallas_call(
        _gla_init_body,
        out_shape=jax.ShapeDtypeStruct((GLA_HEADS // 2, GLA_PAIR_K, GLA_PAIR_V), F32),
        name="gla_init",
    )(kg_m, vg_m, gk_m)


def _gla_body(s0_ref, q_ref, k_ref, v_ref, gk_ref, rg_ref, nw_ref, o_ref, s_ref):
    t = pl.program_id(1)

    @pl.when(t == 0)
    def _():
        s_ref[...] = s0_ref[...]

    causal, tri = _gla_consts()
    nw = nw_ref[...]
    pairs = range(GLA_HEADS // 2)
    units = []
    where = []
    for u in range(GLA_GRID_STEPS):
        rows = slice(u * GLA_STEP, (u + 1) * GLA_STEP)
        for p in pairs:
            ks = slice(p * GLA_PAIR_K, (p + 1) * GLA_PAIR_K)
            vs = slice(p * GLA_PAIR_V, (p + 1) * GLA_PAIR_V)
            units.append((p, q_ref[rows, ks].astype(F32), k_ref[rows, ks].astype(F32), v_ref[rows, vs],
                          gk_ref[rows, ks]))
            where.append((rows, vs))
    outs, states = _gla_steps(units, {p: s_ref[p] for p in pairs}, causal, tri)
    for p in pairs:
        s_ref[p] = states[p]
    for o, (rows, vs) in zip(outs, where):
        heads = []
        for hh in range(2):
            oh = o[:, hh * GLA_VAL_DIM:(hh + 1) * GLA_VAL_DIM]
            ms = jnp.mean(oh * oh, axis=-1, keepdims=True)
            heads.append(oh * lax.rsqrt(ms + NORM_EPS) * nw)
        on = jnp.concatenate(heads, axis=1)
        o_ref[rows, vs] = (on * rg_ref[rows, vs].astype(F32)).astype(BF16)


GLA_GRID_STEPS = 8


def _mixers_body(sink_ref, q_ref, kp_ref, kc_ref, km_ref, vp_ref, vc_ref, vm_ref,
                 s0_ref, qg_ref, kg_ref, vg_ref, gk_ref, rg_ref, nw_ref, ao_ref, go_ref, s_ref):
    _gla_body(s0_ref, qg_ref, kg_ref, vg_ref, gk_ref, rg_ref, nw_ref, go_ref, s_ref)
    _attn_body(sink_ref, q_ref, kp_ref, kc_ref, km_ref, vp_ref, vc_ref, vm_ref, ao_ref)


def _mixers_call(q, kd, vd, km, vm, sinks, s0, qg, kg, vg, gk, rg, nw, batch, seq):
    step = ATTN_STEP_BLOCKS * BLOCK
    assert seq % step == 0 and step == GLA_GRID_STEPS * GLA_STEP
    seq_spec = lambda width: pl.BlockSpec((None, step, width), lambda b, i, s: (b, i, 0))
    prev = pl.BlockSpec((None, BLOCK, ATTN_KV_DUP_W),
                        lambda b, i, s: (b, jnp.maximum(i * ATTN_STEP_BLOCKS - 1, 0), 0))
    const = lambda shape: pl.BlockSpec(shape, lambda b, i, s: (0,) * len(shape))
    meta = const((BLOCK, ATTN_KV_DUP_W))
    grid_spec = pltpu.PrefetchScalarGridSpec(
        num_scalar_prefetch=1,
        grid=(batch, seq // step),
        in_specs=[seq_spec(ATTN_Q_W), prev, seq_spec(ATTN_KV_DUP_W), meta,
                  prev, seq_spec(ATTN_KV_DUP_W), meta,
                  const((GLA_HEADS // 2, GLA_PAIR_K, GLA_PAIR_V)),
                  seq_spec(GLA_K_W), seq_spec(GLA_K_W), seq_spec(GLA_V_W), seq_spec(GLA_K_W),
                  seq_spec(GLA_V_W), const((1, GLA_VAL_DIM))],
        out_specs=[seq_spec(ATTN_Q_W), seq_spec(GLA_V_W)],
        scratch_shapes=[pltpu.VMEM((GLA_HEADS // 2, GLA_PAIR_K, GLA_PAIR_V), F32)],
    )
    return pl.pallas_call(
        _mixers_body,
        grid_spec=grid_spec,
        out_shape=[jax.ShapeDtypeStruct((batch, seq, ATTN_Q_W), BF16),
                   jax.ShapeDtypeStruct((batch, seq, GLA_V_W), BF16)],
        compiler_params=pltpu.CompilerParams(dimension_semantics=("arbitrary", "arbitrary"),
                                             vmem_limit_bytes=VMEM_LIMIT),
        name="mixers",
    )(sinks, q, kd, kd, km, vd, vd, vm, s0, qg, kg, vg, gk, rg, nw)


ROUTER_W = LANES


RT_E0, RT_E1, RT_W0, RT_W1 = 0, 1, 2, 3


MERGE_SPLIT = 4


def _merge_body(x_ref, ao_ref, go_ref, sga_ref, sgg_ref, wab_ref, wgb_ref, wo_ref, n2_ref,
                wr2_ref, h_out, xn_out, rt_out, rtt_out, cnt_out):
    @pl.when(pl.program_id(0) == 0)
    def _():
        cnt_out[...] = jnp.zeros_like(cnt_out)

    rows_per = x_ref.shape[0] // MERGE_SPLIT
    for part in range(MERGE_SPLIT):
        _merge_rows(slice(part * rows_per, (part + 1) * rows_per), x_ref, ao_ref, go_ref, sga_ref, sgg_ref,
                    wab_ref, wgb_ref, wo_ref, n2_ref, wr2_ref, h_out, xn_out, rt_out, rtt_out, cnt_out)


def _merge_rows(rows, x_ref, ao_ref, go_ref, sga_ref, sgg_ref, wab_ref, wgb_ref, wo_ref, n2_ref,
                wr2_ref, h_out, xn_out, rt_out, rtt_out, cnt_out):
    ya = _dot(ao_ref[rows, :], wab_ref[...])
    yg = _dot(go_ref[rows, :], wgb_ref[...])
    merged = sga_ref[rows, :].astype(F32) * ya + sgg_ref[rows, :].astype(F32) * yg
    h = x_ref[rows, :] + _dot(merged.astype(BF16), wo_ref[...])
    xn = _rms_norm(h, n2_ref[...])
    for c in range(ROW_TILES):
        h_out[c, rows, :] = h[:, c * LANES:(c + 1) * LANES]
        xn_out[c, rows, :] = xn[:, c * LANES:(c + 1) * LANES]

    xh, xl = _split_hi_lo(xn)
    wr2 = wr2_ref[...]
    lg2 = _dot(xh, wr2) + _dot(xl, wr2)
    lg = lg2[:, :ROUTER_W] + lg2[:, ROUTER_W:]
    lg_t = lg.T
    tm = lg.shape[0]
    neg = -jnp.inf
    big = jnp.int32(LANES)
    gl = lg_t[N_EXPERTS:N_EXPERTS + N_GROUPS, :]
    grow = lax.broadcasted_iota(jnp.int32, (N_GROUPS, tm), 0)
    gmax = jnp.max(gl, axis=0, keepdims=True)
    g_idx = jnp.min(jnp.where(gl == gmax, grow, big), axis=0, keepdims=True)
    g_w = 1.0 / jnp.sum(jnp.exp(gl - gmax), axis=0, keepdims=True)
    erow = lax.broadcasted_iota(jnp.int32, (N_EXPERTS, tm), 0)
    group_shift = EXPERTS_PER_GROUP.bit_length() - 1
    el = jnp.where((erow >> group_shift) == g_idx, lg_t[0:N_EXPERTS, :], neg)
    m1 = jnp.max(el, axis=0, keepdims=True)
    i1 = jnp.min(jnp.where(el == m1, erow, big), axis=0, keepdims=True)
    el2 = jnp.where(erow == i1, neg, el)
    m2 = jnp.max(el2, axis=0, keepdims=True)
    i2 = jnp.min(jnp.where(el2 == m2, erow, big), axis=0, keepdims=True)
    e2 = jnp.exp(m2 - m1)
    p1 = g_w / (1.0 + e2)
    p2 = g_w * e2 / (1.0 + e2)
    def record(rows):
        r = lax.broadcasted_iota(jnp.int32, (rows, tm), 0)
        return jnp.where(r == RT_E0, i1.astype(F32),
                         jnp.where(r == RT_E1, i2.astype(F32),
                                   jnp.where(r == RT_W0, p1, jnp.where(r == RT_W1, p2, 0.0))))
    rtt_out[:, rows] = record(8)
    rt_out[rows, :] = record(LANES).T
    picked = jnp.where(jnp.logical_or(erow == i1, erow == i2), 1.0, 0.0)
    cnt_out[...] += jnp.sum(picked, axis=1, keepdims=True)


def _merge_call(x2d, ao, go, sga, sgg, w, tm):
    n = x2d.shape[0]
    row = lambda width: pl.BlockSpec((tm, width), lambda i: (i, 0))
    return pl.pallas_call(
        _merge_body,
        grid=(n // tm,),
        in_specs=[row(D_MODEL), row(ATTN_Q_W), row(GLA_V_W), row(D_MODEL), row(D_MODEL),
                  _const_spec((ATTN_Q_W, D_MODEL)), _const_spec((GLA_V_W, D_MODEL)),
                  _const_spec((D_MODEL, D_MODEL)), _const_spec((1, D_MODEL)),
                  _const_spec((D_MODEL, 2 * ROUTER_W))],
        out_specs=[pl.BlockSpec((ROW_TILES, tm, LANES), lambda i: (0, i, 0)),
                   pl.BlockSpec((ROW_TILES, tm, LANES), lambda i: (0, i, 0)), row(ROUTER_W),
                   pl.BlockSpec((8, tm), lambda i: (0, i)), _const_spec((N_EXPERTS, LANES))],
        out_shape=[jax.ShapeDtypeStruct((ROW_TILES, n, LANES), F32),
                   jax.ShapeDtypeStruct((ROW_TILES, n, LANES), F32),
                   jax.ShapeDtypeStruct((n, ROUTER_W), F32),
                   jax.ShapeDtypeStruct((8, n), F32),
                   jax.ShapeDtypeStruct((N_EXPERTS, LANES), F32)],
        compiler_params=pltpu.CompilerParams(dimension_semantics=("arbitrary",),
                                             vmem_limit_bytes=VMEM_LIMIT),
        name="merge",
    )(x2d, ao, go, sga, sgg, w["wab"], w["wgb"], w["wo"], w["norm2"], w["wr2"])


EXPERT_TILE = 512
EXPERT_SPLIT = 2
ROUTE_TILE = 512
VISIT_CAP = 256
VIS_EXPERT, VIS_TILE, VIS_LO, VIS_HI, VIS_COUNT = 0, 1, 2, 3, 4


def _num_visits(n_tokens):
    return (2 * n_tokens) // EXPERT_TILE + N_EXPERTS - 1


def _expert_cumsum_inclusive(x):
    row = lax.broadcasted_iota(jnp.int32, x.shape, 0)
    s = 1
    while s < N_EXPERTS:
        x = x + jnp.where(row >= s, pltpu.roll(x, s, 0), 0.0)
        s *= 2
    return x


def _route_body(rtt_ref, cnt_ref, pos_ref, vis_ref, carry_ref):
    i = pl.program_id(0)
    tm = rtt_ref.shape[1]
    cnt = cnt_ref[...]
    end = _expert_cumsum_inclusive(cnt)
    base = end - cnt

    @pl.when(i == 0)
    def _():
        carry_ref[...] = jnp.zeros_like(carry_ref)
        inv = 1.0 / EXPERT_TILE
        first = jnp.floor(base * inv)
        n_vis = jnp.where(cnt > 0.0, jnp.floor((end - 1.0) * inv) - first + 1.0, 0.0)
        vend = _expert_cumsum_inclusive(n_vis)
        vstart = vend - n_vis
        col = lambda a: a[:, 0:1]
        erow = lax.broadcasted_iota(jnp.int32, (N_EXPERTS, VISIT_CAP), 0).astype(F32)
        v = lax.broadcasted_iota(jnp.int32, (N_EXPERTS, VISIT_CAP), 1).astype(F32)
        ev = jnp.minimum(jnp.sum(jnp.where(col(vend) <= v, 1.0, 0.0), axis=0, keepdims=True),
                         N_EXPERTS - 1.0)
        mine = erow == ev
        pick = lambda a: jnp.sum(jnp.where(mine, a, 0.0), axis=0, keepdims=True)
        tile = pick(col(first) + v - col(vstart))
        lo = jnp.maximum(pick(col(base)) - tile * EXPERT_TILE, 0.0)
        hi = jnp.minimum(pick(col(end)) - tile * EXPERT_TILE, float(EXPERT_TILE))
        total = vend[N_EXPERTS - 1:N_EXPERTS, 0:1]
        r8 = lax.broadcasted_iota(jnp.int32, (8, VISIT_CAP), 0)
        rec = jnp.where(r8 == VIS_EXPERT, ev,
                        jnp.where(r8 == VIS_TILE, tile,
                                  jnp.where(r8 == VIS_LO, lo,
                                            jnp.where(r8 == VIS_HI, hi,
                                                      jnp.where(r8 == VIS_COUNT, total, 0.0)))))
        vis_ref[...] = rec.astype(jnp.int32)

    rtt = rtt_ref[...]
    erow = lax.broadcasted_iota(jnp.int32, (N_EXPERTS, tm), 0).astype(F32)
    oh0 = erow == rtt[RT_E0:RT_E0 + 1, :]
    oh1 = erow == rtt[RT_E1:RT_E1 + 1, :]
    picked = jnp.where(jnp.logical_or(oh0, oh1), 1.0, 0.0)
    r = lax.broadcasted_iota(jnp.int32, (tm, tm), 0)
    c = lax.broadcasted_iota(jnp.int32, (tm, tm), 1)
    earlier = jnp.where(r < c, 1.0, 0.0).astype(BF16)
    slot = _dot(picked.astype(BF16), earlier) + (carry_ref[:, 0:1] + base[:, 0:1])
    p0 = jnp.sum(jnp.where(oh0, slot, 0.0), axis=0, keepdims=True)
    p1 = jnp.sum(jnp.where(oh1, slot, 0.0), axis=0, keepdims=True)
    r8 = lax.broadcasted_iota(jnp.int32, (8, tm), 0)
    pos_ref[...] = jnp.where(r8 == 0, p0, jnp.where(r8 == 1, p1, 0.0)).astype(jnp.int32)
    carry_ref[...] += jnp.sum(picked, axis=1, keepdims=True)


def _route_call(rtt, cnt):
    n = rtt.shape[1]
    tm = _pick_tile(n, ROUTE_TILE)
    return pl.pallas_call(
        _route_body,
        grid=(n // tm,),
        in_specs=[pl.BlockSpec((8, tm), lambda i: (0, i)), _const_spec((N_EXPERTS, LANES))],
        out_specs=[pl.BlockSpec((8, tm), lambda i: (0, i)), _const_spec((8, VISIT_CAP))],
        out_shape=[jax.ShapeDtypeStruct((8, n), jnp.int32),
                   jax.ShapeDtypeStruct((8, VISIT_CAP), jnp.int32)],
        scratch_shapes=[pltpu.VMEM((N_EXPERTS, LANES), F32)],
        compiler_params=pltpu.CompilerParams(dimension_semantics=("arbitrary",)),
        name="route",
    )(rtt, cnt)


SCATTER_TILE = 512
ROW_TILES = D_MODEL // LANES


def _store_slabs(ref, val):
    for c in range(ROW_TILES):
        ref[c] = val[:, c * LANES:(c + 1) * LANES]


def _load_slabs(ref):
    return jnp.concatenate([ref[c] for c in range(ROW_TILES)], axis=1)


def _scatter_body(pos0_ref, pos1_ref, h_ref, hs_hbm, sem):
    tm = h_ref.shape[1]
    for r in range(tm):
        src = h_ref.at[:, r, :]
        pltpu.make_async_copy(src, hs_hbm.at[:, pos0_ref[r], :], sem).start(priority=0)
        pltpu.make_async_copy(src, hs_hbm.at[:, pos1_ref[r], :], sem).start(priority=1)
    for _ in range(2 * tm):
        pltpu.make_async_copy(h_ref.at[:, 0, :], hs_hbm.at[:, 0, :], sem).wait()


def _scatter_call(pos0, pos1, h_slabs, n_slots):
    n = h_slabs.shape[1]
    tm = _pick_tile(n, SCATTER_TILE)
    smem = lambda: pl.BlockSpec((tm,), lambda i: (i,), memory_space=pltpu.SMEM)
    return pl.pallas_call(
        _scatter_body,
        grid=(n // tm,),
        in_specs=[smem(), smem(), pl.BlockSpec((ROW_TILES, tm, LANES), lambda i: (0, i, 0))],
        out_specs=pl.BlockSpec(memory_space=pl.ANY),
        out_shape=jax.ShapeDtypeStruct((ROW_TILES, n_slots, LANES), F32),
        scratch_shapes=[pltpu.SemaphoreType.DMA],
        compiler_params=pltpu.CompilerParams(dimension_semantics=("arbitrary",)),
        name="scatter",
    )(pos0, pos1, h_slabs)


EXPERT_RING = 3


def _expert_body(vis_ref, xs_hbm, wg_ref, wu_ref, wd_ref, o_ref, x_ring, sem):
    j = pl.program_id(0)
    count = vis_ref[VIS_COUNT, 0]
    n_tiles = xs_hbm.shape[1] // EXPERT_TILE

    def tile_copy(t, slot):
        start = pl.multiple_of(t * EXPERT_TILE, EXPERT_TILE)
        return pltpu.make_async_copy(xs_hbm.at[:, pl.ds(start, EXPERT_TILE), :], x_ring.at[slot],
                                     sem.at[slot])

    @pl.when(j == 0)
    def _():
        for t in range(min(EXPERT_RING - 1, n_tiles)):
            tile_copy(t, t).start()

    @pl.when(j < count)
    def _():
        tile = vis_ref[VIS_TILE, j]
        slot = tile % EXPERT_RING
        lo_row = vis_ref[VIS_LO, j]
        hi_row = vis_ref[VIS_HI, j]
        x_ref = x_ring.at[slot]

        @pl.when(lo_row == 0)
        def _():
            tile_copy(tile, slot).wait()

            @pl.when(tile + EXPERT_RING - 1 < n_tiles)
            def _():
                nxt = tile + EXPERT_RING - 1
                tile_copy(nxt, nxt % EXPERT_RING).start()

        @pl.when(lo_row == 0)
        def _():
            o_ref[...] = jnp.zeros_like(o_ref)

        wg = wg_ref[...].astype(BF16)
        wu = wu_ref[...].astype(BF16)
        wd = wd_ref[...].astype(BF16)
        rows_per = EXPERT_TILE // EXPERT_SPLIT
        for part in range(EXPERT_SPLIT):
            rows = slice(part * rows_per, (part + 1) * rows_per)
            x = jnp.concatenate([x_ref[c, rows, :] for c in range(ROW_TILES)], axis=1).astype(BF16)
            g = _dot(x, wg)
            u = _dot(x, wu)
            hg = (g * _sigmoid(g) * u).astype(BF16)
            new = _dot(hg, wd)
            row = lax.broadcasted_iota(jnp.int32, (rows_per, LANES), 0) + part * rows_per
            keep = jnp.logical_and(row >= lo_row, row < hi_row)
            for c in range(ROW_TILES):
                o_ref[c, rows, :] = jnp.where(keep, new[:, c * LANES:(c + 1) * LANES], o_ref[c, rows, :])


def _expert_call(vis, hs, wg, wu, wd, n_visits):
    def visit(j, vis):
        return jnp.minimum(j, vis[VIS_COUNT, 0] - 1)

    def xmap(j, vis):
        return (0, vis[VIS_TILE, visit(j, vis)], 0)

    def wmap(j, vis):
        return (vis[VIS_EXPERT, visit(j, vis)], 0, 0)

    grid_spec = pltpu.PrefetchScalarGridSpec(
        num_scalar_prefetch=1,
        grid=(n_visits,),
        in_specs=[pl.BlockSpec(memory_space=pl.ANY),
                  pl.BlockSpec((None, D_MODEL, EXPERT_FF), wmap),
                  pl.BlockSpec((None, D_MODEL, EXPERT_FF), wmap),
                  pl.BlockSpec((None, EXPERT_FF, D_MODEL), wmap)],
        out_specs=pl.BlockSpec((ROW_TILES, EXPERT_TILE, LANES), xmap),
        scratch_shapes=[pltpu.VMEM((EXPERT_RING, ROW_TILES, EXPERT_TILE, LANES), F32),
                        pltpu.SemaphoreType.DMA((EXPERT_RING,))],
    )
    return pl.pallas_call(
        _expert_body,
        grid_spec=grid_spec,
        out_shape=jax.ShapeDtypeStruct(hs.shape, F32),
        compiler_params=pltpu.CompilerParams(dimension_semantics=("arbitrary",),
                                             vmem_limit_bytes=VMEM_LIMIT),
        name="experts",
    )(vis, hs, wg, wu, wd)


COMBINE_TILE = 256


def _combine_body(p0_ref, p1_ref, p0n_ref, p1n_ref, h_ref, rt_ref, y_hbm, o_ref, g_ref, sem):
    i = pl.program_id(0)
    tm = h_ref.shape[1]
    cur = i % 2

    def issue(q0_ref, q1_ref, buf):
        for r in range(tm):
            pltpu.make_async_copy(y_hbm.at[:, q0_ref[r], :], g_ref.at[buf, 0, :, r, :],
                                  sem.at[buf]).start(priority=0)
            pltpu.make_async_copy(y_hbm.at[:, q1_ref[r], :], g_ref.at[buf, 1, :, r, :],
                                  sem.at[buf]).start(priority=1)

    @pl.when(i == 0)
    def _():
        issue(p0_ref, p1_ref, 0)

    @pl.when(i + 1 < pl.num_programs(0))
    def _():
        issue(p0n_ref, p1n_ref, 1 - cur)

    for _ in range(2 * tm):
        pltpu.make_async_copy(y_hbm.at[:, 0, :], g_ref.at[cur, 0, :, 0, :], sem.at[cur]).wait()
    rt = rt_ref[...]
    o_ref[...] = (_load_slabs(h_ref) + rt[:, RT_W0:RT_W0 + 1] * _load_slabs(g_ref.at[cur, 0])
                  + rt[:, RT_W1:RT_W1 + 1] * _load_slabs(g_ref.at[cur, 1]))


def _combine_call(pos0, pos1, h_slabs, rt, y):
    n = h_slabs.shape[1]
    tm = _pick_tile(n, COMBINE_TILE)
    steps = n // tm
    smem_cur = lambda: pl.BlockSpec((tm,), lambda i: (i,), memory_space=pltpu.SMEM)
    smem_next = lambda: pl.BlockSpec((tm,), lambda i: (jnp.minimum(i + 1, steps - 1),),
                                     memory_space=pltpu.SMEM)
    row = lambda width: pl.BlockSpec((tm, width), lambda i: (i, 0))
    return pl.pallas_call(
        _combine_body,
        grid=(steps,),
        in_specs=[smem_cur(), smem_cur(), smem_next(), smem_next(),
                  pl.BlockSpec((ROW_TILES, tm, LANES), lambda i: (0, i, 0)), row(ROUTER_W),
                  pl.BlockSpec(memory_space=pl.ANY)],
        out_specs=row(D_MODEL),
        out_shape=jax.ShapeDtypeStruct((n, D_MODEL), F32),
        scratch_shapes=[pltpu.VMEM((2, 2, ROW_TILES, tm, LANES), F32), pltpu.SemaphoreType.DMA((2,))],
        compiler_params=pltpu.CompilerParams(dimension_semantics=("arbitrary",),
                                             vmem_limit_bytes=VMEM_LIMIT),
        name="combine",
    )(pos0, pos1, pos0, pos1, h_slabs, rt, y)


def _rope_tables(pos):
    half = HEAD_DIM // 2
    inv_freq = ROPE_THETA ** (-(jnp.arange(half, dtype=F32) * 2.0) / HEAD_DIM)
    ang = pos.astype(F32)[:, None] * inv_freq[None, :]
    cos = jnp.cos(ang)
    sin = jnp.sin(ang)
    cos64 = jnp.concatenate([cos, cos], axis=1)
    sin64 = jnp.concatenate([-sin, sin], axis=1)
    return jnp.tile(cos64, (1, LANES // HEAD_DIM)), jnp.tile(sin64, (1, LANES // HEAD_DIM))


WPREP_ROWS = 128


def _wprep_body(w_ref, qkv_out, gla_out, gate_out):
    w = w_ref[...]
    lane = lax.broadcasted_iota(jnp.int32, (WPREP_ROWS, LANES), 1)
    qkv_out[:, :QKV_GD_OFF] = w[:, :QKV_GD_OFF].astype(BF16)
    gd_tile = w[:, _OFF_GD:_OFF_GD + LANES]
    qkv_out[:, QKV_GD_OFF:QKV_GD_OFF + LANES] = jnp.where(lane < GLA_GATE_RANK, gd_tile, 0.0).astype(BF16)
    qkv_out[:, QKV_GD_OFF + LANES:] = jnp.zeros((WPREP_ROWS, QKV_W - QKV_GD_OFF - LANES), BF16)
    gla_out[...] = w[:, _OFF_GQ:_OFF_GQ + GLA_W].astype(BF16)
    keep = LANES - GLA_GATE_RANK
    n_tiles = 2 * D_MODEL // LANES
    rolled = [pltpu.roll(w[:, _OFF_GD + t * LANES:_OFF_GD + (t + 1) * LANES], keep, 1)
              for t in range(n_tiles)]
    tail = w[:, _OFF_GD + n_tiles * LANES:]
    tail = jnp.concatenate([tail, jnp.zeros((WPREP_ROWS, LANES - tail.shape[1]), F32)], axis=1)
    rolled.append(pltpu.roll(tail, keep, 1))
    for t in range(n_tiles):
        gate_out[:, t * LANES:(t + 1) * LANES] = jnp.where(lane < keep, rolled[t], rolled[t + 1]).astype(BF16)


def _wprep_call(w_in):
    in_total = w_in.shape[2]
    assert in_total == _OFF_GATE + 2 * D_MODEL and D_MODEL % WPREP_ROWS == 0
    row = lambda width: pl.BlockSpec((WPREP_ROWS, width), lambda i: (i, 0))
    return pl.pallas_call(
        _wprep_body,
        grid=(D_MODEL // WPREP_ROWS,),
        in_specs=[pl.BlockSpec((None, WPREP_ROWS, in_total), lambda i: (0, i, 0))],
        out_specs=[row(QKV_W), row(GLA_W), row(2 * D_MODEL)],
        out_shape=[jax.ShapeDtypeStruct((D_MODEL, QKV_W), BF16),
                   jax.ShapeDtypeStruct((D_MODEL, GLA_W), BF16),
                   jax.ShapeDtypeStruct((D_MODEL, 2 * D_MODEL), BF16)],
        compiler_params=pltpu.CompilerParams(dimension_semantics=("arbitrary",)),
        name="wprep",
    )(w_in)


def _prep_weights(norm1_w, w_in, q_norm_w, k_norm_w, gla_gate_up, gla_gate_bias, w_attn_branch,
                  w_gla_branch, w_out, norm2_w, router_group, router_expert):
    wqkv, wgla, wgate = _wprep_call(w_in)
    gup = jnp.zeros((LANES, GLA_K_W), BF16).at[:GLA_GATE_RANK].set(gla_gate_up.astype(BF16))
    wr = jnp.zeros((D_MODEL, ROUTER_W), F32)
    wr = wr.at[:, :N_EXPERTS].set(router_expert.transpose(1, 0, 2).reshape(D_MODEL, N_EXPERTS))
    wr = wr.at[:, N_EXPERTS:N_EXPERTS + N_GROUPS].set(router_group)
    wrh = wr.astype(BF16)
    wrl = (wr - wrh.astype(F32)).astype(BF16)
    return {
        "norm1": norm1_w.reshape(1, D_MODEL),
        "wqkv": wqkv,
        "wgla": wgla,
        "wgate": wgate,
        "qn": jnp.tile(q_norm_w, ATTN_HEADS).reshape(1, ATTN_Q_W),
        "kn": jnp.tile(k_norm_w, ATTN_KV_HEADS).reshape(1, ATTN_KV_W),
        "gup": gup,
        "gb": gla_gate_bias.reshape(1, GLA_K_W),
        "wab": w_attn_branch.astype(BF16),
        "wgb": w_gla_branch.astype(BF16),
        "wo": w_out.astype(BF16),
        "norm2": norm2_w.reshape(1, D_MODEL),
        "wr2": jnp.concatenate([wrh, wrl], axis=1),
    }


def _pick_tile(n, want):
    t = want
    while n % t:
        t //= 2
    return t


def kernel(x, meta_tokens, norm1_w, w_in, q_norm_w, k_norm_w, attn_sinks, gla_gate_up, gla_gate_bias,
           gla_norm_w, w_attn_branch, w_gla_branch, w_out, norm2_w, router_group, router_expert,
           expert_w_gate, expert_w_up, expert_w_down):
    batch, seq, d = x.shape
    assert d == D_MODEL and seq % GLA_STEP == 0 and norm1_w.shape[0] == 1
    n = batch * seq
    w = _prep_weights(norm1_w[0], w_in, q_norm_w[0], k_norm_w[0], gla_gate_up[0], gla_gate_bias[0],
                      w_attn_branch[0], w_gla_branch[0], w_out[0], norm2_w[0], router_group[0],
                      router_expert[0])
    x2d = x.reshape(n, D_MODEL)

    cos_m, sin_m = _rope_tables(jnp.arange(N_META))
    cos_r, sin_r = _rope_tables(jnp.arange(seq) + N_META)
    meta = _proj_call(meta_tokens.astype(F32), N_META, 1, w, cos_m, sin_m)
    tm = _pick_tile(seq, 512)
    (q, kd, vd, qg, kg, vg, rg, gk, sga, sgg) = _proj_call(x2d, tm, seq // tm, w, cos_r, sin_r)

    pad_after = ((0, BLOCK - N_META), (0, 0))
    km = jnp.pad(meta[1], pad_after)
    vm = jnp.pad(meta[2], pad_after)
    pad_before = ((GLA_STEP - N_META, 0), (0, 0))
    s0 = _gla_init_call(jnp.pad(meta[4], pad_before), jnp.pad(meta[5], pad_before),
                        jnp.pad(meta[7], pad_before))
    ao, go = _mixers_call(q.reshape(batch, seq, ATTN_Q_W), kd.reshape(batch, seq, ATTN_KV_DUP_W),
                          vd.reshape(batch, seq, ATTN_KV_DUP_W), km, vm, attn_sinks[0].astype(F32),
                          s0, qg.reshape(batch, seq, GLA_K_W), kg.reshape(batch, seq, GLA_K_W),
                          vg.reshape(batch, seq, GLA_V_W), gk.reshape(batch, seq, GLA_K_W),
                          rg.reshape(batch, seq, GLA_V_W), gla_norm_w[0].reshape(1, GLA_VAL_DIM),
                          batch, seq)

    h, xn2, rt, rtt, cnt = _merge_call(x2d, ao.reshape(n, ATTN_Q_W), go.reshape(n, GLA_V_W), sga, sgg, w,
                                       _pick_tile(n, 1024))

    assert (2 * n) % EXPERT_TILE == 0 and _num_visits(n) <= VISIT_CAP
    pos, vis = _route_call(rtt, cnt)
    pos0, pos1 = pos[0], pos[1]
    xs = _scatter_call(pos0, pos1, xn2, 2 * n)
    wg = expert_w_gate[0].reshape(N_EXPERTS, D_MODEL, EXPERT_FF)
    wu = expert_w_up[0].reshape(N_EXPERTS, D_MODEL, EXPERT_FF)
    wd = expert_w_down[0].reshape(N_EXPERTS, EXPERT_FF, D_MODEL)
    y = _expert_call(vis, xs, wg, wu, wd, _num_visits(n))
    out = _combine_call(pos0, pos1, h, rt, y)
    return out.reshape(batch, seq, D_MODEL)
```
